```python
import math
import jax, jax.numpy as jnp
from jax import lax
import numpy as np

D_MODEL = 1024
BATCH = 8
SEQ = 2048
DEPTH = 2

N_HEADS = 8
HEAD_DIM = 64
ATTN_WIDTH = N_HEADS * HEAD_DIM
CONV_WIDTH = 512
CONV_K = 31
DILATED_GROUPS = ((128, 1), (512, 4), (2048, 16))
BLK = 128
ROPE_THETA = 10000.0
N_EXPERT_GROUPS = 4
EXPERTS_PER_GROUP = 8
N_EXPERTS = N_EXPERT_GROUPS * EXPERTS_PER_GROUP
EXPERT_FF = 512
TOP_K_IN_GROUP = 2
MOE_BLOCK = 128
PLE_DIM = 256
EPS = 1e-6
NEG_INF = -1e30

W_IN_SPLITS = (ATTN_WIDTH, 2 * ATTN_WIDTH, 3 * ATTN_WIDTH,
               3 * ATTN_WIDTH + CONV_WIDTH, 3 * ATTN_WIDTH + 2 * CONV_WIDTH,
               3 * ATTN_WIDTH + 2 * CONV_WIDTH + D_MODEL)
W_IN_COLS = 3 * ATTN_WIDTH + 2 * CONV_WIDTH + 2 * D_MODEL

kernel_name = "hybrid_dilated_attn_conformer_hmoe_ple"


def _rms_norm(t, g):
    tf = t.astype(jnp.float32)
    y = tf * lax.rsqrt(jnp.mean(tf * tf, axis=-1, keepdims=True) + EPS)
    return (y * g.astype(jnp.float32)).astype(t.dtype)


def _layer_norm(t, g, b):
    tf = t.astype(jnp.float32)
    mu = jnp.mean(tf, axis=-1, keepdims=True)
    var = jnp.mean(jnp.square(tf - mu), axis=-1, keepdims=True)
    y = (tf - mu) * lax.rsqrt(var + EPS)
    return (y * g.astype(jnp.float32) + b.astype(jnp.float32)).astype(t.dtype)


def _rope(t):
    S = t.shape[1]
    inv = jnp.power(ROPE_THETA, -jnp.arange(0, HEAD_DIM, 2, dtype=jnp.float32) / HEAD_DIM)
    ang = jnp.arange(S, dtype=jnp.float32)[:, None] * inv[None, :]
    cos = jnp.cos(ang)[None, :, None, :]
    sin = jnp.sin(ang)[None, :, None, :]
    tf = t.astype(jnp.float32)
    t1, t2 = tf[..., :HEAD_DIM // 2], tf[..., HEAD_DIM // 2:]
    return jnp.concatenate([t1 * cos - t2 * sin, t1 * sin + t2 * cos], axis=-1).astype(t.dtype)


def _to_dilated_blocks(t, dilation):
    B, S = t.shape[0], t.shape[1]
    rest = t.shape[2:]
    seg = dilation * BLK
    Sp = -(-S // seg) * seg
    t = jnp.pad(t, [(0, 0), (0, Sp - S)] + [(0, 0)] * len(rest))
    t = t.reshape((B, Sp // dilation, dilation) + rest)
    t = jnp.moveaxis(t, 2, 1)
    return t.reshape((B, dilation, Sp // seg, BLK) + rest)


def _from_dilated_blocks(t, S):
    B, d, nb, blk = t.shape[:4]
    rest = t.shape[4:]
    t = t.reshape((B, d, nb * blk) + rest)
    t = jnp.moveaxis(t, 1, 2).reshape((B, nb * blk * d) + rest)
    return t[:, :S]


def _dilated_window_attention(q, k, v, window, dilation):
    band = window // dilation
    assert band <= BLK
    S = q.shape[1]
    qb, kb, vb = (_to_dilated_blocks(t, dilation) for t in (q, k, v))

    def with_prev(t):
        prev = jnp.pad(t, ((0, 0), (0, 0), (1, 0), (0, 0), (0, 0), (0, 0)))[:, :, :-1]
        return jnp.concatenate([prev, t], axis=3)

    kk, vv = with_prev(kb), with_prev(vb)
    s = jnp.einsum('brnqhe,brnkhe->brnhqk', qb, kk,
                   preferred_element_type=jnp.float32) * (HEAD_DIM ** -0.5)
    nb = qb.shape[2]
    qi = jnp.arange(BLK)[:, None] + BLK
    kj = jnp.arange(2 * BLK)[None, :]
    in_band = (kj <= qi) & (qi - kj <= band)
    not_before_start = (jnp.arange(nb)[:, None, None] > 0) | (kj >= BLK)[None]
    allowed = in_band[None] & not_before_start
    s = jnp.where(allowed[:, None], s, NEG_INF)
    m = jnp.max(s, axis=-1)
    pexp = jnp.exp(s - m[..., None])
    l = jnp.sum(pexp, axis=-1)
    acc = jnp.einsum('brnhqk,brnkhe->brnqhe', pexp.astype(v.dtype), vv,
                     preferred_element_type=jnp.float32)
    m = jnp.moveaxis(m, 3, 4)
    l = jnp.moveaxis(l, 3, 4)
    return (_from_dilated_blocks(acc, S), _from_dilated_blocks(m, S), _from_dilated_blocks(l, S))


def _dilated_mixture_attention(q, k, v):
    parts = [_dilated_window_attention(q, k, v, w, d) for (w, d) in DILATED_GROUPS]
    M = parts[0][1]
    for _, m_g, _ in parts[1:]:
        M = jnp.maximum(M, m_g)
    num = 0.0
    den = 0.0
    for acc_g, m_g, l_g in parts:
        scale = jnp.exp(m_g - M)
        num = num + acc_g * scale[..., None]
        den = den + l_g * scale
    o = num / den[..., None]
    B, S = q.shape[0], q.shape[1]
    return o.reshape(B, S, ATTN_WIDTH).astype(q.dtype)


def _conformer_conv(cu, cg, w_dw, b_dw, ln_g, ln_b):
    u = cu * jax.nn.sigmoid(cg)
    rhs = w_dw.reshape(CONV_K, 1, CONV_WIDTH).astype(u.dtype)
    y = lax.conv_general_dilated(u, rhs, window_strides=(1,), padding=[(CONV_K - 1, 0)],
                                 dimension_numbers=('NWC', 'WIO', 'NWC'),
                                 feature_group_count=CONV_WIDTH)
    y = y + b_dw
    y = _layer_norm(y, ln_g, ln_b)
    return jax.nn.silu(y)


def _hier_moe(h, w_rg, b_rg, w_re, b_re, w_gate, w_up, w_down):
    B, S, D = h.shape
    T = B * S
    hf = h.reshape(T, D)
    lg = (hf @ w_rg + b_rg).astype(jnp.float32)
    pg = jax.nn.softmax(lg, axis=-1)
    gsel = jnp.argmax(lg, axis=-1)
    le = (hf @ w_re + b_re).astype(jnp.float32).reshape(T, N_EXPERT_GROUPS, EXPERTS_PER_GROUP)
    le_sel = jnp.take_along_axis(le, gsel[:, None, None], axis=1)[:, 0]
    top_v, top_i = lax.top_k(le_sel, TOP_K_IN_GROUP)
    gate = jnp.take_along_axis(pg, gsel[:, None], axis=1) * jax.nn.softmax(top_v, axis=-1)
    eid = gsel[:, None] * EXPERTS_PER_GROUP + top_i

    A = T * TOP_K_IN_GROUP
    flat_e = eid.reshape(-1)
    order = jnp.argsort(flat_e)
    sorted_e = flat_e[order]
    counts = jnp.bincount(flat_e, length=N_EXPERTS)
    padded = ((counts + MOE_BLOCK - 1) // MOE_BLOCK) * MOE_BLOCK
    ends = jnp.cumsum(padded)
    pad_start = ends - padded
    start = jnp.cumsum(counts) - counts
    dest = pad_start[sorted_e] + (jnp.arange(A) - start[sorted_e])
    n_rows = (-(-A // MOE_BLOCK)) * MOE_BLOCK + N_EXPERTS * MOE_BLOCK
    n_blocks = n_rows // MOE_BLOCK
    tok = order // TOP_K_IN_GROUP
    buf = jnp.zeros((n_rows, D), hf.dtype).at[dest].set(hf[tok])
    block_e = jnp.minimum(jnp.searchsorted(ends, jnp.arange(n_blocks) * MOE_BLOCK, side='right'),
                          N_EXPERTS - 1)

    def expert_block(args):
        xb, e = args
        hb = jax.nn.silu(xb @ w_gate[e]) * (xb @ w_up[e])
        return hb @ w_down[e]

    yb = lax.map(expert_block, (buf.reshape(n_blocks, MOE_BLOCK, D), block_e))
    y_rows = yb.reshape(n_rows, D)[dest]
    contrib = y_rows * gate.reshape(-1)[order][:, None].astype(y_rows.dtype)
    out = jnp.zeros((T, D), hf.dtype).at[tok].add(contrib)
    return out.reshape(B, S, D)


def setup_inputs(seed: int = 0) -> dict:
    key = jax.random.key(seed)
    ks = jax.random.split(key, 24)
    L, D, f32 = DEPTH, D_MODEL, jnp.float32

    def nrm(k, shape, fan_in):
        return jax.random.normal(k, shape, f32) * (fan_in ** -0.5)

    def gain(k, shape):
        return 1.0 + 0.05 * jax.random.normal(k, shape, f32)

    def small(k, shape, s=0.02):
        return s * jax.random.normal(k, shape, f32)

    return {
        "x": jax.random.normal(ks[0], (BATCH, SEQ, D), f32),
        "p": jax.random.normal(ks[1], (DEPTH, BATCH, SEQ, PLE_DIM), f32),
        "norm_mix": gain(ks[2], (L, D)),
        "w_in": nrm(ks[3], (L, D, W_IN_COLS), D),
        "w_dw": nrm(ks[4], (L, CONV_K, CONV_WIDTH), CONV_K),
        "b_dw": small(ks[5], (L, CONV_WIDTH)),
        "ln_conv_g": gain(ks[6], (L, CONV_WIDTH)),
        "ln_conv_b": small(ks[7], (L, CONV_WIDTH)),
        "w_attn_out": nrm(ks[8], (L, ATTN_WIDTH, D), ATTN_WIDTH),
        "w_conv_out": nrm(ks[9], (L, CONV_WIDTH, D), CONV_WIDTH),
        "b_conv_out": small(ks[10], (L, D)),
        "w_out": nrm(ks[11], (L, D, D), D),
        "norm_ffn": gain(ks[12], (L, D)),
        "w_route_group": nrm(ks[13], (L, D, N_EXPERT_GROUPS), D),
        "b_route_group": small(ks[14], (L, N_EXPERT_GROUPS), 0.01),
        "w_route_expert": nrm(ks[15], (L, D, N_EXPERTS), D),
        "b_route_expert": small(ks[16], (L, N_EXPERTS), 0.01),
        "w_exp_gate": nrm(ks[17], (L, N_EXPERTS, D, EXPERT_FF), D),
        "w_exp_up": nrm(ks[18], (L, N_EXPERTS, D, EXPERT_FF), D),
        "w_exp_down": nrm(ks[19], (L, N_EXPERTS, EXPERT_FF, D), EXPERT_FF),
        "norm_ple": gain(ks[20], (L, D)),
        "w_ple_proj": nrm(ks[21], (L, PLE_DIM, D), PLE_DIM),
        "w_ple_gate": nrm(ks[22], (L, D, D), D),
        "norm_final": gain(ks[23], (D,)),
    }


def reference(x, p, norm_mix, w_in, w_dw, b_dw, ln_conv_g, ln_conv_b, w_attn_out,
              w_conv_out, b_conv_out, w_out, norm_ffn, w_route_group, b_route_group,
              w_route_expert, b_route_expert, w_exp_gate, w_exp_up, w_exp_down,
              norm_ple, w_ple_proj, w_ple_gate, norm_final):
    B, S, _ = x.shape
    for i in range(DEPTH):
        h = _rms_norm(x, norm_mix[i])
        z = h @ w_in[i]
        q, k, v, cu, cg, ga, gb = jnp.split(z, W_IN_SPLITS, axis=-1)
        q = _rope(q.reshape(B, S, N_HEADS, HEAD_DIM))
        k = _rope(k.reshape(B, S, N_HEADS, HEAD_DIM))
        v = v.reshape(B, S, N_HEADS, HEAD_DIM)
        y_a = _dilated_mixture_attention(q, k, v) @ w_attn_out[i]
        y_b = _conformer_conv(cu, cg, w_dw[i], b_dw[i], ln_conv_g[i], ln_conv_b[i]) \
            @ w_conv_out[i] + b_conv_out[i]
        merged = jax.nn.sigmoid(ga) * y_a + jax.nn.sigmoid(gb) * y_b
        x = x + merged @ w_out[i]
        h2 = _rms_norm(x, norm_ffn[i])
        x = x + _hier_moe(h2, w_route_group[i], b_route_group[i], w_route_expert[i],
                          b_route_expert[i], w_exp_gate[i], w_exp_up[i], w_exp_down[i])
        g_ple = jax.nn.sigmoid(_rms_norm(x, norm_ple[i]) @ w_ple_gate[i])
        x = x + g_ple * (p[i] @ w_ple_proj[i])
    return _rms_norm(x, norm_final)
```

```python
import functools

import jax
import jax.numpy as jnp
from jax import lax
from jax.experimental import pallas as pl
from jax.experimental.pallas import tpu as pltpu

D_MODEL = 1024
N_HEADS = 8
HEAD_DIM = 64
ATTN_WIDTH = N_HEADS * HEAD_DIM
CONV_WIDTH = 512
CONV_K = 31
DILATIONS = (16, 4, 1)
BLK = 128
ROPE_THETA = 10000.0
N_GROUPS = 4
EXPERTS_PER_GROUP = 8
N_EXPERTS = N_GROUPS * EXPERTS_PER_GROUP
EXPERT_FF = 512
TOP_K = 2
MOE_BLOCK = 128
PLE_DIM = 256
EPS = 1e-6
NEG_INF = -1e30

LANES = 128
HEADS_PER_MXU_PASS = 4
STATE_W = ATTN_WIDTH + LANES
VMEM_LIMIT = 56 * 1024 * 1024

F32 = jnp.float32
BF16 = jnp.bfloat16


def _cparams(sem):
    return pltpu.CompilerParams(dimension_semantics=sem, vmem_limit_bytes=VMEM_LIMIT)


def _rms(x, g):
    return x * lax.rsqrt(jnp.mean(x * x, axis=-1, keepdims=True) + EPS) * g


def _sigmoid(x):
    return 1.0 / (1.0 + jnp.exp(-x))


def _inproj_kernel(x_ref, g_ref, w_ref, cos_ref, sin_ref,
                   q_ref, k_ref, v_ref, u_ref, ga_ref, gb_ref):
    tm = x_ref.shape[0]
    h = _rms(x_ref[...], g_ref[...]).astype(BF16)

    def proj(c0, width):
        return jnp.dot(h, w_ref[:, c0:c0 + width], preferred_element_type=F32)

    cos = cos_ref[...]
    sin = sin_ref[...]
    lane = lax.broadcasted_iota(jnp.int32, (tm, ATTN_WIDTH), 1)
    first_half = (lane % HEAD_DIM) < (HEAD_DIM // 2)

    def rope(t):
        partner = jnp.where(first_half,
                            pltpu.roll(t, ATTN_WIDTH - HEAD_DIM // 2, 1),
                            pltpu.roll(t, HEAD_DIM // 2, 1))
        return t * cos + partner * sin

    q_ref[...] = rope(proj(0, ATTN_WIDTH)).astype(BF16)
    k_ref[...] = rope(proj(ATTN_WIDTH, ATTN_WIDTH)).astype(BF16)
    v_ref[...] = proj(2 * ATTN_WIDTH, ATTN_WIDTH).astype(BF16)
    c0 = 3 * ATTN_WIDTH
    cu = proj(c0, CONV_WIDTH)
    cg = proj(c0 + CONV_WIDTH, CONV_WIDTH)
    u_ref[...] = cu * _sigmoid(cg)
    c0 += 2 * CONV_WIDTH
    ga_ref[...] = proj(c0, D_MODEL)
    gb_ref[...] = proj(c0 + D_MODEL, D_MODEL)


def _inproj(x, gain, w_in, cos, sin, seq, tm):
    t = x.shape[0]
    n_pos = seq // tm
    row = lambda i: (i, 0)
    const = lambda i: (0, 0)
    out_shape = (
        jax.ShapeDtypeStruct((t, ATTN_WIDTH), BF16),
        jax.ShapeDtypeStruct((t, ATTN_WIDTH), BF16),
        jax.ShapeDtypeStruct((t, ATTN_WIDTH), BF16),
        jax.ShapeDtypeStruct((t, CONV_WIDTH), F32),
        jax.ShapeDtypeStruct((t, D_MODEL), F32),
        jax.ShapeDtypeStruct((t, D_MODEL), F32),
    )
    return pl.pallas_call(
        _inproj_kernel,
        grid=(t // tm,),
        in_specs=[
            pl.BlockSpec((tm, D_MODEL), row),
            pl.BlockSpec((1, D_MODEL), const),
            pl.BlockSpec(w_in.shape, const),
            pl.BlockSpec((tm, ATTN_WIDTH), lambda i: (i % n_pos, 0)),
            pl.BlockSpec((tm, ATTN_WIDTH), lambda i: (i % n_pos, 0)),
        ],
        out_specs=(
            pl.BlockSpec((tm, ATTN_WIDTH), row),
            pl.BlockSpec((tm, ATTN_WIDTH), row),
            pl.BlockSpec((tm, ATTN_WIDTH), row),
            pl.BlockSpec((tm, CONV_WIDTH), row),
            pl.BlockSpec((tm, D_MODEL), row),
            pl.BlockSpec((tm, D_MODEL), row),
        ),
        out_shape=out_shape,
        compiler_params=_cparams(("parallel",)),
        name="inproj",
    )(x, gain, w_in, cos, sin)


def _attn_kernel(*refs, has_prev, has_state, final):
    it = iter(refs)
    q_ref = next(it)
    kc_ref = next(it)
    vc_ref = next(it)
    kp_ref = next(it) if has_prev else None
    vp_ref = next(it) if has_prev else None
    st_ref = next(it) if has_state else None
    out_ref = next(it)
    kbd_ref = next(it)
    vbd_ref = next(it)

    n = pl.program_id(2)
    hw = HEADS_PER_MXU_PASS * HEAD_DIM
    kw = 2 * BLK if has_prev else BLK

    row = lax.broadcasted_iota(jnp.int32, (BLK, BLK), 0)
    col = lax.broadcasted_iota(jnp.int32, (BLK, BLK), 1)
    cur_ok = col <= row
    if has_prev:
        prev_ok = jnp.logical_and(col >= row, n > 0)
        allowed = jnp.concatenate([prev_ok, cur_ok], axis=1)
    else:
        allowed = cur_ok

    key_head = lax.broadcasted_iota(jnp.int32, (BLK, hw), 1) // HEAD_DIM
    lane_head = lax.broadcasted_iota(jnp.int32, (BLK, hw), 1) // HEAD_DIM
    stat_lane = lax.broadcasted_iota(jnp.int32, (BLK, LANES), 1)

    q = q_ref[0]
    if has_state:
        st = st_ref[0]
        old_stat = st[:, ATTN_WIDTH:]
    new_stat = jnp.zeros((BLK, LANES), F32)

    for g in range(N_HEADS // HEADS_PER_MXU_PASS):
        ls = slice(g * hw, (g + 1) * hw)
        for hh in range(HEADS_PER_MXU_PASS):
            keep = key_head == hh
            base = hh * kw
            if has_prev:
                kbd_ref[base:base + BLK, :] = jnp.where(keep, kp_ref[0][:, ls], 0)
                vbd_ref[base:base + BLK, :] = jnp.where(keep, vp_ref[0][:, ls], 0)
                base += BLK
            kbd_ref[base:base + BLK, :] = jnp.where(keep, kc_ref[0][:, ls], 0)
            vbd_ref[base:base + BLK, :] = jnp.where(keep, vc_ref[0][:, ls], 0)

        s_all = lax.dot_general(q[:, ls], kbd_ref[...], (((1,), (1,)), ((), ())),
                                preferred_element_type=F32)
        p_parts = []
        m_heads = []
        l_heads = []
        for hh in range(HEADS_PER_MXU_PASS):
            s = s_all[:, hh * kw:(hh + 1) * kw] * (HEAD_DIM ** -0.5)
            s = jnp.where(allowed, s, NEG_INF)
            m_new = jnp.max(s, axis=-1, keepdims=True)
            p = jnp.exp(s - m_new)
            l_heads.append(jnp.sum(p, axis=-1, keepdims=True))
            m_heads.append(m_new)
            p_parts.append(p.astype(BF16))
        acc = jnp.dot(jnp.concatenate(p_parts, axis=1), vbd_ref[...],
                      preferred_element_type=F32)

        if has_state:
            w_old = jnp.zeros((BLK, hw), F32)
            w_new = jnp.zeros((BLK, hw), F32)
        l_lanes = jnp.zeros((BLK, hw), F32)
        for hh in range(HEADS_PER_MXU_PASS):
            h = g * HEADS_PER_MXU_PASS + hh
            m_h, l_h = m_heads[hh], l_heads[hh]
            if has_state:
                m_old = old_stat[:, h:h + 1]
                l_old = old_stat[:, N_HEADS + h:N_HEADS + h + 1]
                m_tot = jnp.maximum(m_old, m_h)
                a_old = jnp.exp(m_old - m_tot)
                a_new = jnp.exp(m_h - m_tot)
                l_h = l_old * a_old + l_h * a_new
                m_h = m_tot
                w_old = jnp.where(lane_head == hh, a_old, w_old)
                w_new = jnp.where(lane_head == hh, a_new, w_new)
            l_lanes = jnp.where(lane_head == hh, l_h, l_lanes)
            new_stat = jnp.where(stat_lane == h, m_h, new_stat)
            new_stat = jnp.where(stat_lane == N_HEADS + h, l_h, new_stat)
        if has_state:
            acc = st[:, ls] * w_old + acc * w_new
        if final:
            out_ref[0, :, ls] = (acc / l_lanes).astype(out_ref.dtype)
        else:
            out_ref[0, :, ls] = acc
    if not final:
        out_ref[0, :, ATTN_WIDTH:] = new_stat


def _attn_sweep(q, k, v, state, dilation, final):
    b, s, _ = q.shape
    seg = dilation * BLK
    assert s % seg == 0
    nb = s // seg
    has_prev = nb > 1
    has_state = state is not None
    view = lambda a: a.reshape(b, s // dilation, dilation * a.shape[-1])
    cur = lambda bi, r, n: (bi, n, r)
    prev = lambda bi, r, n: (bi, jnp.maximum(n - 1, 0), r)
    blk = lambda w: pl.BlockSpec((1, BLK, w), cur)
    args = [view(q), view(k), view(v)]
    in_specs = [blk(ATTN_WIDTH), blk(ATTN_WIDTH), blk(ATTN_WIDTH)]
    if has_prev:
        args += [view(k), view(v)]
        in_specs += [pl.BlockSpec((1, BLK, ATTN_WIDTH), prev)] * 2
    if has_state:
        args.append(view(state))
        in_specs.append(blk(STATE_W))
    if final:
        out_w, out_dt = ATTN_WIDTH, BF16
    else:
        out_w, out_dt = STATE_W, F32
    kw = 2 * BLK if has_prev else BLK
    hw = HEADS_PER_MXU_PASS * HEAD_DIM
    out = pl.pallas_call(
        functools.partial(_attn_kernel, has_prev=has_prev, has_state=has_state, final=final),
        grid=(b, dilation, nb),
        in_specs=in_specs,
        out_specs=blk(out_w),
        out_shape=jax.ShapeDtypeStruct((b, s // dilation, dilation * out_w), out_dt),
        scratch_shapes=[pltpu.VMEM((HEADS_PER_MXU_PASS * kw, hw), BF16),
                        pltpu.VMEM((HEADS_PER_MXU_PASS * kw, hw), BF16)],
        compiler_params=_cparams(("parallel", "parallel", "arbitrary")),
        name=f"attn_d{dilation}",
    )(*args)
    return out.reshape(b, s, out_w)


CONV_HALO = 32
CONV_ROWS = 32


def _conv_kernel(halo_ref, cur_ref, w_ref, b_ref, g_ref, beta_ref, out_ref, ext_ref):
    ts = cur_ref.shape[1]
    i = pl.program_id(1)
    ext_ref[0:CONV_HALO, :] = jnp.where(i > 0, halo_ref[0], 0.0)
    ext_ref[CONV_HALO:CONV_HALO + ts, :] = cur_ref[0]
    first = CONV_HALO - (CONV_K - 1)
    bias = b_ref[...]
    gamma = g_ref[...]
    beta = beta_ref[...]
    for r0 in range(0, ts, CONV_ROWS):
        acc = jnp.zeros((CONV_ROWS, CONV_WIDTH), F32) + bias
        for j in range(CONV_K):
            lo = first + j + r0
            acc = acc + w_ref[j:j + 1, :] * ext_ref[lo:lo + CONV_ROWS, :]
        mu = jnp.mean(acc, axis=-1, keepdims=True)
        cen = acc - mu
        var = jnp.mean(cen * cen, axis=-1, keepdims=True)
        y = cen * lax.rsqrt(var + EPS) * gamma + beta
        out_ref[0, r0:r0 + CONV_ROWS, :] = (y * _sigmoid(y)).astype(out_ref.dtype)


def _conv(u, w_dw, b_dw, ln_g, ln_b, ts):
    b, s, c = u.shape
    per = ts // CONV_HALO
    const = lambda bi, i: (0, 0)
    return pl.pallas_call(
        _conv_kernel,
        grid=(b, s // ts),
        in_specs=[
            pl.BlockSpec((1, CONV_HALO, c), lambda bi, i: (bi, jnp.maximum(i * per - 1, 0), 0)),
            pl.BlockSpec((1, ts, c), lambda bi, i: (bi, i, 0)),
            pl.BlockSpec((CONV_K, c), const),
            pl.BlockSpec((1, c), const),
            pl.BlockSpec((1, c), const),
            pl.BlockSpec((1, c), const),
        ],
        out_specs=pl.BlockSpec((1, ts, c), lambda bi, i: (bi, i, 0)),
        out_shape=jax.ShapeDtypeStruct((b, s, c), BF16),
        scratch_shapes=[pltpu.VMEM((CONV_HALO + ts, c), F32)],
        compiler_params=_cparams(("parallel", "arbitrary")),
        name="conv",
    )(u, u, w_dw, b_dw, ln_g, ln_b)


ROUTE_COLS = 4
EXPERT_LANE0 = N_GROUPS


def _merge_kernel(o_ref, c_ref, ga_ref, gb_ref, x_ref, wa_ref, wc_ref, bc_ref, wo_ref,
                  gf_ref, wrh_ref, wrl_ref, br_ref,
                  x1_ref, h2_ref, route_ref, gate_ref, cnt_ref, tri_ref):
    tm = x_ref.shape[0]

    @pl.when(pl.program_id(0) == 0)
    def _():
        r = lax.broadcasted_iota(jnp.int32, (tm, tm), 0)
        c = lax.broadcasted_iota(jnp.int32, (tm, tm), 1)
        tri_ref[...] = (c < r).astype(BF16)

    y_a = jnp.dot(o_ref[...], wa_ref[...], preferred_element_type=F32)
    y_b = jnp.dot(c_ref[...], wc_ref[...], preferred_element_type=F32) + bc_ref[...]
    merged = _sigmoid(ga_ref[...]) * y_a + _sigmoid(gb_ref[...]) * y_b
    x1 = x_ref[...] + jnp.dot(merged.astype(BF16), wo_ref[...], preferred_element_type=F32)
    x1_ref[...] = x1
    h2 = _rms(x1, gf_ref[...])
    h2_ref[...] = h2

    h_hi = h2.astype(BF16)
    h_lo = (h2 - h_hi.astype(F32)).astype(BF16)
    logits = (jnp.dot(h_hi, wrh_ref[...], preferred_element_type=F32)
              + jnp.dot(h_hi, wrl_ref[...], preferred_element_type=F32)
              + jnp.dot(h_lo, wrh_ref[...], preferred_element_type=F32)) + br_ref[...]

    lane = lax.broadcasted_iota(jnp.int32, (tm, LANES), 1)
    is_group = lane < N_GROUPS
    g_max = jnp.max(jnp.where(is_group, logits, -jnp.inf), axis=-1, keepdims=True)
    g_sel = jnp.min(jnp.where(jnp.logical_and(is_group, logits == g_max), lane, LANES),
                    axis=-1, keepdims=True)
    p_group = 1.0 / jnp.sum(jnp.where(is_group, jnp.exp(logits - g_max), 0.0),
                            axis=-1, keepdims=True)
    lo = EXPERT_LANE0 + EXPERTS_PER_GROUP * g_sel
    in_sel = jnp.logical_and(lane >= lo, lane < lo + EXPERTS_PER_GROUP)
    cand = jnp.where(in_sel, logits, -jnp.inf)
    v1 = jnp.max(cand, axis=-1, keepdims=True)
    i1 = jnp.min(jnp.where(jnp.logical_and(in_sel, cand == v1), lane, LANES), axis=-1, keepdims=True)
    rest = jnp.logical_and(in_sel, lane != i1)
    cand2 = jnp.where(rest, logits, -jnp.inf)
    v2 = jnp.max(cand2, axis=-1, keepdims=True)
    i2 = jnp.min(jnp.where(jnp.logical_and(rest, cand2 == v2), lane, LANES), axis=-1, keepdims=True)
    e2 = jnp.exp(v2 - v1)
    gate0 = p_group / (1.0 + e2)
    gate1 = p_group * e2 / (1.0 + e2)

    oh0 = (lane == i1)
    oh1 = (lane == i2)
    before0 = jnp.dot(tri_ref[...], oh0.astype(BF16), preferred_element_type=F32)
    before1 = jnp.dot(tri_ref[...], oh1.astype(BF16), preferred_element_type=F32)
    tot0 = jnp.sum(oh0.astype(F32), axis=0, keepdims=True)
    tot1 = jnp.sum(oh1.astype(F32), axis=0, keepdims=True)
    rank0 = jnp.sum(jnp.where(oh0, before0, 0.0), axis=-1, keepdims=True)
    rank1 = jnp.sum(jnp.where(oh1, before1 + tot0, 0.0), axis=-1, keepdims=True)

    rc = lax.broadcasted_iota(jnp.int32, (tm, ROUTE_COLS), 1)
    route = jnp.where(rc == 0, i1 - EXPERT_LANE0,
                      jnp.where(rc == 1, i2 - EXPERT_LANE0,
                                jnp.where(rc == 2, rank0.astype(jnp.int32), rank1.astype(jnp.int32))))
    route_ref[...] = route
    gc = lax.broadcasted_iota(jnp.int32, (tm, TOP_K), 1)
    gate_ref[...] = jnp.where(gc == 0, gate0, gate1)
    cnt = tot0 + tot1
    cnt_ref[0] = jnp.broadcast_to(pltpu.roll(cnt, LANES - EXPERT_LANE0, 1), (8, LANES))


def _merge(o, c, ga, gb, x, wa, wc, bc, wo, gf, wr_hi, wr_lo, br, tm):
    t = x.shape[0]
    nt = t // tm
    row = lambda i: (i, 0)
    const = lambda i: (0, 0)
    full = lambda a: pl.BlockSpec(a.shape, const)
    return pl.pallas_call(
        _merge_kernel,
        grid=(nt,),
        in_specs=[
            pl.BlockSpec((tm, ATTN_WIDTH), row),
            pl.BlockSpec((tm, CONV_WIDTH), row),
            pl.BlockSpec((tm, D_MODEL), row),
            pl.BlockSpec((tm, D_MODEL), row),
            pl.BlockSpec((tm, D_MODEL), row),
            full(wa), full(wc), full(bc), full(wo), full(gf), full(wr_hi), full(wr_lo), full(br),
        ],
        out_specs=(
            pl.BlockSpec((tm, D_MODEL), row),
            pl.BlockSpec((tm, D_MODEL), row),
            pl.BlockSpec((tm, ROUTE_COLS), row),
            pl.BlockSpec((tm, TOP_K), row),
            pl.BlockSpec((1, 8, LANES), lambda i: (i, 0, 0)),
        ),
        out_shape=(
            jax.ShapeDtypeStruct((t, D_MODEL), F32),
            jax.ShapeDtypeStruct((t, D_MODEL), F32),
            jax.ShapeDtypeStruct((t, ROUTE_COLS), jnp.int32),
            jax.ShapeDtypeStruct((t, TOP_K), F32),
            jax.ShapeDtypeStruct((nt, 8, LANES), F32),
        ),
        scratch_shapes=[pltpu.VMEM((tm, tm), BF16)],
        compiler_params=_cparams(("arbitrary",)),
        name="merge",
    )(o, c, ga, gb, x, wa, wc, bc, wo, gf, wr_hi, wr_lo, br)


def _scatter_kernel(base_ref, route_ref, h_ref, xs_in_ref, xs_ref, dest_ref, sem):
    del xs_in_ref
    tm = h_ref.shape[0]
    i = pl.program_id(0)

    def row_copy(t, d):
        return pltpu.make_async_copy(h_ref.at[pl.ds(t, 1)], xs_ref.at[pl.ds(d, 1)], sem)

    def issue(t, carry):
        for k in range(TOP_K):
            e = route_ref[ROUTE_COLS * t + k]
            d = base_ref[i * N_EXPERTS + e] + route_ref[ROUTE_COLS * t + TOP_K + k]
            dest_ref[TOP_K * t + k] = d
            row_copy(t, d).start()
        return carry

    lax.fori_loop(0, tm, issue, 0)

    def drain(t, carry):
        for k in range(TOP_K):
            row_copy(t, 0).wait()
        return carry

    lax.fori_loop(0, tm, drain, 0)


def _scatter(tile_base, route_flat, h2, n_rows, tm):
    t = h2.shape[0]
    xs0 = jnp.zeros((n_rows, D_MODEL), F32)
    grid_spec = pltpu.PrefetchScalarGridSpec(
        num_scalar_prefetch=1,
        grid=(t // tm,),
        in_specs=[
            pl.BlockSpec((ROUTE_COLS * tm,), lambda i, base: (i,), memory_space=pltpu.SMEM),
            pl.BlockSpec((tm, D_MODEL), lambda i, base: (i, 0)),
            pl.BlockSpec(memory_space=pl.ANY),
        ],
        out_specs=(
            pl.BlockSpec(memory_space=pl.ANY),
            pl.BlockSpec((TOP_K * tm,), lambda i, base: (i,), memory_space=pltpu.SMEM),
        ),
        scratch_shapes=[pltpu.SemaphoreType.DMA(())],
    )
    return pl.pallas_call(
        _scatter_kernel,
        grid_spec=grid_spec,
        out_shape=(jax.ShapeDtypeStruct((n_rows, D_MODEL), F32),
                   jax.ShapeDtypeStruct((TOP_K * t,), jnp.int32)),
        input_output_aliases={3: 0},
        compiler_params=_cparams(("arbitrary",)),
        name="scatter",
    )(tile_base, route_flat, h2, xs0)


def _moe_kernel(be_ref, nu_ref, xs_ref, wg_ref, wu_ref, wd_ref, ys_ref, wg_s, wu_s, wd_s):
    b = pl.program_id(0)
    e = be_ref[b]
    e_prev = be_ref[jnp.maximum(b - 1, 0)]

    @pl.when(jnp.logical_or(b == 0, e != e_prev))
    def _():
        wg_s[...] = wg_ref[0].astype(BF16)
        wu_s[...] = wu_ref[0].astype(BF16)
        wd_s[...] = wd_ref[0].astype(BF16)

    @pl.when(b < nu_ref[0])
    def _():
        x = xs_ref[...].astype(BF16)
        hg = jnp.dot(x, wg_s[...], preferred_element_type=F32)
        hu = jnp.dot(x, wu_s[...], preferred_element_type=F32)
        hb = (hg * _sigmoid(hg) * hu).astype(BF16)
        ys_ref[...] = jnp.dot(hb, wd_s[...], preferred_element_type=F32)

    @pl.when(b >= nu_ref[0])
    def _():
        ys_ref[...] = jnp.zeros_like(ys_ref)


def _moe(block_e, n_used, xs, w_gate, w_up, w_down):
    n_rows = xs.shape[0]
    grid_spec = pltpu.PrefetchScalarGridSpec(
        num_scalar_prefetch=2,
        grid=(n_rows // MOE_BLOCK,),
        in_specs=[
            pl.BlockSpec((MOE_BLOCK, D_MODEL), lambda b, be, nu: (b, 0)),
            pl.BlockSpec((1, D_MODEL, EXPERT_FF), lambda b, be, nu: (be[b], 0, 0)),
            pl.BlockSpec((1, D_MODEL, EXPERT_FF), lambda b, be, nu: (be[b], 0, 0)),
            pl.BlockSpec((1, EXPERT_FF, D_MODEL), lambda b, be, nu: (be[b], 0, 0)),
        ],
        out_specs=pl.BlockSpec((MOE_BLOCK, D_MODEL), lambda b, be, nu: (b, 0)),
        scratch_shapes=[pltpu.VMEM((D_MODEL, EXPERT_FF), BF16),
                        pltpu.VMEM((D_MODEL, EXPERT_FF), BF16),
                        pltpu.VMEM((EXPERT_FF, D_MODEL), BF16)],
    )
    return pl.pallas_call(
        _moe_kernel,
        grid_spec=grid_spec,
        out_shape=jax.ShapeDtypeStruct((n_rows, D_MODEL), F32),
        compiler_params=_cparams(("arbitrary",)),
        name="moe",
    )(block_e, n_used, xs, w_gate, w_up, w_down)


def _ple_kernel(dest_ref, gate_ref, x1_ref, p_ref, gp_ref, wpg_ref, wpe_ref, gfin_ref, ys_ref,
                out_ref, buf_ref, sem, *, final_norm):
    tm = x1_ref.shape[0]

    def row_copy(t, k, d):
        return pltpu.make_async_copy(ys_ref.at[pl.ds(d, 1)], buf_ref.at[k, pl.ds(t, 1)], sem)

    def issue(t, carry):
        for k in range(TOP_K):
            row_copy(t, k, dest_ref[TOP_K * t + k]).start()
        return carry

    lax.fori_loop(0, tm, issue, 0)

    def drain(t, carry):
        for k in range(TOP_K):
            row_copy(t, k, 0).wait()
        return carry

    lax.fori_loop(0, tm, drain, 0)

    gate = gate_ref[...]
    x2 = x1_ref[...] + gate[:, 0:1] * buf_ref[0] + gate[:, 1:2] * buf_ref[1]
    hp = _rms(x2, gp_ref[...]).astype(BF16)
    g_ple = _sigmoid(jnp.dot(hp, wpg_ref[...], preferred_element_type=F32))
    emb = jnp.dot(p_ref[...].astype(BF16), wpe_ref[...], preferred_element_type=F32)
    x3 = x2 + g_ple * emb
    if final_norm:
        x3 = _rms(x3, gfin_ref[...])
    out_ref[...] = x3


def _ple(dest, gate, x1, p, gp, wpg, wpe, gfin, ys, tm, final_norm):
    t = x1.shape[0]
    row = lambda i: (i, 0)
    const = lambda i: (0, 0)
    full = lambda a: pl.BlockSpec(a.shape, const)
    return pl.pallas_call(
        functools.partial(_ple_kernel, final_norm=final_norm),
        grid=(t // tm,),
        in_specs=[
            pl.BlockSpec((TOP_K * tm,), lambda i: (i,), memory_space=pltpu.SMEM),
            pl.BlockSpec((tm, TOP_K), row),
            pl.BlockSpec((tm, D_MODEL), row),
            pl.BlockSpec((tm, PLE_DIM), row),
            full(gp), full(wpg), full(wpe), full(gfin),
            pl.BlockSpec(memory_space=pl.ANY),
        ],
        out_specs=pl.BlockSpec((tm, D_MODEL), row),
        out_shape=jax.ShapeDtypeStruct((t, D_MODEL), F32),
        scratch_shapes=[pltpu.VMEM((TOP_K, tm, D_MODEL), F32), pltpu.SemaphoreType.DMA(())],
        compiler_params=_cparams(("arbitrary",)),
        name="ple",
    )(dest, gate, x1, p, gp, wpg, wpe, gfin, ys)


def _rope_tables(seq):
    inv = jnp.power(ROPE_THETA, -jnp.arange(0, HEAD_DIM, 2, dtype=F32) / HEAD_DIM)
    ang = jnp.arange(seq, dtype=F32)[:, None] * inv[None, :]
    cos, sin = jnp.cos(ang), jnp.sin(ang)
    cos_h = jnp.concatenate([cos, cos], axis=-1)
    sin_h = jnp.concatenate([-sin, sin], axis=-1)
    return jnp.tile(cos_h, (1, N_HEADS)), jnp.tile(sin_h, (1, N_HEADS))


def _dispatch_plan(counts, n_blocks):
    total = jnp.sum(counts, axis=0)
    padded = ((total + MOE_BLOCK - 1) // MOE_BLOCK) * MOE_BLOCK
    ends = jnp.cumsum(padded)
    pad_start = ends - padded
    tile_base = pad_start[None, :] + (jnp.cumsum(counts, axis=0) - counts)
    block_e = jnp.minimum(
        jnp.searchsorted(ends, jnp.arange(n_blocks, dtype=jnp.int32) * MOE_BLOCK, side="right"),
        N_EXPERTS - 1).astype(jnp.int32)
    n_used = (ends[-1] // MOE_BLOCK).astype(jnp.int32).reshape(1)
    return tile_base.reshape(-1).astype(jnp.int32), block_e, n_used


def kernel(x, p, norm_mix, w_in, w_dw, b_dw, ln_conv_g, ln_conv_b, w_attn_out, w_conv_out,
           b_conv_out, w_out, norm_ffn, w_route_group, b_route_group, w_route_expert,
           b_route_expert, w_exp_gate, w_exp_up, w_exp_down, norm_ple, w_ple_proj, w_ple_gate,
           norm_final):
    bsz, seq, d = x.shape
    depth = w_in.shape[0]
    t = bsz * seq
    tm = 512
    n_assign = t * TOP_K
    n_rows = (-(-n_assign // MOE_BLOCK)) * MOE_BLOCK + N_EXPERTS * MOE_BLOCK
    cos, sin = _rope_tables(seq)
    row2 = lambda a: a.reshape(1, -1)

    xf = x.reshape(t, d)
    for i in range(depth):
        q, k, v, u, ga, gb = _inproj(xf, row2(norm_mix[i]), w_in[i].astype(BF16), cos, sin, seq, tm)
        shp = (bsz, seq, ATTN_WIDTH)
        q, k, v = q.reshape(shp), k.reshape(shp), v.reshape(shp)
        state = None
        for dil in DILATIONS:
            state = _attn_sweep(q, k, v, state, dil, final=(dil == DILATIONS[-1]))
        o = state.reshape(t, ATTN_WIDTH)
        c = _conv(u.reshape(bsz, seq, CONV_WIDTH), w_dw[i], row2(b_dw[i]), row2(ln_conv_g[i]),
                  row2(ln_conv_b[i]), 256).reshape(t, CONV_WIDTH)

        w_r = jnp.zeros((d, LANES), F32)
        w_r = w_r.at[:, :N_GROUPS].set(w_route_group[i])
        w_r = w_r.at[:, EXPERT_LANE0:EXPERT_LANE0 + N_EXPERTS].set(w_route_expert[i])
        w_r_hi = w_r.astype(BF16)
        w_r_lo = (w_r - w_r_hi.astype(F32)).astype(BF16)
        b_r = jnp.zeros((1, LANES), F32)
        b_r = b_r.at[0, :N_GROUPS].set(b_route_group[i])
        b_r = b_r.at[0, EXPERT_LANE0:EXPERT_LANE0 + N_EXPERTS].set(b_route_expert[i])

        x1, h2, route, gate, cnt = _merge(
            o, c, ga, gb, xf, w_attn_out[i].astype(BF16), w_conv_out[i].astype(BF16),
            row2(b_conv_out[i]), w_out[i].astype(BF16), row2(norm_ffn[i]), w_r_hi, w_r_lo, b_r, tm)

        counts = cnt[:, 0, :N_EXPERTS].astype(jnp.int32)
        tile_base, block_e, n_used = _dispatch_plan(counts, n_rows // MOE_BLOCK)
        xs, dest = _scatter(tile_base, route.reshape(-1), h2, n_rows, tm)
        ys = _moe(block_e, n_used, xs, w_exp_gate[i], w_exp_up[i], w_exp_down[i])
        xf = _ple(dest, gate, x1, p[i].reshape(t, PLE_DIM), row2(norm_ple[i]),
                  w_ple_gate[i].astype(BF16), w_ple_proj[i].astype(BF16), row2(norm_final),
                  ys, 256, final_norm=(i == depth - 1))
    return xf.reshape(bsz, seq, d)
```

```python
import functools

import jax
import jax.numpy as jnp
from jax import lax
from jax.experimental import pallas as pl
from jax.experimental.pallas import tpu as pltpu

D_MODEL = 1024
N_HEADS = 8
HEAD_DIM = 64
ATTN_WIDTH = N_HEADS * HEAD_DIM
CONV_WIDTH = 512
CONV_K = 31
DILATIONS = (16, 4, 1)
BLK = 128
ROPE_THETA = 10000.0
N_GROUPS = 4
EXPERTS_PER_GROUP = 8
N_EXPERTS = N_GROUPS * EXPERTS_PER_GROUP
EXPERT_FF = 512
TOP_K = 2
MOE_BLOCK = 128
PLE_DIM = 256
EPS = 1e-6
NEG_INF = -1e30

LANES = 128
ROW_TILE = D_MODEL // LANES
HEADS_PER_MXU_PASS = 4
STATE_W = ATTN_WIDTH + LANES
VMEM_LIMIT = 56 * 1024 * 1024

F32 = jnp.float32
BF16 = jnp.bfloat16


def _cparams(sem):
    return pltpu.CompilerParams(dimension_semantics=sem, vmem_limit_bytes=VMEM_LIMIT)


def _rms(x, g):
    return x * lax.rsqrt(jnp.mean(x * x, axis=-1, keepdims=True) + EPS) * g


def _sigmoid(x):
    return 1.0 / (1.0 + jnp.exp(-x))


def _store_row_tiles(ref, x):
    rows = x.shape[0]
    for s in range(ROW_TILE):
        ref[pl.ds(s, rows, stride=ROW_TILE), :] = x[:, s * LANES:(s + 1) * LANES]


def _load_row_tiles(ref, rows):
    return jnp.concatenate([ref[pl.ds(s, rows, stride=ROW_TILE), :] for s in range(ROW_TILE)], axis=1)


def _inproj_kernel(x_ref, g_ref, w_ref, cos_ref, sin_ref,
                   q_ref, k_ref, v_ref, u_ref, ga_ref, gb_ref):
    tm = x_ref.shape[0]
    h = _rms(x_ref[...], g_ref[...]).astype(BF16)

    def proj(c0, width):
        return jnp.dot(h, w_ref[:, c0:c0 + width], preferred_element_type=F32)

    cos = cos_ref[...]
    sin = sin_ref[...]
    lane = lax.broadcasted_iota(jnp.int32, (tm, ATTN_WIDTH), 1)
    first_half = (lane % HEAD_DIM) < (HEAD_DIM // 2)

    def rope(t):
        partner = jnp.where(first_half,
                            pltpu.roll(t, ATTN_WIDTH - HEAD_DIM // 2, 1),
                            pltpu.roll(t, HEAD_DIM // 2, 1))
        return t * cos + partner * sin

    q_ref[...] = rope(proj(0, ATTN_WIDTH)).astype(BF16)
    k_ref[...] = rope(proj(ATTN_WIDTH, ATTN_WIDTH)).astype(BF16)
    v_ref[...] = proj(2 * ATTN_WIDTH, ATTN_WIDTH).astype(BF16)
    c0 = 3 * ATTN_WIDTH
    cu = proj(c0, CONV_WIDTH)
    cg = proj(c0 + CONV_WIDTH, CONV_WIDTH)
    u_ref[...] = cu * _sigmoid(cg)
    c0 += 2 * CONV_WIDTH
    ga_ref[...] = proj(c0, D_MODEL)
    gb_ref[...] = proj(c0 + D_MODEL, D_MODEL)


def _inproj(x, gain, w_in, cos, sin, seq, tm):
    t = x.shape[0]
    n_pos = seq // tm
    row = lambda i: (i, 0)
    const = lambda i: (0, 0)
    out_shape = (
        jax.ShapeDtypeStruct((t, ATTN_WIDTH), BF16),
        jax.ShapeDtypeStruct((t, ATTN_WIDTH), BF16),
        jax.ShapeDtypeStruct((t, ATTN_WIDTH), BF16),
        jax.ShapeDtypeStruct((t, CONV_WIDTH), F32),
        jax.ShapeDtypeStruct((t, D_MODEL), F32),
        jax.ShapeDtypeStruct((t, D_MODEL), F32),
    )
    return pl.pallas_call(
        _inproj_kernel,
        grid=(t // tm,),
        in_specs=[
            pl.BlockSpec((tm, D_MODEL), row),
            pl.BlockSpec((1, D_MODEL), const),
            pl.BlockSpec(w_in.shape, const),
            pl.BlockSpec((tm, ATTN_WIDTH), lambda i: (i % n_pos, 0)),
            pl.BlockSpec((tm, ATTN_WIDTH), lambda i: (i % n_pos, 0)),
        ],
        out_specs=(
            pl.BlockSpec((tm, ATTN_WIDTH), row),
            pl.BlockSpec((tm, ATTN_WIDTH), row),
            pl.BlockSpec((tm, ATTN_WIDTH), row),
            pl.BlockSpec((tm, CONV_WIDTH), row),
            pl.BlockSpec((tm, D_MODEL), row),
            pl.BlockSpec((tm, D_MODEL), row),
        ),
        out_shape=out_shape,
        compiler_params=_cparams(("parallel",)),
        name="inproj",
    )(x, gain, w_in, cos, sin)


def _attn_kernel(*refs, has_prev, has_state, final):
    it = iter(refs)
    q_ref = next(it)
    kc_ref = next(it)
    vc_ref = next(it)
    kp_ref = next(it) if has_prev else None
    vp_ref = next(it) if has_prev else None
    st_ref = next(it) if has_state else None
    out_ref = next(it)
    kbd_ref = next(it)
    vbd_ref = next(it)

    n = pl.program_id(2)
    hw = HEADS_PER_MXU_PASS * HEAD_DIM
    kw = 2 * BLK if has_prev else BLK

    row = lax.broadcasted_iota(jnp.int32, (BLK, BLK), 0)
    col = lax.broadcasted_iota(jnp.int32, (BLK, BLK), 1)
    cur_ok = col <= row
    if has_prev:
        prev_ok = jnp.logical_and(col >= row, n > 0)
        allowed = jnp.concatenate([prev_ok, cur_ok], axis=1)
    else:
        allowed = cur_ok

    key_head = lax.broadcasted_iota(jnp.int32, (BLK, hw), 1) // HEAD_DIM
    lane_head = lax.broadcasted_iota(jnp.int32, (BLK, hw), 1) // HEAD_DIM
    stat_lane = lax.broadcasted_iota(jnp.int32, (BLK, LANES), 1)

    q = q_ref[0]
    if has_state:
        st = st_ref[0]
        old_stat = st[:, ATTN_WIDTH:]
    new_stat = jnp.zeros((BLK, LANES), F32)

    for g in range(N_HEADS // HEADS_PER_MXU_PASS):
        ls = slice(g * hw, (g + 1) * hw)
        for hh in range(HEADS_PER_MXU_PASS):
            keep = key_head == hh
            base = hh * kw
            if has_prev:
                kbd_ref[base:base + BLK, :] = jnp.where(keep, kp_ref[0][:, ls], 0)
                vbd_ref[base:base + BLK, :] = jnp.where(keep, vp_ref[0][:, ls], 0)
                base += BLK
            kbd_ref[base:base + BLK, :] = jnp.where(keep, kc_ref[0][:, ls], 0)
            vbd_ref[base:base + BLK, :] = jnp.where(keep, vc_ref[0][:, ls], 0)

        s_all = lax.dot_general(q[:, ls], kbd_ref[...], (((1,), (1,)), ((), ())),
                                preferred_element_type=F32)
        p_parts = []
        m_heads = []
        l_heads = []
        for hh in range(HEADS_PER_MXU_PASS):
            s = s_all[:, hh * kw:(hh + 1) * kw] * (HEAD_DIM ** -0.5)
            s = jnp.where(allowed, s, NEG_INF)
            m_new = jnp.max(s, axis=-1, keepdims=True)
            p = jnp.exp(s - m_new)
            l_heads.append(jnp.sum(p, axis=-1, keepdims=True))
            m_heads.append(m_new)
            p_parts.append(p.astype(BF16))
        acc = jnp.dot(jnp.concatenate(p_parts, axis=1), vbd_ref[...],
                      preferred_element_type=F32)

        if has_state:
            w_old = jnp.zeros((BLK, hw), F32)
            w_new = jnp.zeros((BLK, hw), F32)
        l_lanes = jnp.zeros((BLK, hw), F32)
        for hh in range(HEADS_PER_MXU_PASS):
            h = g * HEADS_PER_MXU_PASS + hh
            m_h, l_h = m_heads[hh], l_heads[hh]
            if has_state:
                m_old = old_stat[:, h:h + 1]
                l_old = old_stat[:, N_HEADS + h:N_HEADS + h + 1]
                m_tot = jnp.maximum(m_old, m_h)
                a_old = jnp.exp(m_old - m_tot)
                a_new = jnp.exp(m_h - m_tot)
                l_h = l_old * a_old + l_h * a_new
                m_h = m_tot
                w_old = jnp.where(lane_head == hh, a_old, w_old)
                w_new = jnp.where(lane_head == hh, a_new, w_new)
            l_lanes = jnp.where(lane_head == hh, l_h, l_lanes)
            new_stat = jnp.where(stat_lane == h, m_h, new_stat)
            new_stat = jnp.where(stat_lane == N_HEADS + h, l_h, new_stat)
        if has_state:
            acc = st[:, ls] * w_old + acc * w_new
        if final:
            out_ref[0, :, ls] = (acc / l_lanes).astype(out_ref.dtype)
        else:
            out_ref[0, :, ls] = acc
    if not final:
        out_ref[0, :, ATTN_WIDTH:] = new_stat


def _attn_sweep(q, k, v, state, dilation, final):
    b, s, _ = q.shape
    seg = dilation * BLK
    assert s % seg == 0
    nb = s // seg
    has_prev = nb > 1
    has_state = state is not None
    view = lambda a: a.reshape(b, s // dilation, dilation * a.shape[-1])
    cur = lambda bi, r, n: (bi, n, r)
    prev = lambda bi, r, n: (bi, jnp.maximum(n - 1, 0), r)
    blk = lambda w: pl.BlockSpec((1, BLK, w), cur)
    args = [view(q), view(k), view(v)]
    in_specs = [blk(ATTN_WIDTH), blk(ATTN_WIDTH), blk(ATTN_WIDTH)]
    if has_prev:
        args += [view(k), view(v)]
        in_specs += [pl.BlockSpec((1, BLK, ATTN_WIDTH), prev)] * 2
    if has_state:
        args.append(view(state))
        in_specs.append(blk(STATE_W))
    if final:
        out_w, out_dt = ATTN_WIDTH, BF16
    else:
        out_w, out_dt = STATE_W, F32
    kw = 2 * BLK if has_prev else BLK
    hw = HEADS_PER_MXU_PASS * HEAD_DIM
    out = pl.pallas_call(
        functools.partial(_attn_kernel, has_prev=has_prev, has_state=has_state, final=final),
        grid=(b, dilation, nb),
        in_specs=in_specs,
        out_specs=blk(out_w),
        out_shape=jax.ShapeDtypeStruct((b, s // dilation, dilation * out_w), out_dt),
        scratch_shapes=[pltpu.VMEM((HEADS_PER_MXU_PASS * kw, hw), BF16),
                        pltpu.VMEM((HEADS_PER_MXU_PASS * kw, hw), BF16)],
        compiler_params=_cparams(("parallel", "parallel", "arbitrary")),
        name=f"attn_d{dilation}",
    )(*args)
    return out.reshape(b, s, out_w)


CONV_HALO = 32
CONV_ROWS = 32


def _conv_kernel(halo_ref, cur_ref, w_ref, b_ref, g_ref, beta_ref, out_ref, ext_ref):
    ts = cur_ref.shape[1]
    i = pl.program_id(1)
    ext_ref[0:CONV_HALO, :] = jnp.where(i > 0, halo_ref[0], 0.0)
    ext_ref[CONV_HALO:CONV_HALO + ts, :] = cur_ref[0]
    first = CONV_HALO - (CONV_K - 1)
    bias = b_ref[...]
    gamma = g_ref[...]
    beta = beta_ref[...]
    for r0 in range(0, ts, CONV_ROWS):
        acc = jnp.zeros((CONV_ROWS, CONV_WIDTH), F32) + bias
        for j in range(CONV_K):
            lo = first + j + r0
            acc = acc + w_ref[j:j + 1, :] * ext_ref[lo:lo + CONV_ROWS, :]
        mu = jnp.mean(acc, axis=-1, keepdims=True)
        cen = acc - mu
        var = jnp.mean(cen * cen, axis=-1, keepdims=True)
        y = cen * lax.rsqrt(var + EPS) * gamma + beta
        out_ref[0, r0:r0 + CONV_ROWS, :] = (y * _sigmoid(y)).astype(out_ref.dtype)


def _conv(u, w_dw, b_dw, ln_g, ln_b, ts):
    b, s, c = u.shape
    per = ts // CONV_HALO
    const = lambda bi, i: (0, 0)
    return pl.pallas_call(
        _conv_kernel,
        grid=(b, s // ts),
        in_specs=[
            pl.BlockSpec((1, CONV_HALO, c), lambda bi, i: (bi, jnp.maximum(i * per - 1, 0), 0)),
            pl.BlockSpec((1, ts, c), lambda bi, i: (bi, i, 0)),
            pl.BlockSpec((CONV_K, c), const),
            pl.BlockSpec((1, c), const),
            pl.BlockSpec((1, c), const),
            pl.BlockSpec((1, c), const),
        ],
        out_specs=pl.BlockSpec((1, ts, c), lambda bi, i: (bi, i, 0)),
        out_shape=jax.ShapeDtypeStruct((b, s, c), BF16),
        scratch_shapes=[pltpu.VMEM((CONV_HALO + ts, c), F32)],
        compiler_params=_cparams(("parallel", "arbitrary")),
        name="conv",
    )(u, u, w_dw, b_dw, ln_g, ln_b)


ROUTE_COLS = 4
EXPERT_LANE0 = N_GROUPS


def _merge_kernel(o_ref, c_ref, ga_ref, gb_ref, x_ref, wa_ref, wc_ref, bc_ref, wo_ref,
                  gf_ref, wrh_ref, wrl_ref, br_ref,
                  x1_ref, h2_ref, route_ref, gate_ref, cnt_ref, tri_ref):
    tm = x_ref.shape[0]

    @pl.when(pl.program_id(0) == 0)
    def _():
        r = lax.broadcasted_iota(jnp.int32, (tm, tm), 0)
        c = lax.broadcasted_iota(jnp.int32, (tm, tm), 1)
        tri_ref[...] = (c < r).astype(BF16)

    y_a = jnp.dot(o_ref[...], wa_ref[...], preferred_element_type=F32)
    y_b = jnp.dot(c_ref[...], wc_ref[...], preferred_element_type=F32) + bc_ref[...]
    merged = _sigmoid(ga_ref[...]) * y_a + _sigmoid(gb_ref[...]) * y_b
    x1 = x_ref[...] + jnp.dot(merged.astype(BF16), wo_ref[...], preferred_element_type=F32)
    x1_ref[...] = x1
    h2 = _rms(x1, gf_ref[...])
    _store_row_tiles(h2_ref, h2)

    h_hi = h2.astype(BF16)
    h_lo = (h2 - h_hi.astype(F32)).astype(BF16)
    logits = (jnp.dot(h_hi, wrh_ref[...], preferred_element_type=F32)
              + jnp.dot(h_hi, wrl_ref[...], preferred_element_type=F32)
              + jnp.dot(h_lo, wrh_ref[...], preferred_element_type=F32)) + br_ref[...]

    lane = lax.broadcasted_iota(jnp.int32, (tm, LANES), 1)
    is_group = lane < N_GROUPS
    g_max = jnp.max(jnp.where(is_group, logits, -jnp.inf), axis=-1, keepdims=True)
    g_sel = jnp.min(jnp.where(jnp.logical_and(is_group, logits == g_max), lane, LANES),
                    axis=-1, keepdims=True)
    p_group = 1.0 / jnp.sum(jnp.where(is_group, jnp.exp(logits - g_max), 0.0),
                            axis=-1, keepdims=True)
    lo = EXPERT_LANE0 + EXPERTS_PER_GROUP * g_sel
    in_sel = jnp.logical_and(lane >= lo, lane < lo + EXPERTS_PER_GROUP)
    cand = jnp.where(in_sel, logits, -jnp.inf)
    v1 = jnp.max(cand, axis=-1, keepdims=True)
    i1 = jnp.min(jnp.where(jnp.logical_and(in_sel, cand == v1), lane, LANES), axis=-1, keepdims=True)
    rest = jnp.logical_and(in_sel, lane != i1)
    cand2 = jnp.where(rest, logits, -jnp.inf)
    v2 = jnp.max(cand2, axis=-1, keepdims=True)
    i2 = jnp.min(jnp.where(jnp.logical_and(rest, cand2 == v2), lane, LANES), axis=-1, keepdims=True)
    e2 = jnp.exp(v2 - v1)
    gate0 = p_group / (1.0 + e2)
    gate1 = p_group * e2 / (1.0 + e2)

    oh0 = (lane == i1)
    oh1 = (lane == i2)
    before0 = jnp.dot(tri_ref[...], oh0.astype(BF16), preferred_element_type=F32)
    before1 = jnp.dot(tri_ref[...], oh1.astype(BF16), preferred_element_type=F32)
    tot0 = jnp.sum(oh0.astype(F32), axis=0, keepdims=True)
    tot1 = jnp.sum(oh1.astype(F32), axis=0, keepdims=True)
    rank0 = jnp.sum(jnp.where(oh0, before0, 0.0), axis=-1, keepdims=True)
    rank1 = jnp.sum(jnp.where(oh1, before1 + tot0, 0.0), axis=-1, keepdims=True)

    rc = lax.broadcasted_iota(jnp.int32, (tm, ROUTE_COLS), 1)
    route = jnp.where(rc == 0, i1 - EXPERT_LANE0,
                      jnp.where(rc == 1, i2 - EXPERT_LANE0,
                                jnp.where(rc == 2, rank0.astype(jnp.int32), rank1.astype(jnp.int32))))
    route_ref[...] = route
    gc = lax.broadcasted_iota(jnp.int32, (tm, TOP_K), 1)
    gate_ref[...] = jnp.where(gc == 0, gate0, gate1)
    cnt = tot0 + tot1
    cnt_ref[0] = jnp.broadcast_to(pltpu.roll(cnt, LANES - EXPERT_LANE0, 1), (8, LANES))


def _merge(o, c, ga, gb, x, wa, wc, bc, wo, gf, wr_hi, wr_lo, br, tm):
    t = x.shape[0]
    nt = t // tm
    row = lambda i: (i, 0)
    const = lambda i: (0, 0)
    full = lambda a: pl.BlockSpec(a.shape, const)
    return pl.pallas_call(
        _merge_kernel,
        grid=(nt,),
        in_specs=[
            pl.BlockSpec((tm, ATTN_WIDTH), row),
            pl.BlockSpec((tm, CONV_WIDTH), row),
            pl.BlockSpec((tm, D_MODEL), row),
            pl.BlockSpec((tm, D_MODEL), row),
            pl.BlockSpec((tm, D_MODEL), row),
            full(wa), full(wc), full(bc), full(wo), full(gf), full(wr_hi), full(wr_lo), full(br),
        ],
        out_specs=(
            pl.BlockSpec((tm, D_MODEL), row),
            pl.BlockSpec((tm * ROW_TILE, LANES), row),
            pl.BlockSpec((tm, ROUTE_COLS), row),
            pl.BlockSpec((tm, TOP_K), row),
            pl.BlockSpec((1, 8, LANES), lambda i: (i, 0, 0)),
        ),
        out_shape=(
            jax.ShapeDtypeStruct((t, D_MODEL), F32),
            jax.ShapeDtypeStruct((t * ROW_TILE, LANES), F32),
            jax.ShapeDtypeStruct((t, ROUTE_COLS), jnp.int32),
            jax.ShapeDtypeStruct((t, TOP_K), F32),
            jax.ShapeDtypeStruct((nt, 8, LANES), F32),
        ),
        scratch_shapes=[pltpu.VMEM((tm, tm), BF16)],
        compiler_params=_cparams(("arbitrary",)),
        name="merge",
    )(o, c, ga, gb, x, wa, wc, bc, wo, gf, wr_hi, wr_lo, br)


NO_ASSIGNMENT = -1


def _inv_kernel(base_ref, route_ref, inv_ref):
    tm = route_ref.shape[0] // ROUTE_COLS
    i = pl.program_id(0)

    @pl.when(i == 0)
    def _():
        def fill(j, carry):
            inv_ref[j] = NO_ASSIGNMENT
            return carry
        lax.fori_loop(0, inv_ref.shape[0], fill, 0, unroll=8)

    def place(t, carry):
        for k in range(TOP_K):
            e = route_ref[ROUTE_COLS * t + k]
            row = base_ref[i * N_EXPERTS + e] + route_ref[ROUTE_COLS * t + TOP_K + k]
            inv_ref[row] = TOP_K * (i * tm + t) + k
        return carry

    lax.fori_loop(0, tm, place, 0, unroll=4)


def _inverse_map(tile_base, route_flat, n_rows, tm):
    n_tok = route_flat.shape[0] // ROUTE_COLS
    grid_spec = pltpu.PrefetchScalarGridSpec(
        num_scalar_prefetch=1,
        grid=(n_tok // tm,),
        in_specs=[pl.BlockSpec((ROUTE_COLS * tm,), lambda i, base: (i,), memory_space=pltpu.SMEM)],
        out_specs=pl.BlockSpec(memory_space=pltpu.SMEM),
    )
    return pl.pallas_call(
        _inv_kernel,
        grid_spec=grid_spec,
        out_shape=jax.ShapeDtypeStruct((n_rows,), jnp.int32),
        compiler_params=_cparams(("arbitrary",)),
        name="inverse_map",
    )(tile_base, route_flat)


TRASH_ROWS = 2 * MOE_BLOCK


def _moe_kernel(be_ref, inv_prev, inv_cur, inv_next, h2_hbm, wg_ref, wu_ref, wd_ref, y2_hbm,
                xbuf, obuf, wg_s, wu_s, wd_s, sem_in, sem_out, *, n_tok):
    b = pl.program_id(0)
    last = pl.num_programs(0) - 1
    slot = b % 2
    other = 1 - slot
    trash0 = TOP_K * n_tok

    def issue_gather(inv, s):
        for j in range(MOE_BLOCK):
            tok = jnp.maximum(inv[j], 0) >> 1
            src = pl.multiple_of(tok * ROW_TILE, ROW_TILE)
            pltpu.make_async_copy(h2_hbm.at[pl.ds(src, ROW_TILE)],
                                  xbuf.at[s, pl.ds(j * ROW_TILE, ROW_TILE)], sem_in.at[s]).start()

    def issue_scatter(inv, s, live):
        for j in range(MOE_BLOCK):
            a = inv[j]
            valid = jnp.logical_and(a >= 0, live)
            dst = jnp.where(valid, (a & 1) * n_tok + (a >> 1), trash0 + MOE_BLOCK * s + j)
            dst = pl.multiple_of(dst * ROW_TILE, ROW_TILE)
            pltpu.make_async_copy(obuf.at[s, pl.ds(j * ROW_TILE, ROW_TILE)],
                                  y2_hbm.at[pl.ds(dst, ROW_TILE)], sem_out.at[s]).start()

    def wait_gather(s):
        pltpu.make_async_copy(h2_hbm.at[pl.ds(0, MOE_BLOCK * ROW_TILE)], xbuf.at[s], sem_in.at[s]).wait()

    def wait_scatter(s):
        pltpu.make_async_copy(obuf.at[s], y2_hbm.at[pl.ds(0, MOE_BLOCK * ROW_TILE)], sem_out.at[s]).wait()

    @pl.when(b == 0)
    def _():
        obuf[...] = jnp.zeros_like(obuf)
        issue_scatter(inv_cur, 0, False)
        wait_scatter(0)
        issue_gather(inv_cur, 0)

    wait_gather(slot)

    @pl.when(b >= 1)
    def _():
        wait_scatter(slot)

    e = be_ref[b]
    e_prev = be_ref[jnp.maximum(b - 1, 0)]

    @pl.when(jnp.logical_or(b == 0, e != e_prev))
    def _():
        wg_s[...] = wg_ref[0].astype(BF16)
        wu_s[...] = wu_ref[0].astype(BF16)
        wd_s[...] = wd_ref[0].astype(BF16)

    issue_gather(inv_next, other)
    issue_scatter(inv_prev, other, b > 0)
    x = _load_row_tiles(xbuf.at[slot], MOE_BLOCK).astype(BF16)
    hg = jnp.dot(x, wg_s[...], preferred_element_type=F32)
    hu = jnp.dot(x, wu_s[...], preferred_element_type=F32)
    hb = (hg * _sigmoid(hg) * hu).astype(BF16)
    _store_row_tiles(obuf.at[slot], jnp.dot(hb, wd_s[...], preferred_element_type=F32))

    @pl.when(b == last)
    def _():
        issue_scatter(inv_cur, slot, True)
        wait_scatter(other)
        wait_scatter(slot)
        wait_gather(other)


def _moe(block_e, inv, h2, w_gate, w_up, w_down, layer):
    n_tok = h2.shape[0] // ROW_TILE
    n_blocks = inv.shape[0] // MOE_BLOCK
    w_idx = lambda b, be: (layer, be[b], 0, 0)
    inv_spec = lambda f: pl.BlockSpec((MOE_BLOCK,), f, memory_space=pltpu.SMEM)
    grid_spec = pltpu.PrefetchScalarGridSpec(
        num_scalar_prefetch=1,
        grid=(n_blocks,),
        in_specs=[
            inv_spec(lambda b, be: (jnp.maximum(b - 1, 0),)),
            inv_spec(lambda b, be: (b,)),
            inv_spec(lambda b, be: (jnp.minimum(b + 1, n_blocks - 1),)),
            pl.BlockSpec(memory_space=pl.ANY),
            pl.BlockSpec((None, 1, D_MODEL, EXPERT_FF), w_idx),
            pl.BlockSpec((None, 1, D_MODEL, EXPERT_FF), w_idx),
            pl.BlockSpec((None, 1, EXPERT_FF, D_MODEL), w_idx),
        ],
        out_specs=pl.BlockSpec(memory_space=pl.ANY),
        scratch_shapes=[pltpu.VMEM((2, MOE_BLOCK * ROW_TILE, LANES), F32),
                        pltpu.VMEM((2, MOE_BLOCK * ROW_TILE, LANES), F32),
                        pltpu.VMEM((D_MODEL, EXPERT_FF), BF16),
                        pltpu.VMEM((D_MODEL, EXPERT_FF), BF16),
                        pltpu.VMEM((EXPERT_FF, D_MODEL), BF16),
                        pltpu.SemaphoreType.DMA((2,)),
                        pltpu.SemaphoreType.DMA((2,))],
    )
    return pl.pallas_call(
        functools.partial(_moe_kernel, n_tok=n_tok),
        grid_spec=grid_spec,
        out_shape=jax.ShapeDtypeStruct(((TOP_K * n_tok + TRASH_ROWS) * ROW_TILE, LANES), F32),
        compiler_params=_cparams(("arbitrary",)),
        name="moe",
    )(block_e, inv, inv, inv, h2, w_gate, w_up, w_down)


def _ple_kernel(gate_ref, y0_ref, y1_ref, x1_ref, p_ref, gp_ref, wpg_ref, wpe_ref, gfin_ref,
                out_ref, *, final_norm):
    gate = gate_ref[...]
    tm = x1_ref.shape[0]
    x2 = (x1_ref[...] + gate[:, 0:1] * _load_row_tiles(y0_ref, tm)
          + gate[:, 1:2] * _load_row_tiles(y1_ref, tm))
    hp = _rms(x2, gp_ref[...]).astype(BF16)
    g_ple = _sigmoid(jnp.dot(hp, wpg_ref[...], preferred_element_type=F32))
    emb = jnp.dot(p_ref[...].astype(BF16), wpe_ref[...], preferred_element_type=F32)
    x3 = x2 + g_ple * emb
    if final_norm:
        x3 = _rms(x3, gfin_ref[...])
    out_ref[...] = x3


def _ple(gate, y2, x1, p, gp, wpg, wpe, gfin, tm, final_norm):
    t = x1.shape[0]
    nt = t // tm
    row = lambda i: (i, 0)
    const = lambda i: (0, 0)
    full = lambda a: pl.BlockSpec(a.shape, const)
    return pl.pallas_call(
        functools.partial(_ple_kernel, final_norm=final_norm),
        grid=(t // tm,),
        in_specs=[
            pl.BlockSpec((tm, TOP_K), row),
            pl.BlockSpec((tm * ROW_TILE, LANES), lambda i: (i, 0)),
            pl.BlockSpec((tm * ROW_TILE, LANES), lambda i: (nt + i, 0)),
            pl.BlockSpec((tm, D_MODEL), row),
            pl.BlockSpec((tm, PLE_DIM), row),
            full(gp), full(wpg), full(wpe), full(gfin),
        ],
        out_specs=pl.BlockSpec((tm, D_MODEL), row),
        out_shape=jax.ShapeDtypeStruct((t, D_MODEL), F32),
        compiler_params=_cparams(("parallel",)),
        name="ple",
    )(gate, y2, y2, x1, p, gp, wpg, wpe, gfin)


def _rope_tables(seq):
    inv = jnp.power(ROPE_THETA, -jnp.arange(0, HEAD_DIM, 2, dtype=F32) / HEAD_DIM)
    ang = jnp.arange(seq, dtype=F32)[:, None] * inv[None, :]
    cos, sin = jnp.cos(ang), jnp.sin(ang)
    cos_h = jnp.concatenate([cos, cos], axis=-1)
    sin_h = jnp.concatenate([-sin, sin], axis=-1)
    return jnp.tile(cos_h, (1, N_HEADS)), jnp.tile(sin_h, (1, N_HEADS))


def _dispatch_plan(counts, n_blocks):
    total = jnp.sum(counts, axis=0)
    padded = ((total + MOE_BLOCK - 1) // MOE_BLOCK) * MOE_BLOCK
    ends = jnp.cumsum(padded)
    pad_start = ends - padded
    tile_base = pad_start[None, :] + (jnp.cumsum(counts, axis=0) - counts)
    block_row0 = jnp.arange(n_blocks, dtype=jnp.int32) * MOE_BLOCK
    block_e = jnp.minimum(jnp.sum((ends[None, :] <= block_row0[:, None]).astype(jnp.int32), axis=1),
                          N_EXPERTS - 1)
    return tile_base.reshape(-1).astype(jnp.int32), block_e


def kernel(x, p, norm_mix, w_in, w_dw, b_dw, ln_conv_g, ln_conv_b, w_attn_out, w_conv_out,
           b_conv_out, w_out, norm_ffn, w_route_group, b_route_group, w_route_expert,
           b_route_expert, w_exp_gate, w_exp_up, w_exp_down, norm_ple, w_ple_proj, w_ple_gate,
           norm_final):
    bsz, seq, d = x.shape
    depth = w_in.shape[0]
    t = bsz * seq
    tm = 512
    n_assign = t * TOP_K
    n_rows = (-(-n_assign // MOE_BLOCK)) * MOE_BLOCK + N_EXPERTS * MOE_BLOCK
    cos, sin = _rope_tables(seq)
    row2 = lambda a: a.reshape(1, -1)

    xf = x.reshape(t, d)
    for i in range(depth):
        q, k, v, u, ga, gb = _inproj(xf, row2(norm_mix[i]), w_in[i].astype(BF16), cos, sin, seq, tm)
        shp = (bsz, seq, ATTN_WIDTH)
        q, k, v = q.reshape(shp), k.reshape(shp), v.reshape(shp)
        state = None
        for dil in DILATIONS:
            state = _attn_sweep(q, k, v, state, dil, final=(dil == DILATIONS[-1]))
        o = state.reshape(t, ATTN_WIDTH)
        c = _conv(u.reshape(bsz, seq, CONV_WIDTH), w_dw[i], row2(b_dw[i]), row2(ln_conv_g[i]),
                  row2(ln_conv_b[i]), 256).reshape(t, CONV_WIDTH)

        w_r = jnp.zeros((d, LANES), F32)
        w_r = w_r.at[:, :N_GROUPS].set(w_route_group[i])
        w_r = w_r.at[:, EXPERT_LANE0:EXPERT_LANE0 + N_EXPERTS].set(w_route_expert[i])
        w_r_hi = w_r.astype(BF16)
        w_r_lo = (w_r - w_r_hi.astype(F32)).astype(BF16)
        b_r = jnp.zeros((1, LANES), F32)
        b_r = b_r.at[0, :N_GROUPS].set(b_route_group[i])
        b_r = b_r.at[0, EXPERT_LANE0:EXPERT_LANE0 + N_EXPERTS].set(b_route_expert[i])

        x1, h2, route, gate, cnt = _merge(
            o, c, ga, gb, xf, w_attn_out[i].astype(BF16), w_conv_out[i].astype(BF16),
            row2(b_conv_out[i]), w_out[i].astype(BF16), row2(norm_ffn[i]), w_r_hi, w_r_lo, b_r, tm)

        counts = cnt[:, 0, :N_EXPERTS].astype(jnp.int32)
        tile_base, block_e = _dispatch_plan(counts, n_rows // MOE_BLOCK)
        inv = _inverse_map(tile_base, route.reshape(-1), n_rows, tm)
        y2 = _moe(block_e, inv, h2, w_exp_gate, w_exp_up, w_exp_down, i)
        xf = _ple(gate, y2, x1, p[i].reshape(t, PLE_DIM), row2(norm_ple[i]),
                  w_ple_gate[i].astype(BF16), w_ple_proj[i].astype(BF16), row2(norm_final),
                  256, final_norm=(i == depth - 1))
    return xf.reshape(bsz, seq, d)
```

```python
import functools

import jax
import jax.numpy as jnp
from jax import lax
from jax.experimental import pallas as pl
from jax.experimental.pallas import tpu as pltpu

D_MODEL = 1024
N_HEADS = 8
HEAD_DIM = 64
ATTN_WIDTH = N_HEADS * HEAD_DIM
CONV_WIDTH = 512
CONV_K = 31
DILATIONS = (16, 4, 1)
BLK = 128
ROPE_THETA = 10000.0
N_GROUPS = 4
EXPERTS_PER_GROUP = 8
N_EXPERTS = N_GROUPS * EXPERTS_PER_GROUP
EXPERT_FF = 512
TOP_K = 2
MOE_BLOCK = 128
PLE_DIM = 256
EPS = 1e-6
NEG_INF = -1e30

LANES = 128
ROW_TILE = D_MODEL // LANES
VMEM_LIMIT = 56 * 1024 * 1024

F32 = jnp.float32
BF16 = jnp.bfloat16


def _cparams(sem):
    return pltpu.CompilerParams(dimension_semantics=sem, vmem_limit_bytes=VMEM_LIMIT)


def _rms(x, g):
    return x * lax.rsqrt(jnp.mean(x * x, axis=-1, keepdims=True) + EPS) * g


def _sigmoid(x):
    return 1.0 / (1.0 + jnp.exp(-x))


def _store_row_tiles(ref, x):
    rows = x.shape[0]
    for s in range(ROW_TILE):
        ref[pl.ds(s, rows, stride=ROW_TILE), :] = x[:, s * LANES:(s + 1) * LANES]


def _load_row_tiles(ref, rows):
    return jnp.concatenate([ref[pl.ds(s, rows, stride=ROW_TILE), :] for s in range(ROW_TILE)], axis=1)


def _inproj_kernel(x_ref, g_ref, w_ref, cos_ref, sin_ref, *out_refs):
    n_lay = len(DILATIONS)
    q_refs, k_refs, v_refs = (out_refs[a * n_lay:(a + 1) * n_lay] for a in range(3))
    u_ref, ga_ref, gb_ref = out_refs[3 * n_lay:3 * n_lay + 3]
    slab_refs = out_refs[3 * n_lay + 3:]
    tm = x_ref.shape[0]
    n_slab = ATTN_WIDTH // LANES

    def put(ref, d, r, row0, g, val):
        cols = slice(g * LANES, (g + 1) * LANES)
        if d == 1:
            ref[row0:row0 + val.shape[0], cols] = val.astype(BF16)
        else:
            ref[0, r, row0:row0 + val.shape[0], cols] = val.astype(BF16)

    def emit(t, refs, slabs, is_query):
        for g in range(n_slab):
            slabs[g] = t[:, g * LANES:(g + 1) * LANES]
        for d, ref in zip(DILATIONS, refs):
            groups = STATE_DILATION // d
            if d == 1 and not is_query:
                ref[...] = t.astype(BF16)
            elif not is_query or groups == 1:
                for r in range(d):
                    for g in range(n_slab):
                        put(ref, d, r, 0, g, slabs[g, pl.ds(r, tm // d, stride=d), :])
            else:
                run = BLK // groups
                for r in range(d):
                    for bl in range(tm // (d * BLK)):
                        for a in range(groups):
                            src = pl.ds(d * BLK * bl + d * a + r, run, stride=STATE_DILATION)
                            for g in range(n_slab):
                                put(ref, d, r, bl * BLK + a * run, g, slabs[g, src, :])

    h = _rms(x_ref[...], g_ref[...]).astype(BF16)

    def proj(c0, width):
        return jnp.dot(h, w_ref[:, c0:c0 + width], preferred_element_type=F32)

    cos = cos_ref[...]
    sin = sin_ref[...]
    lane = lax.broadcasted_iota(jnp.int32, (tm, ATTN_WIDTH), 1)
    first_half = (lane % HEAD_DIM) < (HEAD_DIM // 2)

    def rope(t):
        partner = jnp.where(first_half,
                            pltpu.roll(t, ATTN_WIDTH - HEAD_DIM // 2, 1),
                            pltpu.roll(t, HEAD_DIM // 2, 1))
        return t * cos + partner * sin

    emit(rope(proj(0, ATTN_WIDTH)) * (HEAD_DIM ** -0.5), q_refs, slab_refs[0], True)
    emit(rope(proj(ATTN_WIDTH, ATTN_WIDTH)), k_refs, slab_refs[1], False)
    emit(proj(2 * ATTN_WIDTH, ATTN_WIDTH), v_refs, slab_refs[2], False)
    c0 = 3 * ATTN_WIDTH
    cu = proj(c0, CONV_WIDTH)
    cg = proj(c0 + CONV_WIDTH, CONV_WIDTH)
    u_ref[...] = cu * _sigmoid(cg)
    c0 += 2 * CONV_WIDTH
    ga_ref[...] = proj(c0, D_MODEL)
    gb_ref[...] = proj(c0 + D_MODEL, D_MODEL)


def _inproj(x, gain, w_in, cos, sin, bsz, seq, tm):
    t = x.shape[0]
    n_pos = seq // tm
    row = lambda i: (i, 0)
    const = lambda i: (0, 0)
    qkv_shapes, qkv_specs = [], []
    for d in DILATIONS:
        if d == 1:
            qkv_shapes.append(jax.ShapeDtypeStruct((t, ATTN_WIDTH), BF16))
            qkv_specs.append(pl.BlockSpec((tm, ATTN_WIDTH), row))
        else:
            qkv_shapes.append(jax.ShapeDtypeStruct((bsz, d, seq // d, ATTN_WIDTH), BF16))
            qkv_specs.append(pl.BlockSpec((1, d, tm // d, ATTN_WIDTH),
                                          lambda i: (i // n_pos, 0, i % n_pos, 0)))
    out_shape = tuple(qkv_shapes * 3) + (
        jax.ShapeDtypeStruct((t, CONV_WIDTH), F32),
        jax.ShapeDtypeStruct((t, D_MODEL), F32),
        jax.ShapeDtypeStruct((t, D_MODEL), F32),
    )
    out_specs = tuple(qkv_specs * 3) + (
        pl.BlockSpec((tm, CONV_WIDTH), row),
        pl.BlockSpec((tm, D_MODEL), row),
        pl.BlockSpec((tm, D_MODEL), row),
    )
    outs = pl.pallas_call(
        _inproj_kernel,
        grid=(t // tm,),
        in_specs=[
            pl.BlockSpec((tm, D_MODEL), row),
            pl.BlockSpec((1, D_MODEL), const),
            pl.BlockSpec(w_in.shape, const),
            pl.BlockSpec((tm, ATTN_WIDTH), lambda i: (i % n_pos, 0)),
            pl.BlockSpec((tm, ATTN_WIDTH), lambda i: (i % n_pos, 0)),
        ],
        out_specs=out_specs,
        out_shape=out_shape,
        scratch_shapes=[pltpu.VMEM((ATTN_WIDTH // LANES, tm, LANES), F32) for _ in range(3)],
        compiler_params=_cparams(("parallel",)),
        name="inproj",
    )(x, gain, w_in, cos, sin)
    n_lay = len(DILATIONS)
    as_rows = lambda a: a.reshape(bsz, seq, ATTN_WIDTH)
    q, k, v = ([as_rows(a) for a in outs[j * n_lay:(j + 1) * n_lay]] for j in range(3))
    return q, k, v, outs[3 * n_lay], outs[3 * n_lay + 1], outs[3 * n_lay + 2]


STATE_DILATION = max(DILATIONS)


def _query_order(idx, groups):
    run = BLK // groups
    return groups * (idx % run) + idx // run


def _attn_kernel(*refs, seq):
    n_lay = len(DILATIONS)
    srcs = refs[:3 * n_lay]
    o_ref, buf, acc_s, m_s, l_s, sem = refs[3 * n_lay:]
    b = pl.program_id(0)
    n_b = pl.num_programs(0)
    n_blocks = seq // BLK
    pair_w = 2 * HEAD_DIM
    n_pair = ATTN_WIDTH // pair_w

    def load_sweep(lay, bi, slot):
        return [pltpu.make_async_copy(srcs[a * n_lay + lay].at[bi], buf.at[slot, a, pl.ds(BLK, seq)],
                                      sem.at[slot]) for a in range(3)]

    @pl.when(b == 0)
    def _():
        buf[:, :, 0:BLK, :] = jnp.zeros((2, 3, BLK, ATTN_WIDTH), BF16)
        for c in load_sweep(0, 0, 0):
            c.start()

    row = lax.broadcasted_iota(jnp.int32, (BLK, BLK), 0)
    col = lax.broadcasted_iota(jnp.int32, (BLK, BLK), 1)
    lane = lax.broadcasted_iota(jnp.int32, (BLK, pair_w), 1)
    low_head = lane < HEAD_DIM
    unpermute = (row == _query_order(col, STATE_DILATION // DILATIONS[-1])).astype(BF16)

    for lay, d in enumerate(DILATIONS):
        slot = (b * n_lay + lay) % 2
        for c in load_sweep(lay, b, slot):
            c.wait()
        if lay + 1 < n_lay:
            for c in load_sweep(lay + 1, b, 1 - slot):
                c.start()
        else:
            @pl.when(b + 1 < n_b)
            def _():
                for c in load_sweep(0, b + 1, 1 - slot):
                    c.start()

        nb = seq // (d * BLK)
        has_prev = nb > 1
        first, final = lay == 0, lay == n_lay - 1
        kw = 2 * BLK if has_prev else BLK

        groups = STATE_DILATION // d
        run = BLK // groups
        q_pos = _query_order(row, groups)

        def block(j, carry, d=d, nb=nb, has_prev=has_prev, first=first, final=final, kw=kw,
                  slot=slot, groups=groups, run=run, q_pos=q_pos):
            r, n = j // nb, j % nb
            q_row0 = pl.multiple_of(BLK + j * BLK, BLK)
            k_row0 = pl.multiple_of(j * BLK, BLK) if has_prev else q_row0
            runs = [pl.ds(pl.multiple_of((d * a + r) * BLK + run * n, run), run)
                    for a in range(groups)]
            load = lambda ref, g: jnp.concatenate([ref[g, s, :] for s in runs], axis=0)

            def store(ref, g, val):
                for a, s in enumerate(runs):
                    ref[g, s, :] = val[a * run:(a + 1) * run]

            cur_ok = col <= q_pos
            if has_prev:
                prev_ok = jnp.logical_and(col >= q_pos, n > 0)
                allowed = jnp.concatenate([prev_ok, cur_ok], axis=1)
            else:
                allowed = cur_ok
            allowed2 = jnp.concatenate([allowed, allowed], axis=0)
            lanes = [slice(g * pair_w, (g + 1) * pair_w) for g in range(n_pair)]
            scores = []
            for ls in lanes:
                q2 = buf[slot, 0, pl.ds(q_row0, BLK), ls]
                q_both = jnp.concatenate([jnp.where(low_head, q2, 0),
                                          jnp.where(low_head, 0, q2)], axis=0)
                k2 = buf[slot, 1, pl.ds(k_row0, kw), ls]
                scores.append(lax.dot_general(q_both, k2, (((1,), (1,)), ((), ())),
                                              preferred_element_type=F32))
            probs, maxes, sums = [], [], []
            for s in scores:
                s = jnp.where(allowed2, s, NEG_INF)
                m_h = jnp.max(s, axis=-1, keepdims=True)
                p = jnp.exp(s - m_h)
                sums.append(jnp.sum(p, axis=-1, keepdims=True))
                maxes.append(m_h)
                probs.append(p.astype(BF16))
            pv = [jnp.dot(p, buf[slot, 2, pl.ds(k_row0, kw), ls], preferred_element_type=F32)
                  for p, ls in zip(probs, lanes)]
            for g, ls in enumerate(lanes):
                acc = jnp.where(low_head, pv[g][:BLK], pv[g][BLK:])
                m_new = jnp.where(low_head, maxes[g][:BLK], maxes[g][BLK:])
                l_new = jnp.where(low_head, sums[g][:BLK], sums[g][BLK:])
                if not first:
                    m_old = load(m_s, g)
                    m_tot = jnp.maximum(m_old, m_new)
                    a_old = jnp.exp(m_old - m_tot)
                    a_new = jnp.exp(m_new - m_tot)
                    acc = load(acc_s, g) * a_old + acc * a_new
                    l_new = load(l_s, g) * a_old + l_new * a_new
                    m_new = m_tot
                if final:
                    o_blk = jnp.dot(unpermute, (acc / l_new).astype(BF16), preferred_element_type=F32)
                    o_ref[0, pl.ds(pl.multiple_of(j * BLK, BLK), BLK), ls] = o_blk.astype(o_ref.dtype)
                else:
                    store(acc_s, g, acc)
                    store(m_s, g, m_new)
                    store(l_s, g, l_new)
            return carry

        lax.fori_loop(0, n_blocks, block, 0)


def _attention(q, k, v):
    bsz, seq, _ = q[0].shape
    assert all(seq % (d * BLK) == 0 for d in DILATIONS) and DILATIONS[-1] == 1
    n_pair = ATTN_WIDTH // (2 * HEAD_DIM)
    return pl.pallas_call(
        functools.partial(_attn_kernel, seq=seq),
        grid=(bsz,),
        in_specs=[pl.BlockSpec(memory_space=pl.ANY)] * (3 * len(DILATIONS)),
        out_specs=pl.BlockSpec((1, seq, ATTN_WIDTH), lambda b: (b, 0, 0)),
        out_shape=jax.ShapeDtypeStruct((bsz, seq, ATTN_WIDTH), BF16),
        scratch_shapes=[pltpu.VMEM((2, 3, BLK + seq, ATTN_WIDTH), BF16),
                        pltpu.VMEM((n_pair, seq, 2 * HEAD_DIM), F32),
                        pltpu.VMEM((n_pair, seq, 2 * HEAD_DIM), F32),
                        pltpu.VMEM((n_pair, seq, 2 * HEAD_DIM), F32),
                        pltpu.SemaphoreType.DMA((2,))],
        compiler_params=_cparams(("arbitrary",)),
        name="attention",
    )(*q, *k, *v)


CONV_HALO = 32
CONV_ROWS = 32


def _conv_kernel(halo_ref, cur_ref, w_ref, b_ref, g_ref, beta_ref, out_ref, ext_ref):
    ts = cur_ref.shape[1]
    i = pl.program_id(1)
    ext_ref[0:CONV_HALO, :] = jnp.where(i > 0, halo_ref[0], 0.0)
    ext_ref[CONV_HALO:CONV_HALO + ts, :] = cur_ref[0]
    first = CONV_HALO - (CONV_K - 1)
    bias = b_ref[...]
    gamma = g_ref[...]
    beta = beta_ref[...]
    for r0 in range(0, ts, CONV_ROWS):
        acc = jnp.zeros((CONV_ROWS, CONV_WIDTH), F32) + bias
        for j in range(CONV_K):
            lo = first + j + r0
            acc = acc + w_ref[j:j + 1, :] * ext_ref[lo:lo + CONV_ROWS, :]
        mu = jnp.mean(acc, axis=-1, keepdims=True)
        cen = acc - mu
        var = jnp.mean(cen * cen, axis=-1, keepdims=True)
        y = cen * lax.rsqrt(var + EPS) * gamma + beta
        out_ref[0, r0:r0 + CONV_ROWS, :] = (y * _sigmoid(y)).astype(out_ref.dtype)


def _conv(u, w_dw, b_dw, ln_g, ln_b, ts):
    b, s, c = u.shape
    per = ts // CONV_HALO
    const = lambda bi, i: (0, 0)
    return pl.pallas_call(
        _conv_kernel,
        grid=(b, s // ts),
        in_specs=[
            pl.BlockSpec((1, CONV_HALO, c), lambda bi, i: (bi, jnp.maximum(i * per - 1, 0), 0)),
            pl.BlockSpec((1, ts, c), lambda bi, i: (bi, i, 0)),
            pl.BlockSpec((CONV_K, c), const),
            pl.BlockSpec((1, c), const),
            pl.BlockSpec((1, c), const),
            pl.BlockSpec((1, c), const),
        ],
        out_specs=pl.BlockSpec((1, ts, c), lambda bi, i: (bi, i, 0)),
        out_shape=jax.ShapeDtypeStruct((b, s, c), BF16),
        scratch_shapes=[pltpu.VMEM((CONV_HALO + ts, c), F32)],
        compiler_params=_cparams(("parallel", "arbitrary")),
        name="conv",
    )(u, u, w_dw, b_dw, ln_g, ln_b)


ROUTE_COLS = 4
EXPERT_LANE0 = N_GROUPS


def _merge_kernel(o_ref, c_ref, ga_ref, gb_ref, x_ref, wa_ref, wc_ref, bc_ref, wo_ref,
                  gf_ref, wrh_ref, wrl_ref, br_ref,
                  x1_ref, h2_ref, route_ref, gate_ref, cnt_ref, tri_ref):
    tm = x_ref.shape[0]

    @pl.when(pl.program_id(0) == 0)
    def _():
        r = lax.broadcasted_iota(jnp.int32, (tm, tm), 0)
        c = lax.broadcasted_iota(jnp.int32, (tm, tm), 1)
        tri_ref[...] = (c < r).astype(BF16)

    y_a = jnp.dot(o_ref[...], wa_ref[...], preferred_element_type=F32)
    y_b = jnp.dot(c_ref[...], wc_ref[...], preferred_element_type=F32) + bc_ref[...]
    merged = _sigmoid(ga_ref[...]) * y_a + _sigmoid(gb_ref[...]) * y_b
    x1 = x_ref[...] + jnp.dot(merged.astype(BF16), wo_ref[...], preferred_element_type=F32)
    x1_ref[...] = x1
    h2 = _rms(x1, gf_ref[...])
    _store_row_tiles(h2_ref, h2)

    h_hi = h2.astype(BF16)
    h_lo = (h2 - h_hi.astype(F32)).astype(BF16)
    logits = (jnp.dot(h_hi, wrh_ref[...], preferred_element_type=F32)
              + jnp.dot(h_hi, wrl_ref[...], preferred_element_type=F32)
              + jnp.dot(h_lo, wrh_ref[...], preferred_element_type=F32)) + br_ref[...]

    lane = lax.broadcasted_iota(jnp.int32, (tm, LANES), 1)
    is_group = lane < N_GROUPS
    g_max = jnp.max(jnp.where(is_group, logits, -jnp.inf), axis=-1, keepdims=True)
    g_sel = jnp.min(jnp.where(jnp.logical_and(is_group, logits == g_max), lane, LANES),
                    axis=-1, keepdims=True)
    p_group = 1.0 / jnp.sum(jnp.where(is_group, jnp.exp(logits - g_max), 0.0),
                            axis=-1, keepdims=True)
    lo = EXPERT_LANE0 + EXPERTS_PER_GROUP * g_sel
    in_sel = jnp.logical_and(lane >= lo, lane < lo + EXPERTS_PER_GROUP)
    cand = jnp.where(in_sel, logits, -jnp.inf)
    v1 = jnp.max(cand, axis=-1, keepdims=True)
    i1 = jnp.min(jnp.where(jnp.logical_and(in_sel, cand == v1), lane, LANES), axis=-1, keepdims=True)
    rest = jnp.logical_and(in_sel, lane != i1)
    cand2 = jnp.where(rest, logits, -jnp.inf)
    v2 = jnp.max(cand2, axis=-1, keepdims=True)
    i2 = jnp.min(jnp.where(jnp.logical_and(rest, cand2 == v2), lane, LANES), axis=-1, keepdims=True)
    e2 = jnp.exp(v2 - v1)
    gate0 = p_group / (1.0 + e2)
    gate1 = p_group * e2 / (1.0 + e2)

    oh0 = (lane == i1)
    oh1 = (lane == i2)
    before0 = jnp.dot(tri_ref[...], oh0.astype(BF16), preferred_element_type=F32)
    before1 = jnp.dot(tri_ref[...], oh1.astype(BF16), preferred_element_type=F32)
    tot0 = jnp.sum(oh0.astype(F32), axis=0, keepdims=True)
    tot1 = jnp.sum(oh1.astype(F32), axis=0, keepdims=True)
    rank0 = jnp.sum(jnp.where(oh0, before0, 0.0), axis=-1, keepdims=True)
    rank1 = jnp.sum(jnp.where(oh1, before1 + tot0, 0.0), axis=-1, keepdims=True)

    rc = lax.broadcasted_iota(jnp.int32, (tm, ROUTE_COLS), 1)
    route = jnp.where(rc == 0, i1 - EXPERT_LANE0,
                      jnp.where(rc == 1, i2 - EXPERT_LANE0,
                                jnp.where(rc == 2, rank0.astype(jnp.int32), rank1.astype(jnp.int32))))
    route_ref[...] = route
    gc = lax.broadcasted_iota(jnp.int32, (tm, TOP_K), 1)
    gate_ref[...] = jnp.where(gc == 0, gate0, gate1)
    cnt = tot0 + tot1
    cnt_ref[0] = jnp.broadcast_to(pltpu.roll(cnt, LANES - EXPERT_LANE0, 1), (8, LANES))


def _merge(o, c, ga, gb, x, wa, wc, bc, wo, gf, wr_hi, wr_lo, br, tm):
    t = x.shape[0]
    nt = t // tm
    row = lambda i: (i, 0)
    const = lambda i: (0, 0)
    full = lambda a: pl.BlockSpec(a.shape, const)
    return pl.pallas_call(
        _merge_kernel,
        grid=(nt,),
        in_specs=[
            pl.BlockSpec((tm, ATTN_WIDTH), row),
            pl.BlockSpec((tm, CONV_WIDTH), row),
            pl.BlockSpec((tm, D_MODEL), row),
            pl.BlockSpec((tm, D_MODEL), row),
            pl.BlockSpec((tm, D_MODEL), row),
            full(wa), full(wc), full(bc), full(wo), full(gf), full(wr_hi), full(wr_lo), full(br),
        ],
        out_specs=(
            pl.BlockSpec((tm, D_MODEL), row),
            pl.BlockSpec((tm * ROW_TILE, LANES), row),
            pl.BlockSpec((tm, ROUTE_COLS), row),
            pl.BlockSpec((tm, TOP_K), row),
            pl.BlockSpec((1, 8, LANES), lambda i: (i, 0, 0)),
        ),
        out_shape=(
            jax.ShapeDtypeStruct((t, D_MODEL), F32),
            jax.ShapeDtypeStruct((t * ROW_TILE, LANES), F32),
            jax.ShapeDtypeStruct((t, ROUTE_COLS), jnp.int32),
            jax.ShapeDtypeStruct((t, TOP_K), F32),
            jax.ShapeDtypeStruct((nt, 8, LANES), F32),
        ),
        scratch_shapes=[pltpu.VMEM((tm, tm), BF16)],
        compiler_params=_cparams(("arbitrary",)),
        name="merge",
    )(o, c, ga, gb, x, wa, wc, bc, wo, gf, wr_hi, wr_lo, br)


NO_ASSIGNMENT = -1


def _inv_kernel(base_ref, route_ref, inv_ref):
    tm = route_ref.shape[0] // ROUTE_COLS
    i = pl.program_id(0)

    @pl.when(i == 0)
    def _():
        def fill(j, carry):
            inv_ref[j] = NO_ASSIGNMENT
            return carry
        lax.fori_loop(0, inv_ref.shape[0], fill, 0, unroll=8)

    def place(t, carry):
        for k in range(TOP_K):
            e = route_ref[ROUTE_COLS * t + k]
            row = base_ref[i * N_EXPERTS + e] + route_ref[ROUTE_COLS * t + TOP_K + k]
            inv_ref[row] = TOP_K * (i * tm + t) + k
        return carry

    lax.fori_loop(0, tm, place, 0, unroll=4)


def _inverse_map(tile_base, route_flat, n_rows, tm):
    n_tok = route_flat.shape[0] // ROUTE_COLS
    grid_spec = pltpu.PrefetchScalarGridSpec(
        num_scalar_prefetch=1,
        grid=(n_tok // tm,),
        in_specs=[pl.BlockSpec((ROUTE_COLS * tm,), lambda i, base: (i,), memory_space=pltpu.SMEM)],
        out_specs=pl.BlockSpec(memory_space=pltpu.SMEM),
    )
    return pl.pallas_call(
        _inv_kernel,
        grid_spec=grid_spec,
        out_shape=jax.ShapeDtypeStruct((n_rows,), jnp.int32),
        compiler_params=_cparams(("arbitrary",)),
        name="inverse_map",
    )(tile_base, route_flat)


TRASH_ROWS = 2 * MOE_BLOCK


def _moe_kernel(be_ref, inv_prev, inv_cur, inv_next, h2_hbm, wg_ref, wu_ref, wd_ref, y2_hbm,
                xbuf, obuf, wg_s, wu_s, wd_s, sem_in, sem_out, *, n_tok):
    b = pl.program_id(0)
    last = pl.num_programs(0) - 1
    slot = b % 2
    other = 1 - slot
    trash0 = TOP_K * n_tok

    def issue_gather(inv, s):
        for j in range(MOE_BLOCK):
            tok = jnp.maximum(inv[j], 0) >> 1
            src = pl.multiple_of(tok * ROW_TILE, ROW_TILE)
            pltpu.make_async_copy(h2_hbm.at[pl.ds(src, ROW_TILE)],
                                  xbuf.at[s, pl.ds(j * ROW_TILE, ROW_TILE)], sem_in.at[s]).start()

    def issue_scatter(inv, s, live):
        for j in range(MOE_BLOCK):
            a = inv[j]
            valid = jnp.logical_and(a >= 0, live)
            dst = jnp.where(valid, (a & 1) * n_tok + (a >> 1), trash0 + MOE_BLOCK * s + j)
            dst = pl.multiple_of(dst * ROW_TILE, ROW_TILE)
            pltpu.make_async_copy(obuf.at[s, pl.ds(j * ROW_TILE, ROW_TILE)],
                                  y2_hbm.at[pl.ds(dst, ROW_TILE)], sem_out.at[s]).start()

    def wait_gather(s):
        pltpu.make_async_copy(h2_hbm.at[pl.ds(0, MOE_BLOCK * ROW_TILE)], xbuf.at[s], sem_in.at[s]).wait()

    def wait_scatter(s):
        pltpu.make_async_copy(obuf.at[s], y2_hbm.at[pl.ds(0, MOE_BLOCK * ROW_TILE)], sem_out.at[s]).wait()

    @pl.when(b == 0)
    def _():
        obuf[...] = jnp.zeros_like(obuf)
        issue_scatter(inv_cur, 0, False)
        wait_scatter(0)
        issue_gather(inv_cur, 0)

    wait_gather(slot)

    @pl.when(b >= 1)
    def _():
        wait_scatter(slot)

    e = be_ref[b]
    e_prev = be_ref[jnp.maximum(b - 1, 0)]

    @pl.when(jnp.logical_or(b == 0, e != e_prev))
    def _():
        wg_s[...] = wg_ref[0].astype(BF16)
        wu_s[...] = wu_ref[0].astype(BF16)
        wd_s[...] = wd_ref[0].astype(BF16)

    issue_gather(inv_next, other)
    issue_scatter(inv_prev, other, b > 0)
    x = _load_row_tiles(xbuf.at[slot], MOE_BLOCK).astype(BF16)
    hg = jnp.dot(x, wg_s[...], preferred_element_type=F32)
    hu = jnp.dot(x, wu_s[...], preferred_element_type=F32)
    hb = (hg * _sigmoid(hg) * hu).astype(BF16)
    _store_row_tiles(obuf.at[slot], jnp.dot(hb, wd_s[...], preferred_element_type=F32))

    @pl.when(b == last)
    def _():
        issue_scatter(inv_cur, slot, True)
        wait_scatter(other)
        wait_scatter(slot)
        wait_gather(other)


def _moe(block_e, inv, h2, w_gate, w_up, w_down, layer):
    n_tok = h2.shape[0] // ROW_TILE
    n_blocks = inv.shape[0] // MOE_BLOCK
    w_idx = lambda b, be: (layer, be[b], 0, 0)
    inv_spec = lambda f: pl.BlockSpec((MOE_BLOCK,), f, memory_space=pltpu.SMEM)
    grid_spec = pltpu.PrefetchScalarGridSpec(
        num_scalar_prefetch=1,
        grid=(n_blocks,),
        in_specs=[
            inv_spec(lambda b, be: (jnp.maximum(b - 1, 0),)),
            inv_spec(lambda b, be: (b,)),
            inv_spec(lambda b, be: (jnp.minimum(b + 1, n_blocks - 1),)),
            pl.BlockSpec(memory_space=pl.ANY),
            pl.BlockSpec((None, 1, D_MODEL, EXPERT_FF), w_idx),
            pl.BlockSpec((None, 1, D_MODEL, EXPERT_FF), w_idx),
            pl.BlockSpec((None, 1, EXPERT_FF, D_MODEL), w_idx),
        ],
        out_specs=pl.BlockSpec(memory_space=pl.ANY),
        scratch_shapes=[pltpu.VMEM((2, MOE_BLOCK * ROW_TILE, LANES), F32),
                        pltpu.VMEM((2, MOE_BLOCK * ROW_TILE, LANES), F32),
                        pltpu.VMEM((D_MODEL, EXPERT_FF), BF16),
                        pltpu.VMEM((D_MODEL, EXPERT_FF), BF16),
                        pltpu.VMEM((EXPERT_FF, D_MODEL), BF16),
                        pltpu.SemaphoreType.DMA((2,)),
                        pltpu.SemaphoreType.DMA((2,))],
    )
    return pl.pallas_call(
        functools.partial(_moe_kernel, n_tok=n_tok),
        grid_spec=grid_spec,
        out_shape=jax.ShapeDtypeStruct(((TOP_K * n_tok + TRASH_ROWS) * ROW_TILE, LANES), F32),
        compiler_params=_cparams(("arbitrary",)),
        name="moe",
    )(block_e, inv, inv, inv, h2, w_gate, w_up, w_down)


def _ple_kernel(gate_ref, y0_ref, y1_ref, x1_ref, p_ref, gp_ref, wpg_ref, wpe_ref, gfin_ref,
                out_ref, *, final_norm):
    gate = gate_ref[...]
    tm = x1_ref.shape[0]
    x2 = (x1_ref[...] + gate[:, 0:1] * _load_row_tiles(y0_ref, tm)
          + gate[:, 1:2] * _load_row_tiles(y1_ref, tm))
    hp = _rms(x2, gp_ref[...]).astype(BF16)
    g_ple = _sigmoid(jnp.dot(hp, wpg_ref[...], preferred_element_type=F32))
    emb = jnp.dot(p_ref[...].astype(BF16), wpe_ref[...], preferred_element_type=F32)
    x3 = x2 + g_ple * emb
    if final_norm:
        x3 = _rms(x3, gfin_ref[...])
    out_ref[...] = x3


def _ple(gate, y2, x1, p, gp, wpg, wpe, gfin, tm, final_norm):
    t = x1.shape[0]
    nt = t // tm
    row = lambda i: (i, 0)
    const = lambda i: (0, 0)
    full = lambda a: pl.BlockSpec(a.shape, const)
    return pl.pallas_call(
        functools.partial(_ple_kernel, final_norm=final_norm),
        grid=(t // tm,),
        in_specs=[
            pl.BlockSpec((tm, TOP_K), row),
            pl.BlockSpec((tm * ROW_TILE, LANES), lambda i: (i, 0)),
            pl.BlockSpec((tm * ROW_TILE, LANES), lambda i: (nt + i, 0)),
            pl.BlockSpec((tm, D_MODEL), row),
            pl.BlockSpec((tm, PLE_DIM), row),
            full(gp), full(wpg), full(wpe), full(gfin),
        ],
        out_specs=pl.BlockSpec((tm, D_MODEL), row),
        out_shape=jax.ShapeDtypeStruct((t, D_MODEL), F32),
        compiler_params=_cparams(("parallel",)),
        name="ple",
    )(gate, y2, y2, x1, p, gp, wpg, wpe, gfin)


def _rope_tables(seq):
    inv = jnp.power(ROPE_THETA, -jnp.arange(0, HEAD_DIM, 2, dtype=F32) / HEAD_DIM)
    ang = jnp.arange(seq, dtype=F32)[:, None] * inv[None, :]
    cos, sin = jnp.cos(ang), jnp.sin(ang)
    cos_h = jnp.concatenate([cos, cos], axis=-1)
    sin_h = jnp.concatenate([-sin, sin], axis=-1)
    return jnp.tile(cos_h, (1, N_HEADS)), jnp.tile(sin_h, (1, N_HEADS))


def _dispatch_plan(counts, n_blocks):
    total = jnp.sum(counts, axis=0)
    padded = ((total + MOE_BLOCK - 1) // MOE_BLOCK) * MOE_BLOCK
    ends = jnp.cumsum(padded)
    pad_start = ends - padded
    tile_base = pad_start[None, :] + (jnp.cumsum(counts, axis=0) - counts)
    block_row0 = jnp.arange(n_blocks, dtype=jnp.int32) * MOE_BLOCK
    block_e = jnp.minimum(jnp.sum((ends[None, :] <= block_row0[:, None]).astype(jnp.int32), axis=1),
                          N_EXPERTS - 1)
    return tile_base.reshape(-1).astype(jnp.int32), block_e


def kernel(x, p, norm_mix, w_in, w_dw, b_dw, ln_conv_g, ln_conv_b, w_attn_out, w_conv_out,
           b_conv_out, w_out, norm_ffn, w_route_group, b_route_group, w_route_expert,
           b_route_expert, w_exp_gate, w_exp_up, w_exp_down, norm_ple, w_ple_proj, w_ple_gate,
           norm_final):
    bsz, seq, d = x.shape
    depth = w_in.shape[0]
    t = bsz * seq
    tm = 512
    n_assign = t * TOP_K
    n_rows = (-(-n_assign // MOE_BLOCK)) * MOE_BLOCK + N_EXPERTS * MOE_BLOCK
    cos, sin = _rope_tables(seq)
    row2 = lambda a: a.reshape(1, -1)

    xf = x.reshape(t, d)
    for i in range(depth):
        q, k, v, u, ga, gb = _inproj(xf, row2(norm_mix[i]), w_in[i].astype(BF16), cos, sin,
                                     bsz, seq, tm)
        o = _attention(q, k, v).reshape(t, ATTN_WIDTH)
        c = _conv(u.reshape(bsz, seq, CONV_WIDTH), w_dw[i], row2(b_dw[i]), row2(ln_conv_g[i]),
                  row2(ln_conv_b[i]), 256).reshape(t, CONV_WIDTH)

        w_r = jnp.zeros((d, LANES), F32)
        w_r = w_r.at[:, :N_GROUPS].set(w_route_group[i])
        w_r = w_r.at[:, EXPERT_LANE0:EXPERT_LANE0 + N_EXPERTS].set(w_route_expert[i])
        w_r_hi = w_r.astype(BF16)
        w_r_lo = (w_r - w_r_hi.astype(F32)).astype(BF16)
        b_r = jnp.zeros((1, LANES), F32)
        b_r = b_r.at[0, :N_GROUPS].set(b_route_group[i])
        b_r = b_r.at[0, EXPERT_LANE0:EXPERT_LANE0 + N_EXPERTS].set(b_route_expert[i])

        x1, h2, route, gate, cnt = _merge(
            o, c, ga, gb, xf, w_attn_out[i].astype(BF16), w_conv_out[i].astype(BF16),
            row2(b_conv_out[i]), w_out[i].astype(BF16), row2(norm_ffn[i]), w_r_hi, w_r_lo, b_r, tm)

        counts = cnt[:, 0, :N_EXPERTS].astype(jnp.int32)
        tile_base, block_e = _dispatch_plan(counts, n_rows // MOE_BLOCK)
        inv = _inverse_map(tile_base, route.reshape(-1), n_rows, tm)
        y2 = _moe(block_e, inv, h2, w_exp_gate, w_exp_up, w_exp_down, i)
        xf = _ple(gate, y2, x1, p[i].reshape(t, PLE_DIM), row2(norm_ple[i]),
                  w_ple_gate[i].astype(BF16), w_ple_proj[i].astype(BF16), row2(norm_final),
                  256, final_norm=(i == depth - 1))
    return xf.reshape(bsz, seq, d)
```

```python
import functools

import jax
import jax.numpy as jnp
from jax import lax
from jax.experimental import pallas as pl
from jax.experimental.pallas import tpu as pltpu

D_MODEL = 1024
N_HEADS = 8
HEAD_DIM = 64
ATTN_WIDTH = N_HEADS * HEAD_DIM
CONV_WIDTH = 512
CONV_K = 31
DILATIONS = (16, 4, 1)
BLK = 128
ROPE_THETA = 10000.0
N_GROUPS = 4
EXPERTS_PER_GROUP = 8
N_EXPERTS = N_GROUPS * EXPERTS_PER_GROUP
EXPERT_FF = 512
TOP_K = 2
MOE_BLOCK = 128
PLE_DIM = 256
EPS = 1e-6
NEG_INF = -1e30

LANES = 128
SUBLANES = 8
ROW_TILE = D_MODEL // LANES
VMEM_LIMIT = 56 * 1024 * 1024

F32 = jnp.float32
BF16 = jnp.bfloat16


def _cparams(sem):
    return pltpu.CompilerParams(dimension_semantics=sem, vmem_limit_bytes=VMEM_LIMIT)


def _rms(x, g):
    return x * lax.rsqrt(jnp.mean(x * x, axis=-1, keepdims=True) + EPS) * g


def _sigmoid(x):
    return 1.0 / (1.0 + jnp.exp(-x))


def _store_row_tiles(ref, x):
    rows = x.shape[0]
    for s in range(ROW_TILE):
        ref[pl.ds(s, rows, stride=ROW_TILE), :] = x[:, s * LANES:(s + 1) * LANES]


def _load_row_tiles(ref, rows):
    return jnp.concatenate([ref[pl.ds(s, rows, stride=ROW_TILE), :] for s in range(ROW_TILE)], axis=1)


def _inproj_kernel(x_ref, g_ref, w_ref, cos_ref, sin_ref, *out_refs):
    n_lay = len(DILATIONS)
    q_refs, k_refs, v_refs = (out_refs[a * n_lay:(a + 1) * n_lay] for a in range(3))
    u_ref, ga_ref, gb_ref = out_refs[3 * n_lay:3 * n_lay + 3]
    slab_refs = out_refs[3 * n_lay + 3:]
    tm = x_ref.shape[0]
    n_slab = ATTN_WIDTH // LANES

    def put(ref, d, r, row0, g, val):
        cols = slice(g * LANES, (g + 1) * LANES)
        if d == 1:
            ref[row0:row0 + val.shape[0], cols] = val.astype(BF16)
        else:
            ref[0, r, row0:row0 + val.shape[0], cols] = val.astype(BF16)

    def emit(t, refs, slabs, is_query):
        for g in range(n_slab):
            slabs[g] = t[:, g * LANES:(g + 1) * LANES]
        for d, ref in zip(DILATIONS, refs):
            groups = STATE_DILATION // d
            if d == 1 and not is_query:
                ref[...] = t.astype(BF16)
            elif not is_query or groups == 1:
                for r in range(d):
                    for g in range(n_slab):
                        put(ref, d, r, 0, g, slabs[g, pl.ds(r, tm // d, stride=d), :])
            else:
                run = BLK // groups
                for r in range(d):
                    for bl in range(tm // (d * BLK)):
                        for a in range(groups):
                            src = pl.ds(d * BLK * bl + d * a + r, run, stride=STATE_DILATION)
                            for g in range(n_slab):
                                put(ref, d, r, bl * BLK + a * run, g, slabs[g, src, :])

    h = _rms(x_ref[...], g_ref[...]).astype(BF16)

    def proj(c0, width):
        return jnp.dot(h, w_ref[:, c0:c0 + width], preferred_element_type=F32)

    cos = cos_ref[...]
    sin = sin_ref[...]
    lane = lax.broadcasted_iota(jnp.int32, (tm, ATTN_WIDTH), 1)
    first_half = (lane % HEAD_DIM) < (HEAD_DIM // 2)

    def rope(t):
        partner = jnp.where(first_half,
                            pltpu.roll(t, ATTN_WIDTH - HEAD_DIM // 2, 1),
                            pltpu.roll(t, HEAD_DIM // 2, 1))
        return t * cos + partner * sin

    emit(rope(proj(0, ATTN_WIDTH)) * (HEAD_DIM ** -0.5), q_refs, slab_refs[0], True)
    emit(rope(proj(ATTN_WIDTH, ATTN_WIDTH)), k_refs, slab_refs[1], False)
    emit(proj(2 * ATTN_WIDTH, ATTN_WIDTH), v_refs, slab_refs[2], False)
    c0 = 3 * ATTN_WIDTH
    cu = proj(c0, CONV_WIDTH)
    cg = proj(c0 + CONV_WIDTH, CONV_WIDTH)
    u_ref[...] = cu * _sigmoid(cg)
    c0 += 2 * CONV_WIDTH
    ga_ref[...] = proj(c0, D_MODEL)
    gb_ref[...] = proj(c0 + D_MODEL, D_MODEL)


def _inproj(x, gain, w_in, cos, sin, bsz, seq, tm):
    t = x.shape[0]
    n_pos = seq // tm
    row = lambda i: (i, 0)
    const = lambda i: (0, 0)
    qkv_shapes, qkv_specs = [], []
    for d in DILATIONS:
        if d == 1:
            qkv_shapes.append(jax.ShapeDtypeStruct((t, ATTN_WIDTH), BF16))
            qkv_specs.append(pl.BlockSpec((tm, ATTN_WIDTH), row))
        else:
            qkv_shapes.append(jax.ShapeDtypeStruct((bsz, d, seq // d, ATTN_WIDTH), BF16))
            qkv_specs.append(pl.BlockSpec((1, d, tm // d, ATTN_WIDTH),
                                          lambda i: (i // n_pos, 0, i % n_pos, 0)))
    out_shape = tuple(qkv_shapes * 3) + (
        jax.ShapeDtypeStruct((t, CONV_WIDTH), F32),
        jax.ShapeDtypeStruct((t, D_MODEL), F32),
        jax.ShapeDtypeStruct((t, D_MODEL), F32),
    )
    out_specs = tuple(qkv_specs * 3) + (
        pl.BlockSpec((tm, CONV_WIDTH), row),
        pl.BlockSpec((tm, D_MODEL), row),
        pl.BlockSpec((tm, D_MODEL), row),
    )
    outs = pl.pallas_call(
        _inproj_kernel,
        grid=(t // tm,),
        in_specs=[
            pl.BlockSpec((tm, D_MODEL), row),
            pl.BlockSpec((1, D_MODEL), const),
            pl.BlockSpec(w_in.shape, const),
            pl.BlockSpec((tm, ATTN_WIDTH), lambda i: (i % n_pos, 0)),
            pl.BlockSpec((tm, ATTN_WIDTH), lambda i: (i % n_pos, 0)),
        ],
        out_specs=out_specs,
        out_shape=out_shape,
        scratch_shapes=[pltpu.VMEM((ATTN_WIDTH // LANES, tm, LANES), F32) for _ in range(3)],
        compiler_params=_cparams(("parallel",)),
        name="inproj",
    )(x, gain, w_in, cos, sin)
    n_lay = len(DILATIONS)
    as_rows = lambda a: a.reshape(bsz, seq, ATTN_WIDTH)
    q, k, v = ([as_rows(a) for a in outs[j * n_lay:(j + 1) * n_lay]] for j in range(3))
    return q, k, v, outs[3 * n_lay], outs[3 * n_lay + 1], outs[3 * n_lay + 2]


STATE_DILATION = max(DILATIONS)


def _query_order(idx, groups):
    run = BLK // groups
    return groups * (idx % run) + idx // run


def _attn_kernel(*refs, seq):
    n_lay = len(DILATIONS)
    srcs = refs[:3 * n_lay]
    o_ref, buf, acc_s, m_s, l_s, sem = refs[3 * n_lay:]
    b = pl.program_id(0)
    n_b = pl.num_programs(0)
    n_blocks = seq // BLK
    pair_w = 2 * HEAD_DIM
    n_pair = ATTN_WIDTH // pair_w

    def load_sweep(lay, bi, slot):
        return [pltpu.make_async_copy(srcs[a * n_lay + lay].at[bi], buf.at[slot, a, pl.ds(BLK, seq)],
                                      sem.at[slot]) for a in range(3)]

    @pl.when(b == 0)
    def _():
        buf[:, :, 0:BLK, :] = jnp.zeros((2, 3, BLK, ATTN_WIDTH), BF16)
        for c in load_sweep(0, 0, 0):
            c.start()

    row = lax.broadcasted_iota(jnp.int32, (BLK, BLK), 0)
    col = lax.broadcasted_iota(jnp.int32, (BLK, BLK), 1)
    lane = lax.broadcasted_iota(jnp.int32, (BLK, pair_w), 1)
    low_head = lane < HEAD_DIM
    unpermute = (row == _query_order(col, STATE_DILATION // DILATIONS[-1])).astype(BF16)

    for lay, d in enumerate(DILATIONS):
        slot = (b * n_lay + lay) % 2
        for c in load_sweep(lay, b, slot):
            c.wait()
        if lay + 1 < n_lay:
            for c in load_sweep(lay + 1, b, 1 - slot):
                c.start()
        else:
            @pl.when(b + 1 < n_b)
            def _():
                for c in load_sweep(0, b + 1, 1 - slot):
                    c.start()

        nb = seq // (d * BLK)
        has_prev = nb > 1
        first, final = lay == 0, lay == n_lay - 1
        kw = 2 * BLK if has_prev else BLK

        groups = STATE_DILATION // d
        run = BLK // groups
        q_pos = _query_order(row, groups)

        def block(j, carry, d=d, nb=nb, has_prev=has_prev, first=first, final=final, kw=kw,
                  slot=slot, groups=groups, run=run, q_pos=q_pos):
            r, n = j // nb, j % nb
            q_row0 = pl.multiple_of(BLK + j * BLK, BLK)
            k_row0 = pl.multiple_of(j * BLK, BLK) if has_prev else q_row0
            runs = [pl.ds(pl.multiple_of((d * a + r) * BLK + run * n, run), run)
                    for a in range(groups)]
            load = lambda ref, g: jnp.concatenate([ref[g, s, :] for s in runs], axis=0)

            def store(ref, g, val):
                for a, s in enumerate(runs):
                    ref[g, s, :] = val[a * run:(a + 1) * run]

            cur_ok = col <= q_pos
            if has_prev:
                prev_ok = jnp.logical_and(col >= q_pos, n > 0)
                allowed = jnp.concatenate([prev_ok, cur_ok], axis=1)
            else:
                allowed = cur_ok
            allowed2 = jnp.concatenate([allowed, allowed], axis=0)
            lanes = [slice(g * pair_w, (g + 1) * pair_w) for g in range(n_pair)]
            scores = []
            for ls in lanes:
                q2 = buf[slot, 0, pl.ds(q_row0, BLK), ls]
                q_both = jnp.concatenate([jnp.where(low_head, q2, 0),
                                          jnp.where(low_head, 0, q2)], axis=0)
                k2 = buf[slot, 1, pl.ds(k_row0, kw), ls]
                scores.append(lax.dot_general(q_both, k2, (((1,), (1,)), ((), ())),
                                              preferred_element_type=F32))
            probs, maxes, sums = [], [], []
            for s in scores:
                s = jnp.where(allowed2, s, NEG_INF)
                m_h = jnp.max(s, axis=-1, keepdims=True)
                p = jnp.exp(s - m_h)
                sums.append(jnp.sum(p, axis=-1, keepdims=True))
                maxes.append(m_h)
                probs.append(p.astype(BF16))
            pv = [jnp.dot(p, buf[slot, 2, pl.ds(k_row0, kw), ls], preferred_element_type=F32)
                  for p, ls in zip(probs, lanes)]
            for g, ls in enumerate(lanes):
                acc = jnp.where(low_head, pv[g][:BLK], pv[g][BLK:])
                m_new = jnp.where(low_head, maxes[g][:BLK], maxes[g][BLK:])
                l_new = jnp.where(low_head, sums[g][:BLK], sums[g][BLK:])
                if not first:
                    m_old = load(m_s, g)
                    m_tot = jnp.maximum(m_old, m_new)
                    a_old = jnp.exp(m_old - m_tot)
                    a_new = jnp.exp(m_new - m_tot)
                    acc = load(acc_s, g) * a_old + acc * a_new
                    l_new = load(l_s, g) * a_old + l_new * a_new
                    m_new = m_tot
                if final:
                    o_blk = jnp.dot(unpermute, (acc / l_new).astype(BF16), preferred_element_type=F32)
                    o_ref[0, pl.ds(pl.multiple_of(j * BLK, BLK), BLK), ls] = o_blk.astype(o_ref.dtype)
                else:
                    store(acc_s, g, acc)
                    store(m_s, g, m_new)
                    store(l_s, g, l_new)
            return carry

        lax.fori_loop(0, n_blocks, block, 0)


def _attention(q, k, v):
    bsz, seq, _ = q[0].shape
    assert all(seq % (d * BLK) == 0 for d in DILATIONS) and DILATIONS[-1] == 1
    n_pair = ATTN_WIDTH // (2 * HEAD_DIM)
    return pl.pallas_call(
        functools.partial(_attn_kernel, seq=seq),
        grid=(bsz,),
        in_specs=[pl.BlockSpec(memory_space=pl.ANY)] * (3 * len(DILATIONS)),
        out_specs=pl.BlockSpec((1, seq, ATTN_WIDTH), lambda b: (b, 0, 0)),
        out_shape=jax.ShapeDtypeStruct((bsz, seq, ATTN_WIDTH), BF16),
        scratch_shapes=[pltpu.VMEM((2, 3, BLK + seq, ATTN_WIDTH), BF16),
                        pltpu.VMEM((n_pair, seq, 2 * HEAD_DIM), F32),
                        pltpu.VMEM((n_pair, seq, 2 * HEAD_DIM), F32),
                        pltpu.VMEM((n_pair, seq, 2 * HEAD_DIM), F32),
                        pltpu.SemaphoreType.DMA((2,))],
        compiler_params=_cparams(("arbitrary",)),
        name="attention",
    )(*q, *k, *v)


CONV_HALO = 32
CONV_ROWS = 32


def _conv_kernel(halo_ref, cur_ref, w_ref, b_ref, g_ref, beta_ref, out_ref, ext_ref):
    ts = cur_ref.shape[1]
    i = pl.program_id(1)
    ext_ref[0, 0:CONV_HALO, :] = jnp.where(i > 0, halo_ref[0], 0.0)
    ext_ref[0, CONV_HALO:CONV_HALO + ts, :] = cur_ref[0]
    first = CONV_HALO - (CONV_K - 1)
    n_shift = ext_ref.shape[0]
    rows = CONV_HALO + ts
    for s in range(1, n_shift):
        ext_ref[s, 0:rows - s, :] = ext_ref[0, s:rows, :]
    bias = b_ref[...]
    gamma = g_ref[...]
    beta = beta_ref[...]
    for r0 in range(0, ts, CONV_ROWS):
        acc = jnp.zeros((CONV_ROWS, CONV_WIDTH), F32) + bias
        for j in range(CONV_K):
            lo = first + j + r0
            s, lo = lo % n_shift, lo - lo % n_shift
            acc = acc + w_ref[j:j + 1, :] * ext_ref[s, lo:lo + CONV_ROWS, :]
        mu = jnp.mean(acc, axis=-1, keepdims=True)
        cen = acc - mu
        var = jnp.mean(cen * cen, axis=-1, keepdims=True)
        y = cen * lax.rsqrt(var + EPS) * gamma + beta
        out_ref[0, r0:r0 + CONV_ROWS, :] = (y * _sigmoid(y)).astype(out_ref.dtype)


def _conv(u, w_dw, b_dw, ln_g, ln_b, ts):
    b, s, c = u.shape
    per = ts // CONV_HALO
    const = lambda bi, i: (0, 0)
    return pl.pallas_call(
        _conv_kernel,
        grid=(b, s // ts),
        in_specs=[
            pl.BlockSpec((1, CONV_HALO, c), lambda bi, i: (bi, jnp.maximum(i * per - 1, 0), 0)),
            pl.BlockSpec((1, ts, c), lambda bi, i: (bi, i, 0)),
            pl.BlockSpec((CONV_K, c), const),
            pl.BlockSpec((1, c), const),
            pl.BlockSpec((1, c), const),
            pl.BlockSpec((1, c), const),
        ],
        out_specs=pl.BlockSpec((1, ts, c), lambda bi, i: (bi, i, 0)),
        out_shape=jax.ShapeDtypeStruct((b, s, c), BF16),
        scratch_shapes=[pltpu.VMEM((SUBLANES, CONV_HALO + ts, c), F32)],
        compiler_params=_cparams(("parallel", "arbitrary")),
        name="conv",
    )(u, u, w_dw, b_dw, ln_g, ln_b)


ROUTE_COLS = 4
EXPERT_LANE0 = N_GROUPS


def _merge_kernel(o_ref, c_ref, ga_ref, gb_ref, x_ref, wa_ref, wc_ref, bc_ref, wo_ref,
                  gf_ref, wrh_ref, wrl_ref, br_ref,
                  x1_ref, h2_ref, route_ref, gate_ref, cnt_ref, tri_ref):
    tm = x_ref.shape[0]

    @pl.when(pl.program_id(0) == 0)
    def _():
        r = lax.broadcasted_iota(jnp.int32, (tm, tm), 0)
        c = lax.broadcasted_iota(jnp.int32, (tm, tm), 1)
        tri_ref[...] = (c < r).astype(BF16)

    y_a = jnp.dot(o_ref[...], wa_ref[...], preferred_element_type=F32)
    y_b = jnp.dot(c_ref[...], wc_ref[...], preferred_element_type=F32) + bc_ref[...]
    merged = _sigmoid(ga_ref[...]) * y_a + _sigmoid(gb_ref[...]) * y_b
    x1 = x_ref[...] + jnp.dot(merged.astype(BF16), wo_ref[...], preferred_element_type=F32)
    x1_ref[...] = x1
    h2 = _rms(x1, gf_ref[...])
    _store_row_tiles(h2_ref, h2)

    h_hi = h2.astype(BF16)
    h_lo = (h2 - h_hi.astype(F32)).astype(BF16)
    logits = (jnp.dot(h_hi, wrh_ref[...], preferred_element_type=F32)
              + jnp.dot(h_hi, wrl_ref[...], preferred_element_type=F32)
              + jnp.dot(h_lo, wrh_ref[...], preferred_element_type=F32)) + br_ref[...]

    lane = lax.broadcasted_iota(jnp.int32, (tm, LANES), 1)
    is_group = lane < N_GROUPS
    g_max = jnp.max(jnp.where(is_group, logits, -jnp.inf), axis=-1, keepdims=True)
    g_sel = jnp.min(jnp.where(jnp.logical_and(is_group, logits == g_max), lane, LANES),
                    axis=-1, keepdims=True)
    p_group = 1.0 / jnp.sum(jnp.where(is_group, jnp.exp(logits - g_max), 0.0),
                            axis=-1, keepdims=True)
    lo = EXPERT_LANE0 + EXPERTS_PER_GROUP * g_sel
    in_sel = jnp.logical_and(lane >= lo, lane < lo + EXPERTS_PER_GROUP)
    cand = jnp.where(in_sel, logits, -jnp.inf)
    v1 = jnp.max(cand, axis=-1, keepdims=True)
    i1 = jnp.min(jnp.where(jnp.logical_and(in_sel, cand == v1), lane, LANES), axis=-1, keepdims=True)
    rest = jnp.logical_and(in_sel, lane != i1)
    cand2 = jnp.where(rest, logits, -jnp.inf)
    v2 = jnp.max(cand2, axis=-1, keepdims=True)
    i2 = jnp.min(jnp.where(jnp.logical_and(rest, cand2 == v2), lane, LANES), axis=-1, keepdims=True)
    e2 = jnp.exp(v2 - v1)
    gate0 = p_group / (1.0 + e2)
    gate1 = p_group * e2 / (1.0 + e2)

    oh0 = (lane == i1)
    oh1 = (lane == i2)
    before0 = jnp.dot(tri_ref[...], oh0.astype(BF16), preferred_element_type=F32)
    before1 = jnp.dot(tri_ref[...], oh1.astype(BF16), preferred_element_type=F32)
    tot0 = jnp.sum(oh0.astype(F32), axis=0, keepdims=True)
    tot1 = jnp.sum(oh1.astype(F32), axis=0, keepdims=True)
    rank0 = jnp.sum(jnp.where(oh0, before0, 0.0), axis=-1, keepdims=True)
    rank1 = jnp.sum(jnp.where(oh1, before1 + tot0, 0.0), axis=-1, keepdims=True)

    rc = lax.broadcasted_iota(jnp.int32, (tm, ROUTE_COLS), 1)
    route = jnp.where(rc == 0, i1 - EXPERT_LANE0,
                      jnp.where(rc == 1, i2 - EXPERT_LANE0,
                                jnp.where(rc == 2, rank0.astype(jnp.int32), rank1.astype(jnp.int32))))
    route_ref[...] = route
    gc = lax.broadcasted_iota(jnp.int32, (tm, TOP_K), 1)
    gate_ref[...] = jnp.where(gc == 0, gate0, gate1)
    cnt = tot0 + tot1
    cnt_ref[0] = jnp.broadcast_to(pltpu.roll(cnt, LANES - EXPERT_LANE0, 1), (8, LANES))


def _merge(o, c, ga, gb, x, wa, wc, bc, wo, gf, wr_hi, wr_lo, br, tm):
    t = x.shape[0]
    nt = t // tm
    row = lambda i: (i, 0)
    const = lambda i: (0, 0)
    full = lambda a: pl.BlockSpec(a.shape, const)
    return pl.pallas_call(
        _merge_kernel,
        grid=(nt,),
        in_specs=[
            pl.BlockSpec((tm, ATTN_WIDTH), row),
            pl.BlockSpec((tm, CONV_WIDTH), row),
            pl.BlockSpec((tm, D_MODEL), row),
            pl.BlockSpec((tm, D_MODEL), row),
            pl.BlockSpec((tm, D_MODEL), row),
            full(wa), full(wc), full(bc), full(wo), full(gf), full(wr_hi), full(wr_lo), full(br),
        ],
        out_specs=(
            pl.BlockSpec((tm, D_MODEL), row),
            pl.BlockSpec((tm * ROW_TILE, LANES), row),
            pl.BlockSpec((tm, ROUTE_COLS), row),
            pl.BlockSpec((tm, TOP_K), row),
            pl.BlockSpec((1, 8, LANES), lambda i: (i, 0, 0)),
        ),
        out_shape=(
            jax.ShapeDtypeStruct((t, D_MODEL), F32),
            jax.ShapeDtypeStruct((t * ROW_TILE, LANES), F32),
            jax.ShapeDtypeStruct((t, ROUTE_COLS), jnp.int32),
            jax.ShapeDtypeStruct((t, TOP_K), F32),
            jax.ShapeDtypeStruct((nt, 8, LANES), F32),
        ),
        scratch_shapes=[pltpu.VMEM((tm, tm), BF16)],
        compiler_params=_cparams(("arbitrary",)),
        name="merge",
    )(o, c, ga, gb, x, wa, wc, bc, wo, gf, wr_hi, wr_lo, br)


NO_ASSIGNMENT = -1


def _inv_kernel(base_ref, route_ref, inv_ref):
    tm = route_ref.shape[0] // ROUTE_COLS
    i = pl.program_id(0)

    @pl.when(i == 0)
    def _():
        def fill(j, carry):
            inv_ref[j] = NO_ASSIGNMENT
            return carry
        lax.fori_loop(0, inv_ref.shape[0], fill, 0, unroll=8)

    def place(t, carry):
        for k in range(TOP_K):
            e = route_ref[ROUTE_COLS * t + k]
            row = base_ref[i * N_EXPERTS + e] + route_ref[ROUTE_COLS * t + TOP_K + k]
            inv_ref[row] = TOP_K * (i * tm + t) + k
        return carry

    lax.fori_loop(0, tm, place, 0, unroll=4)


def _inverse_map(tile_base, route_flat, n_rows, tm):
    n_tok = route_flat.shape[0] // ROUTE_COLS
    grid_spec = pltpu.PrefetchScalarGridSpec(
        num_scalar_prefetch=1,
        grid=(n_tok // tm,),
        in_specs=[pl.BlockSpec((ROUTE_COLS * tm,), lambda i, base: (i,), memory_space=pltpu.SMEM)],
        out_specs=pl.BlockSpec(memory_space=pltpu.SMEM),
    )
    return pl.pallas_call(
        _inv_kernel,
        grid_spec=grid_spec,
        out_shape=jax.ShapeDtypeStruct((n_rows,), jnp.int32),
        compiler_params=_cparams(("arbitrary",)),
        name="inverse_map",
    )(tile_base, route_flat)


TRASH_ROWS = 2 * MOE_BLOCK


def _moe_kernel(be_ref, inv_prev, inv_cur, inv_next, h2_hbm, wg_ref, wu_ref, wd_ref, y2_hbm,
                xbuf, obuf, wg_s, wu_s, wd_s, sem_in, sem_out, *, n_tok):
    b = pl.program_id(0)
    last = pl.num_programs(0) - 1
    slot = b % 2
    other = 1 - slot
    trash0 = TOP_K * n_tok

    def issue_gather(inv, s):
        for j in range(MOE_BLOCK):
            tok = jnp.maximum(inv[j], 0) >> 1
            src = pl.multiple_of(tok * ROW_TILE, ROW_TILE)
            pltpu.make_async_copy(h2_hbm.at[pl.ds(src, ROW_TILE)],
                                  xbuf.at[s, pl.ds(j * ROW_TILE, ROW_TILE)],
                                  sem_in.at[s]).start(priority=j % 2)

    def issue_scatter(inv, s, live):
        for j in range(MOE_BLOCK):
            a = inv[j]
            valid = jnp.logical_and(a >= 0, live)
            dst = jnp.where(valid, (a & 1) * n_tok + (a >> 1), trash0 + MOE_BLOCK * s + j)
            dst = pl.multiple_of(dst * ROW_TILE, ROW_TILE)
            pltpu.make_async_copy(obuf.at[s, pl.ds(j * ROW_TILE, ROW_TILE)],
                                  y2_hbm.at[pl.ds(dst, ROW_TILE)], sem_out.at[s]).start(priority=j % 2)

    def wait_gather(s):
        pltpu.make_async_copy(h2_hbm.at[pl.ds(0, MOE_BLOCK * ROW_TILE)], xbuf.at[s], sem_in.at[s]).wait()

    def wait_scatter(s):
        pltpu.make_async_copy(obuf.at[s], y2_hbm.at[pl.ds(0, MOE_BLOCK * ROW_TILE)], sem_out.at[s]).wait()

    @pl.when(b == 0)
    def _():
        obuf[...] = jnp.zeros_like(obuf)
        issue_scatter(inv_cur, 0, False)
        wait_scatter(0)
        issue_gather(inv_cur, 0)

    wait_gather(slot)

    @pl.when(b >= 1)
    def _():
        wait_scatter(slot)

    e = be_ref[b]
    e_prev = be_ref[jnp.maximum(b - 1, 0)]

    @pl.when(jnp.logical_or(b == 0, e != e_prev))
    def _():
        wg_s[...] = wg_ref[0].astype(BF16)
        wu_s[...] = wu_ref[0].astype(BF16)
        wd_s[...] = wd_ref[0].astype(BF16)

    issue_gather(inv_next, other)
    issue_scatter(inv_prev, other, b > 0)
    x = _load_row_tiles(xbuf.at[slot], MOE_BLOCK).astype(BF16)
    hg = jnp.dot(x, wg_s[...], preferred_element_type=F32)
    hu = jnp.dot(x, wu_s[...], preferred_element_type=F32)
    hb = (hg * _sigmoid(hg) * hu).astype(BF16)
    _store_row_tiles(obuf.at[slot], jnp.dot(hb, wd_s[...], preferred_element_type=F32))

    @pl.when(b == last)
    def _():
        issue_scatter(inv_cur, slot, True)
        wait_scatter(other)
        wait_scatter(slot)
        wait_gather(other)


def _moe(block_e, inv, h2, w_gate, w_up, w_down, layer):
    n_tok = h2.shape[0] // ROW_TILE
    n_blocks = inv.shape[0] // MOE_BLOCK
    w_idx = lambda b, be: (layer, be[b], 0, 0)
    inv_spec = lambda f: pl.BlockSpec((MOE_BLOCK,), f, memory_space=pltpu.SMEM)
    grid_spec = pltpu.PrefetchScalarGridSpec(
        num_scalar_prefetch=1,
        grid=(n_blocks,),
        in_specs=[
            inv_spec(lambda b, be: (jnp.maximum(b - 1, 0),)),
            inv_spec(lambda b, be: (b,)),
            inv_spec(lambda b, be: (jnp.minimum(b + 1, n_blocks - 1),)),
            pl.BlockSpec(memory_space=pl.ANY),
            pl.BlockSpec((None, 1, D_MODEL, EXPERT_FF), w_idx),
            pl.BlockSpec((None, 1, D_MODEL, EXPERT_FF), w_idx),
            pl.BlockSpec((None, 1, EXPERT_FF, D_MODEL), w_idx),
        ],
        out_specs=pl.BlockSpec(memory_space=pl.ANY),
        scratch_shapes=[pltpu.VMEM((2, MOE_BLOCK * ROW_TILE, LANES), F32),
                        pltpu.VMEM((2, MOE_BLOCK * ROW_TILE, LANES), F32),
                        pltpu.VMEM((D_MODEL, EXPERT_FF), BF16),
                        pltpu.VMEM((D_MODEL, EXPERT_FF), BF16),
                        pltpu.VMEM((EXPERT_FF, D_MODEL), BF16),
                        pltpu.SemaphoreType.DMA((2,)),
                        pltpu.SemaphoreType.DMA((2,))],
    )
    return pl.pallas_call(
        functools.partial(_moe_kernel, n_tok=n_tok),
        grid_spec=grid_spec,
        out_shape=jax.ShapeDtypeStruct(((TOP_K * n_tok + TRASH_ROWS) * ROW_TILE, LANES), F32),
        compiler_params=_cparams(("arbitrary",)),
        name="moe",
    )(block_e, inv, inv, inv, h2, w_gate, w_up, w_down)


def _ple_kernel(gate_ref, y0_ref, y1_ref, x1_ref, p_ref, gp_ref, wpg_ref, wpe_ref, gfin_ref,
                out_ref, *, final_norm):
    gate = gate_ref[...]
    tm = x1_ref.shape[0]
    x2 = (x1_ref[...] + gate[:, 0:1] * _load_row_tiles(y0_ref, tm)
          + gate[:, 1:2] * _load_row_tiles(y1_ref, tm))
    hp = _rms(x2, gp_ref[...]).astype(BF16)
    g_ple = _sigmoid(jnp.dot(hp, wpg_ref[...], preferred_element_type=F32))
    emb = jnp.dot(p_ref[...].astype(BF16), wpe_ref[...], preferred_element_type=F32)
    x3 = x2 + g_ple * emb
    if final_norm:
        x3 = _rms(x3, gfin_ref[...])
    out_ref[...] = x3


def _ple(gate, y2, x1, p, gp, wpg, wpe, gfin, tm, final_norm):
    t = x1.shape[0]
    nt = t // tm
    row = lambda i: (i, 0)
    const = lambda i: (0, 0)
    full = lambda a: pl.BlockSpec(a.shape, const)
    return pl.pallas_call(
        functools.partial(_ple_kernel, final_norm=final_norm),
        grid=(t // tm,),
        in_specs=[
            pl.BlockSpec((tm, TOP_K), row),
            pl.BlockSpec((tm * ROW_TILE, LANES), lambda i: (i, 0)),
            pl.BlockSpec((tm * ROW_TILE, LANES), lambda i: (nt + i, 0)),
            pl.BlockSpec((tm, D_MODEL), row),
            pl.BlockSpec((tm, PLE_DIM), row),
            full(gp), full(wpg), full(wpe), full(gfin),
        ],
        out_specs=pl.BlockSpec((tm, D_MODEL), row),
        out_shape=jax.ShapeDtypeStruct((t, D_MODEL), F32),
        compiler_params=_cparams(("parallel",)),
        name="ple",
    )(gate, y2, y2, x1, p, gp, wpg, wpe, gfin)


def _rope_tables(seq):
    inv = jnp.power(ROPE_THETA, -jnp.arange(0, HEAD_DIM, 2, dtype=F32) / HEAD_DIM)
    ang = jnp.arange(seq, dtype=F32)[:, None] * inv[None, :]
    cos, sin = jnp.cos(ang), jnp.sin(ang)
    cos_h = jnp.concatenate([cos, cos], axis=-1)
    sin_h = jnp.concatenate([-sin, sin], axis=-1)
    return jnp.tile(cos_h, (1, N_HEADS)), jnp.tile(sin_h, (1, N_HEADS))


def _dispatch_plan(counts, n_blocks):
    total = jnp.sum(counts, axis=0)
    padded = ((total + MOE_BLOCK - 1) // MOE_BLOCK) * MOE_BLOCK
    ends = jnp.cumsum(padded)
    pad_start = ends - padded
    tile_base = pad_start[None, :] + (jnp.cumsum(counts, axis=0) - counts)
    block_row0 = jnp.arange(n_blocks, dtype=jnp.int32) * MOE_BLOCK
    block_e = jnp.minimum(jnp.sum((ends[None, :] <= block_row0[:, None]).astype(jnp.int32), axis=1),
                          N_EXPERTS - 1)
    return tile_base.reshape(-1).astype(jnp.int32), block_e


def kernel(x, p, norm_mix, w_in, w_dw, b_dw, ln_conv_g, ln_conv_b, w_attn_out, w_conv_out,
           b_conv_out, w_out, norm_ffn, w_route_group, b_route_group, w_route_expert,
           b_route_expert, w_exp_gate, w_exp_up, w_exp_down, norm_ple, w_ple_proj, w_ple_gate,
           norm_final):
    bsz, seq, d = x.shape
    depth = w_in.shape[0]
    t = bsz * seq
    tm = 512
    n_assign = t * TOP_K
    n_rows = (-(-n_assign // MOE_BLOCK)) * MOE_BLOCK + N_EXPERTS * MOE_BLOCK
    cos, sin = _rope_tables(seq)
    row2 = lambda a: a.reshape(1, -1)

    xf = x.reshape(t, d)
    for i in range(depth):
        q, k, v, u, ga, gb = _inproj(xf, row2(norm_mix[i]), w_in[i].astype(BF16), cos, sin,
                                     bsz, seq, tm)
        o = _attention(q, k, v).reshape(t, ATTN_WIDTH)
        c = _conv(u.reshape(bsz, seq, CONV_WIDTH), w_dw[i], row2(b_dw[i]), row2(ln_conv_g[i]),
                  row2(ln_conv_b[i]), 256).reshape(t, CONV_WIDTH)

        w_r = jnp.zeros((d, LANES), F32)
        w_r = w_r.at[:, :N_GROUPS].set(w_route_group[i])
        w_r = w_r.at[:, EXPERT_LANE0:EXPERT_LANE0 + N_EXPERTS].set(w_route_expert[i])
        w_r_hi = w_r.astype(BF16)
        w_r_lo = (w_r - w_r_hi.astype(F32)).astype(BF16)
        b_r = jnp.zeros((1, LANES), F32)
        b_r = b_r.at[0, :N_GROUPS].set(b_route_group[i])
        b_r = b_r.at[0, EXPERT_LANE0:EXPERT_LANE0 + N_EXPERTS].set(b_route_expert[i])

        x1, h2, route, gate, cnt = _merge(
            o, c, ga, gb, xf, w_attn_out[i].astype(BF16), w_conv_out[i].astype(BF16),
            row2(b_conv_out[i]), w_out[i].astype(BF16), row2(norm_ffn[i]), w_r_hi, w_r_lo, b_r, tm)

        counts = cnt[:, 0, :N_EXPERTS].astype(jnp.int32)
        tile_base, block_e = _dispatch_plan(counts, n_rows // MOE_BLOCK)
        inv = _inverse_map(tile_base, route.reshape(-1), n_rows, tm)
        y2 = _moe(block_e, inv, h2, w_exp_gate, w_exp_up, w_exp_down, i)
        xf = _ple(gate, y2, x1, p[i].reshape(t, PLE_DIM), row2(norm_ple[i]),
                  w_ple_gate[i].astype(BF16), w_ple_proj[i].astype(BF16), row2(norm_final),
                  tm, final_norm=(i == depth - 1))
    return xf.reshape(bsz, seq, d)
```

```python
import functools

import jax
import jax.numpy as jnp
from jax import lax
from jax.experimental import pallas as pl
from jax.experimental.pallas import tpu as pltpu

D_MODEL = 1024
N_HEADS = 8
HEAD_DIM = 64
ATTN_WIDTH = N_HEADS * HEAD_DIM
CONV_WIDTH = 512
CONV_K = 31
DILATIONS = (16, 4, 1)
BLK = 128
ROPE_THETA = 10000.0
N_GROUPS = 4
EXPERTS_PER_GROUP = 8
N_EXPERTS = N_GROUPS * EXPERTS_PER_GROUP
EXPERT_FF = 512
TOP_K = 2
MOE_BLOCK = 128
PLE_DIM = 256
EPS = 1e-6
NEG_INF = -1e30

LANES = 128
SUBLANES = 8
ROW_TILE = D_MODEL // LANES
VMEM_LIMIT = 56 * 1024 * 1024

F32 = jnp.float32
BF16 = jnp.bfloat16


def _cparams(sem):
    return pltpu.CompilerParams(dimension_semantics=sem, vmem_limit_bytes=VMEM_LIMIT)


def _rms(x, g):
    return x * lax.rsqrt(jnp.mean(x * x, axis=-1, keepdims=True) + EPS) * g


def _sigmoid(x):
    return 1.0 / (1.0 + jnp.exp(-x))


def _store_row_tiles(ref, x):
    rows = x.shape[0]
    for s in range(ROW_TILE):
        ref[pl.ds(s, rows, stride=ROW_TILE), :] = x[:, s * LANES:(s + 1) * LANES]


def _load_row_tiles(ref, rows):
    return jnp.concatenate([ref[pl.ds(s, rows, stride=ROW_TILE), :] for s in range(ROW_TILE)], axis=1)


def _inproj_kernel(x_ref, g_ref, w_ref, cos_ref, sin_ref, *out_refs):
    n_lay = len(DILATIONS)
    q_refs, k_refs, v_refs = (out_refs[a * n_lay:(a + 1) * n_lay] for a in range(3))
    u_ref, ga_ref, gb_ref = out_refs[3 * n_lay:3 * n_lay + 3]
    slab_refs = out_refs[3 * n_lay + 3:]
    tm = x_ref.shape[0]
    n_slab = ATTN_WIDTH // LANES

    def put(ref, d, r, row0, g, val):
        cols = slice(g * LANES, (g + 1) * LANES)
        if d == 1:
            ref[row0:row0 + val.shape[0], cols] = val.astype(BF16)
        else:
            ref[0, r, row0:row0 + val.shape[0], cols] = val.astype(BF16)

    def emit(t, refs, slabs, is_query):
        for g in range(n_slab):
            slabs[g] = t[:, g * LANES:(g + 1) * LANES]
        for d, ref in zip(DILATIONS, refs):
            groups = STATE_DILATION // d
            if d == 1 and not is_query:
                ref[...] = t.astype(BF16)
            elif not is_query or groups == 1:
                for r in range(d):
                    for g in range(n_slab):
                        put(ref, d, r, 0, g, slabs[g, pl.ds(r, tm // d, stride=d), :])
            else:
                run = BLK // groups
                for r in range(d):
                    for bl in range(tm // (d * BLK)):
                        for a in range(groups):
                            src = pl.ds(d * BLK * bl + d * a + r, run, stride=STATE_DILATION)
                            for g in range(n_slab):
                                put(ref, d, r, bl * BLK + a * run, g, slabs[g, src, :])

    h = _rms(x_ref[...], g_ref[...]).astype(BF16)

    def proj(c0, width):
        return jnp.dot(h, w_ref[:, c0:c0 + width], preferred_element_type=F32)

    cos = cos_ref[...]
    sin = sin_ref[...]
    lane = lax.broadcasted_iota(jnp.int32, (tm, ATTN_WIDTH), 1)
    first_half = (lane % HEAD_DIM) < (HEAD_DIM // 2)

    def rope(t):
        partner = jnp.where(first_half,
                            pltpu.roll(t, ATTN_WIDTH - HEAD_DIM // 2, 1),
                            pltpu.roll(t, HEAD_DIM // 2, 1))
        return t * cos + partner * sin

    emit(rope(proj(0, ATTN_WIDTH)) * (HEAD_DIM ** -0.5), q_refs, slab_refs[0], True)
    emit(rope(proj(ATTN_WIDTH, ATTN_WIDTH)), k_refs, slab_refs[1], False)
    emit(proj(2 * ATTN_WIDTH, ATTN_WIDTH), v_refs, slab_refs[2], False)
    c0 = 3 * ATTN_WIDTH
    cu = proj(c0, CONV_WIDTH)
    cg = proj(c0 + CONV_WIDTH, CONV_WIDTH)
    u_ref[...] = cu * _sigmoid(cg)
    c0 += 2 * CONV_WIDTH
    ga_ref[...] = proj(c0, D_MODEL)
    gb_ref[...] = proj(c0 + D_MODEL, D_MODEL)


def _inproj(x, gain, w_in, cos, sin, bsz, seq, tm):
    t = x.shape[0]
    n_pos = seq // tm
    row = lambda i: (i, 0)
    const = lambda i: (0, 0)
    qkv_shapes, qkv_specs = [], []
    for d in DILATIONS:
        if d == 1:
            qkv_shapes.append(jax.ShapeDtypeStruct((t, ATTN_WIDTH), BF16))
            qkv_specs.append(pl.BlockSpec((tm, ATTN_WIDTH), row))
        else:
            qkv_shapes.append(jax.ShapeDtypeStruct((bsz, d, seq // d, ATTN_WIDTH), BF16))
            qkv_specs.append(pl.BlockSpec((1, d, tm // d, ATTN_WIDTH),
                                          lambda i: (i // n_pos, 0, i % n_pos, 0)))
    out_shape = tuple(qkv_shapes * 3) + (
        jax.ShapeDtypeStruct((t, CONV_WIDTH), F32),
        jax.ShapeDtypeStruct((t, D_MODEL), F32),
        jax.ShapeDtypeStruct((t, D_MODEL), F32),
    )
    out_specs = tuple(qkv_specs * 3) + (
        pl.BlockSpec((tm, CONV_WIDTH), row),
        pl.BlockSpec((tm, D_MODEL), row),
        pl.BlockSpec((tm, D_MODEL), row),
    )
    outs = pl.pallas_call(
        _inproj_kernel,
        grid=(t // tm,),
        in_specs=[
            pl.BlockSpec((tm, D_MODEL), row),
            pl.BlockSpec((1, D_MODEL), const),
            pl.BlockSpec(w_in.shape, const),
            pl.BlockSpec((tm, ATTN_WIDTH), lambda i: (i % n_pos, 0)),
            pl.BlockSpec((tm, ATTN_WIDTH), lambda i: (i % n_pos, 0)),
        ],
        out_specs=out_specs,
        out_shape=out_shape,
        scratch_shapes=[pltpu.VMEM((ATTN_WIDTH // LANES, tm, LANES), F32) for _ in range(3)],
        compiler_params=_cparams(("parallel",)),
        name="inproj",
    )(x, gain, w_in, cos, sin)
    n_lay = len(DILATIONS)
    as_rows = lambda a: a.reshape(bsz, seq, ATTN_WIDTH)
    q, k, v = ([as_rows(a) for a in outs[j * n_lay:(j + 1) * n_lay]] for j in range(3))
    return q, k, v, outs[3 * n_lay], outs[3 * n_lay + 1], outs[3 * n_lay + 2]


STATE_DILATION = max(DILATIONS)


def _query_order(idx, groups):
    run = BLK // groups
    return groups * (idx % run) + idx // run


def _attn_kernel(*refs, seq):
    n_lay = len(DILATIONS)
    srcs = refs[:3 * n_lay]
    o_ref, buf, acc_s, m_s, l_s, sem = refs[3 * n_lay:]
    b = pl.program_id(0)
    n_b = pl.num_programs(0)
    n_blocks = seq // BLK
    pair_w = 2 * HEAD_DIM
    n_pair = ATTN_WIDTH // pair_w

    def load_sweep(lay, bi, slot):
        return [pltpu.make_async_copy(srcs[a * n_lay + lay].at[bi], buf.at[slot, a, pl.ds(BLK, seq)],
                                      sem.at[slot]) for a in range(3)]

    @pl.when(b == 0)
    def _():
        buf[:, :, 0:BLK, :] = jnp.zeros((2, 3, BLK, ATTN_WIDTH), BF16)
        for c in load_sweep(0, 0, 0):
            c.start()

    row = lax.broadcasted_iota(jnp.int32, (BLK, BLK), 0)
    col = lax.broadcasted_iota(jnp.int32, (BLK, BLK), 1)
    lane = lax.broadcasted_iota(jnp.int32, (BLK, pair_w), 1)
    low_head = lane < HEAD_DIM
    unpermute = (row == _query_order(col, STATE_DILATION // DILATIONS[-1])).astype(BF16)

    for lay, d in enumerate(DILATIONS):
        slot = (b * n_lay + lay) % 2
        for c in load_sweep(lay, b, slot):
            c.wait()
        if lay + 1 < n_lay:
            for c in load_sweep(lay + 1, b, 1 - slot):
                c.start()
        else:
            @pl.when(b + 1 < n_b)
            def _():
                for c in load_sweep(0, b + 1, 1 - slot):
                    c.start()

        nb = seq // (d * BLK)
        has_prev = nb > 1
        first, final = lay == 0, lay == n_lay - 1
        kw = 2 * BLK if has_prev else BLK

        groups = STATE_DILATION // d
        run = BLK // groups
        q_pos = _query_order(row, groups)

        def block(j, carry, d=d, nb=nb, has_prev=has_prev, first=first, final=final, kw=kw,
                  slot=slot, groups=groups, run=run, q_pos=q_pos):
            r, n = j // nb, j % nb
            q_row0 = pl.multiple_of(BLK + j * BLK, BLK)
            k_row0 = pl.multiple_of(j * BLK, BLK) if has_prev else q_row0
            runs = [pl.ds(pl.multiple_of((d * a + r) * BLK + run * n, run), run)
                    for a in range(groups)]
            load = lambda ref, g: jnp.concatenate([ref[g, s, :] for s in runs], axis=0)

            def store(ref, g, val):
                for a, s in enumerate(runs):
                    ref[g, s, :] = val[a * run:(a + 1) * run]

            cur_ok = col <= q_pos
            if has_prev:
                prev_ok = jnp.logical_and(col >= q_pos, n > 0)
                allowed = jnp.concatenate([prev_ok, cur_ok], axis=1)
            else:
                allowed = cur_ok
            allowed2 = jnp.concatenate([allowed, allowed], axis=0)
            lanes = [slice(g * pair_w, (g + 1) * pair_w) for g in range(n_pair)]
            scores = []
            for ls in lanes:
                q2 = buf[slot, 0, pl.ds(q_row0, BLK), ls]
                q_both = jnp.concatenate([jnp.where(low_head, q2, 0),
                                          jnp.where(low_head, 0, q2)], axis=0)
                k2 = buf[slot, 1, pl.ds(k_row0, kw), ls]
                scores.append(lax.dot_general(q_both, k2, (((1,), (1,)), ((), ())),
                                              preferred_element_type=F32))
            probs, maxes, sums = [], [], []
            for s in scores:
                s = jnp.where(allowed2, s, NEG_INF)
                m_h = jnp.max(s, axis=-1, keepdims=True)
                p = jnp.exp(s - m_h)
                sums.append(jnp.sum(p, axis=-1, keepdims=True))
                maxes.append(m_h)
                probs.append(p.astype(BF16))
            pv = [jnp.dot(p, buf[slot, 2, pl.ds(k_row0, kw), ls], preferred_element_type=F32)
                  for p, ls in zip(probs, lanes)]
            for g, ls in enumerate(lanes):
                acc = jnp.where(low_head, pv[g][:BLK], pv[g][BLK:])
                m_new = jnp.where(low_head, maxes[g][:BLK], maxes[g][BLK:])
                l_new = jnp.where(low_head, sums[g][:BLK], sums[g][BLK:])
                if not first:
                    m_old = load(m_s, g)
                    m_tot = jnp.maximum(m_old, m_new)
                    a_old = jnp.exp(m_old - m_tot)
                    a_new = jnp.exp(m_new - m_tot)
                    acc = load(acc_s, g) * a_old + acc * a_new
                    l_new = load(l_s, g) * a_old + l_new * a_new
                    m_new = m_tot
                if final:
                    o_blk = jnp.dot(unpermute, (acc / l_new).astype(BF16), preferred_element_type=F32)
                    o_ref[0, pl.ds(pl.multiple_of(j * BLK, BLK), BLK), ls] = o_blk.astype(o_ref.dtype)
                else:
                    store(acc_s, g, acc)
                    store(m_s, g, m_new)
                    store(l_s, g, l_new)
            return carry

        lax.fori_loop(0, n_blocks, block, 0)


def _attention(q, k, v):
    bsz, seq, _ = q[0].shape
    assert all(seq % (d * BLK) == 0 for d in DILATIONS) and DILATIONS[-1] == 1
    n_pair = ATTN_WIDTH // (2 * HEAD_DIM)
    return pl.pallas_call(
        functools.partial(_attn_kernel, seq=seq),
        grid=(bsz,),
        in_specs=[pl.BlockSpec(memory_space=pl.ANY)] * (3 * len(DILATIONS)),
        out_specs=pl.BlockSpec((1, seq, ATTN_WIDTH), lambda b: (b, 0, 0)),
        out_shape=jax.ShapeDtypeStruct((bsz, seq, ATTN_WIDTH), BF16),
        scratch_shapes=[pltpu.VMEM((2, 3, BLK + seq, ATTN_WIDTH), BF16),
                        pltpu.VMEM((n_pair, seq, 2 * HEAD_DIM), F32),
                        pltpu.VMEM((n_pair, seq, 2 * HEAD_DIM), F32),
                        pltpu.VMEM((n_pair, seq, 2 * HEAD_DIM), F32),
                        pltpu.SemaphoreType.DMA((2,))],
        compiler_params=_cparams(("arbitrary",)),
        name="attention",
    )(*q, *k, *v)


CONV_HALO = 32
CONV_ROWS = 32


def _conv_kernel(halo_ref, cur_ref, w_ref, b_ref, g_ref, beta_ref, out_ref, ext_ref):
    ts = cur_ref.shape[1]
    i = pl.program_id(1)
    ext_ref[0, 0:CONV_HALO, :] = jnp.where(i > 0, halo_ref[0], 0.0)
    ext_ref[0, CONV_HALO:CONV_HALO + ts, :] = cur_ref[0]
    first = CONV_HALO - (CONV_K - 1)
    n_shift = ext_ref.shape[0]
    rows = CONV_HALO + ts
    for s in range(1, n_shift):
        ext_ref[s, 0:rows - s, :] = ext_ref[0, s:rows, :]
    bias = b_ref[...]
    gamma = g_ref[...]
    beta = beta_ref[...]
    for r0 in range(0, ts, CONV_ROWS):
        acc = jnp.zeros((CONV_ROWS, CONV_WIDTH), F32) + bias
        for j in range(CONV_K):
            lo = first + j + r0
            s, lo = lo % n_shift, lo - lo % n_shift
            acc = acc + w_ref[j:j + 1, :] * ext_ref[s, lo:lo + CONV_ROWS, :]
        mu = jnp.mean(acc, axis=-1, keepdims=True)
        cen = acc - mu
        var = jnp.mean(cen * cen, axis=-1, keepdims=True)
        y = cen * lax.rsqrt(var + EPS) * gamma + beta
        out_ref[0, r0:r0 + CONV_ROWS, :] = (y * _sigmoid(y)).astype(out_ref.dtype)


def _conv(u, w_dw, b_dw, ln_g, ln_b, ts):
    b, s, c = u.shape
    per = ts // CONV_HALO
    const = lambda bi, i: (0, 0)
    return pl.pallas_call(
        _conv_kernel,
        grid=(b, s // ts),
        in_specs=[
            pl.BlockSpec((1, CONV_HALO, c), lambda bi, i: (bi, jnp.maximum(i * per - 1, 0), 0)),
            pl.BlockSpec((1, ts, c), lambda bi, i: (bi, i, 0)),
            pl.BlockSpec((CONV_K, c), const),
            pl.BlockSpec((1, c), const),
            pl.BlockSpec((1, c), const),
            pl.BlockSpec((1, c), const),
        ],
        out_specs=pl.BlockSpec((1, ts, c), lambda bi, i: (bi, i, 0)),
        out_shape=jax.ShapeDtypeStruct((b, s, c), BF16),
        scratch_shapes=[pltpu.VMEM((SUBLANES, CONV_HALO + ts, c), F32)],
        compiler_params=_cparams(("parallel", "arbitrary")),
        name="conv",
    )(u, u, w_dw, b_dw, ln_g, ln_b)


ROUTE_COLS = 4
EXPERT_LANE0 = N_GROUPS


def _merge_kernel(o_ref, c_ref, ga_ref, gb_ref, x_ref, wa_ref, wc_ref, bc_ref, wo_ref,
                  gf_ref, wrh_ref, wrl_ref, br_ref,
                  x1_ref, h2_ref, route_ref, gate_ref, cnt_ref, tri_ref):
    tm = x_ref.shape[0]

    @pl.when(pl.program_id(0) == 0)
    def _():
        r = lax.broadcasted_iota(jnp.int32, (tm, tm), 0)
        c = lax.broadcasted_iota(jnp.int32, (tm, tm), 1)
        tri_ref[...] = (c < r).astype(BF16)

    y_a = jnp.dot(o_ref[...], wa_ref[...], preferred_element_type=F32)
    y_b = jnp.dot(c_ref[...], wc_ref[...], preferred_element_type=F32) + bc_ref[...]
    merged = _sigmoid(ga_ref[...]) * y_a + _sigmoid(gb_ref[...]) * y_b
    x1 = x_ref[...] + jnp.dot(merged.astype(BF16), wo_ref[...], preferred_element_type=F32)
    x1_ref[...] = x1
    h2 = _rms(x1, gf_ref[...])
    _store_row_tiles(h2_ref, h2)

    h_hi = h2.astype(BF16)
    h_lo = (h2 - h_hi.astype(F32)).astype(BF16)
    logits = (jnp.dot(h_hi, wrh_ref[...], preferred_element_type=F32)
              + jnp.dot(h_hi, wrl_ref[...], preferred_element_type=F32)
              + jnp.dot(h_lo, wrh_ref[...], preferred_element_type=F32)) + br_ref[...]

    lane = lax.broadcasted_iota(jnp.int32, (tm, LANES), 1)
    is_group = lane < N_GROUPS
    g_max = jnp.max(jnp.where(is_group, logits, -jnp.inf), axis=-1, keepdims=True)
    g_sel = jnp.min(jnp.where(jnp.logical_and(is_group, logits == g_max), lane, LANES),
                    axis=-1, keepdims=True)
    p_group = 1.0 / jnp.sum(jnp.where(is_group, jnp.exp(logits - g_max), 0.0),
                            axis=-1, keepdims=True)
    lo = EXPERT_LANE0 + EXPERTS_PER_GROUP * g_sel
    in_sel = jnp.logical_and(lane >= lo, lane < lo + EXPERTS_PER_GROUP)
    cand = jnp.where(in_sel, logits, -jnp.inf)
    v1 = jnp.max(cand, axis=-1, keepdims=True)
    i1 = jnp.min(jnp.where(jnp.logical_and(in_sel, cand == v1), lane, LANES), axis=-1, keepdims=True)
    rest = jnp.logical_and(in_sel, lane != i1)
    cand2 = jnp.where(rest, logits, -jnp.inf)
    v2 = jnp.max(cand2, axis=-1, keepdims=True)
    i2 = jnp.min(jnp.where(jnp.logical_and(rest, cand2 == v2), lane, LANES), axis=-1, keepdims=True)
    e2 = jnp.exp(v2 - v1)
    gate0 = p_group / (1.0 + e2)
    gate1 = p_group * e2 / (1.0 + e2)

    oh0 = (lane == i1)
    oh1 = (lane == i2)
    before0 = jnp.dot(tri_ref[...], oh0.astype(BF16), preferred_element_type=F32)
    before1 = jnp.dot(tri_ref[...], oh1.astype(BF16), preferred_element_type=F32)
    tot0 = jnp.sum(oh0.astype(F32), axis=0, keepdims=True)
    tot1 = jnp.sum(oh1.astype(F32), axis=0, keepdims=True)
    cnt = tot0 + tot1
    row_lane = lax.broadcasted_iota(jnp.int32, (1, LANES), 1)
    first_row = cnt
    shift = 1
    while shift < LANES:
        first_row = first_row + jnp.where(row_lane >= shift, pltpu.roll(first_row, shift, 1), 0.0)
        shift *= 2
    first_row = first_row - cnt
    pos0 = jnp.sum(jnp.where(oh0, first_row + before0, 0.0), axis=-1, keepdims=True)
    pos1 = jnp.sum(jnp.where(oh1, first_row + tot0 + before1, 0.0), axis=-1, keepdims=True)

    rc = lax.broadcasted_iota(jnp.int32, (tm, ROUTE_COLS), 1)
    route = jnp.where(rc == 0, i1 - EXPERT_LANE0,
                      jnp.where(rc == 1, i2 - EXPERT_LANE0,
                                jnp.where(rc == 2, pos0.astype(jnp.int32), pos1.astype(jnp.int32))))
    route_ref[...] = route
    gc = lax.broadcasted_iota(jnp.int32, (tm, TOP_K), 1)
    gate_ref[...] = jnp.where(gc == 0, gate0, gate1)
    cnt_ref[0] = jnp.broadcast_to(pltpu.roll(cnt, LANES - EXPERT_LANE0, 1), (8, LANES))


def _merge(o, c, ga, gb, x, wa, wc, bc, wo, gf, wr_hi, wr_lo, br, tm):
    t = x.shape[0]
    nt = t // tm
    row = lambda i: (i, 0)
    const = lambda i: (0, 0)
    full = lambda a: pl.BlockSpec(a.shape, const)
    return pl.pallas_call(
        _merge_kernel,
        grid=(nt,),
        in_specs=[
            pl.BlockSpec((tm, ATTN_WIDTH), row),
            pl.BlockSpec((tm, CONV_WIDTH), row),
            pl.BlockSpec((tm, D_MODEL), row),
            pl.BlockSpec((tm, D_MODEL), row),
            pl.BlockSpec((tm, D_MODEL), row),
            full(wa), full(wc), full(bc), full(wo), full(gf), full(wr_hi), full(wr_lo), full(br),
        ],
        out_specs=(
            pl.BlockSpec((tm, D_MODEL), row),
            pl.BlockSpec((tm * ROW_TILE, LANES), row),
            pl.BlockSpec((tm, ROUTE_COLS), row),
            pl.BlockSpec((tm, TOP_K), row),
            pl.BlockSpec((1, 8, LANES), lambda i: (i, 0, 0)),
        ),
        out_shape=(
            jax.ShapeDtypeStruct((t, D_MODEL), F32),
            jax.ShapeDtypeStruct((t * ROW_TILE, LANES), F32),
            jax.ShapeDtypeStruct((t, ROUTE_COLS), jnp.int32),
            jax.ShapeDtypeStruct((t, TOP_K), F32),
            jax.ShapeDtypeStruct((nt, 8, LANES), F32),
        ),
        scratch_shapes=[pltpu.VMEM((tm, tm), BF16)],
        compiler_params=_cparams(("arbitrary",)),
        name="merge",
    )(o, c, ga, gb, x, wa, wc, bc, wo, gf, wr_hi, wr_lo, br)


def _sort_kernel(route_ref, h_ref, xs_ref):
    tm = h_ref.shape[0] // ROW_TILE

    def place(t, carry):
        row = h_ref[pl.ds(pl.multiple_of(t * ROW_TILE, ROW_TILE), ROW_TILE), :]
        for k in range(TOP_K):
            pos = route_ref[ROUTE_COLS * t + TOP_K + k]
            xs_ref[pl.ds(pl.multiple_of(pos * ROW_TILE, ROW_TILE), ROW_TILE), :] = row
        return carry

    lax.fori_loop(0, tm, place, 0, unroll=8)


def _tile_sort(route_flat, h2, tm):
    nt = h2.shape[0] // (tm * ROW_TILE)
    return pl.pallas_call(
        _sort_kernel,
        grid=(nt,),
        in_specs=[pl.BlockSpec((ROUTE_COLS * tm,), lambda i: (i,), memory_space=pltpu.SMEM),
                  pl.BlockSpec((tm * ROW_TILE, LANES), lambda i: (i, 0))],
        out_specs=pl.BlockSpec((TOP_K * tm * ROW_TILE, LANES), lambda i: (i, 0)),
        out_shape=jax.ShapeDtypeStruct((TOP_K * h2.shape[0], LANES), F32),
        compiler_params=_cparams(("parallel",)),
        name="tile_sort",
    )(route_flat, h2)


def _moe_kernel(be_ref, bj_ref, ilo_ref, ihi_ref, valid_ref, cum_ref, cnt_ref, off_ref,
                xs_hbm, wg_ref, wu_ref, wd_ref, ys_hbm, trash_hbm,
                xbuf, obuf, wg_s, wu_s, wd_s, sem_in, sem_out, *, tile_rows):
    b = pl.program_id(0)
    last = pl.num_programs(0) - 1
    slot = b % 2
    other = 1 - slot
    rows8 = lambda r, n: pl.ds(pl.multiple_of(r * ROW_TILE, ROW_TILE), n * ROW_TILE)

    def for_runs(bb, fn):
        e = be_ref[bb]
        lo_blk = bj_ref[bb] * MOE_BLOCK

        def body(i, carry):
            c = cum_ref[i * N_EXPERTS + e]
            lo = jnp.maximum(c, lo_blk)
            hi = jnp.minimum(c + cnt_ref[i * N_EXPERTS + e], lo_blk + MOE_BLOCK)
            fn(i * tile_rows + off_ref[i * N_EXPERTS + e] + (lo - c), lo - lo_blk,
               jnp.maximum(hi - lo, 0))
            return carry

        lax.fori_loop(ilo_ref[bb], ihi_ref[bb] + 1, body, 0)

    def for_chunks(length, fn):
        for bit in reversed(range(MOE_BLOCK.bit_length())):
            size = 1 << bit

            @pl.when((length & size) != 0)
            def _():
                fn(length & ~(2 * size - 1), size)

    def gather(bb, s):
        def run(src, dst, length):
            for_chunks(length, lambda o, n: pltpu.make_async_copy(
                xs_hbm.at[rows8(src + o, n)], xbuf.at[s, rows8(dst + o, n)], sem_in.at[s]).start())
        for_runs(bb, run)
        v = valid_ref[bb]
        for_chunks(MOE_BLOCK - v, lambda o, n: pltpu.make_async_copy(
            xs_hbm.at[rows8(o, n)], xbuf.at[s, rows8(v + o, n)], sem_in.at[s]).start())

    def scatter(bb, s):
        def run(src, dst, length):
            for_chunks(length, lambda o, n: pltpu.make_async_copy(
                obuf.at[s, rows8(dst + o, n)], ys_hbm.at[rows8(src + o, n)], sem_out.at[s]).start())
        for_runs(bb, run)
        v = valid_ref[bb]
        for_chunks(MOE_BLOCK - v, lambda o, n: pltpu.make_async_copy(
            obuf.at[s, rows8(v + o, n)], trash_hbm.at[rows8(s * MOE_BLOCK + v + o, n)],
            sem_out.at[s]).start())

    def wait_gather(s):
        pltpu.make_async_copy(xs_hbm.at[rows8(0, MOE_BLOCK)], xbuf.at[s], sem_in.at[s]).wait()

    def wait_scatter(s):
        pltpu.make_async_copy(obuf.at[s], ys_hbm.at[rows8(0, MOE_BLOCK)], sem_out.at[s]).wait()

    @pl.when(b == 0)
    def _():
        obuf[...] = jnp.zeros_like(obuf)
        for s in range(2):
            init = pltpu.make_async_copy(obuf.at[s], trash_hbm.at[rows8(s * MOE_BLOCK, MOE_BLOCK)],
                                         sem_out.at[s])
            init.start()
            init.wait()
        gather(0, 0)

    wait_gather(slot)

    @pl.when(b >= 2)
    def _():
        wait_scatter(slot)

    e = be_ref[b]
    e_prev = be_ref[jnp.maximum(b - 1, 0)]

    @pl.when(jnp.logical_or(b == 0, e != e_prev))
    def _():
        wg_s[...] = wg_ref[0].astype(BF16)
        wu_s[...] = wu_ref[0].astype(BF16)
        wd_s[...] = wd_ref[0].astype(BF16)

    @pl.when(b < last)
    def _():
        gather(b + 1, other)

    @pl.when(b >= 1)
    def _():
        scatter(b - 1, other)

    x = _load_row_tiles(xbuf.at[slot], MOE_BLOCK).astype(BF16)
    hg = jnp.dot(x, wg_s[...], preferred_element_type=F32)
    hu = jnp.dot(x, wu_s[...], preferred_element_type=F32)
    hb = (hg * _sigmoid(hg) * hu).astype(BF16)
    _store_row_tiles(obuf.at[slot], jnp.dot(hb, wd_s[...], preferred_element_type=F32))

    @pl.when(b == last)
    def _():
        scatter(b, slot)
        wait_scatter(other)
        wait_scatter(slot)


def _moe(plan, xs, w_gate, w_up, w_down, layer, n_blocks, tile_rows):
    n_pre = len(plan)
    w_idx = lambda b, be, *_: (layer, be[b], 0, 0)
    grid_spec = pltpu.PrefetchScalarGridSpec(
        num_scalar_prefetch=n_pre,
        grid=(n_blocks,),
        in_specs=[
            pl.BlockSpec(memory_space=pl.ANY),
            pl.BlockSpec((None, 1, D_MODEL, EXPERT_FF), w_idx),
            pl.BlockSpec((None, 1, D_MODEL, EXPERT_FF), w_idx),
            pl.BlockSpec((None, 1, EXPERT_FF, D_MODEL), w_idx),
        ],
        out_specs=(pl.BlockSpec(memory_space=pl.ANY), pl.BlockSpec(memory_space=pl.ANY)),
        scratch_shapes=[pltpu.VMEM((2, MOE_BLOCK * ROW_TILE, LANES), F32),
                        pltpu.VMEM((2, MOE_BLOCK * ROW_TILE, LANES), F32),
                        pltpu.VMEM((D_MODEL, EXPERT_FF), BF16),
                        pltpu.VMEM((D_MODEL, EXPERT_FF), BF16),
                        pltpu.VMEM((EXPERT_FF, D_MODEL), BF16),
                        pltpu.SemaphoreType.DMA((2,)),
                        pltpu.SemaphoreType.DMA((2,))],
    )
    ys, _ = pl.pallas_call(
        functools.partial(_moe_kernel, tile_rows=tile_rows),
        grid_spec=grid_spec,
        out_shape=(jax.ShapeDtypeStruct(xs.shape, F32),
                   jax.ShapeDtypeStruct((2 * MOE_BLOCK * ROW_TILE, LANES), F32)),
        compiler_params=_cparams(("arbitrary",)),
        name="moe",
    )(*plan, xs, w_gate, w_up, w_down)
    return ys


def _ple_kernel(route_ref, gate_ref, ys_ref, x1_ref, p_ref, gp_ref, wpg_ref, wpe_ref, gfin_ref,
                out_ref, moe_ref, *, final_norm):
    tm = x1_ref.shape[0]
    tile = lambda r: pl.ds(pl.multiple_of(r * ROW_TILE, ROW_TILE), ROW_TILE)

    def combine(t, carry):
        acc = None
        for k in range(TOP_K):
            term = gate_ref[TOP_K * t + k] * ys_ref[tile(route_ref[ROUTE_COLS * t + TOP_K + k]), :]
            acc = term if acc is None else acc + term
        moe_ref[tile(t), :] = acc
        return carry

    lax.fori_loop(0, tm, combine, 0, unroll=8)
    x2 = x1_ref[...] + _load_row_tiles(moe_ref, tm)
    hp = _rms(x2, gp_ref[...]).astype(BF16)
    g_ple = _sigmoid(jnp.dot(hp, wpg_ref[...], preferred_element_type=F32))
    emb = jnp.dot(p_ref[...].astype(BF16), wpe_ref[...], preferred_element_type=F32)
    x3 = x2 + g_ple * emb
    if final_norm:
        x3 = _rms(x3, gfin_ref[...])
    out_ref[...] = x3


def _ple(route_flat, gate_flat, ys, x1, p, gp, wpg, wpe, gfin, tm, final_norm):
    t = x1.shape[0]
    row = lambda i: (i, 0)
    const = lambda i: (0, 0)
    full = lambda a: pl.BlockSpec(a.shape, const)
    smem = lambda n: pl.BlockSpec((n,), lambda i: (i,), memory_space=pltpu.SMEM)
    return pl.pallas_call(
        functools.partial(_ple_kernel, final_norm=final_norm),
        grid=(t // tm,),
        in_specs=[
            smem(ROUTE_COLS * tm),
            smem(TOP_K * tm),
            pl.BlockSpec((TOP_K * tm * ROW_TILE, LANES), row),
            pl.BlockSpec((tm, D_MODEL), row),
            pl.BlockSpec((tm, PLE_DIM), row),
            full(gp), full(wpg), full(wpe), full(gfin),
        ],
        out_specs=pl.BlockSpec((tm, D_MODEL), row),
        out_shape=jax.ShapeDtypeStruct((t, D_MODEL), F32),
        scratch_shapes=[pltpu.VMEM((tm * ROW_TILE, LANES), F32)],
        compiler_params=_cparams(("parallel",)),
        name="ple",
    )(route_flat, gate_flat, ys, x1, p, gp, wpg, wpe, gfin)


def _rope_tables(seq):
    inv = jnp.power(ROPE_THETA, -jnp.arange(0, HEAD_DIM, 2, dtype=F32) / HEAD_DIM)
    ang = jnp.arange(seq, dtype=F32)[:, None] * inv[None, :]
    cos, sin = jnp.cos(ang), jnp.sin(ang)
    cos_h = jnp.concatenate([cos, cos], axis=-1)
    sin_h = jnp.concatenate([-sin, sin], axis=-1)
    return jnp.tile(cos_h, (1, N_HEADS)), jnp.tile(sin_h, (1, N_HEADS))


def _dispatch_plan(counts, n_blocks):
    i32 = jnp.int32
    total = jnp.sum(counts, axis=0)
    blocks_of = (total + MOE_BLOCK - 1) // MOE_BLOCK
    block_end = jnp.cumsum(blocks_of)
    bidx = jnp.arange(n_blocks, dtype=i32)
    block_e = jnp.minimum(jnp.sum((block_end[None, :] <= bidx[:, None]).astype(i32), axis=1),
                          N_EXPERTS - 1)
    block_j = bidx - (block_end - blocks_of)[block_e]
    used = bidx < block_end[-1]
    cum = jnp.cumsum(counts, axis=0) - counts
    off = jnp.cumsum(counts, axis=1) - counts
    lo = block_j * MOE_BLOCK
    hi = jnp.minimum(lo + MOE_BLOCK, total[block_e])
    run_start = cum[:, block_e].T
    run_end = run_start + counts[:, block_e].T
    tile_lo = jnp.sum((run_end <= lo[:, None]).astype(i32), axis=1)
    tile_hi = jnp.sum((run_start < hi[:, None]).astype(i32), axis=1) - 1
    valid = jnp.clip(hi - lo, 0, MOE_BLOCK)
    tile_lo = jnp.where(used, tile_lo, 0)
    tile_hi = jnp.where(used, tile_hi, -1)
    valid = jnp.where(used, valid, 0)
    flat = lambda a: a.reshape(-1).astype(i32)
    return (flat(block_e), flat(block_j), flat(tile_lo), flat(tile_hi), flat(valid),
            flat(cum), flat(counts), flat(off))


def kernel(x, p, norm_mix, w_in, w_dw, b_dw, ln_conv_g, ln_conv_b, w_attn_out, w_conv_out,
           b_conv_out, w_out, norm_ffn, w_route_group, b_route_group, w_route_expert,
           b_route_expert, w_exp_gate, w_exp_up, w_exp_down, norm_ple, w_ple_proj, w_ple_gate,
           norm_final):
    bsz, seq, d = x.shape
    depth = w_in.shape[0]
    t = bsz * seq
    tm = 512
    n_assign = t * TOP_K
    n_rows = (-(-n_assign // MOE_BLOCK)) * MOE_BLOCK + N_EXPERTS * MOE_BLOCK
    cos, sin = _rope_tables(seq)
    row2 = lambda a: a.reshape(1, -1)

    xf = x.reshape(t, d)
    for i in range(depth):
        q, k, v, u, ga, gb = _inproj(xf, row2(norm_mix[i]), w_in[i].astype(BF16), cos, sin,
                                     bsz, seq, tm)
        o = _attention(q, k, v).reshape(t, ATTN_WIDTH)
        c = _conv(u.reshape(bsz, seq, CONV_WIDTH), w_dw[i], row2(b_dw[i]), row2(ln_conv_g[i]),
                  row2(ln_conv_b[i]), 256).reshape(t, CONV_WIDTH)

        w_r = jnp.zeros((d, LANES), F32)
        w_r = w_r.at[:, :N_GROUPS].set(w_route_group[i])
        w_r = w_r.at[:, EXPERT_LANE0:EXPERT_LANE0 + N_EXPERTS].set(w_route_expert[i])
        w_r_hi = w_r.astype(BF16)
        w_r_lo = (w_r - w_r_hi.astype(F32)).astype(BF16)
        b_r = jnp.zeros((1, LANES), F32)
        b_r = b_r.at[0, :N_GROUPS].set(b_route_group[i])
        b_r = b_r.at[0, EXPERT_LANE0:EXPERT_LANE0 + N_EXPERTS].set(b_route_expert[i])

        x1, h2, route, gate, cnt = _merge(
            o, c, ga, gb, xf, w_attn_out[i].astype(BF16), w_conv_out[i].astype(BF16),
            row2(b_conv_out[i]), w_out[i].astype(BF16), row2(norm_ffn[i]), w_r_hi, w_r_lo, b_r, tm)

        counts = cnt[:, 0, :N_EXPERTS].astype(jnp.int32)
        plan = _dispatch_plan(counts, n_rows // MOE_BLOCK)
        route_flat = route.reshape(-1)
        xs = _tile_sort(route_flat, h2, tm)
        ys = _moe(plan, xs, w_exp_gate, w_exp_up, w_exp_down, i, n_rows // MOE_BLOCK, TOP_K * tm)
        xf = _ple(route_flat, gate.reshape(-1), ys, x1, p[i].reshape(t, PLE_DIM), row2(norm_ple[i]),
                  w_ple_gate[i].astype(BF16), w_ple_proj[i].astype(BF16), row2(norm_final),
                  tm, final_norm=(i == depth - 1))
    return xf.reshape(bsz, seq, d)
```

```python
import functools

import jax
import jax.numpy as jnp
from jax import lax
from jax.experimental import pallas as pl
from jax.experimental.pallas import tpu as pltpu

D_MODEL = 1024
N_HEADS = 8
HEAD_DIM = 64
ATTN_WIDTH = N_HEADS * HEAD_DIM
CONV_WIDTH = 512
CONV_K = 31
DILATIONS = (16, 4, 1)
BLK = 128
ROPE_THETA = 10000.0
N_GROUPS = 4
EXPERTS_PER_GROUP = 8
N_EXPERTS = N_GROUPS * EXPERTS_PER_GROUP
EXPERT_FF = 512
TOP_K = 2
MOE_BLOCK = 256
PLE_DIM = 256
EPS = 1e-6
NEG_INF = -1e30

LANES = 128
SUBLANES = 8
ROW_TILE = D_MODEL // LANES
VMEM_LIMIT = 56 * 1024 * 1024

F32 = jnp.float32
BF16 = jnp.bfloat16


def _cparams(sem):
    return pltpu.CompilerParams(dimension_semantics=sem, vmem_limit_bytes=VMEM_LIMIT)


def _rms(x, g):
    return x * lax.rsqrt(jnp.mean(x * x, axis=-1, keepdims=True) + EPS) * g


def _sigmoid(x):
    return 1.0 / (1.0 + jnp.exp(-x))


def _store_row_tiles(ref, x):
    rows = x.shape[0]
    for s in range(ROW_TILE):
        ref[pl.ds(s, rows, stride=ROW_TILE), :] = x[:, s * LANES:(s + 1) * LANES]


def _load_row_tiles(ref, rows):
    return jnp.concatenate([ref[pl.ds(s, rows, stride=ROW_TILE), :] for s in range(ROW_TILE)], axis=1)


QKV_ROWS = 256


def _inproj_kernel(x_ref, g_ref, w_ref, cos_ref, sin_ref, *out_refs):
    n_lay = len(DILATIONS)
    q_refs, k_refs, v_refs = (out_refs[a * n_lay:(a + 1) * n_lay] for a in range(3))
    slab_refs = out_refs[3 * n_lay:]
    tm = x_ref.shape[0]
    n_slab = ATTN_WIDTH // LANES

    def put(ref, d, r, row0, g, val):
        cols = slice(g * LANES, (g + 1) * LANES)
        if d == 1:
            ref[pl.ds(row0, val.shape[0]), cols] = val.astype(BF16)
        else:
            ref[0, r, pl.ds(row0, val.shape[0]), cols] = val.astype(BF16)

    def fill(raw, rows, refs, slabs, is_query, post):
        for g in range(n_slab):
            cols = slice(g * LANES, (g + 1) * LANES)
            t = post(raw[:, cols], rows)
            slabs[g, rows, :] = t
            for d, ref in zip(DILATIONS, refs):
                if d == 1 and not is_query:
                    ref[rows, cols] = t.astype(BF16)

    def relayout(refs, slabs, is_query):
        for d, ref in zip(DILATIONS, refs):
            groups = STATE_DILATION // d
            if d == 1 and not is_query:
                continue
            elif not is_query or groups == 1:
                def per_class(r, carry, d=d, ref=ref):
                    for g in range(n_slab):
                        put(ref, d, r, 0, g, slabs[g, pl.ds(r, tm // d, stride=d), :])
                    return carry
                lax.fori_loop(0, d, per_class, 0, unroll=min(d, 4))
            else:
                run = BLK // groups

                n_bl = tm // (d * BLK)

                def per_block(idx, carry, d=d, ref=ref, run=run, groups=groups, n_bl=n_bl):
                    r, bl = idx // n_bl, idx % n_bl
                    for a in range(groups):
                        src = pl.ds(d * BLK * bl + d * a + r, run, stride=STATE_DILATION)
                        for g in range(n_slab):
                            put(ref, d, r, pl.multiple_of(bl * BLK, BLK) + a * run, g, slabs[g, src, :])
                    return carry
                lax.fori_loop(0, d * n_bl, per_block, 0, unroll=2)

    lane = lax.broadcasted_iota(jnp.int32, (QKV_ROWS, LANES), 1)
    first_half = (lane % HEAD_DIM) < (HEAD_DIM // 2)

    def rope(t, rows):
        partner = jnp.where(first_half,
                            pltpu.roll(t, LANES - HEAD_DIM // 2, 1),
                            pltpu.roll(t, HEAD_DIM // 2, 1))
        return t * cos_ref[rows, :] + partner * sin_ref[rows, :]

    for r0 in range(0, tm, QKV_ROWS):
        rows = slice(r0, r0 + QKV_ROWS)
        h = _rms(x_ref[rows, :], g_ref[...]).astype(BF16)
        proj = lambda a: jnp.dot(h, w_ref[:, a * ATTN_WIDTH:(a + 1) * ATTN_WIDTH],
                                 preferred_element_type=F32)
        fill(proj(0), rows, q_refs, slab_refs[0], True, lambda t, rw: rope(t, rw) * (HEAD_DIM ** -0.5))
        fill(proj(1), rows, k_refs, slab_refs[1], False, rope)
        fill(proj(2), rows, v_refs, slab_refs[2], False, lambda t, rw: t)
    relayout(q_refs, slab_refs[0], True)
    relayout(k_refs, slab_refs[1], False)
    relayout(v_refs, slab_refs[2], False)


def _gateproj_kernel(x_ref, g_ref, w_ref, u_ref, ga_ref, gb_ref):
    h = _rms(x_ref[...], g_ref[...]).astype(BF16)

    def proj(c0, width):
        return jnp.dot(h, w_ref[:, c0:c0 + width], preferred_element_type=F32)

    u_ref[...] = proj(0, CONV_WIDTH) * _sigmoid(proj(CONV_WIDTH, CONV_WIDTH))
    c0 = 2 * CONV_WIDTH
    ga_ref[...] = proj(c0, D_MODEL).astype(ga_ref.dtype)
    gb_ref[...] = proj(c0 + D_MODEL, D_MODEL).astype(gb_ref.dtype)


def _gateproj(x, gain, w_rest, tm):
    t = x.shape[0]
    row = lambda i: (i, 0)
    const = lambda i: (0, 0)
    return pl.pallas_call(
        _gateproj_kernel,
        grid=(t // tm,),
        in_specs=[pl.BlockSpec((tm, D_MODEL), row), pl.BlockSpec((1, D_MODEL), const),
                  pl.BlockSpec(w_rest.shape, const)],
        out_specs=(pl.BlockSpec((tm, CONV_WIDTH), row), pl.BlockSpec((tm, D_MODEL), row),
                   pl.BlockSpec((tm, D_MODEL), row)),
        out_shape=(jax.ShapeDtypeStruct((t, CONV_WIDTH), F32),
                   jax.ShapeDtypeStruct((t, D_MODEL), BF16),
                   jax.ShapeDtypeStruct((t, D_MODEL), BF16)),
        compiler_params=_cparams(("parallel",)),
        name="gateproj",
    )(x, gain, w_rest)


def _inproj(x, gain, w_in, cos, sin, bsz, seq, tm):
    t = x.shape[0]
    n_pos = seq // tm
    row = lambda i: (i, 0)
    const = lambda i: (0, 0)
    qkv_shapes, qkv_specs = [], []
    for d in DILATIONS:
        if d == 1:
            qkv_shapes.append(jax.ShapeDtypeStruct((t, ATTN_WIDTH), BF16))
            qkv_specs.append(pl.BlockSpec((tm, ATTN_WIDTH), row))
        else:
            qkv_shapes.append(jax.ShapeDtypeStruct((bsz, d, seq // d, ATTN_WIDTH), BF16))
            qkv_specs.append(pl.BlockSpec((1, d, tm // d, ATTN_WIDTH),
                                          lambda i: (i // n_pos, 0, i % n_pos, 0)))
    out_shape = tuple(qkv_shapes * 3)
    out_specs = tuple(qkv_specs * 3)
    outs = pl.pallas_call(
        _inproj_kernel,
        grid=(t // tm,),
        in_specs=[
            pl.BlockSpec((tm, D_MODEL), row),
            pl.BlockSpec((1, D_MODEL), const),
            pl.BlockSpec(w_in.shape, const),
            pl.BlockSpec((tm, LANES), lambda i: (i % n_pos, 0)),
            pl.BlockSpec((tm, LANES), lambda i: (i % n_pos, 0)),
        ],
        out_specs=out_specs,
        out_shape=out_shape,
        scratch_shapes=[pltpu.VMEM((ATTN_WIDTH // LANES, tm, LANES), F32) for _ in range(3)],
        compiler_params=_cparams(("parallel",)),
        name="inproj",
    )(x, gain, w_in, cos, sin)
    n_lay = len(DILATIONS)
    as_rows = lambda a: a.reshape(bsz, seq, ATTN_WIDTH)
    return tuple([as_rows(a) for a in outs[j * n_lay:(j + 1) * n_lay]] for j in range(3))


STATE_DILATION = max(DILATIONS)


def _query_order(idx, groups):
    run = BLK // groups
    return groups * (idx % run) + idx // run


def _attn_kernel(*refs, seq):
    n_lay = len(DILATIONS)
    srcs = refs[:3 * n_lay]
    o_ref, buf, acc_s, m_s, l_s, sem = refs[3 * n_lay:]
    b = pl.program_id(0)
    n_b = pl.num_programs(0)
    n_blocks = seq // BLK
    pair_w = 2 * HEAD_DIM
    n_pair = ATTN_WIDTH // pair_w

    def load_sweep(lay, bi, slot):
        return [pltpu.make_async_copy(srcs[a * n_lay + lay].at[bi], buf.at[slot, a, pl.ds(BLK, seq)],
                                      sem.at[slot]) for a in range(3)]

    @pl.when(b == 0)
    def _():
        buf[:, :, 0:BLK, :] = jnp.zeros((2, 3, BLK, ATTN_WIDTH), BF16)
        for c in load_sweep(0, 0, 0):
            c.start()

    row = lax.broadcasted_iota(jnp.int32, (BLK, BLK), 0)
    col = lax.broadcasted_iota(jnp.int32, (BLK, BLK), 1)
    lane = lax.broadcasted_iota(jnp.int32, (BLK, pair_w), 1)
    low_head = lane < HEAD_DIM
    unpermute = (row == _query_order(col, STATE_DILATION // DILATIONS[-1])).astype(BF16)

    for lay, d in enumerate(DILATIONS):
        slot = (b * n_lay + lay) % 2
        for c in load_sweep(lay, b, slot):
            c.wait()
        if lay + 1 < n_lay:
            for c in load_sweep(lay + 1, b, 1 - slot):
                c.start()
        else:
            @pl.when(b + 1 < n_b)
            def _():
                for c in load_sweep(0, b + 1, 1 - slot):
                    c.start()

        nb = seq // (d * BLK)
        has_prev = nb > 1
        first, final = lay == 0, lay == n_lay - 1
        kw = 2 * BLK if has_prev else BLK

        groups = STATE_DILATION // d
        run = BLK // groups
        q_pos = _query_order(row, groups)

        def block(j, carry, d=d, nb=nb, has_prev=has_prev, first=first, final=final, kw=kw,
                  slot=slot, groups=groups, run=run, q_pos=q_pos):
            r, n = j // nb, j % nb
            q_row0 = pl.multiple_of(BLK + j * BLK, BLK)
            k_row0 = pl.multiple_of(j * BLK, BLK) if has_prev else q_row0
            runs = [pl.ds(pl.multiple_of((d * a + r) * BLK + run * n, run), run)
                    for a in range(groups)]
            load = lambda ref, g: jnp.concatenate([ref[g, s, :] for s in runs], axis=0)

            def store(ref, g, val):
                for a, s in enumerate(runs):
                    ref[g, s, :] = val[a * run:(a + 1) * run]

            cur_ok = col <= q_pos
            if has_prev:
                prev_ok = jnp.logical_and(col >= q_pos, n > 0)
                allowed = jnp.concatenate([prev_ok, cur_ok], axis=1)
            else:
                allowed = cur_ok
            allowed2 = jnp.concatenate([allowed, allowed], axis=0)
            lanes = [slice(g * pair_w, (g + 1) * pair_w) for g in range(n_pair)]
            scores = []
            for ls in lanes:
                q2 = buf[slot, 0, pl.ds(q_row0, BLK), ls]
                q_both = jnp.concatenate([jnp.where(low_head, q2, 0),
                                          jnp.where(low_head, 0, q2)], axis=0)
                k2 = buf[slot, 1, pl.ds(k_row0, kw), ls]
                scores.append(lax.dot_general(q_both, k2, (((1,), (1,)), ((), ())),
                                              preferred_element_type=F32))
            probs, maxes, sums = [], [], []
            for s in scores:
                s = jnp.where(allowed2, s, NEG_INF)
                m_h = jnp.max(s, axis=-1, keepdims=True)
                p = jnp.exp(s - m_h)
                sums.append(jnp.sum(p, axis=-1, keepdims=True))
                maxes.append(m_h)
                probs.append(p.astype(BF16))
            pv = [jnp.dot(p, buf[slot, 2, pl.ds(k_row0, kw), ls], preferred_element_type=F32)
                  for p, ls in zip(probs, lanes)]
            for g, ls in enumerate(lanes):
                acc = jnp.where(low_head, pv[g][:BLK], pv[g][BLK:])
                m_new = jnp.where(low_head, maxes[g][:BLK], maxes[g][BLK:])
                l_new = jnp.where(low_head, sums[g][:BLK], sums[g][BLK:])
                if not first:
                    m_old = load(m_s, g)
                    m_tot = jnp.maximum(m_old, m_new)
                    a_old = jnp.exp(m_old - m_tot)
                    a_new = jnp.exp(m_new - m_tot)
                    acc = load(acc_s, g) * a_old + acc * a_new
                    l_new = load(l_s, g) * a_old + l_new * a_new
                    m_new = m_tot
                if final:
                    o_blk = jnp.dot(unpermute, (acc / l_new).astype(BF16), preferred_element_type=F32)
                    o_ref[0, pl.ds(pl.multiple_of(j * BLK, BLK), BLK), ls] = o_blk.astype(o_ref.dtype)
                else:
                    store(acc_s, g, acc)
                    store(m_s, g, m_new)
                    store(l_s, g, l_new)
            return carry

        lax.fori_loop(0, n_blocks, block, 0)


def _attention(q, k, v):
    bsz, seq, _ = q[0].shape
    assert all(seq % (d * BLK) == 0 for d in DILATIONS) and DILATIONS[-1] == 1
    n_pair = ATTN_WIDTH // (2 * HEAD_DIM)
    return pl.pallas_call(
        functools.partial(_attn_kernel, seq=seq),
        grid=(bsz,),
        in_specs=[pl.BlockSpec(memory_space=pl.ANY)] * (3 * len(DILATIONS)),
        out_specs=pl.BlockSpec((1, seq, ATTN_WIDTH), lambda b: (b, 0, 0)),
        out_shape=jax.ShapeDtypeStruct((bsz, seq, ATTN_WIDTH), BF16),
        scratch_shapes=[pltpu.VMEM((2, 3, BLK + seq, ATTN_WIDTH), BF16),
                        pltpu.VMEM((n_pair, seq, 2 * HEAD_DIM), F32),
                        pltpu.VMEM((n_pair, seq, 2 * HEAD_DIM), F32),
                        pltpu.VMEM((n_pair, seq, 2 * HEAD_DIM), F32),
                        pltpu.SemaphoreType.DMA((2,))],
        compiler_params=_cparams(("arbitrary",)),
        name="attention",
    )(*q, *k, *v)


CONV_HALO = 32
CONV_ROWS = 32


def _conv_kernel(halo_ref, cur_ref, w_ref, b_ref, g_ref, beta_ref, out_ref, ext_ref):
    ts = cur_ref.shape[1]
    i = pl.program_id(1)
    ext_ref[0, 0:CONV_HALO, :] = jnp.where(i > 0, halo_ref[0], 0.0)
    ext_ref[0, CONV_HALO:CONV_HALO + ts, :] = cur_ref[0]
    first = CONV_HALO - (CONV_K - 1)
    n_shift = ext_ref.shape[0]
    rows = CONV_HALO + ts
    for s in range(1, n_shift):
        ext_ref[s, 0:rows - s, :] = ext_ref[0, s:rows, :]
    bias = b_ref[...]
    gamma = g_ref[...]
    beta = beta_ref[...]
    for r0 in range(0, ts, CONV_ROWS):
        acc = jnp.zeros((CONV_ROWS, CONV_WIDTH), F32) + bias
        for j in range(CONV_K):
            lo = first + j + r0
            s, lo = lo % n_shift, lo - lo % n_shift
            acc = acc + w_ref[j:j + 1, :] * ext_ref[s, lo:lo + CONV_ROWS, :]
        mu = jnp.mean(acc, axis=-1, keepdims=True)
        cen = acc - mu
        var = jnp.mean(cen * cen, axis=-1, keepdims=True)
        y = cen * lax.rsqrt(var + EPS) * gamma + beta
        out_ref[0, r0:r0 + CONV_ROWS, :] = (y * _sigmoid(y)).astype(out_ref.dtype)


def _conv(u, w_dw, b_dw, ln_g, ln_b, ts):
    b, s, c = u.shape
    per = ts // CONV_HALO
    const = lambda bi, i: (0, 0)
    return pl.pallas_call(
        _conv_kernel,
        grid=(b, s // ts),
        in_specs=[
            pl.BlockSpec((1, CONV_HALO, c), lambda bi, i: (bi, jnp.maximum(i * per - 1, 0), 0)),
            pl.BlockSpec((1, ts, c), lambda bi, i: (bi, i, 0)),
            pl.BlockSpec((CONV_K, c), const),
            pl.BlockSpec((1, c), const),
            pl.BlockSpec((1, c), const),
            pl.BlockSpec((1, c), const),
        ],
        out_specs=pl.BlockSpec((1, ts, c), lambda bi, i: (bi, i, 0)),
        out_shape=jax.ShapeDtypeStruct((b, s, c), BF16),
        scratch_shapes=[pltpu.VMEM((SUBLANES, CONV_HALO + ts, c), F32)],
        compiler_params=_cparams(("parallel", "arbitrary")),
        name="conv",
    )(u, u, w_dw, b_dw, ln_g, ln_b)


ROUTE_COLS = 4
EXPERT_LANE0 = N_GROUPS


def _merge_kernel(o_ref, c_ref, ga_ref, gb_ref, x_ref, wa_ref, wc_ref, bc_ref, wo_ref,
                  gf_ref, wrh_ref, wrl_ref, br_ref,
                  x1_ref, h2_ref, route_ref, gate_ref, cnt_ref, tri_ref):
    tm = x_ref.shape[0]

    @pl.when(pl.program_id(0) == 0)
    def _():
        r = lax.broadcasted_iota(jnp.int32, (tm, tm), 0)
        c = lax.broadcasted_iota(jnp.int32, (tm, tm), 1)
        tri_ref[...] = (c < r).astype(BF16)

    y_a = jnp.dot(o_ref[...], wa_ref[...], preferred_element_type=F32)
    y_b = jnp.dot(c_ref[...], wc_ref[...], preferred_element_type=F32) + bc_ref[...]
    merged = _sigmoid(ga_ref[...].astype(F32)) * y_a + _sigmoid(gb_ref[...].astype(F32)) * y_b
    x1 = x_ref[...] + jnp.dot(merged.astype(BF16), wo_ref[...], preferred_element_type=F32)
    x1_ref[...] = x1
    h2 = _rms(x1, gf_ref[...])
    _store_row_tiles(h2_ref, h2)

    h_hi = h2.astype(BF16)
    h_lo = (h2 - h_hi.astype(F32)).astype(BF16)
    logits = (jnp.dot(h_hi, wrh_ref[...], preferred_element_type=F32)
              + jnp.dot(h_hi, wrl_ref[...], preferred_element_type=F32)
              + jnp.dot(h_lo, wrh_ref[...], preferred_element_type=F32)) + br_ref[...]

    lane = lax.broadcasted_iota(jnp.int32, (tm, LANES), 1)
    is_group = lane < N_GROUPS
    g_max = jnp.max(jnp.where(is_group, logits, -jnp.inf), axis=-1, keepdims=True)
    g_sel = jnp.min(jnp.where(jnp.logical_and(is_group, logits == g_max), lane, LANES),
                    axis=-1, keepdims=True)
    p_group = 1.0 / jnp.sum(jnp.where(is_group, jnp.exp(logits - g_max), 0.0),
                            axis=-1, keepdims=True)
    lo = EXPERT_LANE0 + EXPERTS_PER_GROUP * g_sel
    in_sel = jnp.logical_and(lane >= lo, lane < lo + EXPERTS_PER_GROUP)
    cand = jnp.where(in_sel, logits, -jnp.inf)
    v1 = jnp.max(cand, axis=-1, keepdims=True)
    i1 = jnp.min(jnp.where(jnp.logical_and(in_sel, cand == v1), lane, LANES), axis=-1, keepdims=True)
    rest = jnp.logical_and(in_sel, lane != i1)
    cand2 = jnp.where(rest, logits, -jnp.inf)
    v2 = jnp.max(cand2, axis=-1, keepdims=True)
    i2 = jnp.min(jnp.where(jnp.logical_and(rest, cand2 == v2), lane, LANES), axis=-1, keepdims=True)
    e2 = jnp.exp(v2 - v1)
    gate0 = p_group / (1.0 + e2)
    gate1 = p_group * e2 / (1.0 + e2)

    oh0 = (lane == i1)
    oh1 = (lane == i2)
    before0 = jnp.dot(tri_ref[...], oh0.astype(BF16), preferred_element_type=F32)
    before1 = jnp.dot(tri_ref[...], oh1.astype(BF16), preferred_element_type=F32)
    tot0 = jnp.sum(oh0.astype(F32), axis=0, keepdims=True)
    tot1 = jnp.sum(oh1.astype(F32), axis=0, keepdims=True)
    cnt = tot0 + tot1
    row_lane = lax.broadcasted_iota(jnp.int32, (1, LANES), 1)
    first_row = cnt
    shift = 1
    while shift < LANES:
        first_row = first_row + jnp.where(row_lane >= shift, pltpu.roll(first_row, shift, 1), 0.0)
        shift *= 2
    first_row = first_row - cnt
    pos0 = jnp.sum(jnp.where(oh0, first_row + before0, 0.0), axis=-1, keepdims=True)
    pos1 = jnp.sum(jnp.where(oh1, first_row + tot0 + before1, 0.0), axis=-1, keepdims=True)

    rc = lax.broadcasted_iota(jnp.int32, (tm, ROUTE_COLS), 1)
    route = jnp.where(rc == 0, i1 - EXPERT_LANE0,
                      jnp.where(rc == 1, i2 - EXPERT_LANE0,
                                jnp.where(rc == 2, pos0.astype(jnp.int32), pos1.astype(jnp.int32))))
    route_ref[...] = route
    gc = lax.broadcasted_iota(jnp.int32, (tm, TOP_K), 1)
    gate_ref[...] = jnp.where(gc == 0, gate0, gate1)
    cnt_ref[0] = jnp.broadcast_to(pltpu.roll(cnt, LANES - EXPERT_LANE0, 1), (8, LANES))


def _merge(o, c, ga, gb, x, wa, wc, bc, wo, gf, wr_hi, wr_lo, br, tm):
    t = x.shape[0]
    nt = t // tm
    row = lambda i: (i, 0)
    const = lambda i: (0, 0)
    full = lambda a: pl.BlockSpec(a.shape, const)
    return pl.pallas_call(
        _merge_kernel,
        grid=(nt,),
        in_specs=[
            pl.BlockSpec((tm, ATTN_WIDTH), row),
            pl.BlockSpec((tm, CONV_WIDTH), row),
            pl.BlockSpec((tm, D_MODEL), row),
            pl.BlockSpec((tm, D_MODEL), row),
            pl.BlockSpec((tm, D_MODEL), row),
            full(wa), full(wc), full(bc), full(wo), full(gf), full(wr_hi), full(wr_lo), full(br),
        ],
        out_specs=(
            pl.BlockSpec((tm, D_MODEL), row),
            pl.BlockSpec((tm * ROW_TILE, LANES), row),
            pl.BlockSpec((tm, ROUTE_COLS), row),
            pl.BlockSpec((tm, TOP_K), row),
            pl.BlockSpec((1, 8, LANES), lambda i: (i, 0, 0)),
        ),
        out_shape=(
            jax.ShapeDtypeStruct((t, D_MODEL), F32),
            jax.ShapeDtypeStruct((t * ROW_TILE, LANES), F32),
            jax.ShapeDtypeStruct((t, ROUTE_COLS), jnp.int32),
            jax.ShapeDtypeStruct((t, TOP_K), F32),
            jax.ShapeDtypeStruct((nt, 8, LANES), F32),
        ),
        scratch_shapes=[pltpu.VMEM((tm, tm), BF16)],
        compiler_params=_cparams(("arbitrary",)),
        name="merge",
    )(o, c, ga, gb, x, wa, wc, bc, wo, gf, wr_hi, wr_lo, br)


def _sort_kernel(route_ref, h_ref, xs_ref):
    tm = h_ref.shape[0] // ROW_TILE

    def place(t, carry):
        row = h_ref[pl.ds(pl.multiple_of(t * ROW_TILE, ROW_TILE), ROW_TILE), :]
        for k in range(TOP_K):
            pos = route_ref[ROUTE_COLS * t + TOP_K + k]
            xs_ref[pl.ds(pl.multiple_of(pos * ROW_TILE, ROW_TILE), ROW_TILE), :] = row
        return carry

    lax.fori_loop(0, tm, place, 0, unroll=8)


def _tile_sort(route_flat, h2, tm):
    nt = h2.shape[0] // (tm * ROW_TILE)
    return pl.pallas_call(
        _sort_kernel,
        grid=(nt,),
        in_specs=[pl.BlockSpec((ROUTE_COLS * tm,), lambda i: (i,), memory_space=pltpu.SMEM),
                  pl.BlockSpec((tm * ROW_TILE, LANES), lambda i: (i, 0))],
        out_specs=pl.BlockSpec((TOP_K * tm * ROW_TILE, LANES), lambda i: (i, 0)),
        out_shape=jax.ShapeDtypeStruct((TOP_K * h2.shape[0], LANES), F32),
        compiler_params=_cparams(("parallel",)),
        name="tile_sort",
    )(route_flat, h2)


def _moe_kernel(be_ref, bj_ref, ilo_ref, ihi_ref, valid_ref, cum_ref, cnt_ref, off_ref,
                xs_hbm, wg_ref, wu_ref, wd_ref, ys_hbm, trash_hbm,
                xbuf, obuf, wg_s, wu_s, wd_s, sem_in, sem_out, *, tile_rows):
    b = pl.program_id(0)
    last = pl.num_programs(0) - 1
    slot = b % 2
    other = 1 - slot
    rows8 = lambda r, n: pl.ds(pl.multiple_of(r * ROW_TILE, ROW_TILE), n * ROW_TILE)

    def for_runs(bb, fn):
        e = be_ref[bb]
        lo_blk = bj_ref[bb] * MOE_BLOCK

        def body(i, carry):
            c = cum_ref[i * N_EXPERTS + e]
            lo = jnp.maximum(c, lo_blk)
            hi = jnp.minimum(c + cnt_ref[i * N_EXPERTS + e], lo_blk + MOE_BLOCK)
            fn(i * tile_rows + off_ref[i * N_EXPERTS + e] + (lo - c), lo - lo_blk,
               jnp.maximum(hi - lo, 0))
            return carry

        lax.fori_loop(ilo_ref[bb], ihi_ref[bb] + 1, body, 0)

    def for_chunks(length, fn):
        for bit in reversed(range(MOE_BLOCK.bit_length())):
            size = 1 << bit

            @pl.when((length & size) != 0)
            def _():
                fn(length & ~(2 * size - 1), size)

    def gather(bb, s):
        def run(src, dst, length):
            for_chunks(length, lambda o, n: pltpu.make_async_copy(
                xs_hbm.at[rows8(src + o, n)], xbuf.at[s, rows8(dst + o, n)], sem_in.at[s]).start())
        for_runs(bb, run)
        v = valid_ref[bb]
        for_chunks(MOE_BLOCK - v, lambda o, n: pltpu.make_async_copy(
            xs_hbm.at[rows8(o, n)], xbuf.at[s, rows8(v + o, n)], sem_in.at[s]).start())

    def scatter(bb, s):
        def run(src, dst, length):
            for_chunks(length, lambda o, n: pltpu.make_async_copy(
                obuf.at[s, rows8(dst + o, n)], ys_hbm.at[rows8(src + o, n)], sem_out.at[s]).start())
        for_runs(bb, run)
        v = valid_ref[bb]
        for_chunks(MOE_BLOCK - v, lambda o, n: pltpu.make_async_copy(
            obuf.at[s, rows8(v + o, n)], trash_hbm.at[rows8(s * MOE_BLOCK + v + o, n)],
            sem_out.at[s]).start())

    def wait_gather(s):
        pltpu.make_async_copy(xs_hbm.at[rows8(0, MOE_BLOCK)], xbuf.at[s], sem_in.at[s]).wait()

    def wait_scatter(s):
        pltpu.make_async_copy(obuf.at[s], ys_hbm.at[rows8(0, MOE_BLOCK)], sem_out.at[s]).wait()

    @pl.when(b == 0)
    def _():
        obuf[...] = jnp.zeros_like(obuf)
        for s in range(2):
            init = pltpu.make_async_copy(obuf.at[s], trash_hbm.at[rows8(s * MOE_BLOCK, MOE_BLOCK)],
                                         sem_out.at[s])
            init.start()
            init.wait()
        gather(0, 0)

    wait_gather(slot)

    @pl.when(b >= 2)
    def _():
        wait_scatter(slot)

    e = be_ref[b]
    e_prev = be_ref[jnp.maximum(b - 1, 0)]

    @pl.when(jnp.logical_or(b == 0, e != e_prev))
    def _():
        wg_s[...] = wg_ref[0].astype(BF16)
        wu_s[...] = wu_ref[0].astype(BF16)
        wd_s[...] = wd_ref[0].astype(BF16)

    @pl.when(b < last)
    def _():
        gather(b + 1, other)

    @pl.when(b >= 1)
    def _():
        scatter(b - 1, other)

    x = _load_row_tiles(xbuf.at[slot], MOE_BLOCK).astype(BF16)
    hg = jnp.dot(x, wg_s[...], preferred_element_type=F32)
    hu = jnp.dot(x, wu_s[...], preferred_element_type=F32)
    hb = (hg * _sigmoid(hg) * hu).astype(BF16)
    _store_row_tiles(obuf.at[slot], jnp.dot(hb, wd_s[...], preferred_element_type=F32))

    @pl.when(b == last)
    def _():
        scatter(b, slot)
        wait_scatter(other)
        wait_scatter(slot)


def _moe(plan, xs, w_gate, w_up, w_down, layer, n_blocks, tile_rows):
    n_pre = len(plan)
    w_idx = lambda b, be, *_: (layer, be[b], 0, 0)
    grid_spec = pltpu.PrefetchScalarGridSpec(
        num_scalar_prefetch=n_pre,
        grid=(n_blocks,),
        in_specs=[
            pl.BlockSpec(memory_space=pl.ANY),
            pl.BlockSpec((None, 1, D_MODEL, EXPERT_FF), w_idx),
            pl.BlockSpec((None, 1, D_MODEL, EXPERT_FF), w_idx),
            pl.BlockSpec((None, 1, EXPERT_FF, D_MODEL), w_idx),
        ],
        out_specs=(pl.BlockSpec(memory_space=pl.ANY), pl.BlockSpec(memory_space=pl.ANY)),
        scratch_shapes=[pltpu.VMEM((2, MOE_BLOCK * ROW_TILE, LANES), F32),
                        pltpu.VMEM((2, MOE_BLOCK * ROW_TILE, LANES), F32),
                        pltpu.VMEM((D_MODEL, EXPERT_FF), BF16),
                        pltpu.VMEM((D_MODEL, EXPERT_FF), BF16),
                        pltpu.VMEM((EXPERT_FF, D_MODEL), BF16),
                        pltpu.SemaphoreType.DMA((2,)),
                        pltpu.SemaphoreType.DMA((2,))],
    )
    ys, _ = pl.pallas_call(
        functools.partial(_moe_kernel, tile_rows=tile_rows),
        grid_spec=grid_spec,
        out_shape=(jax.ShapeDtypeStruct(xs.shape, F32),
                   jax.ShapeDtypeStruct((2 * MOE_BLOCK * ROW_TILE, LANES), F32)),
        compiler_params=_cparams(("arbitrary",)),
        name="moe",
    )(*plan, xs, w_gate, w_up, w_down)
    return ys


def _ple_kernel(route_ref, gate_ref, ys_ref, x1_ref, p_ref, gp_ref, wpg_ref, wpe_ref, gfin_ref,
                out_ref, moe_ref, *, final_norm):
    tm = x1_ref.shape[0]
    tile = lambda r: pl.ds(pl.multiple_of(r * ROW_TILE, ROW_TILE), ROW_TILE)

    def combine(t, carry):
        acc = None
        for k in range(TOP_K):
            term = gate_ref[TOP_K * t + k] * ys_ref[tile(route_ref[ROUTE_COLS * t + TOP_K + k]), :]
            acc = term if acc is None else acc + term
        moe_ref[tile(t), :] = acc
        return carry

    lax.fori_loop(0, tm, combine, 0, unroll=8)
    x2 = x1_ref[...] + _load_row_tiles(moe_ref, tm)
    hp = _rms(x2, gp_ref[...]).astype(BF16)
    g_ple = _sigmoid(jnp.dot(hp, wpg_ref[...], preferred_element_type=F32))
    emb = jnp.dot(p_ref[...].astype(BF16), wpe_ref[...], preferred_element_type=F32)
    x3 = x2 + g_ple * emb
    if final_norm:
        x3 = _rms(x3, gfin_ref[...])
    out_ref[...] = x3


def _ple(route_flat, gate_flat, ys, x1, p, gp, wpg, wpe, gfin, tm, final_norm):
    t = x1.shape[0]
    row = lambda i: (i, 0)
    const = lambda i: (0, 0)
    full = lambda a: pl.BlockSpec(a.shape, const)
    smem = lambda n: pl.BlockSpec((n,), lambda i: (i,), memory_space=pltpu.SMEM)
    return pl.pallas_call(
        functools.partial(_ple_kernel, final_norm=final_norm),
        grid=(t // tm,),
        in_specs=[
            smem(ROUTE_COLS * tm),
            smem(TOP_K * tm),
            pl.BlockSpec((TOP_K * tm * ROW_TILE, LANES), row),
            pl.BlockSpec((tm, D_MODEL), row),
            pl.BlockSpec((tm, PLE_DIM), row),
            full(gp), full(wpg), full(wpe), full(gfin),
        ],
        out_specs=pl.BlockSpec((tm, D_MODEL), row),
        out_shape=jax.ShapeDtypeStruct((t, D_MODEL), F32),
        scratch_shapes=[pltpu.VMEM((tm * ROW_TILE, LANES), F32)],
        compiler_params=_cparams(("parallel",)),
        name="ple",
    )(route_flat, gate_flat, ys, x1, p, gp, wpg, wpe, gfin)


def _rope_tables(seq):
    inv = jnp.power(ROPE_THETA, -jnp.arange(0, HEAD_DIM, 2, dtype=F32) / HEAD_DIM)
    ang = jnp.arange(seq, dtype=F32)[:, None] * inv[None, :]
    cos, sin = jnp.cos(ang), jnp.sin(ang)
    cos_h = jnp.concatenate([cos, cos], axis=-1)
    sin_h = jnp.concatenate([-sin, sin], axis=-1)
    reps = LANES // HEAD_DIM
    return jnp.tile(cos_h, (1, reps)), jnp.tile(sin_h, (1, reps))


def _dispatch_plan(counts, n_blocks):
    i32 = jnp.int32
    total = jnp.sum(counts, axis=0)
    blocks_of = (total + MOE_BLOCK - 1) // MOE_BLOCK
    block_end = jnp.cumsum(blocks_of)
    bidx = jnp.arange(n_blocks, dtype=i32)
    block_e = jnp.minimum(jnp.sum((block_end[None, :] <= bidx[:, None]).astype(i32), axis=1),
                          N_EXPERTS - 1)
    is_e = (block_e[:, None] == jnp.arange(N_EXPERTS, dtype=i32)[None, :]).astype(i32)
    pick = lambda per_expert: jnp.sum(is_e * per_expert[None, :], axis=1)
    pick_tiles = lambda tab: jnp.sum(is_e[:, None, :] * tab[None, :, :], axis=2)
    block_j = bidx - pick(block_end - blocks_of)
    used = bidx < block_end[-1]
    cum = jnp.cumsum(counts, axis=0) - counts
    off = jnp.cumsum(counts, axis=1) - counts
    lo = block_j * MOE_BLOCK
    hi = jnp.minimum(lo + MOE_BLOCK, pick(total))
    run_start = pick_tiles(cum)
    run_end = run_start + pick_tiles(counts)
    tile_lo = jnp.sum((run_end <= lo[:, None]).astype(i32), axis=1)
    tile_hi = jnp.sum((run_start < hi[:, None]).astype(i32), axis=1) - 1
    valid = jnp.clip(hi - lo, 0, MOE_BLOCK)
    tile_lo = jnp.where(used, tile_lo, 0)
    tile_hi = jnp.where(used, tile_hi, -1)
    valid = jnp.where(used, valid, 0)
    flat = lambda a: a.reshape(-1).astype(i32)
    return (flat(block_e), flat(block_j), flat(tile_lo), flat(tile_hi), flat(valid),
            flat(cum), flat(counts), flat(off))


def kernel(x, p, norm_mix, w_in, w_dw, b_dw, ln_conv_g, ln_conv_b, w_attn_out, w_conv_out,
           b_conv_out, w_out, norm_ffn, w_route_group, b_route_group, w_route_expert,
           b_route_expert, w_exp_gate, w_exp_up, w_exp_down, norm_ple, w_ple_proj, w_ple_gate,
           norm_final):
    bsz, seq, d = x.shape
    depth = w_in.shape[0]
    t = bsz * seq
    tm = 512
    n_assign = t * TOP_K
    n_rows = (-(-n_assign // MOE_BLOCK)) * MOE_BLOCK + N_EXPERTS * MOE_BLOCK
    cos, sin = _rope_tables(seq)
    row2 = lambda a: a.reshape(1, -1)

    xf = x.reshape(t, d)
    for i in range(depth):
        n_qkv = 3 * ATTN_WIDTH
        q, k, v = _inproj(xf, row2(norm_mix[i]), w_in[i, :, :n_qkv].astype(BF16), cos, sin,
                          bsz, seq, tm)
        u, ga, gb = _gateproj(xf, row2(norm_mix[i]), w_in[i, :, n_qkv:].astype(BF16), tm)
        o = _attention(q, k, v).reshape(t, ATTN_WIDTH)
        c = _conv(u.reshape(bsz, seq, CONV_WIDTH), w_dw[i], row2(b_dw[i]), row2(ln_conv_g[i]),
                  row2(ln_conv_b[i]), 256).reshape(t, CONV_WIDTH)

        w_r = jnp.zeros((d, LANES), F32)
        w_r = w_r.at[:, :N_GROUPS].set(w_route_group[i])
        w_r = w_r.at[:, EXPERT_LANE0:EXPERT_LANE0 + N_EXPERTS].set(w_route_expert[i])
        w_r_hi = w_r.astype(BF16)
        w_r_lo = (w_r - w_r_hi.astype(F32)).astype(BF16)
        b_r = jnp.zeros((1, LANES), F32)
        b_r = b_r.at[0, :N_GROUPS].set(b_route_group[i])
        b_r = b_r.at[0, EXPERT_LANE0:EXPERT_LANE0 + N_EXPERTS].set(b_route_expert[i])

        x1, h2, route, gate, cnt = _merge(
            o, c, ga, gb, xf, w_attn_out[i].astype(BF16), w_conv_out[i].astype(BF16),
            row2(b_conv_out[i]), w_out[i].astype(BF16), row2(norm_ffn[i]), w_r_hi, w_r_lo, b_r, tm)

        counts = cnt[:, 0, :N_EXPERTS].astype(jnp.int32)
        plan = _dispatch_plan(counts, n_rows // MOE_BLOCK)
        route_flat = route.reshape(-1)
        xs = _tile_sort(route_flat, h2, tm)
        ys = _moe(plan, xs, w_exp_gate, w_exp_up, w_exp_down, i, n_rows // MOE_BLOCK, TOP_K * tm)
        xf = _ple(route_flat, gate.reshape(-1), ys, x1, p[i].reshape(t, PLE_DIM), row2(norm_ple[i]),
                  w_ple_gate[i].astype(BF16), w_ple_proj[i].astype(BF16), row2(norm_final),
                  tm, final_norm=(i == depth - 1))
    return xf.reshape(bsz, seq, d)
```

```python
import functools

import jax
import jax.numpy as jnp
from jax import lax
from jax.experimental import pallas as pl
from jax.experimental.pallas import tpu as pltpu

D_MODEL = 1024
N_HEADS = 8
HEAD_DIM = 64
ATTN_WIDTH = N_HEADS * HEAD_DIM
CONV_WIDTH = 512
CONV_K = 31
DILATIONS = (16, 4, 1)
BLK = 128
ROPE_THETA = 10000.0
N_GROUPS = 4
EXPERTS_PER_GROUP = 8
N_EXPERTS = N_GROUPS * EXPERTS_PER_GROUP
EXPERT_FF = 512
TOP_K = 2
MOE_BLOCK = 256
PLE_DIM = 256
EPS = 1e-6
NEG_INF = -1e30

LANES = 128
SUBLANES = 8
ROW_TILE = D_MODEL // LANES
VMEM_LIMIT = 56 * 1024 * 1024

F32 = jnp.float32
BF16 = jnp.bfloat16


def _cparams(sem):
    return pltpu.CompilerParams(dimension_semantics=sem, vmem_limit_bytes=VMEM_LIMIT)


def _rms(x, g):
    return x * lax.rsqrt(jnp.mean(x * x, axis=-1, keepdims=True) + EPS) * g


def _sigmoid(x):
    return 0.5 * jnp.tanh(0.5 * x) + 0.5


def _store_row_tiles(ref, x):
    rows = x.shape[0]
    for s in range(ROW_TILE):
        ref[pl.ds(s, rows, stride=ROW_TILE), :] = x[:, s * LANES:(s + 1) * LANES]


def _load_row_tiles(ref, rows):
    return jnp.concatenate([ref[pl.ds(s, rows, stride=ROW_TILE), :] for s in range(ROW_TILE)], axis=1)


QKV_ROWS = 256


def _inproj_kernel(x_ref, g_ref, w_ref, cos_ref, sin_ref, *out_refs):
    n_lay = len(DILATIONS)
    q_refs, k_refs, v_refs = (out_refs[a * n_lay:(a + 1) * n_lay] for a in range(3))
    slab_refs = out_refs[3 * n_lay:]
    tm = x_ref.shape[0]
    n_slab = ATTN_WIDTH // LANES

    def put(ref, d, r, row0, g, val):
        cols = slice(g * LANES, (g + 1) * LANES)
        if d == 1:
            ref[pl.ds(row0, val.shape[0]), cols] = val.astype(BF16)
        else:
            ref[0, r, pl.ds(row0, val.shape[0]), cols] = val.astype(BF16)

    def fill(raw, rows, refs, slabs, is_query, post):
        for g in range(n_slab):
            cols = slice(g * LANES, (g + 1) * LANES)
            t = post(raw[:, cols], rows)
            slabs[g, rows, :] = t
            for d, ref in zip(DILATIONS, refs):
                if d == 1 and not is_query:
                    ref[rows, cols] = t.astype(BF16)

    def relayout(refs, slabs, is_query):
        for d, ref in zip(DILATIONS, refs):
            groups = STATE_DILATION // d
            if d == 1 and not is_query:
                continue
            elif not is_query or groups == 1:
                def per_class(r, carry, d=d, ref=ref):
                    for g in range(n_slab):
                        put(ref, d, r, 0, g, slabs[g, pl.ds(r, tm // d, stride=d), :])
                    return carry
                lax.fori_loop(0, d, per_class, 0, unroll=min(d, 4))
            else:
                run = BLK // groups

                n_bl = tm // (d * BLK)

                def per_block(idx, carry, d=d, ref=ref, run=run, groups=groups, n_bl=n_bl):
                    r, bl = idx // n_bl, idx % n_bl
                    for a in range(groups):
                        src = pl.ds(d * BLK * bl + d * a + r, run, stride=STATE_DILATION)
                        for g in range(n_slab):
                            put(ref, d, r, pl.multiple_of(bl * BLK, BLK) + a * run, g, slabs[g, src, :])
                    return carry
                lax.fori_loop(0, d * n_bl, per_block, 0, unroll=2)

    lane = lax.broadcasted_iota(jnp.int32, (QKV_ROWS, LANES), 1)
    first_half = (lane % HEAD_DIM) < (HEAD_DIM // 2)

    def rope(t, rows):
        partner = jnp.where(first_half,
                            pltpu.roll(t, LANES - HEAD_DIM // 2, 1),
                            pltpu.roll(t, HEAD_DIM // 2, 1))
        return t * cos_ref[rows, :] + partner * sin_ref[rows, :]

    for r0 in range(0, tm, QKV_ROWS):
        rows = slice(r0, r0 + QKV_ROWS)
        h = _rms(x_ref[rows, :], g_ref[...]).astype(BF16)
        proj = lambda a: jnp.dot(h, w_ref[:, a * ATTN_WIDTH:(a + 1) * ATTN_WIDTH],
                                 preferred_element_type=F32)
        fill(proj(0), rows, q_refs, slab_refs[0], True, lambda t, rw: rope(t, rw) * (HEAD_DIM ** -0.5))
        fill(proj(1), rows, k_refs, slab_refs[1], False, rope)
        fill(proj(2), rows, v_refs, slab_refs[2], False, lambda t, rw: t)
    relayout(q_refs, slab_refs[0], True)
    relayout(k_refs, slab_refs[1], False)
    relayout(v_refs, slab_refs[2], False)


def _gateproj_kernel(x_ref, g_ref, w_ref, u_ref, ga_ref, gb_ref):
    h = _rms(x_ref[...], g_ref[...]).astype(BF16)

    def proj(c0, width):
        return jnp.dot(h, w_ref[:, c0:c0 + width], preferred_element_type=F32)

    u_ref[...] = proj(0, CONV_WIDTH) * _sigmoid(proj(CONV_WIDTH, CONV_WIDTH))
    c0 = 2 * CONV_WIDTH
    ga_ref[...] = proj(c0, D_MODEL).astype(ga_ref.dtype)
    gb_ref[...] = proj(c0 + D_MODEL, D_MODEL).astype(gb_ref.dtype)


def _gateproj(x, gain, w_rest, tm):
    t = x.shape[0]
    row = lambda i: (i, 0)
    const = lambda i: (0, 0)
    return pl.pallas_call(
        _gateproj_kernel,
        grid=(t // tm,),
        in_specs=[pl.BlockSpec((tm, D_MODEL), row), pl.BlockSpec((1, D_MODEL), const),
                  pl.BlockSpec(w_rest.shape, const)],
        out_specs=(pl.BlockSpec((tm, CONV_WIDTH), row), pl.BlockSpec((tm, D_MODEL), row),
                   pl.BlockSpec((tm, D_MODEL), row)),
        out_shape=(jax.ShapeDtypeStruct((t, CONV_WIDTH), F32),
                   jax.ShapeDtypeStruct((t, D_MODEL), BF16),
                   jax.ShapeDtypeStruct((t, D_MODEL), BF16)),
        compiler_params=_cparams(("parallel",)),
        name="gateproj",
    )(x, gain, w_rest)


def _inproj(x, gain, w_in, cos, sin, bsz, seq, tm):
    t = x.shape[0]
    n_pos = seq // tm
    row = lambda i: (i, 0)
    const = lambda i: (0, 0)
    qkv_shapes, qkv_specs = [], []
    for d in DILATIONS:
        if d == 1:
            qkv_shapes.append(jax.ShapeDtypeStruct((t, ATTN_WIDTH), BF16))
            qkv_specs.append(pl.BlockSpec((tm, ATTN_WIDTH), row))
        else:
            qkv_shapes.append(jax.ShapeDtypeStruct((bsz, d, seq // d, ATTN_WIDTH), BF16))
            qkv_specs.append(pl.BlockSpec((1, d, tm // d, ATTN_WIDTH),
                                          lambda i: (i // n_pos, 0, i % n_pos, 0)))
    out_shape = tuple(qkv_shapes * 3)
    out_specs = tuple(qkv_specs * 3)
    outs = pl.pallas_call(
        _inproj_kernel,
        grid=(t // tm,),
        in_specs=[
            pl.BlockSpec((tm, D_MODEL), row),
            pl.BlockSpec((1, D_MODEL), const),
            pl.BlockSpec(w_in.shape, const),
            pl.BlockSpec((tm, LANES), lambda i: (i % n_pos, 0)),
            pl.BlockSpec((tm, LANES), lambda i: (i % n_pos, 0)),
        ],
        out_specs=out_specs,
        out_shape=out_shape,
        scratch_shapes=[pltpu.VMEM((ATTN_WIDTH // LANES, tm, LANES), F32) for _ in range(3)],
        compiler_params=_cparams(("parallel",)),
        name="inproj",
    )(x, gain, w_in, cos, sin)
    n_lay = len(DILATIONS)
    as_rows = lambda a: a.reshape(bsz, seq, ATTN_WIDTH)
    return tuple([as_rows(a) for a in outs[j * n_lay:(j + 1) * n_lay]] for j in range(3))


STATE_DILATION = max(DILATIONS)


def _query_order(idx, groups):
    run = BLK // groups
    return groups * (idx % run) + idx // run


def _attn_kernel(*refs, seq):
    n_lay = len(DILATIONS)
    srcs = refs[:3 * n_lay]
    o_ref, buf, acc_s, m_s, l_s, sem = refs[3 * n_lay:]
    b = pl.program_id(0)
    n_b = pl.num_programs(0)
    n_blocks = seq // BLK
    pair_w = 2 * HEAD_DIM
    n_pair = ATTN_WIDTH // pair_w

    def load_sweep(lay, bi, slot):
        return [pltpu.make_async_copy(srcs[a * n_lay + lay].at[bi], buf.at[slot, a, pl.ds(BLK, seq)],
                                      sem.at[slot]) for a in range(3)]

    @pl.when(b == 0)
    def _():
        buf[:, :, 0:BLK, :] = jnp.zeros((2, 3, BLK, ATTN_WIDTH), BF16)
        for c in load_sweep(0, 0, 0):
            c.start()

    row = lax.broadcasted_iota(jnp.int32, (BLK, BLK), 0)
    col = lax.broadcasted_iota(jnp.int32, (BLK, BLK), 1)
    lane = lax.broadcasted_iota(jnp.int32, (BLK, pair_w), 1)
    low_head = lane < HEAD_DIM
    unpermute = (row == _query_order(col, STATE_DILATION // DILATIONS[-1])).astype(BF16)

    for lay, d in enumerate(DILATIONS):
        slot = (b * n_lay + lay) % 2
        for c in load_sweep(lay, b, slot):
            c.wait()
        if lay + 1 < n_lay:
            for c in load_sweep(lay + 1, b, 1 - slot):
                c.start()
        else:
            @pl.when(b + 1 < n_b)
            def _():
                for c in load_sweep(0, b + 1, 1 - slot):
                    c.start()

        nb = seq // (d * BLK)
        has_prev = nb > 1
        first, final = lay == 0, lay == n_lay - 1
        kw = 2 * BLK if has_prev else BLK

        groups = STATE_DILATION // d
        run = BLK // groups
        q_pos = _query_order(row, groups)

        def block(j, carry, d=d, nb=nb, has_prev=has_prev, first=first, final=final, kw=kw,
                  slot=slot, groups=groups, run=run, q_pos=q_pos):
            r, n = j // nb, j % nb
            q_row0 = pl.multiple_of(BLK + j * BLK, BLK)
            k_row0 = pl.multiple_of(j * BLK, BLK) if has_prev else q_row0
            runs = [pl.ds(pl.multiple_of((d * a + r) * BLK + run * n, run), run)
                    for a in range(groups)]
            load = lambda ref, g: jnp.concatenate([ref[g, s, :] for s in runs], axis=0)

            def store(ref, g, val):
                for a, s in enumerate(runs):
                    ref[g, s, :] = val[a * run:(a + 1) * run]

            cur_ok = col <= q_pos
            if has_prev:
                prev_ok = jnp.logical_and(col >= q_pos, n > 0)
                allowed = jnp.concatenate([prev_ok, cur_ok], axis=1)
            else:
                allowed = cur_ok
            allowed2 = jnp.concatenate([allowed, allowed], axis=0)
            lanes = [slice(g * pair_w, (g + 1) * pair_w) for g in range(n_pair)]
            scores = []
            for ls in lanes:
                q2 = buf[slot, 0, pl.ds(q_row0, BLK), ls]
                q_both = jnp.concatenate([jnp.where(low_head, q2, 0),
                                          jnp.where(low_head, 0, q2)], axis=0)
                k2 = buf[slot, 1, pl.ds(k_row0, kw), ls]
                scores.append(lax.dot_general(q_both, k2, (((1,), (1,)), ((), ())),
                                              preferred_element_type=F32))
            probs, maxes, sums = [], [], []
            for s in scores:
                s = jnp.where(allowed2, s, NEG_INF)
                m_h = jnp.max(s, axis=-1, keepdims=True)
                p = jnp.exp(s - m_h)
                sums.append(jnp.sum(p, axis=-1, keepdims=True))
                maxes.append(m_h)
                probs.append(p.astype(BF16))
            pv = [jnp.dot(p, buf[slot, 2, pl.ds(k_row0, kw), ls], preferred_element_type=F32)
                  for p, ls in zip(probs, lanes)]
            for g, ls in enumerate(lanes):
                acc = jnp.where(low_head, pv[g][:BLK], pv[g][BLK:])
                m_new = jnp.where(low_head, maxes[g][:BLK], maxes[g][BLK:])
                l_new = jnp.where(low_head, sums[g][:BLK], sums[g][BLK:])
                if not first:
                    m_old = load(m_s, g)
                    m_tot = jnp.maximum(m_old, m_new)
                    a_old = jnp.exp(m_old - m_tot)
                    a_new = jnp.exp(m_new - m_tot)
                    acc = load(acc_s, g) * a_old + acc * a_new
                    l_new = load(l_s, g) * a_old + l_new * a_new
                    m_new = m_tot
                if final:
                    o_blk = jnp.dot(unpermute, (acc / l_new).astype(BF16), preferred_element_type=F32)
                    o_ref[0, pl.ds(pl.multiple_of(j * BLK, BLK), BLK), ls] = o_blk.astype(o_ref.dtype)
                else:
                    store(acc_s, g, acc)
                    store(m_s, g, m_new)
                    store(l_s, g, l_new)
            return carry

        lax.fori_loop(0, n_blocks, block, 0, unroll=2)


def _attention(q, k, v):
    bsz, seq, _ = q[0].shape
    assert all(seq % (d * BLK) == 0 for d in DILATIONS) and DILATIONS[-1] == 1
    n_pair = ATTN_WIDTH // (2 * HEAD_DIM)
    return pl.pallas_call(
        functools.partial(_attn_kernel, seq=seq),
        grid=(bsz,),
        in_specs=[pl.BlockSpec(memory_space=pl.ANY)] * (3 * len(DILATIONS)),
        out_specs=pl.BlockSpec((1, seq, ATTN_WIDTH), lambda b: (b, 0, 0)),
        out_shape=jax.ShapeDtypeStruct((bsz, seq, ATTN_WIDTH), BF16),
        scratch_shapes=[pltpu.VMEM((2, 3, BLK + seq, ATTN_WIDTH), BF16),
                        pltpu.VMEM((n_pair, seq, 2 * HEAD_DIM), F32),
                        pltpu.VMEM((n_pair, seq, 2 * HEAD_DIM), F32),
                        pltpu.VMEM((n_pair, seq, 2 * HEAD_DIM), F32),
                        pltpu.SemaphoreType.DMA((2,))],
        compiler_params=_cparams(("arbitrary",)),
        name="attention",
    )(*q, *k, *v)


CONV_HALO = 32
CONV_ROWS = 32


def _conv_kernel(halo_ref, cur_ref, w_ref, b_ref, g_ref, beta_ref, out_ref, ext_ref):
    ts = cur_ref.shape[1]
    i = pl.program_id(1)
    ext_ref[0, 0:CONV_HALO, :] = jnp.where(i > 0, halo_ref[0], 0.0)
    ext_ref[0, CONV_HALO:CONV_HALO + ts, :] = cur_ref[0]
    first = CONV_HALO - (CONV_K - 1)
    n_shift = ext_ref.shape[0]
    rows = CONV_HALO + ts
    ext0 = ext_ref[0]
    for s in range(1, n_shift):
        ext_ref[s] = pltpu.roll(ext0, rows - s, 0)
    bias = b_ref[...]
    gamma = g_ref[...]
    beta = beta_ref[...]
    for r0 in range(0, ts, CONV_ROWS):
        acc = jnp.zeros((CONV_ROWS, CONV_WIDTH), F32) + bias
        for j in range(CONV_K):
            lo = first + j + r0
            s, lo = lo % n_shift, lo - lo % n_shift
            tap = jnp.concatenate([w_ref[j]] * (CONV_ROWS // SUBLANES), axis=0)
            acc = acc + tap * ext_ref[s, lo:lo + CONV_ROWS, :]
        mu = jnp.mean(acc, axis=-1, keepdims=True)
        cen = acc - mu
        var = jnp.mean(cen * cen, axis=-1, keepdims=True)
        y = cen * lax.rsqrt(var + EPS) * gamma + beta
        out_ref[0, r0:r0 + CONV_ROWS, :] = (y * _sigmoid(y)).astype(out_ref.dtype)


def _conv(u, w_dw, b_dw, ln_g, ln_b, ts):
    b, s, c = u.shape
    per = ts // CONV_HALO
    const = lambda bi, i: (0, 0)
    return pl.pallas_call(
        _conv_kernel,
        grid=(b, s // ts),
        in_specs=[
            pl.BlockSpec((1, CONV_HALO, c), lambda bi, i: (bi, jnp.maximum(i * per - 1, 0), 0)),
            pl.BlockSpec((1, ts, c), lambda bi, i: (bi, i, 0)),
            pl.BlockSpec((CONV_K, SUBLANES, c), lambda bi, i: (0, 0, 0)),
            pl.BlockSpec((1, c), const),
            pl.BlockSpec((1, c), const),
            pl.BlockSpec((1, c), const),
        ],
        out_specs=pl.BlockSpec((1, ts, c), lambda bi, i: (bi, i, 0)),
        out_shape=jax.ShapeDtypeStruct((b, s, c), BF16),
        scratch_shapes=[pltpu.VMEM((SUBLANES, CONV_HALO + ts, c), F32)],
        compiler_params=_cparams(("parallel", "arbitrary")),
        name="conv",
    )(u, u, jnp.broadcast_to(w_dw[:, None, :], (CONV_K, SUBLANES, c)), b_dw, ln_g, ln_b)


ROUTE_COLS = 4
EXPERT_LANE0 = N_GROUPS


def _merge_kernel(o_ref, c_ref, ga_ref, gb_ref, x_ref, wa_ref, wc_ref, bc_ref, wo_ref,
                  gf_ref, wrh_ref, wrl_ref, br_ref,
                  x1_ref, h2_ref, route_ref, gate_ref, cnt_ref, tri_ref):
    tm = x_ref.shape[0]

    @pl.when(pl.program_id(0) == 0)
    def _():
        r = lax.broadcasted_iota(jnp.int32, (tm, tm), 0)
        c = lax.broadcasted_iota(jnp.int32, (tm, tm), 1)
        tri_ref[...] = (c < r).astype(BF16)

    y_a = jnp.dot(o_ref[...], wa_ref[...], preferred_element_type=F32)
    y_b = jnp.dot(c_ref[...], wc_ref[...], preferred_element_type=F32) + bc_ref[...]
    merged = _sigmoid(ga_ref[...].astype(F32)) * y_a + _sigmoid(gb_ref[...].astype(F32)) * y_b
    x1 = x_ref[...] + jnp.dot(merged.astype(BF16), wo_ref[...], preferred_element_type=F32)
    x1_ref[...] = x1
    h2 = _rms(x1, gf_ref[...])
    _store_row_tiles(h2_ref, h2)

    h_hi = h2.astype(BF16)
    h_lo = (h2 - h_hi.astype(F32)).astype(BF16)
    logits = (jnp.dot(h_hi, wrh_ref[...], preferred_element_type=F32)
              + jnp.dot(h_hi, wrl_ref[...], preferred_element_type=F32)
              + jnp.dot(h_lo, wrh_ref[...], preferred_element_type=F32)) + br_ref[...]

    lane = lax.broadcasted_iota(jnp.int32, (tm, LANES), 1).astype(F32)
    no_lane = float(LANES)
    is_group = lane < N_GROUPS
    g_max = jnp.max(jnp.where(is_group, logits, -jnp.inf), axis=-1, keepdims=True)
    g_sel = jnp.min(jnp.where(jnp.logical_and(is_group, logits == g_max), lane, no_lane),
                    axis=-1, keepdims=True)
    p_group = 1.0 / jnp.sum(jnp.where(is_group, jnp.exp(logits - g_max), 0.0),
                            axis=-1, keepdims=True)
    lo = EXPERT_LANE0 + EXPERTS_PER_GROUP * g_sel
    in_sel = jnp.logical_and(lane >= lo, lane < lo + EXPERTS_PER_GROUP)
    cand = jnp.where(in_sel, logits, -jnp.inf)
    v1 = jnp.max(cand, axis=-1, keepdims=True)
    i1 = jnp.min(jnp.where(jnp.logical_and(in_sel, cand == v1), lane, no_lane), axis=-1, keepdims=True)
    rest = jnp.logical_and(in_sel, lane != i1)
    cand2 = jnp.where(rest, logits, -jnp.inf)
    v2 = jnp.max(cand2, axis=-1, keepdims=True)
    i2 = jnp.min(jnp.where(jnp.logical_and(rest, cand2 == v2), lane, no_lane), axis=-1, keepdims=True)
    e2 = jnp.exp(v2 - v1)
    gate0 = p_group / (1.0 + e2)
    gate1 = p_group * e2 / (1.0 + e2)

    oh0 = (lane == i1)
    oh1 = (lane == i2)
    before0 = jnp.dot(tri_ref[...], oh0.astype(BF16), preferred_element_type=F32)
    before1 = jnp.dot(tri_ref[...], oh1.astype(BF16), preferred_element_type=F32)
    tot0 = jnp.sum(oh0.astype(F32), axis=0, keepdims=True)
    tot1 = jnp.sum(oh1.astype(F32), axis=0, keepdims=True)
    cnt = tot0 + tot1
    row_lane = lax.broadcasted_iota(jnp.int32, (1, LANES), 1)
    first_row = cnt
    shift = 1
    while shift < LANES:
        first_row = first_row + jnp.where(row_lane >= shift, pltpu.roll(first_row, shift, 1), 0.0)
        shift *= 2
    first_row = first_row - cnt
    pos0 = jnp.sum(jnp.where(oh0, first_row + before0, 0.0), axis=-1, keepdims=True)
    pos1 = jnp.sum(jnp.where(oh1, first_row + tot0 + before1, 0.0), axis=-1, keepdims=True)

    rc = lax.broadcasted_iota(jnp.int32, (tm, ROUTE_COLS), 1)
    route = jnp.where(rc == 0, i1 - EXPERT_LANE0,
                      jnp.where(rc == 1, i2 - EXPERT_LANE0, jnp.where(rc == 2, pos0, pos1)))
    route_ref[...] = route.astype(jnp.int32)
    gc = lax.broadcasted_iota(jnp.int32, (tm, TOP_K), 1)
    gate_ref[...] = jnp.where(gc == 0, gate0, gate1)
    cnt_ref[0] = jnp.broadcast_to(pltpu.roll(cnt, LANES - EXPERT_LANE0, 1), (8, LANES))


def _merge(o, c, ga, gb, x, wa, wc, bc, wo, gf, wr_hi, wr_lo, br, tm):
    t = x.shape[0]
    nt = t // tm
    row = lambda i: (i, 0)
    const = lambda i: (0, 0)
    full = lambda a: pl.BlockSpec(a.shape, const)
    return pl.pallas_call(
        _merge_kernel,
        grid=(nt,),
        in_specs=[
            pl.BlockSpec((tm, ATTN_WIDTH), row),
            pl.BlockSpec((tm, CONV_WIDTH), row),
            pl.BlockSpec((tm, D_MODEL), row),
            pl.BlockSpec((tm, D_MODEL), row),
            pl.BlockSpec((tm, D_MODEL), row),
            full(wa), full(wc), full(bc), full(wo), full(gf), full(wr_hi), full(wr_lo), full(br),
        ],
        out_specs=(
            pl.BlockSpec((tm, D_MODEL), row),
            pl.BlockSpec((tm * ROW_TILE, LANES), row),
            pl.BlockSpec((tm, ROUTE_COLS), row),
            pl.BlockSpec((tm, TOP_K), row),
            pl.BlockSpec((1, 8, LANES), lambda i: (i, 0, 0)),
        ),
        out_shape=(
            jax.ShapeDtypeStruct((t, D_MODEL), F32),
            jax.ShapeDtypeStruct((t * ROW_TILE, LANES), F32),
            jax.ShapeDtypeStruct((t, ROUTE_COLS), jnp.int32),
            jax.ShapeDtypeStruct((t, TOP_K), F32),
            jax.ShapeDtypeStruct((nt, 8, LANES), F32),
        ),
        scratch_shapes=[pltpu.VMEM((tm, tm), BF16)],
        compiler_params=_cparams(("arbitrary",)),
        name="merge",
    )(o, c, ga, gb, x, wa, wc, bc, wo, gf, wr_hi, wr_lo, br)


def _sort_kernel(route_ref, h_ref, xs_ref):
    tm = h_ref.shape[0] // ROW_TILE

    def place(t, carry):
        row = h_ref[pl.ds(pl.multiple_of(t * ROW_TILE, ROW_TILE), ROW_TILE), :]
        for k in range(TOP_K):
            pos = route_ref[ROUTE_COLS * t + TOP_K + k]
            xs_ref[pl.ds(pl.multiple_of(pos * ROW_TILE, ROW_TILE), ROW_TILE), :] = row
        return carry

    lax.fori_loop(0, tm, place, 0, unroll=8)


def _tile_sort(route_flat, h2, tm):
    nt = h2.shape[0] // (tm * ROW_TILE)
    return pl.pallas_call(
        _sort_kernel,
        grid=(nt,),
        in_specs=[pl.BlockSpec((ROUTE_COLS * tm,), lambda i: (i,), memory_space=pltpu.SMEM),
                  pl.BlockSpec((tm * ROW_TILE, LANES), lambda i: (i, 0))],
        out_specs=pl.BlockSpec((TOP_K * tm * ROW_TILE, LANES), lambda i: (i, 0)),
        out_shape=jax.ShapeDtypeStruct((TOP_K * h2.shape[0], LANES), F32),
        compiler_params=_cparams(("parallel",)),
        name="tile_sort",
    )(route_flat, h2)


def _moe_kernel(be_ref, bj_ref, ilo_ref, ihi_ref, valid_ref, cum_ref, cnt_ref, off_ref,
                xs_hbm, wg_ref, wu_ref, wd_ref, ys_hbm, trash_hbm,
                xbuf, obuf, wg_s, wu_s, wd_s, sem_in, sem_out, *, tile_rows):
    b = pl.program_id(0)
    last = pl.num_programs(0) - 1
    slot = b % 2
    other = 1 - slot
    rows8 = lambda r, n: pl.ds(pl.multiple_of(r * ROW_TILE, ROW_TILE), n * ROW_TILE)

    def for_runs(bb, fn):
        e = be_ref[bb]
        lo_blk = bj_ref[bb] * MOE_BLOCK

        def body(i, carry):
            c = cum_ref[i * N_EXPERTS + e]
            lo = jnp.maximum(c, lo_blk)
            hi = jnp.minimum(c + cnt_ref[i * N_EXPERTS + e], lo_blk + MOE_BLOCK)
            fn(i * tile_rows + off_ref[i * N_EXPERTS + e] + (lo - c), lo - lo_blk,
               jnp.maximum(hi - lo, 0))
            return carry

        lax.fori_loop(ilo_ref[bb], ihi_ref[bb] + 1, body, 0)

    def for_chunks(length, fn):
        for bit in reversed(range(MOE_BLOCK.bit_length())):
            size = 1 << bit

            @pl.when((length & size) != 0)
            def _():
                fn(length & ~(2 * size - 1), size)

    def gather(bb, s):
        def run(src, dst, length):
            for_chunks(length, lambda o, n: pltpu.make_async_copy(
                xs_hbm.at[rows8(src + o, n)], xbuf.at[s, rows8(dst + o, n)], sem_in.at[s]).start())
        for_runs(bb, run)
        v = valid_ref[bb]
        for_chunks(MOE_BLOCK - v, lambda o, n: pltpu.make_async_copy(
            xs_hbm.at[rows8(o, n)], xbuf.at[s, rows8(v + o, n)], sem_in.at[s]).start())

    def scatter(bb, s):
        def run(src, dst, length):
            for_chunks(length, lambda o, n: pltpu.make_async_copy(
                obuf.at[s, rows8(dst + o, n)], ys_hbm.at[rows8(src + o, n)], sem_out.at[s]).start())
        for_runs(bb, run)
        v = valid_ref[bb]
        for_chunks(MOE_BLOCK - v, lambda o, n: pltpu.make_async_copy(
            obuf.at[s, rows8(v + o, n)], trash_hbm.at[rows8(s * MOE_BLOCK + v + o, n)],
            sem_out.at[s]).start())

    def wait_gather(s):
        pltpu.make_async_copy(xs_hbm.at[rows8(0, MOE_BLOCK)], xbuf.at[s], sem_in.at[s]).wait()

    def wait_scatter(s):
        pltpu.make_async_copy(obuf.at[s], ys_hbm.at[rows8(0, MOE_BLOCK)], sem_out.at[s]).wait()

    @pl.when(b == 0)
    def _():
        obuf[...] = jnp.zeros_like(obuf)
        for s in range(2):
            init = pltpu.make_async_copy(obuf.at[s], trash_hbm.at[rows8(s * MOE_BLOCK, MOE_BLOCK)],
                                         sem_out.at[s])
            init.start()
            init.wait()
        gather(0, 0)

    wait_gather(slot)

    @pl.when(b >= 2)
    def _():
        wait_scatter(slot)

    e = be_ref[b]
    e_prev = be_ref[jnp.maximum(b - 1, 0)]

    @pl.when(jnp.logical_or(b == 0, e != e_prev))
    def _():
        wg_s[...] = wg_ref[0].astype(BF16)
        wu_s[...] = wu_ref[0].astype(BF16)
        wd_s[...] = wd_ref[0].astype(BF16)

    @pl.when(b < last)
    def _():
        gather(b + 1, other)

    @pl.when(b >= 1)
    def _():
        scatter(b - 1, other)

    x = _load_row_tiles(xbuf.at[slot], MOE_BLOCK).astype(BF16)
    hg = jnp.dot(x, wg_s[...], preferred_element_type=F32)
    hu = jnp.dot(x, wu_s[...], preferred_element_type=F32)
    hb = (hg * _sigmoid(hg) * hu).astype(BF16)
    _store_row_tiles(obuf.at[slot], jnp.dot(hb, wd_s[...], preferred_element_type=F32))

    @pl.when(b == last)
    def _():
        scatter(b, slot)
        wait_scatter(other)
        wait_scatter(slot)


def _moe(plan, xs, w_gate, w_up, w_down, layer, n_blocks, tile_rows):
    n_pre = len(plan)
    w_idx = lambda b, be, *_: (layer, be[b], 0, 0)
    grid_spec = pltpu.PrefetchScalarGridSpec(
        num_scalar_prefetch=n_pre,
        grid=(n_blocks,),
        in_specs=[
            pl.BlockSpec(memory_space=pl.ANY),
            pl.BlockSpec((None, 1, D_MODEL, EXPERT_FF), w_idx),
            pl.BlockSpec((None, 1, D_MODEL, EXPERT_FF), w_idx),
            pl.BlockSpec((None, 1, EXPERT_FF, D_MODEL), w_idx),
        ],
        out_specs=(pl.BlockSpec(memory_space=pl.ANY), pl.BlockSpec(memory_space=pl.ANY)),
        scratch_shapes=[pltpu.VMEM((2, MOE_BLOCK * ROW_TILE, LANES), F32),
                        pltpu.VMEM((2, MOE_BLOCK * ROW_TILE, LANES), F32),
                        pltpu.VMEM((D_MODEL, EXPERT_FF), BF16),
                        pltpu.VMEM((D_MODEL, EXPERT_FF), BF16),
                        pltpu.VMEM((EXPERT_FF, D_MODEL), BF16),
                        pltpu.SemaphoreType.DMA((2,)),
                        pltpu.SemaphoreType.DMA((2,))],
    )
    ys, _ = pl.pallas_call(
        functools.partial(_moe_kernel, tile_rows=tile_rows),
        grid_spec=grid_spec,
        out_shape=(jax.ShapeDtypeStruct(xs.shape, F32),
                   jax.ShapeDtypeStruct((2 * MOE_BLOCK * ROW_TILE, LANES), F32)),
        compiler_params=_cparams(("arbitrary",)),
        name="moe",
    )(*plan, xs, w_gate, w_up, w_down)
    return ys


def _ple_kernel(route_ref, gate_ref, ys_ref, x1_ref, p_ref, gp_ref, wpg_ref, wpe_ref, gfin_ref,
                out_ref, moe_ref, *, final_norm):
    tm = x1_ref.shape[0]
    tile = lambda r: pl.ds(pl.multiple_of(r * ROW_TILE, ROW_TILE), ROW_TILE)

    def combine(t, carry):
        acc = None
        for k in range(TOP_K):
            term = gate_ref[TOP_K * t + k] * ys_ref[tile(route_ref[ROUTE_COLS * t + TOP_K + k]), :]
            acc = term if acc is None else acc + term
        moe_ref[tile(t), :] = acc
        return carry

    lax.fori_loop(0, tm, combine, 0, unroll=8)
    x2 = x1_ref[...] + _load_row_tiles(moe_ref, tm)
    hp = _rms(x2, gp_ref[...]).astype(BF16)
    g_ple = _sigmoid(jnp.dot(hp, wpg_ref[...], preferred_element_type=F32))
    emb = jnp.dot(p_ref[...].astype(BF16), wpe_ref[...], preferred_element_type=F32)
    x3 = x2 + g_ple * emb
    if final_norm:
        x3 = _rms(x3, gfin_ref[...])
    out_ref[...] = x3


def _ple(route_flat, gate_flat, ys, x1, p, gp, wpg, wpe, gfin, tm, final_norm):
    t = x1.shape[0]
    row = lambda i: (i, 0)
    const = lambda i: (0, 0)
    full = lambda a: pl.BlockSpec(a.shape, const)
    smem = lambda n: pl.BlockSpec((n,), lambda i: (i,), memory_space=pltpu.SMEM)
    return pl.pallas_call(
        functools.partial(_ple_kernel, final_norm=final_norm),
        grid=(t // tm,),
        in_specs=[
            smem(ROUTE_COLS * tm),
            smem(TOP_K * tm),
            pl.BlockSpec((TOP_K * tm * ROW_TILE, LANES), row),
            pl.BlockSpec((tm, D_MODEL), row),
            pl.BlockSpec((tm, PLE_DIM), row),
            full(gp), full(wpg), full(wpe), full(gfin),
        ],
        out_specs=pl.BlockSpec((tm, D_MODEL), row),
        out_shape=jax.ShapeDtypeStruct((t, D_MODEL), F32),
        scratch_shapes=[pltpu.VMEM((tm * ROW_TILE, LANES), F32)],
        compiler_params=_cparams(("parallel",)),
        name="ple",
    )(route_flat, gate_flat, ys, x1, p, gp, wpg, wpe, gfin)


def _rope_tables(seq):
    inv = jnp.power(ROPE_THETA, -jnp.arange(0, HEAD_DIM, 2, dtype=F32) / HEAD_DIM)
    ang = jnp.arange(seq, dtype=F32)[:, None] * inv[None, :]
    cos, sin = jnp.cos(ang), jnp.sin(ang)
    cos_h = jnp.concatenate([cos, cos], axis=-1)
    sin_h = jnp.concatenate([-sin, sin], axis=-1)
    reps = LANES // HEAD_DIM
    return jnp.tile(cos_h, (1, reps)), jnp.tile(sin_h, (1, reps))


def _dispatch_plan(counts, n_blocks):
    i32 = jnp.int32
    total = jnp.sum(counts, axis=0)
    blocks_of = (total + MOE_BLOCK - 1) // MOE_BLOCK
    block_end = jnp.cumsum(blocks_of)
    bidx = jnp.arange(n_blocks, dtype=i32)
    block_e = jnp.minimum(jnp.sum((block_end[None, :] <= bidx[:, None]).astype(i32), axis=1),
                          N_EXPERTS - 1)
    is_e = (block_e[:, None] == jnp.arange(N_EXPERTS, dtype=i32)[None, :]).astype(i32)
    pick = lambda per_expert: jnp.sum(is_e * per_expert[None, :], axis=1)
    pick_tiles = lambda tab: jnp.sum(is_e[:, None, :] * tab[None, :, :], axis=2)
    block_j = bidx - pick(block_end - blocks_of)
    used = bidx < block_end[-1]
    cum = jnp.cumsum(counts, axis=0) - counts
    off = jnp.cumsum(counts, axis=1) - counts
    lo = block_j * MOE_BLOCK
    hi = jnp.minimum(lo + MOE_BLOCK, pick(total))
    run_start = pick_tiles(cum)
    run_end = run_start + pick_tiles(counts)
    tile_lo = jnp.sum((run_end <= lo[:, None]).astype(i32), axis=1)
    tile_hi = jnp.sum((run_start < hi[:, None]).astype(i32), axis=1) - 1
    valid = jnp.clip(hi - lo, 0, MOE_BLOCK)
    tile_lo = jnp.where(used, tile_lo, 0)
    tile_hi = jnp.where(used, tile_hi, -1)
    valid = jnp.where(used, valid, 0)
    flat = lambda a: a.reshape(-1).astype(i32)
    return (flat(block_e), flat(block_j), flat(tile_lo), flat(tile_hi), flat(valid),
            flat(cum), flat(counts), flat(off))


def kernel(x, p, norm_mix, w_in, w_dw, b_dw, ln_conv_g, ln_conv_b, w_attn_out, w_conv_out,
           b_conv_out, w_out, norm_ffn, w_route_group, b_route_group, w_route_expert,
           b_route_expert, w_exp_gate, w_exp_up, w_exp_down, norm_ple, w_ple_proj, w_ple_gate,
           norm_final):
    bsz, seq, d = x.shape
    depth = w_in.shape[0]
    t = bsz * seq
    tm = 512
    n_assign = t * TOP_K
    n_rows = (-(-n_assign // MOE_BLOCK)) * MOE_BLOCK + N_EXPERTS * MOE_BLOCK
    cos, sin = _rope_tables(seq)
    row2 = lambda a: a.reshape(1, -1)

    xf = x.reshape(t, d)
    for i in range(depth):
        n_qkv = 3 * ATTN_WIDTH
        q, k, v = _inproj(xf, row2(norm_mix[i]), w_in[i, :, :n_qkv].astype(BF16), cos, sin,
                          bsz, seq, tm)
        u, ga, gb = _gateproj(xf, row2(norm_mix[i]), w_in[i, :, n_qkv:].astype(BF16), tm)
        o = _attention(q, k, v).reshape(t, ATTN_WIDTH)
        c = _conv(u.reshape(bsz, seq, CONV_WIDTH), w_dw[i], row2(b_dw[i]), row2(ln_conv_g[i]),
                  row2(ln_conv_b[i]), 256).reshape(t, CONV_WIDTH)

        w_r = jnp.zeros((d, LANES), F32)
        w_r = w_r.at[:, :N_GROUPS].set(w_route_group[i])
        w_r = w_r.at[:, EXPERT_LANE0:EXPERT_LANE0 + N_EXPERTS].set(w_route_expert[i])
        w_r_hi = w_r.astype(BF16)
        w_r_lo = (w_r - w_r_hi.astype(F32)).astype(BF16)
        b_r = jnp.zeros((1, LANES), F32)
        b_r = b_r.at[0, :N_GROUPS].set(b_route_group[i])
        b_r = b_r.at[0, EXPERT_LANE0:EXPERT_LANE0 + N_EXPERTS].set(b_route_expert[i])

        x1, h2, route, gate, cnt = _merge(
            o, c, ga, gb, xf, w_attn_out[i].astype(BF16), w_conv_out[i].astype(BF16),
            row2(b_conv_out[i]), w_out[i].astype(BF16), row2(norm_ffn[i]), w_r_hi, w_r_lo, b_r, tm)

        counts = cnt[:, 0, :N_EXPERTS].astype(jnp.int32)
        plan = _dispatch_plan(counts, n_rows // MOE_BLOCK)
        route_flat = route.reshape(-1)
        xs = _tile_sort(route_flat, h2, tm)
        ys = _moe(plan, xs, w_exp_gate, w_exp_up, w_exp_down, i, n_rows // MOE_BLOCK, TOP_K * tm)
        xf = _ple(route_flat, gate.reshape(-1), ys, x1, p[i].reshape(t, PLE_DIM), row2(norm_ple[i]),
                  w_ple_gate[i].astype(BF16), w_ple_proj[i].astype(BF16), row2(norm_final),
                  tm, final_norm=(i == depth - 1))
    return xf.reshape(bsz, seq, d)
```

```python
import functools

import jax
import jax.numpy as jnp
from jax import lax
from jax.experimental import pallas as pl
from jax.experimental.pallas import tpu as pltpu

D_MODEL = 1024
N_HEADS = 8
HEAD_DIM = 64
ATTN_WIDTH = N_HEADS * HEAD_DIM
CONV_WIDTH = 512
CONV_K = 31
DILATIONS = (16, 4, 1)
BLK = 128
ROPE_THETA = 10000.0
N_GROUPS = 4
EXPERTS_PER_GROUP = 8
N_EXPERTS = N_GROUPS * EXPERTS_PER_GROUP
EXPERT_FF = 512
TOP_K = 2
MOE_BLOCK = 256
PLE_DIM = 256
EPS = 1e-6
NEG_INF = -1e30

LANES = 128
SUBLANES = 8
ROW_TILE = D_MODEL // LANES
VMEM_LIMIT = 56 * 1024 * 1024

F32 = jnp.float32
BF16 = jnp.bfloat16


def _cparams(sem):
    return pltpu.CompilerParams(dimension_semantics=sem, vmem_limit_bytes=VMEM_LIMIT)


def _rms(x, g):
    return x * lax.rsqrt(jnp.mean(x * x, axis=-1, keepdims=True) + EPS) * g


def _sigmoid(x):
    return 0.5 * jnp.tanh(0.5 * x) + 0.5


def _store_row_tiles(ref, x):
    rows = x.shape[0]
    for s in range(ROW_TILE):
        ref[pl.ds(s, rows, stride=ROW_TILE), :] = x[:, s * LANES:(s + 1) * LANES]


def _load_row_tiles(ref, rows):
    return jnp.concatenate([ref[pl.ds(s, rows, stride=ROW_TILE), :] for s in range(ROW_TILE)], axis=1)


QKV_ROWS = 256


def _inproj_kernel(x_ref, g_ref, w_ref, cos_ref, sin_ref, *out_refs):
    n_lay = len(DILATIONS)
    q_refs, k_refs, v_refs = (out_refs[a * n_lay:(a + 1) * n_lay] for a in range(3))
    slab_refs = out_refs[3 * n_lay:]
    tm = x_ref.shape[0]
    n_slab = ATTN_WIDTH // LANES

    def put(ref, d, r, row0, g, val):
        cols = slice(g * LANES, (g + 1) * LANES)
        if d == 1:
            ref[pl.ds(row0, val.shape[0]), cols] = val.astype(BF16)
        else:
            ref[0, r, pl.ds(row0, val.shape[0]), cols] = val.astype(BF16)

    def fill(raw, rows, refs, slabs, is_query, post):
        for g in range(n_slab):
            cols = slice(g * LANES, (g + 1) * LANES)
            t = post(raw[:, cols], rows)
            slabs[g, rows, :] = t
            for d, ref in zip(DILATIONS, refs):
                if d == 1 and not is_query:
                    ref[rows, cols] = t.astype(BF16)

    def relayout(refs, slabs, by4, is_query):
        ref16, ref4, ref1 = refs
        quarter = tm // 4
        class4 = lambda r: pl.multiple_of(r * quarter, quarter)

        def to_by4(r, carry):
            for g in range(n_slab):
                val = slabs[g, pl.ds(r, quarter, stride=4), :]
                by4[g, pl.ds(class4(r), quarter), :] = val
                if not is_query:
                    put(ref4, 4, r, 0, g, val)
            return carry
        lax.fori_loop(0, 4, to_by4, 0, unroll=2)

        def to_16(r, carry):
            for g in range(n_slab):
                put(ref16, 16, r, 0, g, by4[g, pl.ds(class4(r % 4) + r // 4, tm // 16, stride=4), :])
            return carry
        lax.fori_loop(0, 16, to_16, 0, unroll=4)

        if is_query:
            def q4_class(r, carry):
                for bl in range(quarter // BLK):
                    for a in range(4):
                        src = pl.ds(class4(r) + bl * BLK + a, BLK // 4, stride=4)
                        for g in range(n_slab):
                            put(ref4, 4, r, bl * BLK + a * (BLK // 4), g, by4[g, src, :])
                return carry
            lax.fori_loop(0, 4, q4_class, 0, unroll=2)

            def q1_block(bl, carry):
                for a in range(16):
                    src = pl.ds((a % 4) * quarter + (BLK // 4) * bl + a // 4, BLK // 16, stride=4)
                    for g in range(n_slab):
                        put(ref1, 1, 0, pl.multiple_of(bl * BLK, BLK) + a * (BLK // 16), g, by4[g, src, :])
                return carry
            lax.fori_loop(0, tm // BLK, q1_block, 0, unroll=2)

    lane = lax.broadcasted_iota(jnp.int32, (QKV_ROWS, LANES), 1)
    first_half = (lane % HEAD_DIM) < (HEAD_DIM // 2)

    def rope(t, rows):
        partner = jnp.where(first_half,
                            pltpu.roll(t, LANES - HEAD_DIM // 2, 1),
                            pltpu.roll(t, HEAD_DIM // 2, 1))
        return t * cos_ref[rows, :] + partner * sin_ref[rows, :]

    for r0 in range(0, tm, QKV_ROWS):
        rows = slice(r0, r0 + QKV_ROWS)
        h = _rms(x_ref[rows, :], g_ref[...]).astype(BF16)
        proj = lambda a: jnp.dot(h, w_ref[:, a * ATTN_WIDTH:(a + 1) * ATTN_WIDTH],
                                 preferred_element_type=F32)
        fill(proj(0), rows, q_refs, slab_refs[0], True, lambda t, rw: rope(t, rw) * (HEAD_DIM ** -0.5))
        fill(proj(1), rows, k_refs, slab_refs[1], False, rope)
        fill(proj(2), rows, v_refs, slab_refs[2], False, lambda t, rw: t)
    assert DILATIONS == (16, 4, 1) and tm % (4 * BLK) == 0
    relayout(q_refs, slab_refs[0], slab_refs[3], True)
    relayout(k_refs, slab_refs[1], slab_refs[4], False)
    relayout(v_refs, slab_refs[2], slab_refs[5], False)


def _gateproj_kernel(x_ref, g_ref, w_ref, u_ref, ga_ref, gb_ref):
    h = _rms(x_ref[...], g_ref[...]).astype(BF16)

    def proj(c0, width):
        return jnp.dot(h, w_ref[:, c0:c0 + width], preferred_element_type=F32)

    u_ref[...] = proj(0, CONV_WIDTH) * _sigmoid(proj(CONV_WIDTH, CONV_WIDTH))
    c0 = 2 * CONV_WIDTH
    ga_ref[...] = proj(c0, D_MODEL).astype(ga_ref.dtype)
    gb_ref[...] = proj(c0 + D_MODEL, D_MODEL).astype(gb_ref.dtype)


def _gateproj(x, gain, w_rest, tm):
    t = x.shape[0]
    row = lambda i: (i, 0)
    const = lambda i: (0, 0)
    return pl.pallas_call(
        _gateproj_kernel,
        grid=(t // tm,),
        in_specs=[pl.BlockSpec((tm, D_MODEL), row), pl.BlockSpec((1, D_MODEL), const),
                  pl.BlockSpec(w_rest.shape, const)],
        out_specs=(pl.BlockSpec((tm, CONV_WIDTH), row), pl.BlockSpec((tm, D_MODEL), row),
                   pl.BlockSpec((tm, D_MODEL), row)),
        out_shape=(jax.ShapeDtypeStruct((t, CONV_WIDTH), F32),
                   jax.ShapeDtypeStruct((t, D_MODEL), BF16),
                   jax.ShapeDtypeStruct((t, D_MODEL), BF16)),
        compiler_params=_cparams(("parallel",)),
        name="gateproj",
    )(x, gain, w_rest)


def _inproj(x, gain, w_in, cos, sin, bsz, seq, tm):
    t = x.shape[0]
    n_pos = seq // tm
    row = lambda i: (i, 0)
    const = lambda i: (0, 0)
    qkv_shapes, qkv_specs = [], []
    for d in DILATIONS:
        if d == 1:
            qkv_shapes.append(jax.ShapeDtypeStruct((t, ATTN_WIDTH), BF16))
            qkv_specs.append(pl.BlockSpec((tm, ATTN_WIDTH), row))
        else:
            qkv_shapes.append(jax.ShapeDtypeStruct((bsz, d, seq // d, ATTN_WIDTH), BF16))
            qkv_specs.append(pl.BlockSpec((1, d, tm // d, ATTN_WIDTH),
                                          lambda i: (i // n_pos, 0, i % n_pos, 0)))
    out_shape = tuple(qkv_shapes * 3)
    out_specs = tuple(qkv_specs * 3)
    outs = pl.pallas_call(
        _inproj_kernel,
        grid=(t // tm,),
        in_specs=[
            pl.BlockSpec((tm, D_MODEL), row),
            pl.BlockSpec((1, D_MODEL), const),
            pl.BlockSpec(w_in.shape, const),
            pl.BlockSpec((tm, LANES), lambda i: (i % n_pos, 0)),
            pl.BlockSpec((tm, LANES), lambda i: (i % n_pos, 0)),
        ],
        out_specs=out_specs,
        out_shape=out_shape,
        scratch_shapes=[pltpu.VMEM((ATTN_WIDTH // LANES, tm, LANES), F32) for _ in range(6)],
        compiler_params=_cparams(("parallel",)),
        name="inproj",
    )(x, gain, w_in, cos, sin)
    n_lay = len(DILATIONS)
    as_rows = lambda a: a.reshape(bsz, seq, ATTN_WIDTH)
    return tuple([as_rows(a) for a in outs[j * n_lay:(j + 1) * n_lay]] for j in range(3))


STATE_DILATION = max(DILATIONS)


def _query_order(idx, groups):
    run = BLK // groups
    return groups * (idx % run) + idx // run


def _attn_kernel(*refs, seq):
    n_lay = len(DILATIONS)
    srcs = refs[:3 * n_lay]
    o_ref, buf, acc_s, m_s, l_s, sem = refs[3 * n_lay:]
    b = pl.program_id(0)
    n_b = pl.num_programs(0)
    n_blocks = seq // BLK
    pair_w = 2 * HEAD_DIM
    n_pair = ATTN_WIDTH // pair_w

    def load_sweep(lay, bi, slot):
        return [pltpu.make_async_copy(srcs[a * n_lay + lay].at[bi], buf.at[slot, a, pl.ds(BLK, seq)],
                                      sem.at[slot]) for a in range(3)]

    @pl.when(b == 0)
    def _():
        buf[:, :, 0:BLK, :] = jnp.zeros((2, 3, BLK, ATTN_WIDTH), BF16)
        for c in load_sweep(0, 0, 0):
            c.start()

    row = lax.broadcasted_iota(jnp.int32, (BLK, BLK), 0)
    col = lax.broadcasted_iota(jnp.int32, (BLK, BLK), 1)
    lane = lax.broadcasted_iota(jnp.int32, (BLK, pair_w), 1)
    low_head = lane < HEAD_DIM
    unpermute = (row == _query_order(col, STATE_DILATION // DILATIONS[-1])).astype(BF16)

    for lay, d in enumerate(DILATIONS):
        slot = (b * n_lay + lay) % 2
        for c in load_sweep(lay, b, slot):
            c.wait()
        if lay + 1 < n_lay:
            for c in load_sweep(lay + 1, b, 1 - slot):
                c.start()
        else:
            @pl.when(b + 1 < n_b)
            def _():
                for c in load_sweep(0, b + 1, 1 - slot):
                    c.start()

        nb = seq // (d * BLK)
        has_prev = nb > 1
        first, final = lay == 0, lay == n_lay - 1
        kw = 2 * BLK if has_prev else BLK

        groups = STATE_DILATION // d
        run = BLK // groups
        q_pos = _query_order(row, groups)

        def block(j, carry, d=d, nb=nb, has_prev=has_prev, first=first, final=final, kw=kw,
                  slot=slot, groups=groups, run=run, q_pos=q_pos):
            r, n = j // nb, j % nb
            q_row0 = pl.multiple_of(BLK + j * BLK, BLK)
            k_row0 = pl.multiple_of(j * BLK, BLK) if has_prev else q_row0
            runs = [pl.ds(pl.multiple_of((d * a + r) * BLK + run * n, run), run)
                    for a in range(groups)]
            load = lambda ref, g: jnp.concatenate([ref[g, s, :] for s in runs], axis=0)

            def store(ref, g, val):
                for a, s in enumerate(runs):
                    ref[g, s, :] = val[a * run:(a + 1) * run]

            cur_ok = col <= q_pos
            if has_prev:
                prev_ok = jnp.logical_and(col >= q_pos, n > 0)
                allowed = jnp.concatenate([prev_ok, cur_ok], axis=1)
            else:
                allowed = cur_ok
            allowed2 = jnp.concatenate([allowed, allowed], axis=0)
            lanes = [slice(g * pair_w, (g + 1) * pair_w) for g in range(n_pair)]
            scores = []
            for ls in lanes:
                q2 = buf[slot, 0, pl.ds(q_row0, BLK), ls]
                q_both = jnp.concatenate([jnp.where(low_head, q2, 0),
                                          jnp.where(low_head, 0, q2)], axis=0)
                k2 = buf[slot, 1, pl.ds(k_row0, kw), ls]
                scores.append(lax.dot_general(q_both, k2, (((1,), (1,)), ((), ())),
                                              preferred_element_type=F32))
            probs, maxes, sums = [], [], []
            for s in scores:
                s = jnp.where(allowed2, s, NEG_INF)
                m_h = jnp.max(s, axis=-1, keepdims=True)
                p = jnp.exp(s - m_h)
                sums.append(jnp.sum(p, axis=-1, keepdims=True))
                maxes.append(m_h)
                probs.append(p.astype(BF16))
            pv = [jnp.dot(p, buf[slot, 2, pl.ds(k_row0, kw), ls], preferred_element_type=F32)
                  for p, ls in zip(probs, lanes)]
            for g, ls in enumerate(lanes):
                acc = jnp.where(low_head, pv[g][:BLK], pv[g][BLK:])
                m_new = jnp.where(low_head, maxes[g][:BLK], maxes[g][BLK:])
                l_new = jnp.where(low_head, sums[g][:BLK], sums[g][BLK:])
                if not first:
                    m_old = load(m_s, g)
                    m_tot = jnp.maximum(m_old, m_new)
                    a_old = jnp.exp(m_old - m_tot)
                    a_new = jnp.exp(m_new - m_tot)
                    acc = load(acc_s, g) * a_old + acc * a_new
                    l_new = load(l_s, g) * a_old + l_new * a_new
                    m_new = m_tot
                if final:
                    o_blk = jnp.dot(unpermute, (acc / l_new).astype(BF16), preferred_element_type=F32)
                    o_ref[0, pl.ds(pl.multiple_of(j * BLK, BLK), BLK), ls] = o_blk.astype(o_ref.dtype)
                else:
                    store(acc_s, g, acc)
                    store(m_s, g, m_new)
                    store(l_s, g, l_new)
            return carry

        lax.fori_loop(0, n_blocks, block, 0, unroll=2)


def _attention(q, k, v):
    bsz, seq, _ = q[0].shape
    assert all(seq % (d * BLK) == 0 for d in DILATIONS) and DILATIONS[-1] == 1
    n_pair = ATTN_WIDTH // (2 * HEAD_DIM)
    return pl.pallas_call(
        functools.partial(_attn_kernel, seq=seq),
        grid=(bsz,),
        in_specs=[pl.BlockSpec(memory_space=pl.ANY)] * (3 * len(DILATIONS)),
        out_specs=pl.BlockSpec((1, seq, ATTN_WIDTH), lambda b: (b, 0, 0)),
        out_shape=jax.ShapeDtypeStruct((bsz, seq, ATTN_WIDTH), BF16),
        scratch_shapes=[pltpu.VMEM((2, 3, BLK + seq, ATTN_WIDTH), BF16),
                        pltpu.VMEM((n_pair, seq, 2 * HEAD_DIM), F32),
                        pltpu.VMEM((n_pair, seq, 2 * HEAD_DIM), F32),
                        pltpu.VMEM((n_pair, seq, 2 * HEAD_DIM), F32),
                        pltpu.SemaphoreType.DMA((2,))],
        compiler_params=_cparams(("arbitrary",)),
        name="attention",
    )(*q, *k, *v)


CONV_HALO = 32
CONV_ROWS = 32


def _conv_kernel(halo_ref, cur_ref, w_ref, b_ref, g_ref, beta_ref, out_ref, ext_ref):
    ts = cur_ref.shape[1]
    i = pl.program_id(1)
    ext_ref[0, 0:CONV_HALO, :] = jnp.where(i > 0, halo_ref[0], 0.0)
    ext_ref[0, CONV_HALO:CONV_HALO + ts, :] = cur_ref[0]
    first = CONV_HALO - (CONV_K - 1)
    n_shift = ext_ref.shape[0]
    rows = CONV_HALO + ts
    ext0 = ext_ref[0]
    for s in range(1, n_shift):
        ext_ref[s] = pltpu.roll(ext0, rows - s, 0)
    bias = b_ref[...]
    gamma = g_ref[...]
    beta = beta_ref[...]
    for r0 in range(0, ts, CONV_ROWS):
        acc = jnp.zeros((CONV_ROWS, CONV_WIDTH), F32) + bias
        for j in range(CONV_K):
            lo = first + j + r0
            s, lo = lo % n_shift, lo - lo % n_shift
            tap = jnp.concatenate([w_ref[j]] * (CONV_ROWS // SUBLANES), axis=0)
            acc = acc + tap * ext_ref[s, lo:lo + CONV_ROWS, :]
        mu = jnp.mean(acc, axis=-1, keepdims=True)
        cen = acc - mu
        var = jnp.mean(cen * cen, axis=-1, keepdims=True)
        y = cen * lax.rsqrt(var + EPS) * gamma + beta
        out_ref[0, r0:r0 + CONV_ROWS, :] = (y * _sigmoid(y)).astype(out_ref.dtype)


def _conv(u, w_dw, b_dw, ln_g, ln_b, ts):
    b, s, c = u.shape
    per = ts // CONV_HALO
    const = lambda bi, i: (0, 0)
    return pl.pallas_call(
        _conv_kernel,
        grid=(b, s // ts),
        in_specs=[
            pl.BlockSpec((1, CONV_HALO, c), lambda bi, i: (bi, jnp.maximum(i * per - 1, 0), 0)),
            pl.BlockSpec((1, ts, c), lambda bi, i: (bi, i, 0)),
            pl.BlockSpec((CONV_K, SUBLANES, c), lambda bi, i: (0, 0, 0)),
            pl.BlockSpec((1, c), const),
            pl.BlockSpec((1, c), const),
            pl.BlockSpec((1, c), const),
        ],
        out_specs=pl.BlockSpec((1, ts, c), lambda bi, i: (bi, i, 0)),
        out_shape=jax.ShapeDtypeStruct((b, s, c), BF16),
        scratch_shapes=[pltpu.VMEM((SUBLANES, CONV_HALO + ts, c), F32)],
        compiler_params=_cparams(("parallel", "arbitrary")),
        name="conv",
    )(u, u, jnp.broadcast_to(w_dw[:, None, :], (CONV_K, SUBLANES, c)), b_dw, ln_g, ln_b)


ROUTE_COLS = 4
EXPERT_LANE0 = N_GROUPS


def _merge_kernel(o_ref, c_ref, ga_ref, gb_ref, x_ref, wa_ref, wc_ref, bc_ref, wo_ref,
                  gf_ref, wrh_ref, wrl_ref, br_ref,
                  x1_ref, h2_ref, route_ref, gate_ref, cnt_ref, tri_ref):
    tm = x_ref.shape[0]

    @pl.when(pl.program_id(0) == 0)
    def _():
        r = lax.broadcasted_iota(jnp.int32, (tm, tm), 0)
        c = lax.broadcasted_iota(jnp.int32, (tm, tm), 1)
        tri_ref[...] = (c < r).astype(BF16)

    y_a = jnp.dot(o_ref[...], wa_ref[...], preferred_element_type=F32)
    y_b = jnp.dot(c_ref[...], wc_ref[...], preferred_element_type=F32) + bc_ref[...]
    merged = _sigmoid(ga_ref[...].astype(F32)) * y_a + _sigmoid(gb_ref[...].astype(F32)) * y_b
    x1 = x_ref[...] + jnp.dot(merged.astype(BF16), wo_ref[...], preferred_element_type=F32)
    x1_ref[...] = x1
    h2 = _rms(x1, gf_ref[...])
    _store_row_tiles(h2_ref, h2)

    h_hi = h2.astype(BF16)
    h_lo = (h2 - h_hi.astype(F32)).astype(BF16)
    logits = (jnp.dot(h_hi, wrh_ref[...], preferred_element_type=F32)
              + jnp.dot(h_hi, wrl_ref[...], preferred_element_type=F32)
              + jnp.dot(h_lo, wrh_ref[...], preferred_element_type=F32)) + br_ref[...]

    lane = lax.broadcasted_iota(jnp.int32, (tm, LANES), 1).astype(F32)
    no_lane = float(LANES)
    is_group = lane < N_GROUPS
    g_max = jnp.max(jnp.where(is_group, logits, -jnp.inf), axis=-1, keepdims=True)
    g_sel = jnp.min(jnp.where(jnp.logical_and(is_group, logits == g_max), lane, no_lane),
                    axis=-1, keepdims=True)
    p_group = 1.0 / jnp.sum(jnp.where(is_group, jnp.exp(logits - g_max), 0.0),
                            axis=-1, keepdims=True)
    lo = EXPERT_LANE0 + EXPERTS_PER_GROUP * g_sel
    in_sel = jnp.logical_and(lane >= lo, lane < lo + EXPERTS_PER_GROUP)
    cand = jnp.where(in_sel, logits, -jnp.inf)
    v1 = jnp.max(cand, axis=-1, keepdims=True)
    i1 = jnp.min(jnp.where(jnp.logical_and(in_sel, cand == v1), lane, no_lane), axis=-1, keepdims=True)
    rest = jnp.logical_and(in_sel, lane != i1)
    cand2 = jnp.where(rest, logits, -jnp.inf)
    v2 = jnp.max(cand2, axis=-1, keepdims=True)
    i2 = jnp.min(jnp.where(jnp.logical_and(rest, cand2 == v2), lane, no_lane), axis=-1, keepdims=True)
    e2 = jnp.exp(v2 - v1)
    gate0 = p_group / (1.0 + e2)
    gate1 = p_group * e2 / (1.0 + e2)

    oh0 = (lane == i1)
    oh1 = (lane == i2)
    before0 = jnp.dot(tri_ref[...], oh0.astype(BF16), preferred_element_type=F32)
    before1 = jnp.dot(tri_ref[...], oh1.astype(BF16), preferred_element_type=F32)
    tot0 = jnp.sum(oh0.astype(F32), axis=0, keepdims=True)
    tot1 = jnp.sum(oh1.astype(F32), axis=0, keepdims=True)
    cnt = tot0 + tot1
    row_lane = lax.broadcasted_iota(jnp.int32, (1, LANES), 1)
    first_row = cnt
    shift = 1
    while shift < LANES:
        first_row = first_row + jnp.where(row_lane >= shift, pltpu.roll(first_row, shift, 1), 0.0)
        shift *= 2
    first_row = first_row - cnt
    pos0 = jnp.sum(jnp.where(oh0, first_row + before0, 0.0), axis=-1, keepdims=True)
    pos1 = jnp.sum(jnp.where(oh1, first_row + tot0 + before1, 0.0), axis=-1, keepdims=True)

    rc = lax.broadcasted_iota(jnp.int32, (tm, ROUTE_COLS), 1)
    route = jnp.where(rc == 0, i1 - EXPERT_LANE0,
                      jnp.where(rc == 1, i2 - EXPERT_LANE0, jnp.where(rc == 2, pos0, pos1)))
    route_ref[...] = route.astype(jnp.int32)
    gc = lax.broadcasted_iota(jnp.int32, (tm, TOP_K), 1)
    gate_ref[...] = jnp.where(gc == 0, gate0, gate1)
    cnt_ref[0] = jnp.broadcast_to(pltpu.roll(cnt, LANES - EXPERT_LANE0, 1), (8, LANES))


def _merge(o, c, ga, gb, x, wa, wc, bc, wo, gf, wr_hi, wr_lo, br, tm):
    t = x.shape[0]
    nt = t // tm
    row = lambda i: (i, 0)
    const = lambda i: (0, 0)
    full = lambda a: pl.BlockSpec(a.shape, const)
    return pl.pallas_call(
        _merge_kernel,
        grid=(nt,),
        in_specs=[
            pl.BlockSpec((tm, ATTN_WIDTH), row),
            pl.BlockSpec((tm, CONV_WIDTH), row),
            pl.BlockSpec((tm, D_MODEL), row),
            pl.BlockSpec((tm, D_MODEL), row),
            pl.BlockSpec((tm, D_MODEL), row),
            full(wa), full(wc), full(bc), full(wo), full(gf), full(wr_hi), full(wr_lo), full(br),
        ],
        out_specs=(
            pl.BlockSpec((tm, D_MODEL), row),
            pl.BlockSpec((tm * ROW_TILE, LANES), row),
            pl.BlockSpec((tm, ROUTE_COLS), row),
            pl.BlockSpec((tm, TOP_K), row),
            pl.BlockSpec((1, 8, LANES), lambda i: (i, 0, 0)),
        ),
        out_shape=(
            jax.ShapeDtypeStruct((t, D_MODEL), F32),
            jax.ShapeDtypeStruct((t * ROW_TILE, LANES), F32),
            jax.ShapeDtypeStruct((t, ROUTE_COLS), jnp.int32),
            jax.ShapeDtypeStruct((t, TOP_K), F32),
            jax.ShapeDtypeStruct((nt, 8, LANES), F32),
        ),
        scratch_shapes=[pltpu.VMEM((tm, tm), BF16)],
        compiler_params=_cparams(("arbitrary",)),
        name="merge",
    )(o, c, ga, gb, x, wa, wc, bc, wo, gf, wr_hi, wr_lo, br)


def _sort_kernel(route_ref, h_ref, xs_ref):
    tm = h_ref.shape[0] // ROW_TILE

    def place(t, carry):
        row = h_ref[pl.ds(pl.multiple_of(t * ROW_TILE, ROW_TILE), ROW_TILE), :]
        for k in range(TOP_K):
            pos = route_ref[ROUTE_COLS * t + TOP_K + k]
            xs_ref[pl.ds(pl.multiple_of(pos * ROW_TILE, ROW_TILE), ROW_TILE), :] = row
        return carry

    lax.fori_loop(0, tm, place, 0, unroll=8)


def _tile_sort(route_flat, h2, tm):
    nt = h2.shape[0] // (tm * ROW_TILE)
    return pl.pallas_call(
        _sort_kernel,
        grid=(nt,),
        in_specs=[pl.BlockSpec((ROUTE_COLS * tm,), lambda i: (i,), memory_space=pltpu.SMEM),
                  pl.BlockSpec((tm * ROW_TILE, LANES), lambda i: (i, 0))],
        out_specs=pl.BlockSpec((TOP_K * tm * ROW_TILE, LANES), lambda i: (i, 0)),
        out_shape=jax.ShapeDtypeStruct((TOP_K * h2.shape[0], LANES), F32),
        compiler_params=_cparams(("parallel",)),
        name="tile_sort",
    )(route_flat, h2)


RARE_RUN = 64


def _moe_kernel(be_ref, bj_ref, ilo_ref, ihi_ref, valid_ref, cum_ref, cnt_ref, off_ref, nused_ref,
                xs_hbm, wg_ref, wu_ref, wd_ref, ys_hbm, trash_hbm,
                xbuf, obuf, wg_s, wu_s, wd_s, sem_in, sem_out, *, tile_rows):
    b = pl.program_id(0)
    slot = b % 2
    other = 1 - slot
    rows8 = lambda r, n: pl.ds(pl.multiple_of(r * ROW_TILE, ROW_TILE), n * ROW_TILE)

    def for_runs(bb, fn):
        e = be_ref[bb]
        lo_blk = bj_ref[bb] * MOE_BLOCK

        def body(i, carry):
            c = cum_ref[i * N_EXPERTS + e]
            lo = jnp.maximum(c, lo_blk)
            hi = jnp.minimum(c + cnt_ref[i * N_EXPERTS + e], lo_blk + MOE_BLOCK)
            fn(i * tile_rows + off_ref[i * N_EXPERTS + e] + (lo - c), lo - lo_blk,
               jnp.maximum(hi - lo, 0))
            return carry

        lax.fori_loop(ilo_ref[bb], ihi_ref[bb] + 1, body, 0)

    def for_chunks(length, fn):
        def piece(size):
            @pl.when((length & size) != 0)
            def _():
                fn(length & ~(2 * size - 1), size)

        sizes = [1 << bit for bit in reversed(range(MOE_BLOCK.bit_length()))]

        @pl.when(length >= RARE_RUN)
        def _():
            for size in sizes:
                if size >= RARE_RUN:
                    piece(size)

        for size in sizes:
            if size < RARE_RUN:
                piece(size)

    def gather(bb, s):
        def run(src, dst, length):
            for_chunks(length, lambda o, n: pltpu.make_async_copy(
                xs_hbm.at[rows8(src + o, n)], xbuf.at[s, rows8(dst + o, n)], sem_in.at[s]).start())
        for_runs(bb, run)
        v = valid_ref[bb]
        for_chunks(MOE_BLOCK - v, lambda o, n: pltpu.make_async_copy(
            xs_hbm.at[rows8(o, n)], xbuf.at[s, rows8(v + o, n)], sem_in.at[s]).start())

    def scatter(bb, s):
        def run(src, dst, length):
            for_chunks(length, lambda o, n: pltpu.make_async_copy(
                obuf.at[s, rows8(dst + o, n)], ys_hbm.at[rows8(src + o, n)], sem_out.at[s]).start())
        for_runs(bb, run)
        v = valid_ref[bb]
        for_chunks(MOE_BLOCK - v, lambda o, n: pltpu.make_async_copy(
            obuf.at[s, rows8(v + o, n)], trash_hbm.at[rows8(s * MOE_BLOCK + v + o, n)],
            sem_out.at[s]).start())

    def wait_gather(s):
        pltpu.make_async_copy(xs_hbm.at[rows8(0, MOE_BLOCK)], xbuf.at[s], sem_in.at[s]).wait()

    def wait_scatter(s):
        pltpu.make_async_copy(obuf.at[s], ys_hbm.at[rows8(0, MOE_BLOCK)], sem_out.at[s]).wait()

    @pl.when(b == 0)
    def _():
        obuf[...] = jnp.zeros_like(obuf)
        for s in range(2):
            init = pltpu.make_async_copy(obuf.at[s], trash_hbm.at[rows8(s * MOE_BLOCK, MOE_BLOCK)],
                                         sem_out.at[s])
            init.start()
            init.wait()
        gather(0, 0)

    n_used = nused_ref[0]
    live = b < n_used

    @pl.when(live)
    def _():
        wait_gather(slot)

    @pl.when(jnp.logical_and(live, b >= 2))
    def _():
        wait_scatter(slot)

    e = be_ref[b]
    e_prev = be_ref[jnp.maximum(b - 1, 0)]

    @pl.when(jnp.logical_and(live, jnp.logical_or(b == 0, e != e_prev)))
    def _():
        wg_s[...] = wg_ref[0].astype(BF16)
        wu_s[...] = wu_ref[0].astype(BF16)
        wd_s[...] = wd_ref[0].astype(BF16)

    @pl.when(b + 1 < n_used)
    def _():
        gather(b + 1, other)

    @pl.when(jnp.logical_and(live, b >= 1))
    def _():
        scatter(b - 1, other)

    @pl.when(live)
    def _():
        x = _load_row_tiles(xbuf.at[slot], MOE_BLOCK).astype(BF16)
        hg = jnp.dot(x, wg_s[...], preferred_element_type=F32)
        hu = jnp.dot(x, wu_s[...], preferred_element_type=F32)
        hb = (hg * _sigmoid(hg) * hu).astype(BF16)
        _store_row_tiles(obuf.at[slot], jnp.dot(hb, wd_s[...], preferred_element_type=F32))

    @pl.when(b == n_used - 1)
    def _():
        scatter(b, slot)
        wait_scatter(other)
        wait_scatter(slot)


def _moe(plan, xs, w_gate, w_up, w_down, layer, n_blocks, tile_rows):
    n_pre = len(plan)
    w_idx = lambda b, be, *_: (layer, be[b], 0, 0)
    grid_spec = pltpu.PrefetchScalarGridSpec(
        num_scalar_prefetch=n_pre,
        grid=(n_blocks,),
        in_specs=[
            pl.BlockSpec(memory_space=pl.ANY),
            pl.BlockSpec((None, 1, D_MODEL, EXPERT_FF), w_idx),
            pl.BlockSpec((None, 1, D_MODEL, EXPERT_FF), w_idx),
            pl.BlockSpec((None, 1, EXPERT_FF, D_MODEL), w_idx),
        ],
        out_specs=(pl.BlockSpec(memory_space=pl.ANY), pl.BlockSpec(memory_space=pl.ANY)),
        scratch_shapes=[pltpu.VMEM((2, MOE_BLOCK * ROW_TILE, LANES), F32),
                        pltpu.VMEM((2, MOE_BLOCK * ROW_TILE, LANES), F32),
                        pltpu.VMEM((D_MODEL, EXPERT_FF), BF16),
                        pltpu.VMEM((D_MODEL, EXPERT_FF), BF16),
                        pltpu.VMEM((EXPERT_FF, D_MODEL), BF16),
                        pltpu.SemaphoreType.DMA((2,)),
                        pltpu.SemaphoreType.DMA((2,))],
    )
    ys, _ = pl.pallas_call(
        functools.partial(_moe_kernel, tile_rows=tile_rows),
        grid_spec=grid_spec,
        out_shape=(jax.ShapeDtypeStruct(xs.shape, F32),
                   jax.ShapeDtypeStruct((2 * MOE_BLOCK * ROW_TILE, LANES), F32)),
        compiler_params=_cparams(("arbitrary",)),
        name="moe",
    )(*plan, xs, w_gate, w_up, w_down)
    return ys


def _ple_kernel(route_ref, gate_ref, ys_ref, x1_ref, p_ref, gp_ref, wpg_ref, wpe_ref, gfin_ref,
                out_ref, moe_ref, *, final_norm):
    tm = x1_ref.shape[0]
    tile = lambda r: pl.ds(pl.multiple_of(r * ROW_TILE, ROW_TILE), ROW_TILE)

    def combine(t, carry):
        acc = None
        for k in range(TOP_K):
            term = gate_ref[TOP_K * t + k] * ys_ref[tile(route_ref[ROUTE_COLS * t + TOP_K + k]), :]
            acc = term if acc is None else acc + term
        moe_ref[tile(t), :] = acc
        return carry

    lax.fori_loop(0, tm, combine, 0, unroll=8)
    x2 = x1_ref[...] + _load_row_tiles(moe_ref, tm)
    hp = _rms(x2, gp_ref[...]).astype(BF16)
    g_ple = _sigmoid(jnp.dot(hp, wpg_ref[...], preferred_element_type=F32))
    emb = jnp.dot(p_ref[...].astype(BF16), wpe_ref[...], preferred_element_type=F32)
    x3 = x2 + g_ple * emb
    if final_norm:
        x3 = _rms(x3, gfin_ref[...])
    out_ref[...] = x3


def _ple(route_flat, gate_flat, ys, x1, p, gp, wpg, wpe, gfin, tm, final_norm):
    t = x1.shape[0]
    row = lambda i: (i, 0)
    const = lambda i: (0, 0)
    full = lambda a: pl.BlockSpec(a.shape, const)
    smem = lambda n: pl.BlockSpec((n,), lambda i: (i,), memory_space=pltpu.SMEM)
    return pl.pallas_call(
        functools.partial(_ple_kernel, final_norm=final_norm),
        grid=(t // tm,),
        in_specs=[
            smem(ROUTE_COLS * tm),
            smem(TOP_K * tm),
            pl.BlockSpec((TOP_K * tm * ROW_TILE, LANES), row),
            pl.BlockSpec((tm, D_MODEL), row),
            pl.BlockSpec((tm, PLE_DIM), row),
            full(gp), full(wpg), full(wpe), full(gfin),
        ],
        out_specs=pl.BlockSpec((tm, D_MODEL), row),
        out_shape=jax.ShapeDtypeStruct((t, D_MODEL), F32),
        scratch_shapes=[pltpu.VMEM((tm * ROW_TILE, LANES), F32)],
        compiler_params=_cparams(("parallel",)),
        name="ple",
    )(route_flat, gate_flat, ys, x1, p, gp, wpg, wpe, gfin)


def _rope_tables(seq):
    inv = jnp.power(ROPE_THETA, -jnp.arange(0, HEAD_DIM, 2, dtype=F32) / HEAD_DIM)
    ang = jnp.arange(seq, dtype=F32)[:, None] * inv[None, :]
    cos, sin = jnp.cos(ang), jnp.sin(ang)
    cos_h = jnp.concatenate([cos, cos], axis=-1)
    sin_h = jnp.concatenate([-sin, sin], axis=-1)
    reps = LANES // HEAD_DIM
    return jnp.tile(cos_h, (1, reps)), jnp.tile(sin_h, (1, reps))


def _dispatch_plan(counts, n_blocks):
    i32 = jnp.int32
    total = jnp.sum(counts, axis=0)
    blocks_of = (total + MOE_BLOCK - 1) // MOE_BLOCK
    block_end = jnp.cumsum(blocks_of)
    bidx = jnp.arange(n_blocks, dtype=i32)
    block_e = jnp.minimum(jnp.sum((block_end[None, :] <= bidx[:, None]).astype(i32), axis=1),
                          N_EXPERTS - 1)
    is_e = (block_e[:, None] == jnp.arange(N_EXPERTS, dtype=i32)[None, :]).astype(i32)
    pick = lambda per_expert: jnp.sum(is_e * per_expert[None, :], axis=1)
    pick_tiles = lambda tab: jnp.sum(is_e[:, None, :] * tab[None, :, :], axis=2)
    block_j = bidx - pick(block_end - blocks_of)
    used = bidx < block_end[-1]
    cum = jnp.cumsum(counts, axis=0) - counts
    off = jnp.cumsum(counts, axis=1) - counts
    lo = block_j * MOE_BLOCK
    hi = jnp.minimum(lo + MOE_BLOCK, pick(total))
    run_start = pick_tiles(cum)
    run_end = run_start + pick_tiles(counts)
    tile_lo = jnp.sum((run_end <= lo[:, None]).astype(i32), axis=1)
    tile_hi = jnp.sum((run_start < hi[:, None]).astype(i32), axis=1) - 1
    valid = jnp.clip(hi - lo, 0, MOE_BLOCK)
    tile_lo = jnp.where(used, tile_lo, 0)
    tile_hi = jnp.where(used, tile_hi, -1)
    valid = jnp.where(used, valid, 0)
    flat = lambda a: a.reshape(-1).astype(i32)
    return (flat(block_e), flat(block_j), flat(tile_lo), flat(tile_hi), flat(valid),
            flat(cum), flat(counts), flat(off), flat(block_end[-1:]))


def kernel(x, p, norm_mix, w_in, w_dw, b_dw, ln_conv_g, ln_conv_b, w_attn_out, w_conv_out,
           b_conv_out, w_out, norm_ffn, w_route_group, b_route_group, w_route_expert,
           b_route_expert, w_exp_gate, w_exp_up, w_exp_down, norm_ple, w_ple_proj, w_ple_gate,
           norm_final):
    bsz, seq, d = x.shape
    depth = w_in.shape[0]
    t = bsz * seq
    tm = 512
    n_assign = t * TOP_K
    assert n_assign > MOE_BLOCK
    n_rows = (-(-n_assign // MOE_BLOCK)) * MOE_BLOCK + N_EXPERTS * MOE_BLOCK
    cos, sin = _rope_tables(seq)
    row2 = lambda a: a.reshape(1, -1)

    xf = x.reshape(t, d)
    for i in range(depth):
        n_qkv = 3 * ATTN_WIDTH
        q, k, v = _inproj(xf, row2(norm_mix[i]), w_in[i, :, :n_qkv].astype(BF16), cos, sin,
                          bsz, seq, tm)
        u, ga, gb = _gateproj(xf, row2(norm_mix[i]), w_in[i, :, n_qkv:].astype(BF16), tm)
        o = _attention(q, k, v).reshape(t, ATTN_WIDTH)
        c = _conv(u.reshape(bsz, seq, CONV_WIDTH), w_dw[i], row2(b_dw[i]), row2(ln_conv_g[i]),
                  row2(ln_conv_b[i]), 256).reshape(t, CONV_WIDTH)

        w_r = jnp.zeros((d, LANES), F32)
        w_r = w_r.at[:, :N_GROUPS].set(w_route_group[i])
        w_r = w_r.at[:, EXPERT_LANE0:EXPERT_LANE0 + N_EXPERTS].set(w_route_expert[i])
        w_r_hi = w_r.astype(BF16)
        w_r_lo = (w_r - w_r_hi.astype(F32)).astype(BF16)
        b_r = jnp.zeros((1, LANES), F32)
        b_r = b_r.at[0, :N_GROUPS].set(b_route_group[i])
        b_r = b_r.at[0, EXPERT_LANE0:EXPERT_LANE0 + N_EXPERTS].set(b_route_expert[i])

        x1, h2, route, gate, cnt = _merge(
            o, c, ga, gb, xf, w_attn_out[i].astype(BF16), w_conv_out[i].astype(BF16),
            row2(b_conv_out[i]), w_out[i].astype(BF16), row2(norm_ffn[i]), w_r_hi, w_r_lo, b_r, tm)

        counts = cnt[:, 0, :N_EXPERTS].astype(jnp.int32)
        plan = _dispatch_plan(counts, n_rows // MOE_BLOCK)
        route_flat = route.reshape(-1)
        xs = _tile_sort(route_flat, h2, tm)
        ys = _moe(plan, xs, w_exp_gate, w_exp_up, w_exp_down, i, n_rows // MOE_BLOCK, TOP_K * tm)
        xf = _ple(route_flat, gate.reshape(-1), ys, x1, p[i].reshape(t, PLE_DIM), row2(norm_ple[i]),
                  w_ple_gate[i].astype(BF16), w_ple_proj[i].astype(BF16), row2(norm_final),
                  tm, final_norm=(i == depth - 1))
    return xf.reshape(bsz, seq, d)
```

```python
import functools

import jax
import jax.numpy as jnp
from jax import lax
from jax.experimental import pallas as pl
from jax.experimental.pallas import tpu as pltpu

D_MODEL = 1024
N_HEADS = 8
HEAD_DIM = 64
ATTN_WIDTH = N_HEADS * HEAD_DIM
CONV_WIDTH = 512
CONV_K = 31
DILATIONS = (16, 4, 1)
BLK = 128
ROPE_THETA = 10000.0
N_GROUPS = 4
EXPERTS_PER_GROUP = 8
N_EXPERTS = N_GROUPS * EXPERTS_PER_GROUP
EXPERT_FF = 512
TOP_K = 2
MOE_BLOCK = 256
PLE_DIM = 256
EPS = 1e-6
NEG_INF = -1e30

LANES = 128
SUBLANES = 8
ROW_TILE = D_MODEL // LANES
VMEM_LIMIT = 56 * 1024 * 1024

F32 = jnp.float32
BF16 = jnp.bfloat16


def _cparams(sem):
    return pltpu.CompilerParams(dimension_semantics=sem, vmem_limit_bytes=VMEM_LIMIT)


def _rms(x, g):
    return x * lax.rsqrt(jnp.mean(x * x, axis=-1, keepdims=True) + EPS) * g


def _sigmoid(x):
    return 0.5 * jnp.tanh(0.5 * x) + 0.5


def _store_row_tiles(ref, x):
    rows = x.shape[0]
    for s in range(ROW_TILE):
        ref[pl.ds(s, rows, stride=ROW_TILE), :] = x[:, s * LANES:(s + 1) * LANES]


def _load_row_tiles(ref, rows):
    return jnp.concatenate([ref[pl.ds(s, rows, stride=ROW_TILE), :] for s in range(ROW_TILE)], axis=1)


QKV_ROWS = 256


def _inproj_kernel(x_ref, g_ref, w_ref, cos_ref, sin_ref, *out_refs):
    n_lay = len(DILATIONS)
    q_refs, k_refs, v_refs = (out_refs[a * n_lay:(a + 1) * n_lay] for a in range(3))
    slab_refs = out_refs[3 * n_lay:]
    tm = x_ref.shape[0]
    n_slab = ATTN_WIDTH // LANES

    def put(ref, d, r, row0, g, val):
        cols = slice(g * LANES, (g + 1) * LANES)
        if d == 1:
            ref[pl.ds(row0, val.shape[0]), cols] = val.astype(BF16)
        else:
            ref[0, r, pl.ds(row0, val.shape[0]), cols] = val.astype(BF16)

    def fill(raw, rows, refs, slabs, is_query, post):
        for g in range(n_slab):
            cols = slice(g * LANES, (g + 1) * LANES)
            t = post(raw[:, cols], rows)
            slabs[g, rows, :] = t
            for d, ref in zip(DILATIONS, refs):
                if d == 1 and not is_query:
                    ref[rows, cols] = t.astype(BF16)

    def relayout(refs, slabs, by4, is_query):
        ref16, ref4, ref1 = refs
        quarter = tm // 4
        class4 = lambda r: pl.multiple_of(r * quarter, quarter)

        def to_by4(r, carry):
            for g in range(n_slab):
                val = slabs[g, pl.ds(r, quarter, stride=4), :]
                by4[g, pl.ds(class4(r), quarter), :] = val
                if not is_query:
                    put(ref4, 4, r, 0, g, val)
            return carry
        lax.fori_loop(0, 4, to_by4, 0, unroll=2)

        def to_16(r, carry):
            for g in range(n_slab):
                put(ref16, 16, r, 0, g, by4[g, pl.ds(class4(r % 4) + r // 4, tm // 16, stride=4), :])
            return carry
        lax.fori_loop(0, 16, to_16, 0, unroll=4)

        if is_query:
            def q4_class(r, carry):
                for bl in range(quarter // BLK):
                    for a in range(4):
                        src = pl.ds(class4(r) + bl * BLK + a, BLK // 4, stride=4)
                        for g in range(n_slab):
                            put(ref4, 4, r, bl * BLK + a * (BLK // 4), g, by4[g, src, :])
                return carry
            lax.fori_loop(0, 4, q4_class, 0, unroll=2)

            def q1_block(bl, carry):
                for a in range(16):
                    src = pl.ds((a % 4) * quarter + (BLK // 4) * bl + a // 4, BLK // 16, stride=4)
                    for g in range(n_slab):
                        put(ref1, 1, 0, pl.multiple_of(bl * BLK, BLK) + a * (BLK // 16), g, by4[g, src, :])
                return carry
            lax.fori_loop(0, tm // BLK, q1_block, 0, unroll=2)

    lane = lax.broadcasted_iota(jnp.int32, (QKV_ROWS, LANES), 1)
    first_half = (lane % HEAD_DIM) < (HEAD_DIM // 2)

    def rope(t, rows):
        partner = jnp.where(first_half,
                            pltpu.roll(t, LANES - HEAD_DIM // 2, 1),
                            pltpu.roll(t, HEAD_DIM // 2, 1))
        return t * cos_ref[rows, :] + partner * sin_ref[rows, :]

    for r0 in range(0, tm, QKV_ROWS):
        rows = slice(r0, r0 + QKV_ROWS)
        h = _rms(x_ref[rows, :], g_ref[...]).astype(BF16)
        proj = lambda a: jnp.dot(h, w_ref[:, a * ATTN_WIDTH:(a + 1) * ATTN_WIDTH],
                                 preferred_element_type=F32)
        fill(proj(0), rows, q_refs, slab_refs[0], True, lambda t, rw: rope(t, rw) * (HEAD_DIM ** -0.5))
        fill(proj(1), rows, k_refs, slab_refs[1], False, rope)
        fill(proj(2), rows, v_refs, slab_refs[2], False, lambda t, rw: t)
    assert DILATIONS == (16, 4, 1) and tm % (4 * BLK) == 0
    relayout(q_refs, slab_refs[0], slab_refs[3], True)
    relayout(k_refs, slab_refs[1], slab_refs[4], False)
    relayout(v_refs, slab_refs[2], slab_refs[5], False)


def _gateproj_kernel(x_ref, g_ref, w_ref, u_ref, ga_ref, gb_ref):
    h = _rms(x_ref[...], g_ref[...]).astype(BF16)

    def proj(c0, width):
        return jnp.dot(h, w_ref[:, c0:c0 + width], preferred_element_type=F32)

    u_ref[...] = proj(0, CONV_WIDTH) * _sigmoid(proj(CONV_WIDTH, CONV_WIDTH))
    c0 = 2 * CONV_WIDTH
    ga_ref[...] = proj(c0, D_MODEL).astype(ga_ref.dtype)
    gb_ref[...] = proj(c0 + D_MODEL, D_MODEL).astype(gb_ref.dtype)


def _gateproj(x, gain, w_rest, tm):
    t = x.shape[0]
    row = lambda i: (i, 0)
    const = lambda i: (0, 0)
    return pl.pallas_call(
        _gateproj_kernel,
        grid=(t // tm,),
        in_specs=[pl.BlockSpec((tm, D_MODEL), row), pl.BlockSpec((1, D_MODEL), const),
                  pl.BlockSpec(w_rest.shape, const)],
        out_specs=(pl.BlockSpec((tm, CONV_WIDTH), row), pl.BlockSpec((tm, D_MODEL), row),
                   pl.BlockSpec((tm, D_MODEL), row)),
        out_shape=(jax.ShapeDtypeStruct((t, CONV_WIDTH), F32),
                   jax.ShapeDtypeStruct((t, D_MODEL), BF16),
                   jax.ShapeDtypeStruct((t, D_MODEL), BF16)),
        compiler_params=_cparams(("parallel",)),
        name="gateproj",
    )(x, gain, w_rest)


def _inproj(x, gain, w_in, cos, sin, bsz, seq, tm):
    t = x.shape[0]
    n_pos = seq // tm
    row = lambda i: (i, 0)
    const = lambda i: (0, 0)
    qkv_shapes, qkv_specs = [], []
    for d in DILATIONS:
        if d == 1:
            qkv_shapes.append(jax.ShapeDtypeStruct((t, ATTN_WIDTH), BF16))
            qkv_specs.append(pl.BlockSpec((tm, ATTN_WIDTH), row))
        else:
            qkv_shapes.append(jax.ShapeDtypeStruct((bsz, d, seq // d, ATTN_WIDTH), BF16))
            qkv_specs.append(pl.BlockSpec((1, d, tm // d, ATTN_WIDTH),
                                          lambda i: (i // n_pos, 0, i % n_pos, 0)))
    out_shape = tuple(qkv_shapes * 3)
    out_specs = tuple(qkv_specs * 3)
    outs = pl.pallas_call(
        _inproj_kernel,
        grid=(t // tm,),
        in_specs=[
            pl.BlockSpec((tm, D_MODEL), row),
            pl.BlockSpec((1, D_MODEL), const),
            pl.BlockSpec(w_in.shape, const),
            pl.BlockSpec((tm, LANES), lambda i: (i % n_pos, 0)),
            pl.BlockSpec((tm, LANES), lambda i: (i % n_pos, 0)),
        ],
        out_specs=out_specs,
        out_shape=out_shape,
        scratch_shapes=[pltpu.VMEM((ATTN_WIDTH // LANES, tm, LANES), F32) for _ in range(6)],
        compiler_params=_cparams(("parallel",)),
        name="inproj",
    )(x, gain, w_in, cos, sin)
    n_lay = len(DILATIONS)
    as_rows = lambda a: a.reshape(bsz, seq, ATTN_WIDTH)
    return tuple([as_rows(a) for a in outs[j * n_lay:(j + 1) * n_lay]] for j in range(3))


STATE_DILATION = max(DILATIONS)


def _query_order(idx, groups):
    run = BLK // groups
    return groups * (idx % run) + idx // run


def _attn_kernel(*refs, seq):
    n_lay = len(DILATIONS)
    srcs = refs[:3 * n_lay]
    o_ref, buf, acc_s, m_s, l_s, sem = refs[3 * n_lay:]
    b = pl.program_id(0)
    n_b = pl.num_programs(0)
    n_blocks = seq // BLK
    pair_w = 2 * HEAD_DIM
    n_pair = ATTN_WIDTH // pair_w

    def load_sweep(lay, bi, slot):
        return [pltpu.make_async_copy(srcs[a * n_lay + lay].at[bi], buf.at[slot, a, pl.ds(BLK, seq)],
                                      sem.at[slot]) for a in range(3)]

    @pl.when(b == 0)
    def _():
        buf[:, :, 0:BLK, :] = jnp.zeros((2, 3, BLK, ATTN_WIDTH), BF16)
        for c in load_sweep(0, 0, 0):
            c.start()

    row = lax.broadcasted_iota(jnp.int32, (BLK, BLK), 0)
    col = lax.broadcasted_iota(jnp.int32, (BLK, BLK), 1)
    lane = lax.broadcasted_iota(jnp.int32, (BLK, pair_w), 1)
    low_head = lane < HEAD_DIM
    unpermute = (row == _query_order(col, STATE_DILATION // DILATIONS[-1])).astype(BF16)

    for lay, d in enumerate(DILATIONS):
        slot = (b * n_lay + lay) % 2
        for c in load_sweep(lay, b, slot):
            c.wait()
        if lay + 1 < n_lay:
            for c in load_sweep(lay + 1, b, 1 - slot):
                c.start()
        else:
            @pl.when(b + 1 < n_b)
            def _():
                for c in load_sweep(0, b + 1, 1 - slot):
                    c.start()

        nb = seq // (d * BLK)
        has_prev = nb > 1
        first, final = lay == 0, lay == n_lay - 1
        kw = 2 * BLK if has_prev else BLK

        groups = STATE_DILATION // d
        run = BLK // groups
        q_pos = _query_order(row, groups)

        def block(j, carry, d=d, nb=nb, has_prev=has_prev, first=first, final=final, kw=kw,
                  slot=slot, groups=groups, run=run, q_pos=q_pos):
            r, n = j // nb, j % nb
            q_row0 = pl.multiple_of(BLK + j * BLK, BLK)
            k_row0 = pl.multiple_of(j * BLK, BLK) if has_prev else q_row0
            runs = [pl.ds(pl.multiple_of((d * a + r) * BLK + run * n, run), run)
                    for a in range(groups)]
            load = lambda ref, g: jnp.concatenate([ref[g, s, :] for s in runs], axis=0)

            def store(ref, g, val):
                for a, s in enumerate(runs):
                    ref[g, s, :] = val[a * run:(a + 1) * run]

            cur_ok = col <= q_pos
            if has_prev:
                prev_ok = jnp.logical_and(col >= q_pos, n > 0)
                allowed = jnp.concatenate([prev_ok, cur_ok], axis=1)
            else:
                allowed = cur_ok
            allowed2 = jnp.concatenate([allowed, allowed], axis=0)
            lanes = [slice(g * pair_w, (g + 1) * pair_w) for g in range(n_pair)]
            scores = []
            for ls in lanes:
                q2 = buf[slot, 0, pl.ds(q_row0, BLK), ls]
                q_both = jnp.concatenate([jnp.where(low_head, q2, 0),
                                          jnp.where(low_head, 0, q2)], axis=0)
                k2 = buf[slot, 1, pl.ds(k_row0, kw), ls]
                scores.append(lax.dot_general(q_both, k2, (((1,), (1,)), ((), ())),
                                              preferred_element_type=F32))
            probs, maxes, sums = [], [], []
            for s in scores:
                s = jnp.where(allowed2, s, NEG_INF)
                m_h = jnp.max(s, axis=-1, keepdims=True)
                p = jnp.exp(s - m_h)
                sums.append(jnp.sum(p, axis=-1, keepdims=True))
                maxes.append(m_h)
                probs.append(p.astype(BF16))
            pv = [jnp.dot(p, buf[slot, 2, pl.ds(k_row0, kw), ls], preferred_element_type=F32)
                  for p, ls in zip(probs, lanes)]
            for g, ls in enumerate(lanes):
                acc = jnp.where(low_head, pv[g][:BLK], pv[g][BLK:])
                m_new = jnp.where(low_head, maxes[g][:BLK], maxes[g][BLK:])
                l_new = jnp.where(low_head, sums[g][:BLK], sums[g][BLK:])
                if not first:
                    m_old = load(m_s, g)
                    m_tot = jnp.maximum(m_old, m_new)
                    a_old = jnp.exp(m_old - m_tot)
                    a_new = jnp.exp(m_new - m_tot)
                    acc = load(acc_s, g) * a_old + acc * a_new
                    l_new = load(l_s, g) * a_old + l_new * a_new
                    m_new = m_tot
                if final:
                    o_blk = jnp.dot(unpermute, (acc / l_new).astype(BF16), preferred_element_type=F32)
                    o_ref[0, pl.ds(pl.multiple_of(j * BLK, BLK), BLK), ls] = o_blk.astype(o_ref.dtype)
                else:
                    store(acc_s, g, acc)
                    store(m_s, g, m_new)
                    store(l_s, g, l_new)
            return carry

        lax.fori_loop(0, n_blocks, block, 0, unroll=2)


def _attention(q, k, v):
    bsz, seq, _ = q[0].shape
    assert all(seq % (d * BLK) == 0 for d in DILATIONS) and DILATIONS[-1] == 1
    n_pair = ATTN_WIDTH // (2 * HEAD_DIM)
    return pl.pallas_call(
        functools.partial(_attn_kernel, seq=seq),
        grid=(bsz,),
        in_specs=[pl.BlockSpec(memory_space=pl.ANY)] * (3 * len(DILATIONS)),
        out_specs=pl.BlockSpec((1, seq, ATTN_WIDTH), lambda b: (b, 0, 0)),
        out_shape=jax.ShapeDtypeStruct((bsz, seq, ATTN_WIDTH), BF16),
        scratch_shapes=[pltpu.VMEM((2, 3, BLK + seq, ATTN_WIDTH), BF16),
                        pltpu.VMEM((n_pair, seq, 2 * HEAD_DIM), F32),
                        pltpu.VMEM((n_pair, seq, 2 * HEAD_DIM), F32),
                        pltpu.VMEM((n_pair, seq, 2 * HEAD_DIM), F32),
                        pltpu.SemaphoreType.DMA((2,))],
        compiler_params=_cparams(("arbitrary",)),
        name="attention",
    )(*q, *k, *v)


CONV_HALO = 32
CONV_ROWS = 32


def _conv_kernel(halo_ref, cur_ref, w_ref, b_ref, g_ref, beta_ref, out_ref, ext_ref):
    ts = cur_ref.shape[1]
    i = pl.program_id(1)
    ext_ref[0, 0:CONV_HALO, :] = jnp.where(i > 0, halo_ref[0], 0.0)
    ext_ref[0, CONV_HALO:CONV_HALO + ts, :] = cur_ref[0]
    first = CONV_HALO - (CONV_K - 1)
    n_shift = ext_ref.shape[0]
    rows = CONV_HALO + ts
    ext0 = ext_ref[0]
    for s in range(1, n_shift):
        ext_ref[s] = pltpu.roll(ext0, rows - s, 0)
    bias = b_ref[...]
    gamma = g_ref[...]
    beta = beta_ref[...]
    for r0 in range(0, ts, CONV_ROWS):
        acc = jnp.zeros((CONV_ROWS, CONV_WIDTH), F32) + bias
        for j in range(CONV_K):
            lo = first + j + r0
            s, lo = lo % n_shift, lo - lo % n_shift
            tap = jnp.concatenate([w_ref[j]] * (CONV_ROWS // SUBLANES), axis=0)
            acc = acc + tap * ext_ref[s, lo:lo + CONV_ROWS, :]
        mu = jnp.mean(acc, axis=-1, keepdims=True)
        cen = acc - mu
        var = jnp.mean(cen * cen, axis=-1, keepdims=True)
        y = cen * lax.rsqrt(var + EPS) * gamma + beta
        out_ref[0, r0:r0 + CONV_ROWS, :] = (y * _sigmoid(y)).astype(out_ref.dtype)


def _conv(u, w_dw, b_dw, ln_g, ln_b, ts):
    b, s, c = u.shape
    per = ts // CONV_HALO
    const = lambda bi, i: (0, 0)
    return pl.pallas_call(
        _conv_kernel,
        grid=(b, s // ts),
        in_specs=[
            pl.BlockSpec((1, CONV_HALO, c), lambda bi, i: (bi, jnp.maximum(i * per - 1, 0), 0)),
            pl.BlockSpec((1, ts, c), lambda bi, i: (bi, i, 0)),
            pl.BlockSpec((CONV_K, SUBLANES, c), lambda bi, i: (0, 0, 0)),
            pl.BlockSpec((1, c), const),
            pl.BlockSpec((1, c), const),
            pl.BlockSpec((1, c), const),
        ],
        out_specs=pl.BlockSpec((1, ts, c), lambda bi, i: (bi, i, 0)),
        out_shape=jax.ShapeDtypeStruct((b, s, c), BF16),
        scratch_shapes=[pltpu.VMEM((SUBLANES, CONV_HALO + ts, c), F32)],
        compiler_params=_cparams(("parallel", "arbitrary")),
        name="conv",
    )(u, u, jnp.broadcast_to(w_dw[:, None, :], (CONV_K, SUBLANES, c)), b_dw, ln_g, ln_b)


ROUTE_COLS = 4
EXPERT_LANE0 = N_GROUPS


def _merge_kernel(o_ref, c_ref, ga_ref, gb_ref, x_ref, wa_ref, wc_ref, bc_ref, wo_ref,
                  gf_ref, wrh_ref, wrl_ref, br_ref,
                  x1_ref, h2_ref, route_ref, gate_ref, cnt_ref, tri_ref):
    tm = x_ref.shape[0]

    @pl.when(pl.program_id(0) == 0)
    def _():
        r = lax.broadcasted_iota(jnp.int32, (tm, tm), 0)
        c = lax.broadcasted_iota(jnp.int32, (tm, tm), 1)
        tri_ref[...] = (c < r).astype(BF16)

    y_a = jnp.dot(o_ref[...], wa_ref[...], preferred_element_type=F32)
    y_b = jnp.dot(c_ref[...], wc_ref[...], preferred_element_type=F32) + bc_ref[...]
    merged = _sigmoid(ga_ref[...].astype(F32)) * y_a + _sigmoid(gb_ref[...].astype(F32)) * y_b
    x1 = x_ref[...] + jnp.dot(merged.astype(BF16), wo_ref[...], preferred_element_type=F32)
    x1_ref[...] = x1
    h2 = _rms(x1, gf_ref[...])
    _store_row_tiles(h2_ref, h2)

    h_hi = h2.astype(BF16)
    h_lo = (h2 - h_hi.astype(F32)).astype(BF16)
    logits = (jnp.dot(h_hi, wrh_ref[...], preferred_element_type=F32)
              + jnp.dot(h_hi, wrl_ref[...], preferred_element_type=F32)
              + jnp.dot(h_lo, wrh_ref[...], preferred_element_type=F32)) + br_ref[...]

    lane = lax.broadcasted_iota(jnp.int32, (tm, LANES), 1).astype(F32)
    no_lane = float(LANES)
    is_group = lane < N_GROUPS
    g_max = jnp.max(jnp.where(is_group, logits, -jnp.inf), axis=-1, keepdims=True)
    g_sel = jnp.min(jnp.where(jnp.logical_and(is_group, logits == g_max), lane, no_lane),
                    axis=-1, keepdims=True)
    p_group = 1.0 / jnp.sum(jnp.where(is_group, jnp.exp(logits - g_max), 0.0),
                            axis=-1, keepdims=True)
    lo = EXPERT_LANE0 + EXPERTS_PER_GROUP * g_sel
    in_sel = jnp.logical_and(lane >= lo, lane < lo + EXPERTS_PER_GROUP)
    cand = jnp.where(in_sel, logits, -jnp.inf)
    v1 = jnp.max(cand, axis=-1, keepdims=True)
    i1 = jnp.min(jnp.where(jnp.logical_and(in_sel, cand == v1), lane, no_lane), axis=-1, keepdims=True)
    rest = jnp.logical_and(in_sel, lane != i1)
    cand2 = jnp.where(rest, logits, -jnp.inf)
    v2 = jnp.max(cand2, axis=-1, keepdims=True)
    i2 = jnp.min(jnp.where(jnp.logical_and(rest, cand2 == v2), lane, no_lane), axis=-1, keepdims=True)
    e2 = jnp.exp(v2 - v1)
    gate0 = p_group / (1.0 + e2)
    gate1 = p_group * e2 / (1.0 + e2)

    oh0 = (lane == i1)
    oh1 = (lane == i2)
    before0 = jnp.dot(tri_ref[...], oh0.astype(BF16), preferred_element_type=F32)
    before1 = jnp.dot(tri_ref[...], oh1.astype(BF16), preferred_element_type=F32)
    tot0 = jnp.sum(oh0.astype(F32), axis=0, keepdims=True)
    tot1 = jnp.sum(oh1.astype(F32), axis=0, keepdims=True)
    cnt = tot0 + tot1
    rank0 = jnp.sum(jnp.where(oh0, before0, 0.0), axis=-1, keepdims=True)
    rank1 = jnp.sum(jnp.where(oh1, tot0 + before1, 0.0), axis=-1, keepdims=True)

    rc = lax.broadcasted_iota(jnp.int32, (tm, ROUTE_COLS), 1)
    route = jnp.where(rc == 0, i1 - EXPERT_LANE0,
                      jnp.where(rc == 1, i2 - EXPERT_LANE0, jnp.where(rc == 2, rank0, rank1)))
    route_ref[...] = route.astype(jnp.int32)
    gc = lax.broadcasted_iota(jnp.int32, (tm, TOP_K), 1)
    gate_ref[...] = jnp.where(gc == 0, gate0, gate1)
    cnt_ref[0] = jnp.broadcast_to(pltpu.roll(cnt, LANES - EXPERT_LANE0, 1), (8, LANES))


def _merge(o, c, ga, gb, x, wa, wc, bc, wo, gf, wr_hi, wr_lo, br, tm):
    t = x.shape[0]
    nt = t // tm
    row = lambda i: (i, 0)
    const = lambda i: (0, 0)
    full = lambda a: pl.BlockSpec(a.shape, const)
    return pl.pallas_call(
        _merge_kernel,
        grid=(nt,),
        in_specs=[
            pl.BlockSpec((tm, ATTN_WIDTH), row),
            pl.BlockSpec((tm, CONV_WIDTH), row),
            pl.BlockSpec((tm, D_MODEL), row),
            pl.BlockSpec((tm, D_MODEL), row),
            pl.BlockSpec((tm, D_MODEL), row),
            full(wa), full(wc), full(bc), full(wo), full(gf), full(wr_hi), full(wr_lo), full(br),
        ],
        out_specs=(
            pl.BlockSpec((tm, D_MODEL), row),
            pl.BlockSpec((tm * ROW_TILE, LANES), row),
            pl.BlockSpec((tm, ROUTE_COLS), row),
            pl.BlockSpec((tm, TOP_K), row),
            pl.BlockSpec((1, 8, LANES), lambda i: (i, 0, 0)),
        ),
        out_shape=(
            jax.ShapeDtypeStruct((t, D_MODEL), F32),
            jax.ShapeDtypeStruct((t * ROW_TILE, LANES), F32),
            jax.ShapeDtypeStruct((t, ROUTE_COLS), jnp.int32),
            jax.ShapeDtypeStruct((t, TOP_K), F32),
            jax.ShapeDtypeStruct((nt, 8, LANES), F32),
        ),
        scratch_shapes=[pltpu.VMEM((tm, tm), BF16)],
        compiler_params=_cparams(("arbitrary",)),
        name="merge",
    )(o, c, ga, gb, x, wa, wc, bc, wo, gf, wr_hi, wr_lo, br)


def _sort_kernel(base_ref, route_ref, h_ref, xs_ref, pos_ref, *, route_tm):
    tm = h_ref.shape[0] // ROW_TILE
    i = pl.program_id(0)
    for sub in range(tm // route_tm):
        base0 = (i * (tm // route_tm) + sub) * N_EXPERTS

        def place(t, carry, base0=base0):
            row = h_ref[pl.ds(pl.multiple_of(t * ROW_TILE, ROW_TILE), ROW_TILE), :]
            for k in range(TOP_K):
                pos = base_ref[base0 + route_ref[ROUTE_COLS * t + k]] + route_ref[ROUTE_COLS * t + TOP_K + k]
                pos_ref[TOP_K * t + k] = pos
                xs_ref[pl.ds(pl.multiple_of(pos * ROW_TILE, ROW_TILE), ROW_TILE), :] = row
            return carry

        lax.fori_loop(sub * route_tm, (sub + 1) * route_tm, place, 0, unroll=8)


def _tile_sort(base, route_flat, h2, tm, route_tm):
    n_tok = h2.shape[0] // ROW_TILE
    grid_spec = pltpu.PrefetchScalarGridSpec(
        num_scalar_prefetch=1,
        grid=(n_tok // tm,),
        in_specs=[pl.BlockSpec((ROUTE_COLS * tm,), lambda i, base: (i,), memory_space=pltpu.SMEM),
                  pl.BlockSpec((tm * ROW_TILE, LANES), lambda i, base: (i, 0))],
        out_specs=(pl.BlockSpec((TOP_K * tm * ROW_TILE, LANES), lambda i, base: (i, 0)),
                   pl.BlockSpec((TOP_K * tm,), lambda i, base: (i,), memory_space=pltpu.SMEM)),
    )
    return pl.pallas_call(
        functools.partial(_sort_kernel, route_tm=route_tm),
        grid_spec=grid_spec,
        out_shape=(jax.ShapeDtypeStruct((TOP_K * h2.shape[0], LANES), F32),
                   jax.ShapeDtypeStruct((TOP_K * n_tok,), jnp.int32)),
        compiler_params=_cparams(("parallel",)),
        name="tile_sort",
    )(base, route_flat, h2)


RARE_RUN = 64


def _moe_kernel(be_ref, bj_ref, ilo_ref, ihi_ref, valid_ref, cum_ref, cnt_ref, off_ref, nused_ref,
                xs_hbm, wg_ref, wu_ref, wd_ref, ys_hbm, trash_hbm,
                xbuf, obuf, wg_s, wu_s, wd_s, sem_in, sem_out, *, tile_rows):
    b = pl.program_id(0)
    slot = b % 2
    other = 1 - slot
    rows8 = lambda r, n: pl.ds(pl.multiple_of(r * ROW_TILE, ROW_TILE), n * ROW_TILE)

    def for_runs(bb, fn):
        e = be_ref[bb]
        lo_blk = bj_ref[bb] * MOE_BLOCK

        def body(i, carry):
            c = cum_ref[i * N_EXPERTS + e]
            lo = jnp.maximum(c, lo_blk)
            hi = jnp.minimum(c + cnt_ref[i * N_EXPERTS + e], lo_blk + MOE_BLOCK)
            fn(i * tile_rows + off_ref[i * N_EXPERTS + e] + (lo - c), lo - lo_blk,
               jnp.maximum(hi - lo, 0))
            return carry

        lax.fori_loop(ilo_ref[bb], ihi_ref[bb] + 1, body, 0)

    def for_chunks(length, fn):
        def piece(size):
            @pl.when((length & size) != 0)
            def _():
                fn(length & ~(2 * size - 1), size)

        sizes = [1 << bit for bit in reversed(range(MOE_BLOCK.bit_length()))]

        @pl.when(length >= RARE_RUN)
        def _():
            for size in sizes:
                if size >= RARE_RUN:
                    piece(size)

        for size in sizes:
            if size < RARE_RUN:
                piece(size)

    def gather(bb, s):
        def run(src, dst, length):
            for_chunks(length, lambda o, n: pltpu.make_async_copy(
                xs_hbm.at[rows8(src + o, n)], xbuf.at[s, rows8(dst + o, n)], sem_in.at[s]).start())
        for_runs(bb, run)
        v = valid_ref[bb]
        for_chunks(MOE_BLOCK - v, lambda o, n: pltpu.make_async_copy(
            xs_hbm.at[rows8(o, n)], xbuf.at[s, rows8(v + o, n)], sem_in.at[s]).start())

    def scatter(bb, s):
        def run(src, dst, length):
            for_chunks(length, lambda o, n: pltpu.make_async_copy(
                obuf.at[s, rows8(dst + o, n)], ys_hbm.at[rows8(src + o, n)], sem_out.at[s]).start())
        for_runs(bb, run)
        v = valid_ref[bb]
        for_chunks(MOE_BLOCK - v, lambda o, n: pltpu.make_async_copy(
            obuf.at[s, rows8(v + o, n)], trash_hbm.at[rows8(s * MOE_BLOCK + v + o, n)],
            sem_out.at[s]).start())

    def wait_gather(s):
        pltpu.make_async_copy(xs_hbm.at[rows8(0, MOE_BLOCK)], xbuf.at[s], sem_in.at[s]).wait()

    def wait_scatter(s):
        pltpu.make_async_copy(obuf.at[s], ys_hbm.at[rows8(0, MOE_BLOCK)], sem_out.at[s]).wait()

    @pl.when(b == 0)
    def _():
        obuf[...] = jnp.zeros_like(obuf)
        for s in range(2):
            init = pltpu.make_async_copy(obuf.at[s], trash_hbm.at[rows8(s * MOE_BLOCK, MOE_BLOCK)],
                                         sem_out.at[s])
            init.start()
            init.wait()
        gather(0, 0)

    n_used = nused_ref[0]
    live = b < n_used

    @pl.when(live)
    def _():
        wait_gather(slot)

    @pl.when(jnp.logical_and(live, b >= 2))
    def _():
        wait_scatter(slot)

    e = be_ref[b]
    e_prev = be_ref[jnp.maximum(b - 1, 0)]

    @pl.when(jnp.logical_and(live, jnp.logical_or(b == 0, e != e_prev)))
    def _():
        wg_s[...] = wg_ref[0].astype(BF16)
        wu_s[...] = wu_ref[0].astype(BF16)
        wd_s[...] = wd_ref[0].astype(BF16)

    @pl.when(b + 1 < n_used)
    def _():
        gather(b + 1, other)

    @pl.when(jnp.logical_and(live, b >= 1))
    def _():
        scatter(b - 1, other)

    @pl.when(live)
    def _():
        x = _load_row_tiles(xbuf.at[slot], MOE_BLOCK).astype(BF16)
        hg = jnp.dot(x, wg_s[...], preferred_element_type=F32)
        hu = jnp.dot(x, wu_s[...], preferred_element_type=F32)
        hb = (hg * _sigmoid(hg) * hu).astype(BF16)
        _store_row_tiles(obuf.at[slot], jnp.dot(hb, wd_s[...], preferred_element_type=F32))

    @pl.when(b == n_used - 1)
    def _():
        scatter(b, slot)
        wait_scatter(other)
        wait_scatter(slot)


def _moe(plan, xs, w_gate, w_up, w_down, layer, n_blocks, tile_rows):
    n_pre = len(plan)
    w_idx = lambda b, be, *_: (layer, be[b], 0, 0)
    grid_spec = pltpu.PrefetchScalarGridSpec(
        num_scalar_prefetch=n_pre,
        grid=(n_blocks,),
        in_specs=[
            pl.BlockSpec(memory_space=pl.ANY),
            pl.BlockSpec((None, 1, D_MODEL, EXPERT_FF), w_idx),
            pl.BlockSpec((None, 1, D_MODEL, EXPERT_FF), w_idx),
            pl.BlockSpec((None, 1, EXPERT_FF, D_MODEL), w_idx),
        ],
        out_specs=(pl.BlockSpec(memory_space=pl.ANY), pl.BlockSpec(memory_space=pl.ANY)),
        scratch_shapes=[pltpu.VMEM((2, MOE_BLOCK * ROW_TILE, LANES), F32),
                        pltpu.VMEM((2, MOE_BLOCK * ROW_TILE, LANES), F32),
                        pltpu.VMEM((D_MODEL, EXPERT_FF), BF16),
                        pltpu.VMEM((D_MODEL, EXPERT_FF), BF16),
                        pltpu.VMEM((EXPERT_FF, D_MODEL), BF16),
                        pltpu.SemaphoreType.DMA((2,)),
                        pltpu.SemaphoreType.DMA((2,))],
    )
    ys, _ = pl.pallas_call(
        functools.partial(_moe_kernel, tile_rows=tile_rows),
        grid_spec=grid_spec,
        out_shape=(jax.ShapeDtypeStruct(xs.shape, F32),
                   jax.ShapeDtypeStruct((2 * MOE_BLOCK * ROW_TILE, LANES), F32)),
        compiler_params=_cparams(("arbitrary",)),
        name="moe",
    )(*plan, xs, w_gate, w_up, w_down)
    return ys


def _ple_kernel(pos_ref, gate_ref, ys_ref, x1_ref, p_ref, gp_ref, wpg_ref, wpe_ref, gfin_ref,
                out_ref, moe_ref, *, final_norm):
    tm = x1_ref.shape[0]
    tile = lambda r: pl.ds(pl.multiple_of(r * ROW_TILE, ROW_TILE), ROW_TILE)

    def combine(t, carry):
        acc = None
        for k in range(TOP_K):
            term = gate_ref[TOP_K * t + k] * ys_ref[tile(pos_ref[TOP_K * t + k]), :]
            acc = term if acc is None else acc + term
        moe_ref[tile(t), :] = acc
        return carry

    lax.fori_loop(0, tm, combine, 0, unroll=8)
    x2 = x1_ref[...] + _load_row_tiles(moe_ref, tm)
    hp = _rms(x2, gp_ref[...]).astype(BF16)
    g_ple = _sigmoid(jnp.dot(hp, wpg_ref[...], preferred_element_type=F32))
    emb = jnp.dot(p_ref[...].astype(BF16), wpe_ref[...], preferred_element_type=F32)
    x3 = x2 + g_ple * emb
    if final_norm:
        x3 = _rms(x3, gfin_ref[...])
    out_ref[...] = x3


def _ple(pos, gate_flat, ys, x1, p, gp, wpg, wpe, gfin, tm, final_norm):
    t = x1.shape[0]
    row = lambda i: (i, 0)
    const = lambda i: (0, 0)
    full = lambda a: pl.BlockSpec(a.shape, const)
    smem = lambda n: pl.BlockSpec((n,), lambda i: (i,), memory_space=pltpu.SMEM)
    return pl.pallas_call(
        functools.partial(_ple_kernel, final_norm=final_norm),
        grid=(t // tm,),
        in_specs=[
            smem(TOP_K * tm),
            smem(TOP_K * tm),
            pl.BlockSpec((TOP_K * tm * ROW_TILE, LANES), row),
            pl.BlockSpec((tm, D_MODEL), row),
            pl.BlockSpec((tm, PLE_DIM), row),
            full(gp), full(wpg), full(wpe), full(gfin),
        ],
        out_specs=pl.BlockSpec((tm, D_MODEL), row),
        out_shape=jax.ShapeDtypeStruct((t, D_MODEL), F32),
        scratch_shapes=[pltpu.VMEM((tm * ROW_TILE, LANES), F32)],
        compiler_params=_cparams(("parallel",)),
        name="ple",
    )(pos, gate_flat, ys, x1, p, gp, wpg, wpe, gfin)


def _rope_tables(seq):
    inv = jnp.power(ROPE_THETA, -jnp.arange(0, HEAD_DIM, 2, dtype=F32) / HEAD_DIM)
    ang = jnp.arange(seq, dtype=F32)[:, None] * inv[None, :]
    cos, sin = jnp.cos(ang), jnp.sin(ang)
    cos_h = jnp.concatenate([cos, cos], axis=-1)
    sin_h = jnp.concatenate([-sin, sin], axis=-1)
    reps = LANES // HEAD_DIM
    return jnp.tile(cos_h, (1, reps)), jnp.tile(sin_h, (1, reps))


def _dispatch_plan(route_counts, group, n_blocks):
    i32 = jnp.int32
    grouped = route_counts.reshape(-1, group, N_EXPERTS)
    counts = jnp.sum(grouped, axis=1)
    base = ((jnp.cumsum(counts, axis=1) - counts)[:, None, :]
            + jnp.cumsum(grouped, axis=1) - grouped).reshape(-1).astype(i32)
    total = jnp.sum(counts, axis=0)
    blocks_of = (total + MOE_BLOCK - 1) // MOE_BLOCK
    block_end = jnp.cumsum(blocks_of)
    bidx = jnp.arange(n_blocks, dtype=i32)
    block_e = jnp.minimum(jnp.sum((block_end[None, :] <= bidx[:, None]).astype(i32), axis=1),
                          N_EXPERTS - 1)
    is_e = (block_e[:, None] == jnp.arange(N_EXPERTS, dtype=i32)[None, :]).astype(i32)
    pick = lambda per_expert: jnp.sum(is_e * per_expert[None, :], axis=1)
    pick_tiles = lambda tab: jnp.sum(is_e[:, None, :] * tab[None, :, :], axis=2)
    block_j = bidx - pick(block_end - blocks_of)
    used = bidx < block_end[-1]
    cum = jnp.cumsum(counts, axis=0) - counts
    off = jnp.cumsum(counts, axis=1) - counts
    lo = block_j * MOE_BLOCK
    hi = jnp.minimum(lo + MOE_BLOCK, pick(total))
    run_start = pick_tiles(cum)
    run_end = run_start + pick_tiles(counts)
    tile_lo = jnp.sum((run_end <= lo[:, None]).astype(i32), axis=1)
    tile_hi = jnp.sum((run_start < hi[:, None]).astype(i32), axis=1) - 1
    valid = jnp.clip(hi - lo, 0, MOE_BLOCK)
    tile_lo = jnp.where(used, tile_lo, 0)
    tile_hi = jnp.where(used, tile_hi, -1)
    valid = jnp.where(used, valid, 0)
    flat = lambda a: a.reshape(-1).astype(i32)
    return base, (flat(block_e), flat(block_j), flat(tile_lo), flat(tile_hi), flat(valid),
                  flat(cum), flat(counts), flat(off), flat(block_end[-1:]))


def kernel(x, p, norm_mix, w_in, w_dw, b_dw, ln_conv_g, ln_conv_b, w_attn_out, w_conv_out,
           b_conv_out, w_out, norm_ffn, w_route_group, b_route_group, w_route_expert,
           b_route_expert, w_exp_gate, w_exp_up, w_exp_down, norm_ple, w_ple_proj, w_ple_gate,
           norm_final):
    bsz, seq, d = x.shape
    depth = w_in.shape[0]
    t = bsz * seq
    tm = 512
    sort_tm = 1024
    n_assign = t * TOP_K
    assert n_assign > MOE_BLOCK
    n_rows = (-(-n_assign // MOE_BLOCK)) * MOE_BLOCK + N_EXPERTS * MOE_BLOCK
    cos, sin = _rope_tables(seq)
    row2 = lambda a: a.reshape(1, -1)

    xf = x.reshape(t, d)
    for i in range(depth):
        n_qkv = 3 * ATTN_WIDTH
        q, k, v = _inproj(xf, row2(norm_mix[i]), w_in[i, :, :n_qkv].astype(BF16), cos, sin,
                          bsz, seq, tm)
        u, ga, gb = _gateproj(xf, row2(norm_mix[i]), w_in[i, :, n_qkv:].astype(BF16), tm)
        o = _attention(q, k, v).reshape(t, ATTN_WIDTH)
        c = _conv(u.reshape(bsz, seq, CONV_WIDTH), w_dw[i], row2(b_dw[i]), row2(ln_conv_g[i]),
                  row2(ln_conv_b[i]), 256).reshape(t, CONV_WIDTH)

        w_r = jnp.zeros((d, LANES), F32)
        w_r = w_r.at[:, :N_GROUPS].set(w_route_group[i])
        w_r = w_r.at[:, EXPERT_LANE0:EXPERT_LANE0 + N_EXPERTS].set(w_route_expert[i])
        w_r_hi = w_r.astype(BF16)
        w_r_lo = (w_r - w_r_hi.astype(F32)).astype(BF16)
        b_r = jnp.zeros((1, LANES), F32)
        b_r = b_r.at[0, :N_GROUPS].set(b_route_group[i])
        b_r = b_r.at[0, EXPERT_LANE0:EXPERT_LANE0 + N_EXPERTS].set(b_route_expert[i])

        x1, h2, route, gate, cnt = _merge(
            o, c, ga, gb, xf, w_attn_out[i].astype(BF16), w_conv_out[i].astype(BF16),
            row2(b_conv_out[i]), w_out[i].astype(BF16), row2(norm_ffn[i]), w_r_hi, w_r_lo, b_r, tm)

        counts = cnt[:, 0, :N_EXPERTS].astype(jnp.int32)
        base, plan = _dispatch_plan(counts, sort_tm // tm, n_rows // MOE_BLOCK)
        xs, pos = _tile_sort(base, route.reshape(-1), h2, sort_tm, tm)
        ys = _moe(plan, xs, w_exp_gate, w_exp_up, w_exp_down, i, n_rows // MOE_BLOCK, TOP_K * sort_tm)
        xf = _ple(pos, gate.reshape(-1), ys, x1, p[i].reshape(t, PLE_DIM), row2(norm_ple[i]),
                  w_ple_gate[i].astype(BF16), w_ple_proj[i].astype(BF16), row2(norm_final),
                  sort_tm, final_norm=(i == depth - 1))
    return xf.reshape(bsz, seq, d)
```

```python
import functools

import jax
import jax.numpy as jnp
from jax import lax
from jax.experimental import pallas as pl
from jax.experimental.pallas import tpu as pltpu

D_MODEL = 1024
N_HEADS = 8
HEAD_DIM = 64
ATTN_WIDTH = N_HEADS * HEAD_DIM
CONV_WIDTH = 512
CONV_K = 31
DILATIONS = (16, 4, 1)
BLK = 128
ROPE_THETA = 10000.0
N_GROUPS = 4
EXPERTS_PER_GROUP = 8
N_EXPERTS = N_GROUPS * EXPERTS_PER_GROUP
EXPERT_FF = 512
TOP_K = 2
MOE_BLOCK = 256
PLE_DIM = 256
EPS = 1e-6
NEG_INF = -1e30

LANES = 128
SUBLANES = 8
ROW_TILE = D_MODEL // LANES
VMEM_LIMIT = 56 * 1024 * 1024

F32 = jnp.float32
BF16 = jnp.bfloat16


def _cparams(sem):
    return pltpu.CompilerParams(dimension_semantics=sem, vmem_limit_bytes=VMEM_LIMIT)


def _rms(x, g):
    return x * lax.rsqrt(jnp.mean(x * x, axis=-1, keepdims=True) + EPS) * g


def _sigmoid(x):
    return 0.5 * jnp.tanh(0.5 * x) + 0.5


def _store_row_tiles(ref, x):
    rows = x.shape[0]
    for s in range(ROW_TILE):
        ref[pl.ds(s, rows, stride=ROW_TILE), :] = x[:, s * LANES:(s + 1) * LANES]


def _load_row_tiles(ref, rows):
    return jnp.concatenate([ref[pl.ds(s, rows, stride=ROW_TILE), :] for s in range(ROW_TILE)], axis=1)


QKV_ROWS = 256


def _inproj_kernel(x_ref, g_ref, w_ref, cos_ref, sin_ref, *out_refs):
    n_lay = len(DILATIONS)
    q_refs, k_refs, v_refs = (out_refs[a * n_lay:(a + 1) * n_lay] for a in range(3))
    slab_refs = out_refs[3 * n_lay:]
    tm = x_ref.shape[0]
    n_slab = ATTN_WIDTH // LANES

    def put(ref, d, r, row0, g, val):
        cols = slice(g * LANES, (g + 1) * LANES)
        if d == 1:
            ref[pl.ds(row0, val.shape[0]), cols] = val.astype(BF16)
        else:
            ref[0, r, pl.ds(row0, val.shape[0]), cols] = val.astype(BF16)

    def fill(raw, rows, refs, slabs, is_query, post):
        for g in range(n_slab):
            cols = slice(g * LANES, (g + 1) * LANES)
            t = post(raw[:, cols], rows)
            slabs[g, rows, :] = t
            for d, ref in zip(DILATIONS, refs):
                if d == 1 and not is_query:
                    ref[rows, cols] = t.astype(BF16)

    def relayout(refs, slabs, by4, is_query):
        ref16, ref4, ref1 = refs
        quarter = tm // 4
        class4 = lambda r: pl.multiple_of(r * quarter, quarter)

        def to_by4(r, carry):
            for g in range(n_slab):
                val = slabs[g, pl.ds(r, quarter, stride=4), :]
                by4[g, pl.ds(class4(r), quarter), :] = val
                if not is_query:
                    put(ref4, 4, r, 0, g, val)
            return carry
        lax.fori_loop(0, 4, to_by4, 0, unroll=2)

        def to_16(r, carry):
            for g in range(n_slab):
                put(ref16, 16, r, 0, g, by4[g, pl.ds(class4(r % 4) + r // 4, tm // 16, stride=4), :])
            return carry
        lax.fori_loop(0, 16, to_16, 0, unroll=4)

        if is_query:
            def q4_class(r, carry):
                for bl in range(quarter // BLK):
                    for a in range(4):
                        src = pl.ds(class4(r) + bl * BLK + a, BLK // 4, stride=4)
                        for g in range(n_slab):
                            put(ref4, 4, r, bl * BLK + a * (BLK // 4), g, by4[g, src, :])
                return carry
            lax.fori_loop(0, 4, q4_class, 0, unroll=2)

            def q1_block(bl, carry):
                for a in range(16):
                    src = pl.ds((a % 4) * quarter + (BLK // 4) * bl + a // 4, BLK // 16, stride=4)
                    for g in range(n_slab):
                        put(ref1, 1, 0, pl.multiple_of(bl * BLK, BLK) + a * (BLK // 16), g, by4[g, src, :])
                return carry
            lax.fori_loop(0, tm // BLK, q1_block, 0, unroll=2)

    lane = lax.broadcasted_iota(jnp.int32, (QKV_ROWS, LANES), 1)
    first_half = (lane % HEAD_DIM) < (HEAD_DIM // 2)

    def rope(t, rows):
        partner = jnp.where(first_half,
                            pltpu.roll(t, LANES - HEAD_DIM // 2, 1),
                            pltpu.roll(t, HEAD_DIM // 2, 1))
        return t * cos_ref[rows, :] + partner * sin_ref[rows, :]

    for r0 in range(0, tm, QKV_ROWS):
        rows = slice(r0, r0 + QKV_ROWS)
        h = _rms(x_ref[rows, :], g_ref[...]).astype(BF16)
        proj = lambda a: jnp.dot(h, w_ref[:, a * ATTN_WIDTH:(a + 1) * ATTN_WIDTH],
                                 preferred_element_type=F32)
        fill(proj(0), rows, q_refs, slab_refs[0], True, lambda t, rw: rope(t, rw) * (HEAD_DIM ** -0.5))
        fill(proj(1), rows, k_refs, slab_refs[1], False, rope)
        fill(proj(2), rows, v_refs, slab_refs[2], False, lambda t, rw: t)
    assert DILATIONS == (16, 4, 1) and tm % (4 * BLK) == 0
    relayout(q_refs, slab_refs[0], slab_refs[3], True)
    relayout(k_refs, slab_refs[1], slab_refs[4], False)
    relayout(v_refs, slab_refs[2], slab_refs[5], False)


def _gateproj_kernel(x_ref, g_ref, w_ref, u_ref, ga_ref, gb_ref):
    h = _rms(x_ref[...], g_ref[...]).astype(BF16)

    def proj(c0, width):
        return jnp.dot(h, w_ref[:, c0:c0 + width], preferred_element_type=F32)

    u_ref[...] = proj(0, CONV_WIDTH) * _sigmoid(proj(CONV_WIDTH, CONV_WIDTH))
    c0 = 2 * CONV_WIDTH
    ga_ref[...] = proj(c0, D_MODEL).astype(ga_ref.dtype)
    gb_ref[...] = proj(c0 + D_MODEL, D_MODEL).astype(gb_ref.dtype)


def _gateproj(x, gain, w_rest, tm):
    t = x.shape[0]
    row = lambda i: (i, 0)
    const = lambda i: (0, 0)
    return pl.pallas_call(
        _gateproj_kernel,
        grid=(t // tm,),
        in_specs=[pl.BlockSpec((tm, D_MODEL), row), pl.BlockSpec((1, D_MODEL), const),
                  pl.BlockSpec(w_rest.shape, const)],
        out_specs=(pl.BlockSpec((tm, CONV_WIDTH), row), pl.BlockSpec((tm, D_MODEL), row),
                   pl.BlockSpec((tm, D_MODEL), row)),
        out_shape=(jax.ShapeDtypeStruct((t, CONV_WIDTH), F32),
                   jax.ShapeDtypeStruct((t, D_MODEL), BF16),
                   jax.ShapeDtypeStruct((t, D_MODEL), BF16)),
        compiler_params=_cparams(("parallel",)),
        name="gateproj",
    )(x, gain, w_rest)


def _inproj(x, gain, w_in, cos, sin, bsz, seq, tm):
    t = x.shape[0]
    n_pos = seq // tm
    row = lambda i: (i, 0)
    const = lambda i: (0, 0)
    qkv_shapes, qkv_specs = [], []
    for d in DILATIONS:
        if d == 1:
            qkv_shapes.append(jax.ShapeDtypeStruct((t, ATTN_WIDTH), BF16))
            qkv_specs.append(pl.BlockSpec((tm, ATTN_WIDTH), row))
        else:
            qkv_shapes.append(jax.ShapeDtypeStruct((bsz, d, seq // d, ATTN_WIDTH), BF16))
            qkv_specs.append(pl.BlockSpec((1, d, tm // d, ATTN_WIDTH),
                                          lambda i: (i // n_pos, 0, i % n_pos, 0)))
    out_shape = tuple(qkv_shapes * 3)
    out_specs = tuple(qkv_specs * 3)
    outs = pl.pallas_call(
        _inproj_kernel,
        grid=(t // tm,),
        in_specs=[
            pl.BlockSpec((tm, D_MODEL), row),
            pl.BlockSpec((1, D_MODEL), const),
            pl.BlockSpec(w_in.shape, const),
            pl.BlockSpec((tm, LANES), lambda i: (i % n_pos, 0)),
            pl.BlockSpec((tm, LANES), lambda i: (i % n_pos, 0)),
        ],
        out_specs=out_specs,
        out_shape=out_shape,
        scratch_shapes=[pltpu.VMEM((ATTN_WIDTH // LANES, tm, LANES), F32) for _ in range(6)],
        compiler_params=_cparams(("parallel",)),
        name="inproj",
    )(x, gain, w_in, cos, sin)
    n_lay = len(DILATIONS)
    as_rows = lambda a: a.reshape(bsz, seq, ATTN_WIDTH)
    return tuple([as_rows(a) for a in outs[j * n_lay:(j + 1) * n_lay]] for j in range(3))


STATE_DILATION = max(DILATIONS)


def _query_order(idx, groups):
    run = BLK // groups
    return groups * (idx % run) + idx // run


def _attn_kernel(*refs, seq):
    n_lay = len(DILATIONS)
    srcs = refs[:3 * n_lay]
    o_ref, buf, acc_s, m_s, l_s, sem = refs[3 * n_lay:]
    b = pl.program_id(0)
    n_b = pl.num_programs(0)
    n_blocks = seq // BLK
    pair_w = 2 * HEAD_DIM
    n_pair = ATTN_WIDTH // pair_w

    def load_sweep(lay, bi, slot):
        return [pltpu.make_async_copy(srcs[a * n_lay + lay].at[bi], buf.at[slot, a, pl.ds(BLK, seq)],
                                      sem.at[slot]) for a in range(3)]

    @pl.when(b == 0)
    def _():
        buf[:, :, 0:BLK, :] = jnp.zeros((2, 3, BLK, ATTN_WIDTH), BF16)
        for c in load_sweep(0, 0, 0):
            c.start()

    row = lax.broadcasted_iota(jnp.int32, (BLK, BLK), 0)
    col = lax.broadcasted_iota(jnp.int32, (BLK, BLK), 1)
    lane = lax.broadcasted_iota(jnp.int32, (BLK, pair_w), 1)
    low_head = lane < HEAD_DIM
    unpermute = (row == _query_order(col, STATE_DILATION // DILATIONS[-1])).astype(BF16)

    for lay, d in enumerate(DILATIONS):
        slot = (b * n_lay + lay) % 2
        for c in load_sweep(lay, b, slot):
            c.wait()
        if lay + 1 < n_lay:
            for c in load_sweep(lay + 1, b, 1 - slot):
                c.start()
        else:
            @pl.when(b + 1 < n_b)
            def _():
                for c in load_sweep(0, b + 1, 1 - slot):
                    c.start()

        nb = seq // (d * BLK)
        has_prev = nb > 1
        first, final = lay == 0, lay == n_lay - 1
        kw = 2 * BLK if has_prev else BLK

        groups = STATE_DILATION // d
        run = BLK // groups
        q_pos = _query_order(row, groups)

        def block(j, carry, d=d, nb=nb, has_prev=has_prev, first=first, final=final, kw=kw,
                  slot=slot, groups=groups, run=run, q_pos=q_pos):
            r, n = j // nb, j % nb
            q_row0 = pl.multiple_of(BLK + j * BLK, BLK)
            k_row0 = pl.multiple_of(j * BLK, BLK) if has_prev else q_row0
            runs = [pl.ds(pl.multiple_of((d * a + r) * BLK + run * n, run), run)
                    for a in range(groups)]
            load = lambda ref, g: jnp.concatenate([ref[g, s, :] for s in runs], axis=0)

            def store(ref, g, val):
                for a, s in enumerate(runs):
                    ref[g, s, :] = val[a * run:(a + 1) * run]

            cur_ok = col <= q_pos
            if has_prev:
                prev_ok = jnp.logical_and(col >= q_pos, n > 0)
                allowed = jnp.concatenate([prev_ok, cur_ok], axis=1)
            else:
                allowed = cur_ok
            allowed2 = jnp.concatenate([allowed, allowed], axis=0)
            lanes = [slice(g * pair_w, (g + 1) * pair_w) for g in range(n_pair)]
            scores = []
            for ls in lanes:
                q2 = buf[slot, 0, pl.ds(q_row0, BLK), ls]
                q_both = jnp.concatenate([jnp.where(low_head, q2, 0),
                                          jnp.where(low_head, 0, q2)], axis=0)
                k2 = buf[slot, 1, pl.ds(k_row0, kw), ls]
                scores.append(lax.dot_general(q_both, k2, (((1,), (1,)), ((), ())),
                                              preferred_element_type=F32))
            probs, maxes, sums = [], [], []
            for s in scores:
                s = jnp.where(allowed2, s, NEG_INF)
                m_h = jnp.max(s, axis=-1, keepdims=True)
                p = jnp.exp(s - m_h)
                sums.append(jnp.sum(p, axis=-1, keepdims=True))
                maxes.append(m_h)
                probs.append(p.astype(BF16))
            pv = [jnp.dot(p, buf[slot, 2, pl.ds(k_row0, kw), ls], preferred_element_type=F32)
                  for p, ls in zip(probs, lanes)]
            for g, ls in enumerate(lanes):
                acc = jnp.where(low_head, pv[g][:BLK], pv[g][BLK:])
                m_new = jnp.where(low_head, maxes[g][:BLK], maxes[g][BLK:])
                l_new = jnp.where(low_head, sums[g][:BLK], sums[g][BLK:])
                if not first:
                    m_old = load(m_s, g)
                    m_tot = jnp.maximum(m_old, m_new)
                    a_old = jnp.exp(m_old - m_tot)
                    a_new = jnp.exp(m_new - m_tot)
                    acc = load(acc_s, g) * a_old + acc * a_new
                    l_new = load(l_s, g) * a_old + l_new * a_new
                    m_new = m_tot
                if final:
                    o_blk = jnp.dot(unpermute, (acc / l_new).astype(BF16), preferred_element_type=F32)
                    o_ref[0, pl.ds(pl.multiple_of(j * BLK, BLK), BLK), ls] = o_blk.astype(o_ref.dtype)
                else:
                    store(acc_s, g, acc)
                    store(m_s, g, m_new)
                    store(l_s, g, l_new)
            return carry

        lax.fori_loop(0, n_blocks, block, 0, unroll=2)


def _attention(q, k, v):
    bsz, seq, _ = q[0].shape
    assert all(seq % (d * BLK) == 0 for d in DILATIONS) and DILATIONS[-1] == 1
    n_pair = ATTN_WIDTH // (2 * HEAD_DIM)
    return pl.pallas_call(
        functools.partial(_attn_kernel, seq=seq),
        grid=(bsz,),
        in_specs=[pl.BlockSpec(memory_space=pl.ANY)] * (3 * len(DILATIONS)),
        out_specs=pl.BlockSpec((1, seq, ATTN_WIDTH), lambda b: (b, 0, 0)),
        out_shape=jax.ShapeDtypeStruct((bsz, seq, ATTN_WIDTH), BF16),
        scratch_shapes=[pltpu.VMEM((2, 3, BLK + seq, ATTN_WIDTH), BF16),
                        pltpu.VMEM((n_pair, seq, 2 * HEAD_DIM), F32),
                        pltpu.VMEM((n_pair, seq, 2 * HEAD_DIM), F32),
                        pltpu.VMEM((n_pair, seq, 2 * HEAD_DIM), F32),
                        pltpu.SemaphoreType.DMA((2,))],
        compiler_params=_cparams(("arbitrary",)),
        name="attention",
    )(*q, *k, *v)


CONV_HALO = 32
CONV_ROWS = 32


def _conv_kernel(halo_ref, cur_ref, w_ref, b_ref, g_ref, beta_ref, out_ref, ext_ref):
    ts = cur_ref.shape[1]
    i = pl.program_id(1)
    ext_ref[0, 0:CONV_HALO, :] = jnp.where(i > 0, halo_ref[0], 0.0)
    ext_ref[0, CONV_HALO:CONV_HALO + ts, :] = cur_ref[0]
    first = CONV_HALO - (CONV_K - 1)
    n_shift = ext_ref.shape[0]
    rows = CONV_HALO + ts
    ext0 = ext_ref[0]
    for s in range(1, n_shift):
        ext_ref[s] = pltpu.roll(ext0, rows - s, 0)
    bias = b_ref[...]
    gamma = g_ref[...]
    beta = beta_ref[...]
    for r0 in range(0, ts, CONV_ROWS):
        acc = jnp.zeros((CONV_ROWS, CONV_WIDTH), F32) + bias
        for j in range(CONV_K):
            lo = first + j + r0
            s, lo = lo % n_shift, lo - lo % n_shift
            tap = jnp.concatenate([w_ref[j]] * (CONV_ROWS // SUBLANES), axis=0)
            acc = acc + tap * ext_ref[s, lo:lo + CONV_ROWS, :]
        mu = jnp.mean(acc, axis=-1, keepdims=True)
        cen = acc - mu
        var = jnp.mean(cen * cen, axis=-1, keepdims=True)
        y = cen * lax.rsqrt(var + EPS) * gamma + beta
        out_ref[0, r0:r0 + CONV_ROWS, :] = (y * _sigmoid(y)).astype(out_ref.dtype)


def _conv(u, w_dw, b_dw, ln_g, ln_b, ts):
    b, s, c = u.shape
    per = ts // CONV_HALO
    const = lambda bi, i: (0, 0)
    return pl.pallas_call(
        _conv_kernel,
        grid=(b, s // ts),
        in_specs=[
            pl.BlockSpec((1, CONV_HALO, c), lambda bi, i: (bi, jnp.maximum(i * per - 1, 0), 0)),
            pl.BlockSpec((1, ts, c), lambda bi, i: (bi, i, 0)),
            pl.BlockSpec((CONV_K, SUBLANES, c), lambda bi, i: (0, 0, 0)),
            pl.BlockSpec((1, c), const),
            pl.BlockSpec((1, c), const),
            pl.BlockSpec((1, c), const),
        ],
        out_specs=pl.BlockSpec((1, ts, c), lambda bi, i: (bi, i, 0)),
        out_shape=jax.ShapeDtypeStruct((b, s, c), BF16),
        scratch_shapes=[pltpu.VMEM((SUBLANES, CONV_HALO + ts, c), F32)],
        compiler_params=_cparams(("parallel", "arbitrary")),
        name="conv",
    )(u, u, jnp.broadcast_to(w_dw[:, None, :], (CONV_K, SUBLANES, c)), b_dw, ln_g, ln_b)


ROUTE_COLS = 4
EXPERT_LANE0 = N_GROUPS


def _merge_kernel(o_ref, c_ref, ga_ref, gb_ref, x_ref, wa_ref, wc_ref, bc_ref, wo_ref,
                  gf_ref, wrh_ref, wrl_ref, br_ref,
                  x1_ref, h2_ref, route_ref, gate_ref, cnt_ref, tri_ref):
    tm = x_ref.shape[0]

    @pl.when(pl.program_id(0) == 0)
    def _():
        r = lax.broadcasted_iota(jnp.int32, (tm, tm), 0)
        c = lax.broadcasted_iota(jnp.int32, (tm, tm), 1)
        tri_ref[...] = (c < r).astype(BF16)

    y_a = jnp.dot(o_ref[...], wa_ref[...], preferred_element_type=F32)
    y_b = jnp.dot(c_ref[...], wc_ref[...], preferred_element_type=F32) + bc_ref[...]
    merged = _sigmoid(ga_ref[...].astype(F32)) * y_a + _sigmoid(gb_ref[...].astype(F32)) * y_b
    x1 = x_ref[...] + jnp.dot(merged.astype(BF16), wo_ref[...], preferred_element_type=F32)
    x1_ref[...] = x1
    h2 = _rms(x1, gf_ref[...])
    _store_row_tiles(h2_ref, h2)

    h_hi = h2.astype(BF16)
    h_lo = (h2 - h_hi.astype(F32)).astype(BF16)
    logits = (jnp.dot(h_hi, wrh_ref[...], preferred_element_type=F32)
              + jnp.dot(h_hi, wrl_ref[...], preferred_element_type=F32)
              + jnp.dot(h_lo, wrh_ref[...], preferred_element_type=F32)) + br_ref[...]

    lane = lax.broadcasted_iota(jnp.int32, (tm, LANES), 1).astype(F32)
    no_lane = float(LANES)
    is_group = lane < N_GROUPS
    g_max = jnp.max(jnp.where(is_group, logits, -jnp.inf), axis=-1, keepdims=True)
    g_sel = jnp.min(jnp.where(jnp.logical_and(is_group, logits == g_max), lane, no_lane),
                    axis=-1, keepdims=True)
    p_group = 1.0 / jnp.sum(jnp.where(is_group, jnp.exp(logits - g_max), 0.0),
                            axis=-1, keepdims=True)
    lo = EXPERT_LANE0 + EXPERTS_PER_GROUP * g_sel
    in_sel = jnp.logical_and(lane >= lo, lane < lo + EXPERTS_PER_GROUP)
    cand = jnp.where(in_sel, logits, -jnp.inf)
    v1 = jnp.max(cand, axis=-1, keepdims=True)
    i1 = jnp.min(jnp.where(jnp.logical_and(in_sel, cand == v1), lane, no_lane), axis=-1, keepdims=True)
    rest = jnp.logical_and(in_sel, lane != i1)
    cand2 = jnp.where(rest, logits, -jnp.inf)
    v2 = jnp.max(cand2, axis=-1, keepdims=True)
    i2 = jnp.min(jnp.where(jnp.logical_and(rest, cand2 == v2), lane, no_lane), axis=-1, keepdims=True)
    e2 = jnp.exp(v2 - v1)
    gate0 = p_group / (1.0 + e2)
    gate1 = p_group * e2 / (1.0 + e2)

    oh0 = (lane == i1)
    oh1 = (lane == i2)
    before0 = jnp.dot(tri_ref[...], oh0.astype(BF16), preferred_element_type=F32)
    before1 = jnp.dot(tri_ref[...], oh1.astype(BF16), preferred_element_type=F32)
    tot0 = jnp.sum(oh0.astype(F32), axis=0, keepdims=True)
    tot1 = jnp.sum(oh1.astype(F32), axis=0, keepdims=True)
    cnt = tot0 + tot1
    rank0 = jnp.sum(jnp.where(oh0, before0, 0.0), axis=-1, keepdims=True)
    rank1 = jnp.sum(jnp.where(oh1, tot0 + before1, 0.0), axis=-1, keepdims=True)

    rc = lax.broadcasted_iota(jnp.int32, (tm, ROUTE_COLS), 1)
    route = jnp.where(rc == 0, i1 - EXPERT_LANE0,
                      jnp.where(rc == 1, i2 - EXPERT_LANE0, jnp.where(rc == 2, rank0, rank1)))
    route_ref[...] = route.astype(jnp.int32)
    gc = lax.broadcasted_iota(jnp.int32, (tm, TOP_K), 1)
    gate_ref[...] = jnp.where(gc == 0, gate0, gate1)
    cnt_ref[0] = jnp.broadcast_to(pltpu.roll(cnt, LANES - EXPERT_LANE0, 1), (8, LANES))


def _merge(o, c, ga, gb, x, wa, wc, bc, wo, gf, wr_hi, wr_lo, br, tm):
    t = x.shape[0]
    nt = t // tm
    row = lambda i: (i, 0)
    const = lambda i: (0, 0)
    full = lambda a: pl.BlockSpec(a.shape, const)
    return pl.pallas_call(
        _merge_kernel,
        grid=(nt,),
        in_specs=[
            pl.BlockSpec((tm, ATTN_WIDTH), row),
            pl.BlockSpec((tm, CONV_WIDTH), row),
            pl.BlockSpec((tm, D_MODEL), row),
            pl.BlockSpec((tm, D_MODEL), row),
            pl.BlockSpec((tm, D_MODEL), row),
            full(wa), full(wc), full(bc), full(wo), full(gf), full(wr_hi), full(wr_lo), full(br),
        ],
        out_specs=(
            pl.BlockSpec((tm, D_MODEL), row),
            pl.BlockSpec((tm * ROW_TILE, LANES), row),
            pl.BlockSpec((tm, ROUTE_COLS), row),
            pl.BlockSpec((tm, TOP_K), row),
            pl.BlockSpec((1, 8, LANES), lambda i: (i, 0, 0)),
        ),
        out_shape=(
            jax.ShapeDtypeStruct((t, D_MODEL), F32),
            jax.ShapeDtypeStruct((t * ROW_TILE, LANES), F32),
            jax.ShapeDtypeStruct((t, ROUTE_COLS), jnp.int32),
            jax.ShapeDtypeStruct((t, TOP_K), F32),
            jax.ShapeDtypeStruct((nt, 8, LANES), F32),
        ),
        scratch_shapes=[pltpu.VMEM((tm, tm), BF16)],
        compiler_params=_cparams(("arbitrary",)),
        name="merge",
    )(o, c, ga, gb, x, wa, wc, bc, wo, gf, wr_hi, wr_lo, br)


def _pos_kernel(route_ref, base_ref, pos_ref):
    tm = route_ref.shape[0]
    route = route_ref[...].astype(F32)
    lane = lax.broadcasted_iota(jnp.int32, (tm, LANES), 1).astype(F32)
    cols = []
    for k in range(TOP_K):
        start = jnp.sum(jnp.where(lane == route[:, k:k + 1], base_ref[0, 0:1, :], 0.0),
                        axis=-1, keepdims=True)
        cols.append(start + route[:, TOP_K + k:TOP_K + k + 1])
    sel = lax.broadcasted_iota(jnp.int32, (tm, TOP_K), 1)
    pos_ref[...] = jnp.where(sel == 0, cols[0], cols[1]).astype(jnp.int32)


def _positions(route, base, tm):
    nt = base.shape[0]
    base_rows = jnp.zeros((nt, SUBLANES, LANES), F32).at[:, :, :N_EXPERTS].set(
        base.astype(F32)[:, None, :])
    return pl.pallas_call(
        _pos_kernel,
        grid=(nt,),
        in_specs=[pl.BlockSpec((tm, ROUTE_COLS), lambda i: (i, 0)),
                  pl.BlockSpec((1, SUBLANES, LANES), lambda i: (i, 0, 0))],
        out_specs=pl.BlockSpec((tm, TOP_K), lambda i: (i, 0)),
        out_shape=jax.ShapeDtypeStruct((route.shape[0], TOP_K), jnp.int32),
        compiler_params=_cparams(("parallel",)),
        name="positions",
    )(route, base_rows)


def _sort_kernel(pos_ref, h_ref, xs_ref):
    tm = h_ref.shape[0] // ROW_TILE

    def place(t, carry):
        row = h_ref[pl.ds(pl.multiple_of(t * ROW_TILE, ROW_TILE), ROW_TILE), :]
        for k in range(TOP_K):
            pos = pos_ref[TOP_K * t + k]
            xs_ref[pl.ds(pl.multiple_of(pos * ROW_TILE, ROW_TILE), ROW_TILE), :] = row
        return carry

    lax.fori_loop(0, tm, place, 0, unroll=8)


def _tile_sort(pos_flat, h2, tm):
    nt = h2.shape[0] // (tm * ROW_TILE)
    return pl.pallas_call(
        _sort_kernel,
        grid=(nt,),
        in_specs=[pl.BlockSpec((TOP_K * tm,), lambda i: (i,), memory_space=pltpu.SMEM),
                  pl.BlockSpec((tm * ROW_TILE, LANES), lambda i: (i, 0))],
        out_specs=pl.BlockSpec((TOP_K * tm * ROW_TILE, LANES), lambda i: (i, 0)),
        out_shape=jax.ShapeDtypeStruct((TOP_K * h2.shape[0], LANES), F32),
        compiler_params=_cparams(("parallel",)),
        name="tile_sort",
    )(pos_flat, h2)


RARE_RUN = 64


def _moe_kernel(be_ref, bj_ref, ilo_ref, ihi_ref, valid_ref, cum_ref, cnt_ref, off_ref, nused_ref,
                xs_hbm, wg_ref, wu_ref, wd_ref, ys_hbm, trash_hbm,
                xbuf, obuf, wg_s, wu_s, wd_s, sem_in, sem_out, *, tile_rows):
    b = pl.program_id(0)
    slot = b % 2
    other = 1 - slot
    rows8 = lambda r, n: pl.ds(pl.multiple_of(r * ROW_TILE, ROW_TILE), n * ROW_TILE)

    def for_runs(bb, fn):
        e = be_ref[bb]
        lo_blk = bj_ref[bb] * MOE_BLOCK

        def body(i, carry):
            c = cum_ref[i * N_EXPERTS + e]
            lo = jnp.maximum(c, lo_blk)
            hi = jnp.minimum(c + cnt_ref[i * N_EXPERTS + e], lo_blk + MOE_BLOCK)
            fn(i * tile_rows + off_ref[i * N_EXPERTS + e] + (lo - c), lo - lo_blk,
               jnp.maximum(hi - lo, 0))
            return carry

        lax.fori_loop(ilo_ref[bb], ihi_ref[bb] + 1, body, 0)

    def for_chunks(length, fn):
        def piece(size):
            @pl.when((length & size) != 0)
            def _():
                fn(length & ~(2 * size - 1), size)

        sizes = [1 << bit for bit in reversed(range(MOE_BLOCK.bit_length()))]

        @pl.when(length >= RARE_RUN)
        def _():
            for size in sizes:
                if size >= RARE_RUN:
                    piece(size)

        for size in sizes:
            if size < RARE_RUN:
                piece(size)

    def gather(bb, s):
        def run(src, dst, length):
            for_chunks(length, lambda o, n: pltpu.make_async_copy(
                xs_hbm.at[rows8(src + o, n)], xbuf.at[s, rows8(dst + o, n)], sem_in.at[s]).start())
        for_runs(bb, run)
        v = valid_ref[bb]
        for_chunks(MOE_BLOCK - v, lambda o, n: pltpu.make_async_copy(
            xs_hbm.at[rows8(o, n)], xbuf.at[s, rows8(v + o, n)], sem_in.at[s]).start())

    def scatter(bb, s):
        def run(src, dst, length):
            for_chunks(length, lambda o, n: pltpu.make_async_copy(
                obuf.at[s, rows8(dst + o, n)], ys_hbm.at[rows8(src + o, n)], sem_out.at[s]).start())
        for_runs(bb, run)
        v = valid_ref[bb]
        for_chunks(MOE_BLOCK - v, lambda o, n: pltpu.make_async_copy(
            obuf.at[s, rows8(v + o, n)], trash_hbm.at[rows8(s * MOE_BLOCK + v + o, n)],
            sem_out.at[s]).start())

    def wait_gather(s):
        pltpu.make_async_copy(xs_hbm.at[rows8(0, MOE_BLOCK)], xbuf.at[s], sem_in.at[s]).wait()

    def wait_scatter(s):
        pltpu.make_async_copy(obuf.at[s], ys_hbm.at[rows8(0, MOE_BLOCK)], sem_out.at[s]).wait()

    @pl.when(b == 0)
    def _():
        obuf[...] = jnp.zeros_like(obuf)
        for s in range(2):
            init = pltpu.make_async_copy(obuf.at[s], trash_hbm.at[rows8(s * MOE_BLOCK, MOE_BLOCK)],
                                         sem_out.at[s])
            init.start()
            init.wait()
        gather(0, 0)

    n_used = nused_ref[0]
    live = b < n_used

    @pl.when(live)
    def _():
        wait_gather(slot)

    @pl.when(jnp.logical_and(live, b >= 2))
    def _():
        wait_scatter(slot)

    e = be_ref[b]
    e_prev = be_ref[jnp.maximum(b - 1, 0)]

    @pl.when(jnp.logical_and(live, jnp.logical_or(b == 0, e != e_prev)))
    def _():
        wg_s[...] = wg_ref[0].astype(BF16)
        wu_s[...] = wu_ref[0].astype(BF16)
        wd_s[...] = wd_ref[0].astype(BF16)

    @pl.when(b + 1 < n_used)
    def _():
        gather(b + 1, other)

    @pl.when(jnp.logical_and(live, b >= 1))
    def _():
        scatter(b - 1, other)

    @pl.when(live)
    def _():
        x = _load_row_tiles(xbuf.at[slot], MOE_BLOCK).astype(BF16)
        hg = jnp.dot(x, wg_s[...], preferred_element_type=F32)
        hu = jnp.dot(x, wu_s[...], preferred_element_type=F32)
        hb = (hg * _sigmoid(hg) * hu).astype(BF16)
        _store_row_tiles(obuf.at[slot], jnp.dot(hb, wd_s[...], preferred_element_type=F32))

    @pl.when(b == n_used - 1)
    def _():
        scatter(b, slot)
        wait_scatter(other)
        wait_scatter(slot)


def _moe(plan, xs, w_gate, w_up, w_down, layer, n_blocks, tile_rows):
    n_pre = len(plan)
    w_idx = lambda b, be, *_: (layer, be[b], 0, 0)
    grid_spec = pltpu.PrefetchScalarGridSpec(
        num_scalar_prefetch=n_pre,
        grid=(n_blocks,),
        in_specs=[
            pl.BlockSpec(memory_space=pl.ANY),
            pl.BlockSpec((None, 1, D_MODEL, EXPERT_FF), w_idx),
            pl.BlockSpec((None, 1, D_MODEL, EXPERT_FF), w_idx),
            pl.BlockSpec((None, 1, EXPERT_FF, D_MODEL), w_idx),
        ],
        out_specs=(pl.BlockSpec(memory_space=pl.ANY), pl.BlockSpec(memory_space=pl.ANY)),
        scratch_shapes=[pltpu.VMEM((2, MOE_BLOCK * ROW_TILE, LANES), F32),
                        pltpu.VMEM((2, MOE_BLOCK * ROW_TILE, LANES), F32),
                        pltpu.VMEM((D_MODEL, EXPERT_FF), BF16),
                        pltpu.VMEM((D_MODEL, EXPERT_FF), BF16),
                        pltpu.VMEM((EXPERT_FF, D_MODEL), BF16),
                        pltpu.SemaphoreType.DMA((2,)),
                        pltpu.SemaphoreType.DMA((2,))],
    )
    ys, _ = pl.pallas_call(
        functools.partial(_moe_kernel, tile_rows=tile_rows),
        grid_spec=grid_spec,
        out_shape=(jax.ShapeDtypeStruct(xs.shape, F32),
                   jax.ShapeDtypeStruct((2 * MOE_BLOCK * ROW_TILE, LANES), F32)),
        compiler_params=_cparams(("arbitrary",)),
        name="moe",
    )(*plan, xs, w_gate, w_up, w_down)
    return ys


def _ple_kernel(pos_ref, gate_ref, ys_ref, x1_ref, p_ref, gp_ref, wpg_ref, wpe_ref, gfin_ref,
                out_ref, moe_ref, *, final_norm):
    tm = x1_ref.shape[0]
    tile = lambda r: pl.ds(pl.multiple_of(r * ROW_TILE, ROW_TILE), ROW_TILE)

    def combine(t, carry):
        acc = None
        for k in range(TOP_K):
            term = gate_ref[TOP_K * t + k] * ys_ref[tile(pos_ref[TOP_K * t + k]), :]
            acc = term if acc is None else acc + term
        moe_ref[tile(t), :] = acc
        return carry

    lax.fori_loop(0, tm, combine, 0, unroll=8)
    x2 = x1_ref[...] + _load_row_tiles(moe_ref, tm)
    hp = _rms(x2, gp_ref[...]).astype(BF16)
    g_ple = _sigmoid(jnp.dot(hp, wpg_ref[...], preferred_element_type=F32))
    emb = jnp.dot(p_ref[...].astype(BF16), wpe_ref[...], preferred_element_type=F32)
    x3 = x2 + g_ple * emb
    if final_norm:
        x3 = _rms(x3, gfin_ref[...])
    out_ref[...] = x3


def _ple(pos, gate_flat, ys, x1, p, gp, wpg, wpe, gfin, tm, final_norm):
    t = x1.shape[0]
    row = lambda i: (i, 0)
    const = lambda i: (0, 0)
    full = lambda a: pl.BlockSpec(a.shape, const)
    smem = lambda n: pl.BlockSpec((n,), lambda i: (i,), memory_space=pltpu.SMEM)
    return pl.pallas_call(
        functools.partial(_ple_kernel, final_norm=final_norm),
        grid=(t // tm,),
        in_specs=[
            smem(TOP_K * tm),
            smem(TOP_K * tm),
            pl.BlockSpec((TOP_K * tm * ROW_TILE, LANES), row),
            pl.BlockSpec((tm, D_MODEL), row),
            pl.BlockSpec((tm, PLE_DIM), row),
            full(gp), full(wpg), full(wpe), full(gfin),
        ],
        out_specs=pl.BlockSpec((tm, D_MODEL), row),
        out_shape=jax.ShapeDtypeStruct((t, D_MODEL), F32),
        scratch_shapes=[pltpu.VMEM((tm * ROW_TILE, LANES), F32)],
        compiler_params=_cparams(("parallel",)),
        name="ple",
    )(pos, gate_flat, ys, x1, p, gp, wpg, wpe, gfin)


def _rope_tables(seq):
    inv = jnp.power(ROPE_THETA, -jnp.arange(0, HEAD_DIM, 2, dtype=F32) / HEAD_DIM)
    ang = jnp.arange(seq, dtype=F32)[:, None] * inv[None, :]
    cos, sin = jnp.cos(ang), jnp.sin(ang)
    cos_h = jnp.concatenate([cos, cos], axis=-1)
    sin_h = jnp.concatenate([-sin, sin], axis=-1)
    reps = LANES // HEAD_DIM
    return jnp.tile(cos_h, (1, reps)), jnp.tile(sin_h, (1, reps))


def _dispatch_plan(route_counts, group, n_blocks):
    i32 = jnp.int32
    grouped = route_counts.reshape(-1, group, N_EXPERTS)
    counts = jnp.sum(grouped, axis=1)
    base = ((jnp.cumsum(counts, axis=1) - counts)[:, None, :]
            + jnp.cumsum(grouped, axis=1) - grouped).reshape(-1, N_EXPERTS).astype(i32)
    total = jnp.sum(counts, axis=0)
    blocks_of = (total + MOE_BLOCK - 1) // MOE_BLOCK
    block_end = jnp.cumsum(blocks_of)
    bidx = jnp.arange(n_blocks, dtype=i32)
    block_e = jnp.minimum(jnp.sum((block_end[None, :] <= bidx[:, None]).astype(i32), axis=1),
                          N_EXPERTS - 1)
    is_e = (block_e[:, None] == jnp.arange(N_EXPERTS, dtype=i32)[None, :]).astype(i32)
    pick = lambda per_expert: jnp.sum(is_e * per_expert[None, :], axis=1)
    pick_tiles = lambda tab: jnp.sum(is_e[:, None, :] * tab[None, :, :], axis=2)
    block_j = bidx - pick(block_end - blocks_of)
    used = bidx < block_end[-1]
    cum = jnp.cumsum(counts, axis=0) - counts
    off = jnp.cumsum(counts, axis=1) - counts
    lo = block_j * MOE_BLOCK
    hi = jnp.minimum(lo + MOE_BLOCK, pick(total))
    run_start = pick_tiles(cum)
    run_end = run_start + pick_tiles(counts)
    tile_lo = jnp.sum((run_end <= lo[:, None]).astype(i32), axis=1)
    tile_hi = jnp.sum((run_start < hi[:, None]).astype(i32), axis=1) - 1
    valid = jnp.clip(hi - lo, 0, MOE_BLOCK)
    tile_lo = jnp.where(used, tile_lo, 0)
    tile_hi = jnp.where(used, tile_hi, -1)
    valid = jnp.where(used, valid, 0)
    flat = lambda a: a.reshape(-1).astype(i32)
    return base, (flat(block_e), flat(block_j), flat(tile_lo), flat(tile_hi), flat(valid),
                  flat(cum), flat(counts), flat(off), flat(block_end[-1:]))


def kernel(x, p, norm_mix, w_in, w_dw, b_dw, ln_conv_g, ln_conv_b, w_attn_out, w_conv_out,
           b_conv_out, w_out, norm_ffn, w_route_group, b_route_group, w_route_expert,
           b_route_expert, w_exp_gate, w_exp_up, w_exp_down, norm_ple, w_ple_proj, w_ple_gate,
           norm_final):
    bsz, seq, d = x.shape
    depth = w_in.shape[0]
    t = bsz * seq
    tm = 512
    sort_tm = 1024
    n_assign = t * TOP_K
    assert n_assign > MOE_BLOCK
    n_rows = (-(-n_assign // MOE_BLOCK)) * MOE_BLOCK + N_EXPERTS * MOE_BLOCK
    cos, sin = _rope_tables(seq)
    row2 = lambda a: a.reshape(1, -1)

    xf = x.reshape(t, d)
    for i in range(depth):
        n_qkv = 3 * ATTN_WIDTH
        q, k, v = _inproj(xf, row2(norm_mix[i]), w_in[i, :, :n_qkv].astype(BF16), cos, sin,
                          bsz, seq, tm)
        u, ga, gb = _gateproj(xf, row2(norm_mix[i]), w_in[i, :, n_qkv:].astype(BF16), tm)
        o = _attention(q, k, v).reshape(t, ATTN_WIDTH)
        c = _conv(u.reshape(bsz, seq, CONV_WIDTH), w_dw[i], row2(b_dw[i]), row2(ln_conv_g[i]),
                  row2(ln_conv_b[i]), 256).reshape(t, CONV_WIDTH)

        w_r = jnp.zeros((d, LANES), F32)
        w_r = w_r.at[:, :N_GROUPS].set(w_route_group[i])
        w_r = w_r.at[:, EXPERT_LANE0:EXPERT_LANE0 + N_EXPERTS].set(w_route_expert[i])
        w_r_hi = w_r.astype(BF16)
        w_r_lo = (w_r - w_r_hi.astype(F32)).astype(BF16)
        b_r = jnp.zeros((1, LANES), F32)
        b_r = b_r.at[0, :N_GROUPS].set(b_route_group[i])
        b_r = b_r.at[0, EXPERT_LANE0:EXPERT_LANE0 + N_EXPERTS].set(b_route_expert[i])

        x1, h2, route, gate, cnt = _merge(
            o, c, ga, gb, xf, w_attn_out[i].astype(BF16), w_conv_out[i].astype(BF16),
            row2(b_conv_out[i]), w_out[i].astype(BF16), row2(norm_ffn[i]), w_r_hi, w_r_lo, b_r, tm)

        counts = cnt[:, 0, :N_EXPERTS].astype(jnp.int32)
        base, plan = _dispatch_plan(counts, sort_tm // tm, n_rows // MOE_BLOCK)
        pos = _positions(route, base, tm).reshape(-1)
        xs = _tile_sort(pos, h2, sort_tm)
        ys = _moe(plan, xs, w_exp_gate, w_exp_up, w_exp_down, i, n_rows // MOE_BLOCK, TOP_K * sort_tm)
        xf = _ple(pos, gate.reshape(-1), ys, x1, p[i].reshape(t, PLE_DIM), row2(norm_ple[i]),
                  w_ple_gate[i].astype(BF16), w_ple_proj[i].astype(BF16), row2(norm_final),
                  sort_tm, final_norm=(i == depth - 1))
    return xf.reshape(bsz, seq, d)
```

```python
import functools

import jax
import jax.numpy as jnp
from jax import lax
from jax.experimental import pallas as pl
from jax.experimental.pallas import tpu as pltpu

D_MODEL = 1024
N_HEADS = 8
HEAD_DIM = 64
ATTN_WIDTH = N_HEADS * HEAD_DIM
CONV_WIDTH = 512
CONV_K = 31
DILATIONS = (16, 4, 1)
BLK = 128
ROPE_THETA = 10000.0
N_GROUPS = 4
EXPERTS_PER_GROUP = 8
N_EXPERTS = N_GROUPS * EXPERTS_PER_GROUP
EXPERT_FF = 512
TOP_K = 2
MOE_BLOCK = 256
PLE_DIM = 256
EPS = 1e-6
NEG_INF = -1e30

LANES = 128
SUBLANES = 8
ROW_TILE = D_MODEL // LANES
VMEM_LIMIT = 56 * 1024 * 1024

F32 = jnp.float32
BF16 = jnp.bfloat16


def _cparams(sem):
    return pltpu.CompilerParams(dimension_semantics=sem, vmem_limit_bytes=VMEM_LIMIT)


def _rms(x, g):
    return x * lax.rsqrt(jnp.mean(x * x, axis=-1, keepdims=True) + EPS) * g


def _sigmoid(x):
    return 0.5 * jnp.tanh(0.5 * x) + 0.5


def _store_row_tiles(ref, x):
    rows = x.shape[0]
    for s in range(ROW_TILE):
        ref[pl.ds(s, rows, stride=ROW_TILE), :] = x[:, s * LANES:(s + 1) * LANES]


def _load_row_tiles(ref, rows):
    return jnp.concatenate([ref[pl.ds(s, rows, stride=ROW_TILE), :] for s in range(ROW_TILE)], axis=1)


QKV_ROWS = 256


def _inproj_kernel(x_ref, g_ref, w_ref, cos_ref, sin_ref, *out_refs):
    n_lay = len(DILATIONS)
    q_refs, k_refs, v_refs = (out_refs[a * n_lay:(a + 1) * n_lay] for a in range(3))
    slab_refs = out_refs[3 * n_lay:]
    tm = x_ref.shape[0]
    n_slab = ATTN_WIDTH // LANES

    def put(ref, d, r, row0, g, val):
        cols = slice(g * LANES, (g + 1) * LANES)
        if d == 1:
            ref[pl.ds(row0, val.shape[0]), cols] = val.astype(BF16)
        else:
            ref[0, r, pl.ds(row0, val.shape[0]), cols] = val.astype(BF16)

    def fill(raw, rows, refs, slabs, is_query, post):
        for g in range(n_slab):
            cols = slice(g * LANES, (g + 1) * LANES)
            t = post(raw[:, cols], rows)
            slabs[g, rows, :] = t
            for d, ref in zip(DILATIONS, refs):
                if d == 1 and not is_query:
                    ref[rows, cols] = t.astype(BF16)

    def relayout(refs, slabs, by4, is_query):
        ref16, ref4, ref1 = refs
        quarter = tm // 4
        class4 = lambda r: pl.multiple_of(r * quarter, quarter)

        def to_by4(r, carry):
            for g in range(n_slab):
                val = slabs[g, pl.ds(r, quarter, stride=4), :]
                by4[g, pl.ds(class4(r), quarter), :] = val
                if not is_query:
                    put(ref4, 4, r, 0, g, val)
            return carry
        lax.fori_loop(0, 4, to_by4, 0, unroll=2)

        def to_16(r, carry):
            for g in range(n_slab):
                put(ref16, 16, r, 0, g, by4[g, pl.ds(class4(r % 4) + r // 4, tm // 16, stride=4), :])
            return carry
        lax.fori_loop(0, 16, to_16, 0, unroll=4)

        if is_query:
            def q4_class(r, carry):
                for bl in range(quarter // BLK):
                    for a in range(4):
                        src = pl.ds(class4(r) + bl * BLK + a, BLK // 4, stride=4)
                        for g in range(n_slab):
                            put(ref4, 4, r, bl * BLK + a * (BLK // 4), g, by4[g, src, :])
                return carry
            lax.fori_loop(0, 4, q4_class, 0, unroll=2)

            def q1_block(bl, carry):
                for a in range(16):
                    src = pl.ds((a % 4) * quarter + (BLK // 4) * bl + a // 4, BLK // 16, stride=4)
                    for g in range(n_slab):
                        put(ref1, 1, 0, pl.multiple_of(bl * BLK, BLK) + a * (BLK // 16), g, by4[g, src, :])
                return carry
            lax.fori_loop(0, tm // BLK, q1_block, 0, unroll=2)

    lane = lax.broadcasted_iota(jnp.int32, (QKV_ROWS, LANES), 1)
    first_half = (lane % HEAD_DIM) < (HEAD_DIM // 2)

    def rope(t, rows):
        partner = jnp.where(first_half,
                            pltpu.roll(t, LANES - HEAD_DIM // 2, 1),
                            pltpu.roll(t, HEAD_DIM // 2, 1))
        return t * cos_ref[rows, :] + partner * sin_ref[rows, :]

    for r0 in range(0, tm, QKV_ROWS):
        rows = slice(r0, r0 + QKV_ROWS)
        h = _rms(x_ref[rows, :], g_ref[...]).astype(BF16)
        proj = lambda a: jnp.dot(h, w_ref[:, a * ATTN_WIDTH:(a + 1) * ATTN_WIDTH],
                                 preferred_element_type=F32)
        fill(proj(0), rows, q_refs, slab_refs[0], True, lambda t, rw: rope(t, rw) * (HEAD_DIM ** -0.5))
        fill(proj(1), rows, k_refs, slab_refs[1], False, rope)
        fill(proj(2), rows, v_refs, slab_refs[2], False, lambda t, rw: t)
    assert DILATIONS == (16, 4, 1) and tm % (4 * BLK) == 0
    relayout(q_refs, slab_refs[0], slab_refs[3], True)
    relayout(k_refs, slab_refs[1], slab_refs[4], False)
    relayout(v_refs, slab_refs[2], slab_refs[5], False)


CONV_HALO = 32
CONV_ROWS = 32


def _gateproj_kernel(x_ref, g_ref, w_ref, cw_ref, cb_ref, lg_ref, lb_ref, c_ref, ga_ref, gb_ref,
                     ext_ref, halo_ref, *, tiles_per_seq):
    tm = x_ref.shape[0]
    h = _rms(x_ref[...], g_ref[...]).astype(BF16)

    def proj(c0, width):
        return jnp.dot(h, w_ref[:, c0:c0 + width], preferred_element_type=F32)

    u = proj(0, CONV_WIDTH) * _sigmoid(proj(CONV_WIDTH, CONV_WIDTH))
    first_of_sequence = pl.program_id(0) % tiles_per_seq == 0
    ext_ref[0, 0:CONV_HALO, :] = jnp.where(first_of_sequence, 0.0, halo_ref[...])
    ext_ref[0, CONV_HALO:CONV_HALO + tm, :] = u
    halo_ref[...] = u[tm - CONV_HALO:, :]
    c0 = 2 * CONV_WIDTH
    ga_ref[...] = proj(c0, D_MODEL).astype(ga_ref.dtype)
    gb_ref[...] = proj(c0 + D_MODEL, D_MODEL).astype(gb_ref.dtype)
    _conv_rows(ext_ref, cw_ref, cb_ref, lg_ref, lb_ref, c_ref, tm)


def _conv_rows(ext_ref, w_ref, b_ref, g_ref, beta_ref, out_ref, ts):
    first = CONV_HALO - (CONV_K - 1)
    n_shift = ext_ref.shape[0]
    rows = CONV_HALO + ts
    ext0 = ext_ref[0]
    for s in range(1, n_shift):
        ext_ref[s] = pltpu.roll(ext0, rows - s, 0)
    bias = b_ref[...]
    gamma = g_ref[...]
    beta = beta_ref[...]

    for r0 in range(0, ts, CONV_ROWS):
        acc = jnp.zeros((CONV_ROWS, CONV_WIDTH), F32) + bias
        for j in range(CONV_K):
            lo = first + j + r0
            s, lo = lo % n_shift, lo - lo % n_shift
            tap = jnp.concatenate([w_ref[j]] * (CONV_ROWS // SUBLANES), axis=0)
            acc = acc + tap * ext_ref[s, lo:lo + CONV_ROWS, :]
        mu = jnp.mean(acc, axis=-1, keepdims=True)
        cen = acc - mu
        var = jnp.mean(cen * cen, axis=-1, keepdims=True)
        y = cen * lax.rsqrt(var + EPS) * gamma + beta
        out_ref[r0:r0 + CONV_ROWS, :] = (y * _sigmoid(y)).astype(out_ref.dtype)


def _gateproj(x, gain, w_rest, w_dw, b_dw, ln_g, ln_b, seq, tm):
    t = x.shape[0]
    row = lambda i: (i, 0)
    const = lambda i: (0, 0)
    taps = jnp.broadcast_to(w_dw[:, None, :], (CONV_K, SUBLANES, CONV_WIDTH))
    return pl.pallas_call(
        functools.partial(_gateproj_kernel, tiles_per_seq=seq // tm),
        grid=(t // tm,),
        in_specs=[pl.BlockSpec((tm, D_MODEL), row), pl.BlockSpec((1, D_MODEL), const),
                  pl.BlockSpec(w_rest.shape, const),
                  pl.BlockSpec(taps.shape, lambda i: (0, 0, 0)),
                  pl.BlockSpec((1, CONV_WIDTH), const), pl.BlockSpec((1, CONV_WIDTH), const),
                  pl.BlockSpec((1, CONV_WIDTH), const)],
        out_specs=(pl.BlockSpec((tm, CONV_WIDTH), row), pl.BlockSpec((tm, D_MODEL), row),
                   pl.BlockSpec((tm, D_MODEL), row)),
        out_shape=(jax.ShapeDtypeStruct((t, CONV_WIDTH), BF16),
                   jax.ShapeDtypeStruct((t, D_MODEL), BF16),
                   jax.ShapeDtypeStruct((t, D_MODEL), BF16)),
        scratch_shapes=[pltpu.VMEM((SUBLANES, CONV_HALO + tm, CONV_WIDTH), F32),
                        pltpu.VMEM((CONV_HALO, CONV_WIDTH), F32)],
        compiler_params=_cparams(("arbitrary",)),
        name="gateproj",
    )(x, gain, w_rest, taps, b_dw, ln_g, ln_b)


def _inproj(x, gain, w_in, cos, sin, bsz, seq, tm):
    t = x.shape[0]
    n_pos = seq // tm
    row = lambda i: (i, 0)
    const = lambda i: (0, 0)
    qkv_shapes, qkv_specs = [], []
    for d in DILATIONS:
        if d == 1:
            qkv_shapes.append(jax.ShapeDtypeStruct((t, ATTN_WIDTH), BF16))
            qkv_specs.append(pl.BlockSpec((tm, ATTN_WIDTH), row))
        else:
            qkv_shapes.append(jax.ShapeDtypeStruct((bsz, d, seq // d, ATTN_WIDTH), BF16))
            qkv_specs.append(pl.BlockSpec((1, d, tm // d, ATTN_WIDTH),
                                          lambda i: (i // n_pos, 0, i % n_pos, 0)))
    out_shape = tuple(qkv_shapes * 3)
    out_specs = tuple(qkv_specs * 3)
    outs = pl.pallas_call(
        _inproj_kernel,
        grid=(t // tm,),
        in_specs=[
            pl.BlockSpec((tm, D_MODEL), row),
            pl.BlockSpec((1, D_MODEL), const),
            pl.BlockSpec(w_in.shape, const),
            pl.BlockSpec((tm, LANES), lambda i: (i % n_pos, 0)),
            pl.BlockSpec((tm, LANES), lambda i: (i % n_pos, 0)),
        ],
        out_specs=out_specs,
        out_shape=out_shape,
        scratch_shapes=[pltpu.VMEM((ATTN_WIDTH // LANES, tm, LANES), F32) for _ in range(6)],
        compiler_params=_cparams(("parallel",)),
        name="inproj",
    )(x, gain, w_in, cos, sin)
    n_lay = len(DILATIONS)
    as_rows = lambda a: a.reshape(bsz, seq, ATTN_WIDTH)
    return tuple([as_rows(a) for a in outs[j * n_lay:(j + 1) * n_lay]] for j in range(3))


STATE_DILATION = max(DILATIONS)


def _query_order(idx, groups):
    run = BLK // groups
    return groups * (idx % run) + idx // run


def _attn_kernel(*refs, seq):
    n_lay = len(DILATIONS)
    srcs = refs[:3 * n_lay]
    o_ref, buf, acc_s, m_s, l_s, sem = refs[3 * n_lay:]
    b = pl.program_id(0)
    n_b = pl.num_programs(0)
    n_blocks = seq // BLK
    pair_w = 2 * HEAD_DIM
    n_pair = ATTN_WIDTH // pair_w

    def load_sweep(lay, bi, slot):
        return [pltpu.make_async_copy(srcs[a * n_lay + lay].at[bi], buf.at[slot, a, pl.ds(BLK, seq)],
                                      sem.at[slot]) for a in range(3)]

    @pl.when(b == 0)
    def _():
        buf[:, :, 0:BLK, :] = jnp.zeros((2, 3, BLK, ATTN_WIDTH), BF16)
        for c in load_sweep(0, 0, 0):
            c.start()

    row = lax.broadcasted_iota(jnp.int32, (BLK, BLK), 0)
    col = lax.broadcasted_iota(jnp.int32, (BLK, BLK), 1)
    lane = lax.broadcasted_iota(jnp.int32, (BLK, pair_w), 1)
    low_head = lane < HEAD_DIM
    unpermute = (row == _query_order(col, STATE_DILATION // DILATIONS[-1])).astype(BF16)

    for lay, d in enumerate(DILATIONS):
        slot = (b * n_lay + lay) % 2
        for c in load_sweep(lay, b, slot):
            c.wait()
        if lay + 1 < n_lay:
            for c in load_sweep(lay + 1, b, 1 - slot):
                c.start()
        else:
            @pl.when(b + 1 < n_b)
            def _():
                for c in load_sweep(0, b + 1, 1 - slot):
                    c.start()

        nb = seq // (d * BLK)
        has_prev = nb > 1
        first, final = lay == 0, lay == n_lay - 1
        kw = 2 * BLK if has_prev else BLK

        groups = STATE_DILATION // d
        run = BLK // groups
        q_pos = _query_order(row, groups)

        def block(j, carry, d=d, nb=nb, has_prev=has_prev, first=first, final=final, kw=kw,
                  slot=slot, groups=groups, run=run, q_pos=q_pos):
            r, n = j // nb, j % nb
            q_row0 = pl.multiple_of(BLK + j * BLK, BLK)
            k_row0 = pl.multiple_of(j * BLK, BLK) if has_prev else q_row0
            runs = [pl.ds(pl.multiple_of((d * a + r) * BLK + run * n, run), run)
                    for a in range(groups)]
            load = lambda ref, g: jnp.concatenate([ref[g, s, :] for s in runs], axis=0)

            def store(ref, g, val):
                for a, s in enumerate(runs):
                    ref[g, s, :] = val[a * run:(a + 1) * run]

            cur_ok = col <= q_pos
            if has_prev:
                prev_ok = jnp.logical_and(col >= q_pos, n > 0)
                allowed = jnp.concatenate([prev_ok, cur_ok], axis=1)
            else:
                allowed = cur_ok
            allowed2 = jnp.concatenate([allowed, allowed], axis=0)
            lanes = [slice(g * pair_w, (g + 1) * pair_w) for g in range(n_pair)]
            scores = []
            for ls in lanes:
                q2 = buf[slot, 0, pl.ds(q_row0, BLK), ls]
                q_both = jnp.concatenate([jnp.where(low_head, q2, 0),
                                          jnp.where(low_head, 0, q2)], axis=0)
                k2 = buf[slot, 1, pl.ds(k_row0, kw), ls]
                scores.append(lax.dot_general(q_both, k2, (((1,), (1,)), ((), ())),
                                              preferred_element_type=F32))
            probs, maxes, sums = [], [], []
            for s in scores:
                s = jnp.where(allowed2, s, NEG_INF)
                m_h = jnp.max(s, axis=-1, keepdims=True)
                p = jnp.exp(s - m_h)
                sums.append(jnp.sum(p, axis=-1, keepdims=True))
                maxes.append(m_h)
                probs.append(p.astype(BF16))
            pv = [jnp.dot(p, buf[slot, 2, pl.ds(k_row0, kw), ls], preferred_element_type=F32)
                  for p, ls in zip(probs, lanes)]
            for g, ls in enumerate(lanes):
                acc = jnp.where(low_head, pv[g][:BLK], pv[g][BLK:])
                m_new = jnp.where(low_head, maxes[g][:BLK], maxes[g][BLK:])
                l_new = jnp.where(low_head, sums[g][:BLK], sums[g][BLK:])
                if not first:
                    m_old = load(m_s, g)
                    m_tot = jnp.maximum(m_old, m_new)
                    a_old = jnp.exp(m_old - m_tot)
                    a_new = jnp.exp(m_new - m_tot)
                    acc = load(acc_s, g) * a_old + acc * a_new
                    l_new = load(l_s, g) * a_old + l_new * a_new
                    m_new = m_tot
                if final:
                    o_blk = jnp.dot(unpermute, (acc / l_new).astype(BF16), preferred_element_type=F32)
                    o_ref[0, pl.ds(pl.multiple_of(j * BLK, BLK), BLK), ls] = o_blk.astype(o_ref.dtype)
                else:
                    store(acc_s, g, acc)
                    store(m_s, g, m_new)
                    store(l_s, g, l_new)
            return carry

        lax.fori_loop(0, n_blocks, block, 0, unroll=2)


def _attention(q, k, v):
    bsz, seq, _ = q[0].shape
    assert all(seq % (d * BLK) == 0 for d in DILATIONS) and DILATIONS[-1] == 1
    n_pair = ATTN_WIDTH // (2 * HEAD_DIM)
    return pl.pallas_call(
        functools.partial(_attn_kernel, seq=seq),
        grid=(bsz,),
        in_specs=[pl.BlockSpec(memory_space=pl.ANY)] * (3 * len(DILATIONS)),
        out_specs=pl.BlockSpec((1, seq, ATTN_WIDTH), lambda b: (b, 0, 0)),
        out_shape=jax.ShapeDtypeStruct((bsz, seq, ATTN_WIDTH), BF16),
        scratch_shapes=[pltpu.VMEM((2, 3, BLK + seq, ATTN_WIDTH), BF16),
                        pltpu.VMEM((n_pair, seq, 2 * HEAD_DIM), F32),
                        pltpu.VMEM((n_pair, seq, 2 * HEAD_DIM), F32),
                        pltpu.VMEM((n_pair, seq, 2 * HEAD_DIM), F32),
                        pltpu.SemaphoreType.DMA((2,))],
        compiler_params=_cparams(("arbitrary",)),
        name="attention",
    )(*q, *k, *v)


ROUTE_COLS = 4
EXPERT_LANE0 = N_GROUPS
MERGE_ROWS = 512


def _merge_kernel(o_ref, c_ref, ga_ref, gb_ref, x_ref, wa_ref, wc_ref, bc_ref, wo_ref,
                  gf_ref, wrh_ref, wrl_ref, br_ref,
                  x1_ref, h2_ref, route_ref, gate_ref, cnt_ref, tri_ref):
    tm = x_ref.shape[0]

    @pl.when(pl.program_id(0) == 0)
    def _():
        r = lax.broadcasted_iota(jnp.int32, (tm, tm), 0)
        c = lax.broadcasted_iota(jnp.int32, (tm, tm), 1)
        tri_ref[...] = (c < r).astype(BF16)

    logit_rows = []
    for r0 in range(0, tm, MERGE_ROWS):
        rows = slice(r0, r0 + MERGE_ROWS)
        y_a = jnp.dot(o_ref[rows, :], wa_ref[...], preferred_element_type=F32)
        y_b = jnp.dot(c_ref[rows, :], wc_ref[...], preferred_element_type=F32) + bc_ref[...]
        merged = (_sigmoid(ga_ref[rows, :].astype(F32)) * y_a
                  + _sigmoid(gb_ref[rows, :].astype(F32)) * y_b)
        x1 = x_ref[rows, :] + jnp.dot(merged.astype(BF16), wo_ref[...], preferred_element_type=F32)
        x1_ref[rows, :] = x1
        h2 = _rms(x1, gf_ref[...])
        _store_row_tiles(h2_ref.at[pl.ds(r0 * ROW_TILE, MERGE_ROWS * ROW_TILE), :], h2)
        h_hi = h2.astype(BF16)
        h_lo = (h2 - h_hi.astype(F32)).astype(BF16)
        logit_rows.append(jnp.dot(h_hi, wrh_ref[...], preferred_element_type=F32)
                          + jnp.dot(h_hi, wrl_ref[...], preferred_element_type=F32)
                          + jnp.dot(h_lo, wrh_ref[...], preferred_element_type=F32))
    logits = jnp.concatenate(logit_rows, axis=0) + br_ref[...]

    lane = lax.broadcasted_iota(jnp.int32, (tm, LANES), 1).astype(F32)
    no_lane = float(LANES)
    is_group = lane < N_GROUPS
    g_max = jnp.max(jnp.where(is_group, logits, -jnp.inf), axis=-1, keepdims=True)
    g_sel = jnp.min(jnp.where(jnp.logical_and(is_group, logits == g_max), lane, no_lane),
                    axis=-1, keepdims=True)
    p_group = 1.0 / jnp.sum(jnp.where(is_group, jnp.exp(logits - g_max), 0.0),
                            axis=-1, keepdims=True)
    lo = EXPERT_LANE0 + EXPERTS_PER_GROUP * g_sel
    in_sel = jnp.logical_and(lane >= lo, lane < lo + EXPERTS_PER_GROUP)
    cand = jnp.where(in_sel, logits, -jnp.inf)
    v1 = jnp.max(cand, axis=-1, keepdims=True)
    i1 = jnp.min(jnp.where(jnp.logical_and(in_sel, cand == v1), lane, no_lane), axis=-1, keepdims=True)
    rest = jnp.logical_and(in_sel, lane != i1)
    cand2 = jnp.where(rest, logits, -jnp.inf)
    v2 = jnp.max(cand2, axis=-1, keepdims=True)
    i2 = jnp.min(jnp.where(jnp.logical_and(rest, cand2 == v2), lane, no_lane), axis=-1, keepdims=True)
    e2 = jnp.exp(v2 - v1)
    gate0 = p_group / (1.0 + e2)
    gate1 = p_group * e2 / (1.0 + e2)

    oh0 = (lane == i1)
    oh1 = (lane == i2)
    before0 = jnp.dot(tri_ref[...], oh0.astype(BF16), preferred_element_type=F32)
    before1 = jnp.dot(tri_ref[...], oh1.astype(BF16), preferred_element_type=F32)
    tot0 = jnp.sum(oh0.astype(F32), axis=0, keepdims=True)
    tot1 = jnp.sum(oh1.astype(F32), axis=0, keepdims=True)
    cnt = tot0 + tot1
    rank0 = jnp.sum(jnp.where(oh0, before0, 0.0), axis=-1, keepdims=True)
    rank1 = jnp.sum(jnp.where(oh1, tot0 + before1, 0.0), axis=-1, keepdims=True)

    rc = lax.broadcasted_iota(jnp.int32, (tm, ROUTE_COLS), 1)
    route = jnp.where(rc == 0, i1 - EXPERT_LANE0,
                      jnp.where(rc == 1, i2 - EXPERT_LANE0, jnp.where(rc == 2, rank0, rank1)))
    route_ref[...] = route.astype(jnp.int32)
    gc = lax.broadcasted_iota(jnp.int32, (tm, TOP_K), 1)
    gate_ref[...] = jnp.where(gc == 0, gate0, gate1)
    cnt_ref[0] = jnp.broadcast_to(pltpu.roll(cnt, LANES - EXPERT_LANE0, 1), (8, LANES))


def _merge(o, c, ga, gb, x, wa, wc, bc, wo, gf, wr_hi, wr_lo, br, tm):
    t = x.shape[0]
    nt = t // tm
    row = lambda i: (i, 0)
    const = lambda i: (0, 0)
    full = lambda a: pl.BlockSpec(a.shape, const)
    return pl.pallas_call(
        _merge_kernel,
        grid=(nt,),
        in_specs=[
            pl.BlockSpec((tm, ATTN_WIDTH), row),
            pl.BlockSpec((tm, CONV_WIDTH), row),
            pl.BlockSpec((tm, D_MODEL), row),
            pl.BlockSpec((tm, D_MODEL), row),
            pl.BlockSpec((tm, D_MODEL), row),
            full(wa), full(wc), full(bc), full(wo), full(gf), full(wr_hi), full(wr_lo), full(br),
        ],
        out_specs=(
            pl.BlockSpec((tm, D_MODEL), row),
            pl.BlockSpec((tm * ROW_TILE, LANES), row),
            pl.BlockSpec((tm, ROUTE_COLS), row),
            pl.BlockSpec((tm, TOP_K), row),
            pl.BlockSpec((1, 8, LANES), lambda i: (i, 0, 0)),
        ),
        out_shape=(
            jax.ShapeDtypeStruct((t, D_MODEL), F32),
            jax.ShapeDtypeStruct((t * ROW_TILE, LANES), F32),
            jax.ShapeDtypeStruct((t, ROUTE_COLS), jnp.int32),
            jax.ShapeDtypeStruct((t, TOP_K), F32),
            jax.ShapeDtypeStruct((nt, 8, LANES), F32),
        ),
        scratch_shapes=[pltpu.VMEM((tm, tm), BF16)],
        compiler_params=_cparams(("arbitrary",)),
        name="merge",
    )(o, c, ga, gb, x, wa, wc, bc, wo, gf, wr_hi, wr_lo, br)


def _pos_kernel(route_ref, base_ref, pos_ref):
    tm = route_ref.shape[0]
    route = route_ref[...].astype(F32)
    lane = lax.broadcasted_iota(jnp.int32, (tm, LANES), 1).astype(F32)
    cols = []
    for k in range(TOP_K):
        start = jnp.sum(jnp.where(lane == route[:, k:k + 1], base_ref[0, 0:1, :], 0.0),
                        axis=-1, keepdims=True)
        cols.append(start + route[:, TOP_K + k:TOP_K + k + 1])
    sel = lax.broadcasted_iota(jnp.int32, (tm, TOP_K), 1)
    pos_ref[...] = jnp.where(sel == 0, cols[0], cols[1]).astype(jnp.int32)


def _positions(route, base, tm):
    nt = base.shape[0]
    base_rows = jnp.zeros((nt, SUBLANES, LANES), F32).at[:, :, :N_EXPERTS].set(
        base.astype(F32)[:, None, :])
    return pl.pallas_call(
        _pos_kernel,
        grid=(nt,),
        in_specs=[pl.BlockSpec((tm, ROUTE_COLS), lambda i: (i, 0)),
                  pl.BlockSpec((1, SUBLANES, LANES), lambda i: (i, 0, 0))],
        out_specs=pl.BlockSpec((tm, TOP_K), lambda i: (i, 0)),
        out_shape=jax.ShapeDtypeStruct((route.shape[0], TOP_K), jnp.int32),
        compiler_params=_cparams(("parallel",)),
        name="positions",
    )(route, base_rows)


def _sort_kernel(pos_ref, h_ref, xs_ref):
    tm = h_ref.shape[0] // ROW_TILE

    def place(t, carry):
        row = h_ref[pl.ds(pl.multiple_of(t * ROW_TILE, ROW_TILE), ROW_TILE), :]
        for k in range(TOP_K):
            pos = pos_ref[TOP_K * t + k]
            xs_ref[pl.ds(pl.multiple_of(pos * ROW_TILE, ROW_TILE), ROW_TILE), :] = row
        return carry

    lax.fori_loop(0, tm, place, 0, unroll=8)


def _tile_sort(pos_flat, h2, tm):
    nt = h2.shape[0] // (tm * ROW_TILE)
    return pl.pallas_call(
        _sort_kernel,
        grid=(nt,),
        in_specs=[pl.BlockSpec((TOP_K * tm,), lambda i: (i,), memory_space=pltpu.SMEM),
                  pl.BlockSpec((tm * ROW_TILE, LANES), lambda i: (i, 0))],
        out_specs=pl.BlockSpec((TOP_K * tm * ROW_TILE, LANES), lambda i: (i, 0)),
        out_shape=jax.ShapeDtypeStruct((TOP_K * h2.shape[0], LANES), F32),
        compiler_params=_cparams(("parallel",)),
        name="tile_sort",
    )(pos_flat, h2)


RARE_RUN = 64


def _moe_kernel(be_ref, bj_ref, ilo_ref, ihi_ref, valid_ref, cum_ref, cnt_ref, off_ref, nused_ref,
                xs_hbm, wg_ref, wu_ref, wd_ref, ys_hbm, trash_hbm,
                xbuf, obuf, wg_s, wu_s, wd_s, sem_in, sem_out, *, tile_rows):
    b = pl.program_id(0)
    slot = b % 2
    other = 1 - slot
    rows8 = lambda r, n: pl.ds(pl.multiple_of(r * ROW_TILE, ROW_TILE), n * ROW_TILE)

    def for_runs(bb, fn):
        e = be_ref[bb]
        lo_blk = bj_ref[bb] * MOE_BLOCK

        def body(i, carry):
            c = cum_ref[i * N_EXPERTS + e]
            lo = jnp.maximum(c, lo_blk)
            hi = jnp.minimum(c + cnt_ref[i * N_EXPERTS + e], lo_blk + MOE_BLOCK)
            fn(i * tile_rows + off_ref[i * N_EXPERTS + e] + (lo - c), lo - lo_blk,
               jnp.maximum(hi - lo, 0))
            return carry

        lax.fori_loop(ilo_ref[bb], ihi_ref[bb] + 1, body, 0)

    def for_chunks(length, fn):
        def piece(size):
            @pl.when((length & size) != 0)
            def _():
                fn(length & ~(2 * size - 1), size)

        sizes = [1 << bit for bit in reversed(range(MOE_BLOCK.bit_length()))]

        @pl.when(length >= RARE_RUN)
        def _():
            for size in sizes:
                if size >= RARE_RUN:
                    piece(size)

        for size in sizes:
            if size < RARE_RUN:
                piece(size)

    def gather(bb, s):
        def run(src, dst, length):
            for_chunks(length, lambda o, n: pltpu.make_async_copy(
                xs_hbm.at[rows8(src + o, n)], xbuf.at[s, rows8(dst + o, n)], sem_in.at[s]).start())
        for_runs(bb, run)
        v = valid_ref[bb]
        for_chunks(MOE_BLOCK - v, lambda o, n: pltpu.make_async_copy(
            xs_hbm.at[rows8(o, n)], xbuf.at[s, rows8(v + o, n)], sem_in.at[s]).start())

    def scatter(bb, s):
        def run(src, dst, length):
            for_chunks(length, lambda o, n: pltpu.make_async_copy(
                obuf.at[s, rows8(dst + o, n)], ys_hbm.at[rows8(src + o, n)], sem_out.at[s]).start())
        for_runs(bb, run)
        v = valid_ref[bb]
        for_chunks(MOE_BLOCK - v, lambda o, n: pltpu.make_async_copy(
            obuf.at[s, rows8(v + o, n)], trash_hbm.at[rows8(s * MOE_BLOCK + v + o, n)],
            sem_out.at[s]).start())

    def wait_gather(s):
        pltpu.make_async_copy(xs_hbm.at[rows8(0, MOE_BLOCK)], xbuf.at[s], sem_in.at[s]).wait()

    def wait_scatter(s):
        pltpu.make_async_copy(obuf.at[s], ys_hbm.at[rows8(0, MOE_BLOCK)], sem_out.at[s]).wait()

    @pl.when(b == 0)
    def _():
        obuf[...] = jnp.zeros_like(obuf)
        for s in range(2):
            init = pltpu.make_async_copy(obuf.at[s], trash_hbm.at[rows8(s * MOE_BLOCK, MOE_BLOCK)],
                                         sem_out.at[s])
            init.start()
            init.wait()
        gather(0, 0)

    n_used = nused_ref[0]
    live = b < n_used

    @pl.when(live)
    def _():
        wait_gather(slot)

    @pl.when(jnp.logical_and(live, b >= 2))
    def _():
        wait_scatter(slot)

    e = be_ref[b]
    e_prev = be_ref[jnp.maximum(b - 1, 0)]

    @pl.when(jnp.logical_and(live, jnp.logical_or(b == 0, e != e_prev)))
    def _():
        wg_s[...] = wg_ref[0].astype(BF16)
        wu_s[...] = wu_ref[0].astype(BF16)
        wd_s[...] = wd_ref[0].astype(BF16)

    @pl.when(b + 1 < n_used)
    def _():
        gather(b + 1, other)

    @pl.when(jnp.logical_and(live, b >= 1))
    def _():
        scatter(b - 1, other)

    @pl.when(live)
    def _():
        x = _load_row_tiles(xbuf.at[slot], MOE_BLOCK).astype(BF16)
        hg = jnp.dot(x, wg_s[...], preferred_element_type=F32)
        hu = jnp.dot(x, wu_s[...], preferred_element_type=F32)
        hb = (hg * _sigmoid(hg) * hu).astype(BF16)
        _store_row_tiles(obuf.at[slot], jnp.dot(hb, wd_s[...], preferred_element_type=F32))

    @pl.when(b == n_used - 1)
    def _():
        scatter(b, slot)
        wait_scatter(other)
        wait_scatter(slot)


def _moe(plan, xs, w_gate, w_up, w_down, layer, n_blocks, tile_rows):
    n_pre = len(plan)
    w_idx = lambda b, be, *_: (layer, be[b], 0, 0)
    grid_spec = pltpu.PrefetchScalarGridSpec(
        num_scalar_prefetch=n_pre,
        grid=(n_blocks,),
        in_specs=[
            pl.BlockSpec(memory_space=pl.ANY),
            pl.BlockSpec((None, 1, D_MODEL, EXPERT_FF), w_idx),
            pl.BlockSpec((None, 1, D_MODEL, EXPERT_FF), w_idx),
            pl.BlockSpec((None, 1, EXPERT_FF, D_MODEL), w_idx),
        ],
        out_specs=(pl.BlockSpec(memory_space=pl.ANY), pl.BlockSpec(memory_space=pl.ANY)),
        scratch_shapes=[pltpu.VMEM((2, MOE_BLOCK * ROW_TILE, LANES), F32),
                        pltpu.VMEM((2, MOE_BLOCK * ROW_TILE, LANES), F32),
                        pltpu.VMEM((D_MODEL, EXPERT_FF), BF16),
                        pltpu.VMEM((D_MODEL, EXPERT_FF), BF16),
                        pltpu.VMEM((EXPERT_FF, D_MODEL), BF16),
                        pltpu.SemaphoreType.DMA((2,)),
                        pltpu.SemaphoreType.DMA((2,))],
    )
    ys, _ = pl.pallas_call(
        functools.partial(_moe_kernel, tile_rows=tile_rows),
        grid_spec=grid_spec,
        out_shape=(jax.ShapeDtypeStruct(xs.shape, F32),
                   jax.ShapeDtypeStruct((2 * MOE_BLOCK * ROW_TILE, LANES), F32)),
        compiler_params=_cparams(("arbitrary",)),
        name="moe",
    )(*plan, xs, w_gate, w_up, w_down)
    return ys


def _ple_kernel(pos_ref, gate_ref, ys_ref, x1_ref, p_ref, gp_ref, wpg_ref, wpe_ref, gfin_ref,
                out_ref, moe_ref, *, final_norm):
    tm = x1_ref.shape[0]
    tile = lambda r: pl.ds(pl.multiple_of(r * ROW_TILE, ROW_TILE), ROW_TILE)

    def combine(t, carry):
        acc = None
        for k in range(TOP_K):
            term = gate_ref[TOP_K * t + k] * ys_ref[tile(pos_ref[TOP_K * t + k]), :]
            acc = term if acc is None else acc + term
        moe_ref[tile(t), :] = acc
        return carry

    lax.fori_loop(0, tm, combine, 0, unroll=8)
    x2 = x1_ref[...] + _load_row_tiles(moe_ref, tm)
    hp = _rms(x2, gp_ref[...]).astype(BF16)
    g_ple = _sigmoid(jnp.dot(hp, wpg_ref[...], preferred_element_type=F32))
    emb = jnp.dot(p_ref[...].astype(BF16), wpe_ref[...], preferred_element_type=F32)
    x3 = x2 + g_ple * emb
    if final_norm:
        x3 = _rms(x3, gfin_ref[...])
    out_ref[...] = x3


def _ple(pos, gate_flat, ys, x1, p, gp, wpg, wpe, gfin, tm, final_norm):
    t = x1.shape[0]
    row = lambda i: (i, 0)
    const = lambda i: (0, 0)
    full = lambda a: pl.BlockSpec(a.shape, const)
    smem = lambda n: pl.BlockSpec((n,), lambda i: (i,), memory_space=pltpu.SMEM)
    return pl.pallas_call(
        functools.partial(_ple_kernel, final_norm=final_norm),
        grid=(t // tm,),
        in_specs=[
            smem(TOP_K * tm),
            smem(TOP_K * tm),
            pl.BlockSpec((TOP_K * tm * ROW_TILE, LANES), row),
            pl.BlockSpec((tm, D_MODEL), row),
            pl.BlockSpec((tm, PLE_DIM), row),
            full(gp), full(wpg), full(wpe), full(gfin),
        ],
        out_specs=pl.BlockSpec((tm, D_MODEL), row),
        out_shape=jax.ShapeDtypeStruct((t, D_MODEL), F32),
        scratch_shapes=[pltpu.VMEM((tm * ROW_TILE, LANES), F32)],
        compiler_params=_cparams(("parallel",)),
        name="ple",
    )(pos, gate_flat, ys, x1, p, gp, wpg, wpe, gfin)


def _rope_tables(seq):
    inv = jnp.power(ROPE_THETA, -jnp.arange(0, HEAD_DIM, 2, dtype=F32) / HEAD_DIM)
    ang = jnp.arange(seq, dtype=F32)[:, None] * inv[None, :]
    cos, sin = jnp.cos(ang), jnp.sin(ang)
    cos_h = jnp.concatenate([cos, cos], axis=-1)
    sin_h = jnp.concatenate([-sin, sin], axis=-1)
    reps = LANES // HEAD_DIM
    return jnp.tile(cos_h, (1, reps)), jnp.tile(sin_h, (1, reps))


def _dispatch_plan(route_counts, group, n_blocks):
    i32 = jnp.int32
    grouped = route_counts.reshape(-1, group, N_EXPERTS)
    counts = jnp.sum(grouped, axis=1)
    base = ((jnp.cumsum(counts, axis=1) - counts)[:, None, :]
            + jnp.cumsum(grouped, axis=1) - grouped).reshape(-1, N_EXPERTS).astype(i32)
    total = jnp.sum(counts, axis=0)
    blocks_of = (total + MOE_BLOCK - 1) // MOE_BLOCK
    block_end = jnp.cumsum(blocks_of)
    bidx = jnp.arange(n_blocks, dtype=i32)
    block_e = jnp.minimum(jnp.sum((block_end[None, :] <= bidx[:, None]).astype(i32), axis=1),
                          N_EXPERTS - 1)
    is_e = (block_e[:, None] == jnp.arange(N_EXPERTS, dtype=i32)[None, :]).astype(i32)
    pick = lambda per_expert: jnp.sum(is_e * per_expert[None, :], axis=1)
    pick_tiles = lambda tab: jnp.sum(is_e[:, None, :] * tab[None, :, :], axis=2)
    block_j = bidx - pick(block_end - blocks_of)
    used = bidx < block_end[-1]
    cum = jnp.cumsum(counts, axis=0) - counts
    off = jnp.cumsum(counts, axis=1) - counts
    lo = block_j * MOE_BLOCK
    hi = jnp.minimum(lo + MOE_BLOCK, pick(total))
    run_start = pick_tiles(cum)
    run_end = run_start + pick_tiles(counts)
    tile_lo = jnp.sum((run_end <= lo[:, None]).astype(i32), axis=1)
    tile_hi = jnp.sum((run_start < hi[:, None]).astype(i32), axis=1) - 1
    valid = jnp.clip(hi - lo, 0, MOE_BLOCK)
    tile_lo = jnp.where(used, tile_lo, 0)
    tile_hi = jnp.where(used, tile_hi, -1)
    valid = jnp.where(used, valid, 0)
    flat = lambda a: a.reshape(-1).astype(i32)
    return base, (flat(block_e), flat(block_j), flat(tile_lo), flat(tile_hi), flat(valid),
                  flat(cum), flat(counts), flat(off), flat(block_end[-1:]))


def kernel(x, p, norm_mix, w_in, w_dw, b_dw, ln_conv_g, ln_conv_b, w_attn_out, w_conv_out,
           b_conv_out, w_out, norm_ffn, w_route_group, b_route_group, w_route_expert,
           b_route_expert, w_exp_gate, w_exp_up, w_exp_down, norm_ple, w_ple_proj, w_ple_gate,
           norm_final):
    bsz, seq, d = x.shape
    depth = w_in.shape[0]
    t = bsz * seq
    tm = 512
    sort_tm = 1024
    n_assign = t * TOP_K
    assert n_assign > MOE_BLOCK
    n_rows = (-(-n_assign // MOE_BLOCK)) * MOE_BLOCK + N_EXPERTS * MOE_BLOCK
    cos, sin = _rope_tables(seq)
    row2 = lambda a: a.reshape(1, -1)

    xf = x.reshape(t, d)
    for i in range(depth):
        n_qkv = 3 * ATTN_WIDTH
        q, k, v = _inproj(xf, row2(norm_mix[i]), w_in[i, :, :n_qkv].astype(BF16), cos, sin,
                          bsz, seq, tm)
        c, ga, gb = _gateproj(xf, row2(norm_mix[i]), w_in[i, :, n_qkv:].astype(BF16), w_dw[i],
                              row2(b_dw[i]), row2(ln_conv_g[i]), row2(ln_conv_b[i]), seq, tm)
        o = _attention(q, k, v).reshape(t, ATTN_WIDTH)

        w_r = jnp.zeros((d, LANES), F32)
        w_r = w_r.at[:, :N_GROUPS].set(w_route_group[i])
        w_r = w_r.at[:, EXPERT_LANE0:EXPERT_LANE0 + N_EXPERTS].set(w_route_expert[i])
        w_r_hi = w_r.astype(BF16)
        w_r_lo = (w_r - w_r_hi.astype(F32)).astype(BF16)
        b_r = jnp.zeros((1, LANES), F32)
        b_r = b_r.at[0, :N_GROUPS].set(b_route_group[i])
        b_r = b_r.at[0, EXPERT_LANE0:EXPERT_LANE0 + N_EXPERTS].set(b_route_expert[i])

        x1, h2, route, gate, cnt = _merge(
            o, c, ga, gb, xf, w_attn_out[i].astype(BF16), w_conv_out[i].astype(BF16),
            row2(b_conv_out[i]), w_out[i].astype(BF16), row2(norm_ffn[i]), w_r_hi, w_r_lo, b_r, tm)

        counts = cnt[:, 0, :N_EXPERTS].astype(jnp.int32)
        base, plan = _dispatch_plan(counts, sort_tm // tm, n_rows // MOE_BLOCK)
        pos = _positions(route, base, tm).reshape(-1)
        xs = _tile_sort(pos, h2, sort_tm)
        ys = _moe(plan, xs, w_exp_gate, w_exp_up, w_exp_down, i, n_rows // MOE_BLOCK, TOP_K * sort_tm)
        xf = _ple(pos, gate.reshape(-1), ys, x1, p[i].reshape(t, PLE_DIM), row2(norm_ple[i]),
                  w_ple_gate[i].astype(BF16), w_ple_proj[i].astype(BF16), row2(norm_final),
                  sort_tm, final_norm=(i == depth - 1))
    return xf.reshape(bsz, seq, d)
```

```python
import functools

import jax
import jax.numpy as jnp
from jax import lax
from jax.experimental import pallas as pl
from jax.experimental.pallas import tpu as pltpu

D_MODEL = 1024
N_HEADS = 8
HEAD_DIM = 64
ATTN_WIDTH = N_HEADS * HEAD_DIM
CONV_WIDTH = 512
CONV_K = 31
DILATIONS = (16, 4, 1)
BLK = 128
ROPE_THETA = 10000.0
N_GROUPS = 4
EXPERTS_PER_GROUP = 8
N_EXPERTS = N_GROUPS * EXPERTS_PER_GROUP
EXPERT_FF = 512
TOP_K = 2
MOE_BLOCK = 256
PLE_DIM = 256
EPS = 1e-6
NEG_INF = -1e30

LANES = 128
SUBLANES = 8
ROW_TILE = D_MODEL // LANES
VMEM_LIMIT = 56 * 1024 * 1024

F32 = jnp.float32
BF16 = jnp.bfloat16


def _cparams(sem):
    return pltpu.CompilerParams(dimension_semantics=sem, vmem_limit_bytes=VMEM_LIMIT)


def _rms(x, g):
    return x * lax.rsqrt(jnp.mean(x * x, axis=-1, keepdims=True) + EPS) * g


def _sigmoid(x):
    return 0.5 * jnp.tanh(0.5 * x) + 0.5


def _store_row_tiles(ref, x):
    rows = x.shape[0]
    for s in range(ROW_TILE):
        ref[pl.ds(s, rows, stride=ROW_TILE), :] = x[:, s * LANES:(s + 1) * LANES]


def _load_row_tiles(ref, rows):
    return jnp.concatenate([ref[pl.ds(s, rows, stride=ROW_TILE), :] for s in range(ROW_TILE)], axis=1)


QKV_ROWS = 256


def _inproj_kernel(x_ref, g_ref, w_ref, cos_ref, sin_ref, *out_refs):
    n_lay = len(DILATIONS)
    q_refs, k_refs, v_refs = (out_refs[a * n_lay:(a + 1) * n_lay] for a in range(3))
    slab_refs = out_refs[3 * n_lay:]
    tm = x_ref.shape[0]
    n_slab = ATTN_WIDTH // LANES

    def put(ref, d, r, row0, g, val):
        cols = slice(g * LANES, (g + 1) * LANES)
        if d == 1:
            ref[pl.ds(row0, val.shape[0]), cols] = val.astype(BF16)
        else:
            ref[0, r, pl.ds(row0, val.shape[0]), cols] = val.astype(BF16)

    def fill(raw, rows, refs, slabs, is_query, post):
        for g in range(n_slab):
            cols = slice(g * LANES, (g + 1) * LANES)
            t = post(raw[:, cols], rows)
            slabs[g, rows, :] = t
            for d, ref in zip(DILATIONS, refs):
                if d == 1 and not is_query:
                    ref[rows, cols] = t.astype(BF16)

    def relayout(refs, slabs, by4, is_query):
        ref16, ref4, ref1 = refs
        quarter = tm // 4
        class4 = lambda r: pl.multiple_of(r * quarter, quarter)

        def to_by4(r, carry):
            for g in range(n_slab):
                val = slabs[g, pl.ds(r, quarter, stride=4), :]
                by4[g, pl.ds(class4(r), quarter), :] = val
                if not is_query:
                    put(ref4, 4, r, 0, g, val)
            return carry
        lax.fori_loop(0, 4, to_by4, 0, unroll=2)

        def to_16(r, carry):
            for g in range(n_slab):
                put(ref16, 16, r, 0, g, by4[g, pl.ds(class4(r % 4) + r // 4, tm // 16, stride=4), :])
            return carry
        lax.fori_loop(0, 16, to_16, 0, unroll=4)

        if is_query:
            def q4_class(r, carry):
                for bl in range(quarter // BLK):
                    for a in range(4):
                        src = pl.ds(class4(r) + bl * BLK + a, BLK // 4, stride=4)
                        for g in range(n_slab):
                            put(ref4, 4, r, bl * BLK + a * (BLK // 4), g, by4[g, src, :])
                return carry
            lax.fori_loop(0, 4, q4_class, 0, unroll=2)

            def q1_block(bl, carry):
                for a in range(16):
                    src = pl.ds((a % 4) * quarter + (BLK // 4) * bl + a // 4, BLK // 16, stride=4)
                    for g in range(n_slab):
                        put(ref1, 1, 0, pl.multiple_of(bl * BLK, BLK) + a * (BLK // 16), g, by4[g, src, :])
                return carry
            lax.fori_loop(0, tm // BLK, q1_block, 0, unroll=2)

    lane = lax.broadcasted_iota(jnp.int32, (QKV_ROWS, LANES), 1)
    first_half = (lane % HEAD_DIM) < (HEAD_DIM // 2)

    def rope(t, rows):
        partner = jnp.where(first_half,
                            pltpu.roll(t, LANES - HEAD_DIM // 2, 1),
                            pltpu.roll(t, HEAD_DIM // 2, 1))
        return t * cos_ref[rows, :] + partner * sin_ref[rows, :]

    for r0 in range(0, tm, QKV_ROWS):
        rows = slice(r0, r0 + QKV_ROWS)
        h = _rms(x_ref[rows, :], g_ref[...]).astype(BF16)
        proj = lambda a: jnp.dot(h, w_ref[:, a * ATTN_WIDTH:(a + 1) * ATTN_WIDTH],
                                 preferred_element_type=F32)
        fill(proj(0), rows, q_refs, slab_refs[0], True, lambda t, rw: rope(t, rw) * (HEAD_DIM ** -0.5))
        fill(proj(1), rows, k_refs, slab_refs[1], False, rope)
        fill(proj(2), rows, v_refs, slab_refs[2], False, lambda t, rw: t)
    assert DILATIONS == (16, 4, 1) and tm % (4 * BLK) == 0
    relayout(q_refs, slab_refs[0], slab_refs[3], True)
    relayout(k_refs, slab_refs[1], slab_refs[4], False)
    relayout(v_refs, slab_refs[2], slab_refs[5], False)


CONV_HALO = 32
CONV_ROWS = 32


def _gateproj_kernel(x_ref, g_ref, w_ref, cw_ref, cb_ref, lg_ref, lb_ref, c_ref, ga_ref, gb_ref,
                     ext_ref, halo_ref, *, tiles_per_seq):
    tm = x_ref.shape[0]
    h = _rms(x_ref[...], g_ref[...]).astype(BF16)

    def proj(c0, width):
        return jnp.dot(h, w_ref[:, c0:c0 + width], preferred_element_type=F32)

    u = proj(0, CONV_WIDTH) * _sigmoid(proj(CONV_WIDTH, CONV_WIDTH))
    first_of_sequence = pl.program_id(0) % tiles_per_seq == 0
    ext_ref[0, 0:CONV_HALO, :] = jnp.where(first_of_sequence, 0.0, halo_ref[...])
    ext_ref[0, CONV_HALO:CONV_HALO + tm, :] = u
    halo_ref[...] = u[tm - CONV_HALO:, :]
    c0 = 2 * CONV_WIDTH
    ga_ref[...] = proj(c0, D_MODEL).astype(ga_ref.dtype)
    gb_ref[...] = proj(c0 + D_MODEL, D_MODEL).astype(gb_ref.dtype)
    _conv_rows(ext_ref, cw_ref, cb_ref, lg_ref, lb_ref, c_ref, tm)


def _conv_rows(ext_ref, w_ref, b_ref, g_ref, beta_ref, out_ref, ts):
    first = CONV_HALO - (CONV_K - 1)
    n_shift = ext_ref.shape[0]
    rows = CONV_HALO + ts
    ext0 = ext_ref[0]
    for s in range(1, n_shift):
        ext_ref[s] = pltpu.roll(ext0, rows - s, 0)
    bias = b_ref[...]
    gamma = g_ref[...]
    beta = beta_ref[...]

    for r0 in range(0, ts, CONV_ROWS):
        acc = jnp.zeros((CONV_ROWS, CONV_WIDTH), F32) + bias
        for j in range(CONV_K):
            lo = first + j + r0
            s, lo = lo % n_shift, lo - lo % n_shift
            tap = jnp.concatenate([w_ref[j]] * (CONV_ROWS // SUBLANES), axis=0)
            acc = acc + tap * ext_ref[s, lo:lo + CONV_ROWS, :]
        mu = jnp.mean(acc, axis=-1, keepdims=True)
        cen = acc - mu
        var = jnp.mean(cen * cen, axis=-1, keepdims=True)
        y = cen * lax.rsqrt(var + EPS) * gamma + beta
        out_ref[r0:r0 + CONV_ROWS, :] = (y * _sigmoid(y)).astype(out_ref.dtype)


def _gateproj(x, gain, w_rest, w_dw, b_dw, ln_g, ln_b, seq, tm):
    t = x.shape[0]
    row = lambda i: (i, 0)
    const = lambda i: (0, 0)
    taps = jnp.broadcast_to(w_dw[:, None, :], (CONV_K, SUBLANES, CONV_WIDTH))
    return pl.pallas_call(
        functools.partial(_gateproj_kernel, tiles_per_seq=seq // tm),
        grid=(t // tm,),
        in_specs=[pl.BlockSpec((tm, D_MODEL), row), pl.BlockSpec((1, D_MODEL), const),
                  pl.BlockSpec(w_rest.shape, const),
                  pl.BlockSpec(taps.shape, lambda i: (0, 0, 0)),
                  pl.BlockSpec((1, CONV_WIDTH), const), pl.BlockSpec((1, CONV_WIDTH), const),
                  pl.BlockSpec((1, CONV_WIDTH), const)],
        out_specs=(pl.BlockSpec((tm, CONV_WIDTH), row), pl.BlockSpec((tm, D_MODEL), row),
                   pl.BlockSpec((tm, D_MODEL), row)),
        out_shape=(jax.ShapeDtypeStruct((t, CONV_WIDTH), BF16),
                   jax.ShapeDtypeStruct((t, D_MODEL), BF16),
                   jax.ShapeDtypeStruct((t, D_MODEL), BF16)),
        scratch_shapes=[pltpu.VMEM((SUBLANES, CONV_HALO + tm, CONV_WIDTH), F32),
                        pltpu.VMEM((CONV_HALO, CONV_WIDTH), F32)],
        compiler_params=_cparams(("arbitrary",)),
        name="gateproj",
    )(x, gain, w_rest, taps, b_dw, ln_g, ln_b)


def _inproj(x, gain, w_in, cos, sin, bsz, seq, tm):
    t = x.shape[0]
    n_pos = seq // tm
    row = lambda i: (i, 0)
    const = lambda i: (0, 0)
    qkv_shapes, qkv_specs = [], []
    for d in DILATIONS:
        if d == 1:
            qkv_shapes.append(jax.ShapeDtypeStruct((t, ATTN_WIDTH), BF16))
            qkv_specs.append(pl.BlockSpec((tm, ATTN_WIDTH), row))
        else:
            qkv_shapes.append(jax.ShapeDtypeStruct((bsz, d, seq // d, ATTN_WIDTH), BF16))
            qkv_specs.append(pl.BlockSpec((1, d, tm // d, ATTN_WIDTH),
                                          lambda i: (i // n_pos, 0, i % n_pos, 0)))
    out_shape = tuple(qkv_shapes * 3)
    out_specs = tuple(qkv_specs * 3)
    outs = pl.pallas_call(
        _inproj_kernel,
        grid=(t // tm,),
        in_specs=[
            pl.BlockSpec((tm, D_MODEL), row),
            pl.BlockSpec((1, D_MODEL), const),
            pl.BlockSpec(w_in.shape, const),
            pl.BlockSpec((tm, LANES), lambda i: (i % n_pos, 0)),
            pl.BlockSpec((tm, LANES), lambda i: (i % n_pos, 0)),
        ],
        out_specs=out_specs,
        out_shape=out_shape,
        scratch_shapes=[pltpu.VMEM((ATTN_WIDTH // LANES, tm, LANES), F32) for _ in range(6)],
        compiler_params=_cparams(("parallel",)),
        name="inproj",
    )(x, gain, w_in, cos, sin)
    n_lay = len(DILATIONS)
    as_rows = lambda a: a.reshape(bsz, seq, ATTN_WIDTH)
    return tuple([as_rows(a) for a in outs[j * n_lay:(j + 1) * n_lay]] for j in range(3))


STATE_DILATION = max(DILATIONS)


def _query_order(idx, groups):
    run = BLK // groups
    return groups * (idx % run) + idx // run


def _attn_kernel(*refs, seq):
    n_lay = len(DILATIONS)
    srcs = refs[:3 * n_lay]
    o_ref, buf, acc_s, m_s, l_s, sem = refs[3 * n_lay:]
    b = pl.program_id(0)
    n_b = pl.num_programs(0)
    n_blocks = seq // BLK
    pair_w = 2 * HEAD_DIM
    n_pair = ATTN_WIDTH // pair_w

    def load_sweep(lay, bi, slot):
        return [pltpu.make_async_copy(srcs[a * n_lay + lay].at[bi], buf.at[slot, a, pl.ds(BLK, seq)],
                                      sem.at[slot]) for a in range(3)]

    @pl.when(b == 0)
    def _():
        buf[:, :, 0:BLK, :] = jnp.zeros((2, 3, BLK, ATTN_WIDTH), BF16)
        for c in load_sweep(0, 0, 0):
            c.start()

    row = lax.broadcasted_iota(jnp.int32, (BLK, BLK), 0)
    col = lax.broadcasted_iota(jnp.int32, (BLK, BLK), 1)
    lane = lax.broadcasted_iota(jnp.int32, (BLK, pair_w), 1)
    low_head = lane < HEAD_DIM
    unpermute = (row == _query_order(col, STATE_DILATION // DILATIONS[-1])).astype(BF16)

    for lay, d in enumerate(DILATIONS):
        slot = (b * n_lay + lay) % 2
        for c in load_sweep(lay, b, slot):
            c.wait()
        if lay + 1 < n_lay:
            for c in load_sweep(lay + 1, b, 1 - slot):
                c.start()
        else:
            @pl.when(b + 1 < n_b)
            def _():
                for c in load_sweep(0, b + 1, 1 - slot):
                    c.start()

        nb = seq // (d * BLK)
        has_prev = nb > 1
        first, final = lay == 0, lay == n_lay - 1
        kw = 2 * BLK if has_prev else BLK

        groups = STATE_DILATION // d
        run = BLK // groups
        q_pos = _query_order(row, groups)

        def block(j, carry, d=d, nb=nb, has_prev=has_prev, first=first, final=final, kw=kw,
                  slot=slot, groups=groups, run=run, q_pos=q_pos):
            r, n = j // nb, j % nb
            q_row0 = pl.multiple_of(BLK + j * BLK, BLK)
            k_row0 = pl.multiple_of(j * BLK, BLK) if has_prev else q_row0
            runs = [pl.ds(pl.multiple_of((d * a + r) * BLK + run * n, run), run)
                    for a in range(groups)]
            load = lambda ref, g: jnp.concatenate([ref[g, s, :] for s in runs], axis=0)

            def store(ref, g, val):
                for a, s in enumerate(runs):
                    ref[g, s, :] = val[a * run:(a + 1) * run]

            cur_ok = col <= q_pos
            if has_prev:
                prev_ok = jnp.logical_and(col >= q_pos, n > 0)
                allowed = jnp.concatenate([prev_ok, cur_ok], axis=1)
            else:
                allowed = cur_ok
            allowed2 = jnp.concatenate([allowed, allowed], axis=0)
            lanes = [slice(g * pair_w, (g + 1) * pair_w) for g in range(n_pair)]
            scores = []
            for ls in lanes:
                q2 = buf[slot, 0, pl.ds(q_row0, BLK), ls]
                q_both = jnp.concatenate([jnp.where(low_head, q2, 0),
                                          jnp.where(low_head, 0, q2)], axis=0)
                k2 = buf[slot, 1, pl.ds(k_row0, kw), ls]
                scores.append(lax.dot_general(q_both, k2, (((1,), (1,)), ((), ())),
                                              preferred_element_type=F32))
            probs, maxes, sums = [], [], []
            for s in scores:
                s = jnp.where(allowed2, s, NEG_INF)
                m_h = jnp.max(s, axis=-1, keepdims=True)
                p = jnp.exp(s - m_h)
                sums.append(jnp.sum(p, axis=-1, keepdims=True))
                maxes.append(m_h)
                probs.append(p.astype(BF16))
            pv = [jnp.dot(p, buf[slot, 2, pl.ds(k_row0, kw), ls], preferred_element_type=F32)
                  for p, ls in zip(probs, lanes)]
            for g, ls in enumerate(lanes):
                acc = jnp.where(low_head, pv[g][:BLK], pv[g][BLK:])
                m_new = jnp.where(low_head, maxes[g][:BLK], maxes[g][BLK:])
                l_new = jnp.where(low_head, sums[g][:BLK], sums[g][BLK:])
                if not first:
                    m_old = load(m_s, g)
                    m_tot = jnp.maximum(m_old, m_new)
                    a_old = jnp.exp(m_old - m_tot)
                    a_new = jnp.exp(m_new - m_tot)
                    acc = load(acc_s, g) * a_old + acc * a_new
                    l_new = load(l_s, g) * a_old + l_new * a_new
                    m_new = m_tot
                if final:
                    o_blk = jnp.dot(unpermute, (acc / l_new).astype(BF16), preferred_element_type=F32)
                    o_ref[0, pl.ds(pl.multiple_of(j * BLK, BLK), BLK), ls] = o_blk.astype(o_ref.dtype)
                else:
                    store(acc_s, g, acc)
                    store(m_s, g, m_new)
                    store(l_s, g, l_new)
            return carry

        lax.fori_loop(0, n_blocks, block, 0, unroll=2)


def _attention(q, k, v):
    bsz, seq, _ = q[0].shape
    assert all(seq % (d * BLK) == 0 for d in DILATIONS) and DILATIONS[-1] == 1
    n_pair = ATTN_WIDTH // (2 * HEAD_DIM)
    return pl.pallas_call(
        functools.partial(_attn_kernel, seq=seq),
        grid=(bsz,),
        in_specs=[pl.BlockSpec(memory_space=pl.ANY)] * (3 * len(DILATIONS)),
        out_specs=pl.BlockSpec((1, seq, ATTN_WIDTH), lambda b: (b, 0, 0)),
        out_shape=jax.ShapeDtypeStruct((bsz, seq, ATTN_WIDTH), BF16),
        scratch_shapes=[pltpu.VMEM((2, 3, BLK + seq, ATTN_WIDTH), BF16),
                        pltpu.VMEM((n_pair, seq, 2 * HEAD_DIM), F32),
                        pltpu.VMEM((n_pair, seq, 2 * HEAD_DIM), F32),
                        pltpu.VMEM((n_pair, seq, 2 * HEAD_DIM), F32),
                        pltpu.SemaphoreType.DMA((2,))],
        compiler_params=_cparams(("arbitrary",)),
        name="attention",
    )(*q, *k, *v)


ROUTE_COLS = 4
EXPERT_LANE0 = N_GROUPS
MERGE_ROWS = 512


def _merge_kernel(o_ref, c_ref, ga_ref, gb_ref, x_ref, wa_ref, wc_ref, bc_ref, wo_ref,
                  gf_ref, wrh_ref, wrl_ref, br_ref,
                  x1_ref, h2_ref, route_ref, gate_ref, cnt_ref, tri_ref):
    tm = x_ref.shape[0]

    @pl.when(pl.program_id(0) == 0)
    def _():
        r = lax.broadcasted_iota(jnp.int32, (tm, tm), 0)
        c = lax.broadcasted_iota(jnp.int32, (tm, tm), 1)
        tri_ref[...] = (c < r).astype(BF16)

    logit_rows = []
    for r0 in range(0, tm, MERGE_ROWS):
        rows = slice(r0, r0 + MERGE_ROWS)
        y_a = jnp.dot(o_ref[rows, :], wa_ref[...], preferred_element_type=F32)
        y_b = jnp.dot(c_ref[rows, :], wc_ref[...], preferred_element_type=F32) + bc_ref[...]
        merged = (_sigmoid(ga_ref[rows, :].astype(F32)) * y_a
                  + _sigmoid(gb_ref[rows, :].astype(F32)) * y_b)
        x1 = x_ref[rows, :] + jnp.dot(merged.astype(BF16), wo_ref[...], preferred_element_type=F32)
        x1_ref[rows, :] = x1
        h2 = _rms(x1, gf_ref[...])
        _store_row_tiles(h2_ref.at[pl.ds(r0 * ROW_TILE, MERGE_ROWS * ROW_TILE), :], h2)
        h_hi = h2.astype(BF16)
        h_lo = (h2 - h_hi.astype(F32)).astype(BF16)
        logit_rows.append(jnp.dot(h_hi, wrh_ref[...], preferred_element_type=F32)
                          + jnp.dot(h_hi, wrl_ref[...], preferred_element_type=F32)
                          + jnp.dot(h_lo, wrh_ref[...], preferred_element_type=F32))
    logits = jnp.concatenate(logit_rows, axis=0) + br_ref[...]

    lane = lax.broadcasted_iota(jnp.int32, (tm, LANES), 1).astype(F32)
    no_lane = float(LANES)
    is_group = lane < N_GROUPS
    g_max = jnp.max(jnp.where(is_group, logits, -jnp.inf), axis=-1, keepdims=True)
    g_sel = jnp.min(jnp.where(jnp.logical_and(is_group, logits == g_max), lane, no_lane),
                    axis=-1, keepdims=True)
    p_group = 1.0 / jnp.sum(jnp.where(is_group, jnp.exp(logits - g_max), 0.0),
                            axis=-1, keepdims=True)
    lo = EXPERT_LANE0 + EXPERTS_PER_GROUP * g_sel
    in_sel = jnp.logical_and(lane >= lo, lane < lo + EXPERTS_PER_GROUP)
    cand = jnp.where(in_sel, logits, -jnp.inf)
    v1 = jnp.max(cand, axis=-1, keepdims=True)
    i1 = jnp.min(jnp.where(jnp.logical_and(in_sel, cand == v1), lane, no_lane), axis=-1, keepdims=True)
    rest = jnp.logical_and(in_sel, lane != i1)
    cand2 = jnp.where(rest, logits, -jnp.inf)
    v2 = jnp.max(cand2, axis=-1, keepdims=True)
    i2 = jnp.min(jnp.where(jnp.logical_and(rest, cand2 == v2), lane, no_lane), axis=-1, keepdims=True)
    e2 = jnp.exp(v2 - v1)
    gate0 = p_group / (1.0 + e2)
    gate1 = p_group * e2 / (1.0 + e2)

    oh0 = (lane == i1)
    oh1 = (lane == i2)
    before0 = jnp.dot(tri_ref[...], oh0.astype(BF16), preferred_element_type=F32)
    before1 = jnp.dot(tri_ref[...], oh1.astype(BF16), preferred_element_type=F32)
    tot0 = jnp.sum(oh0.astype(F32), axis=0, keepdims=True)
    tot1 = jnp.sum(oh1.astype(F32), axis=0, keepdims=True)
    cnt = tot0 + tot1
    rank0 = jnp.sum(jnp.where(oh0, before0, 0.0), axis=-1, keepdims=True)
    rank1 = jnp.sum(jnp.where(oh1, tot0 + before1, 0.0), axis=-1, keepdims=True)

    rc = lax.broadcasted_iota(jnp.int32, (tm, ROUTE_COLS), 1)
    route = jnp.where(rc == 0, i1 - EXPERT_LANE0,
                      jnp.where(rc == 1, i2 - EXPERT_LANE0, jnp.where(rc == 2, rank0, rank1)))
    route_ref[...] = route.astype(jnp.int32)
    gc = lax.broadcasted_iota(jnp.int32, (tm, TOP_K), 1)
    gate_ref[...] = jnp.where(gc == 0, gate0, gate1)
    cnt_ref[0] = jnp.broadcast_to(pltpu.roll(cnt, LANES - EXPERT_LANE0, 1), (8, LANES))


def _merge(o, c, ga, gb, x, wa, wc, bc, wo, gf, wr_hi, wr_lo, br, tm):
    t = x.shape[0]
    nt = t // tm
    row = lambda i: (i, 0)
    const = lambda i: (0, 0)
    full = lambda a: pl.BlockSpec(a.shape, const)
    return pl.pallas_call(
        _merge_kernel,
        grid=(nt,),
        in_specs=[
            pl.BlockSpec((tm, ATTN_WIDTH), row),
            pl.BlockSpec((tm, CONV_WIDTH), row),
            pl.BlockSpec((tm, D_MODEL), row),
            pl.BlockSpec((tm, D_MODEL), row),
            pl.BlockSpec((tm, D_MODEL), row),
            full(wa), full(wc), full(bc), full(wo), full(gf), full(wr_hi), full(wr_lo), full(br),
        ],
        out_specs=(
            pl.BlockSpec((tm, D_MODEL), row),
            pl.BlockSpec((tm * ROW_TILE, LANES), row),
            pl.BlockSpec((tm, ROUTE_COLS), row),
            pl.BlockSpec((tm, TOP_K), row),
            pl.BlockSpec((1, 8, LANES), lambda i: (i, 0, 0)),
        ),
        out_shape=(
            jax.ShapeDtypeStruct((t, D_MODEL), F32),
            jax.ShapeDtypeStruct((t * ROW_TILE, LANES), F32),
            jax.ShapeDtypeStruct((t, ROUTE_COLS), jnp.int32),
            jax.ShapeDtypeStruct((t, TOP_K), F32),
            jax.ShapeDtypeStruct((nt, 8, LANES), F32),
        ),
        scratch_shapes=[pltpu.VMEM((tm, tm), BF16)],
        compiler_params=_cparams(("arbitrary",)),
        name="merge",
    )(o, c, ga, gb, x, wa, wc, bc, wo, gf, wr_hi, wr_lo, br)


def _pos_kernel(route_ref, base_ref, pos_ref):
    tm = route_ref.shape[0]
    route = route_ref[...].astype(F32)
    lane = lax.broadcasted_iota(jnp.int32, (tm, LANES), 1).astype(F32)
    cols = []
    for k in range(TOP_K):
        start = jnp.sum(jnp.where(lane == route[:, k:k + 1], base_ref[0, 0:1, :], 0.0),
                        axis=-1, keepdims=True)
        cols.append(start + route[:, TOP_K + k:TOP_K + k + 1])
    sel = lax.broadcasted_iota(jnp.int32, (tm, TOP_K), 1)
    pos_ref[...] = jnp.where(sel == 0, cols[0], cols[1]).astype(jnp.int32)


def _positions(route, base, tm):
    nt = base.shape[0]
    base_rows = jnp.zeros((nt, SUBLANES, LANES), F32).at[:, :, :N_EXPERTS].set(
        base.astype(F32)[:, None, :])
    return pl.pallas_call(
        _pos_kernel,
        grid=(nt,),
        in_specs=[pl.BlockSpec((tm, ROUTE_COLS), lambda i: (i, 0)),
                  pl.BlockSpec((1, SUBLANES, LANES), lambda i: (i, 0, 0))],
        out_specs=pl.BlockSpec((tm, TOP_K), lambda i: (i, 0)),
        out_shape=jax.ShapeDtypeStruct((route.shape[0], TOP_K), jnp.int32),
        compiler_params=_cparams(("parallel",)),
        name="positions",
    )(route, base_rows)


def _sort_kernel(pos_ref, h_ref, xs_ref):
    tm = h_ref.shape[0] // ROW_TILE

    def place(t, carry):
        row = h_ref[pl.ds(pl.multiple_of(t * ROW_TILE, ROW_TILE), ROW_TILE), :]
        for k in range(TOP_K):
            pos = pos_ref[TOP_K * t + k]
            xs_ref[pl.ds(pl.multiple_of(pos * ROW_TILE, ROW_TILE), ROW_TILE), :] = row
        return carry

    lax.fori_loop(0, tm, place, 0, unroll=8)


def _tile_sort(pos_flat, h2, tm):
    nt = h2.shape[0] // (tm * ROW_TILE)
    return pl.pallas_call(
        _sort_kernel,
        grid=(nt,),
        in_specs=[pl.BlockSpec((TOP_K * tm,), lambda i: (i,), memory_space=pltpu.SMEM),
                  pl.BlockSpec((tm * ROW_TILE, LANES), lambda i: (i, 0))],
        out_specs=pl.BlockSpec((TOP_K * tm * ROW_TILE, LANES), lambda i: (i, 0)),
        out_shape=jax.ShapeDtypeStruct((TOP_K * h2.shape[0], LANES), F32),
        compiler_params=_cparams(("parallel",)),
        name="tile_sort",
    )(pos_flat, h2)


RARE_RUN = 64


def _moe_kernel(be_ref, bj_ref, ilo_ref, ihi_ref, valid_ref, cum_ref, cnt_ref, off_ref, nused_ref,
                next_e_ref, xs_hbm, wg_hbm, wu_hbm, wd_hbm, ys_hbm, trash_hbm,
                xbuf, obuf, wg_s, wu_s, wd_s, wg_f, wu_f, wd_f, wslot, sem_in, sem_out, sem_w,
                *, tile_rows, layer):
    b = pl.program_id(0)
    slot = b % 2
    other = 1 - slot
    rows8 = lambda r, n: pl.ds(pl.multiple_of(r * ROW_TILE, ROW_TILE), n * ROW_TILE)

    def for_runs(bb, fn):
        e = be_ref[bb]
        lo_blk = bj_ref[bb] * MOE_BLOCK

        def body(i, carry):
            c = cum_ref[i * N_EXPERTS + e]
            lo = jnp.maximum(c, lo_blk)
            hi = jnp.minimum(c + cnt_ref[i * N_EXPERTS + e], lo_blk + MOE_BLOCK)
            fn(i * tile_rows + off_ref[i * N_EXPERTS + e] + (lo - c), lo - lo_blk,
               jnp.maximum(hi - lo, 0))
            return carry

        lax.fori_loop(ilo_ref[bb], ihi_ref[bb] + 1, body, 0)

    def for_chunks(length, fn):
        def piece(size):
            @pl.when((length & size) != 0)
            def _():
                fn(length & ~(2 * size - 1), size)

        sizes = [1 << bit for bit in reversed(range(MOE_BLOCK.bit_length()))]

        @pl.when(length >= RARE_RUN)
        def _():
            for size in sizes:
                if size >= RARE_RUN:
                    piece(size)

        for size in sizes:
            if size < RARE_RUN:
                piece(size)

    def gather(bb, s):
        def run(src, dst, length):
            for_chunks(length, lambda o, n: pltpu.make_async_copy(
                xs_hbm.at[rows8(src + o, n)], xbuf.at[s, rows8(dst + o, n)], sem_in.at[s]).start())
        for_runs(bb, run)
        v = valid_ref[bb]
        for_chunks(MOE_BLOCK - v, lambda o, n: pltpu.make_async_copy(
            xs_hbm.at[rows8(o, n)], xbuf.at[s, rows8(v + o, n)], sem_in.at[s]).start())

    def scatter(bb, s):
        def run(src, dst, length):
            for_chunks(length, lambda o, n: pltpu.make_async_copy(
                obuf.at[s, rows8(dst + o, n)], ys_hbm.at[rows8(src + o, n)], sem_out.at[s]).start())
        for_runs(bb, run)
        v = valid_ref[bb]
        for_chunks(MOE_BLOCK - v, lambda o, n: pltpu.make_async_copy(
            obuf.at[s, rows8(v + o, n)], trash_hbm.at[rows8(s * MOE_BLOCK + v + o, n)],
            sem_out.at[s]).start())

    def wait_gather(s):
        pltpu.make_async_copy(xs_hbm.at[rows8(0, MOE_BLOCK)], xbuf.at[s], sem_in.at[s]).wait()

    def wait_scatter(s):
        pltpu.make_async_copy(obuf.at[s], ys_hbm.at[rows8(0, MOE_BLOCK)], sem_out.at[s]).wait()

    @pl.when(b == 0)
    def _():
        obuf[...] = jnp.zeros_like(obuf)
        for s in range(2):
            init = pltpu.make_async_copy(obuf.at[s], trash_hbm.at[rows8(s * MOE_BLOCK, MOE_BLOCK)],
                                         sem_out.at[s])
            init.start()
            init.wait()
        gather(0, 0)

    n_used = nused_ref[0]
    live = b < n_used

    @pl.when(live)
    def _():
        wait_gather(slot)

    @pl.when(jnp.logical_and(live, b >= 2))
    def _():
        wait_scatter(slot)

    def weight_copies(expert, s):
        return [pltpu.make_async_copy(src.at[layer, expert], dst.at[s], sem_w.at[s])
                for src, dst in ((wg_hbm, wg_f), (wu_hbm, wu_f), (wd_hbm, wd_f))]

    e = be_ref[b]
    e_prev = be_ref[jnp.maximum(b - 1, 0)]

    @pl.when(b == 0)
    def _():
        wslot[0] = 0
        for c in weight_copies(e, 0):
            c.start()

    @pl.when(jnp.logical_and(live, jnp.logical_or(b == 0, e != e_prev)))
    def _():
        s = wslot[0]
        for c in weight_copies(e, s):
            c.wait()
        wg_s[...] = wg_f[s].astype(BF16)
        wu_s[...] = wu_f[s].astype(BF16)
        wd_s[...] = wd_f[s].astype(BF16)
        e_next = next_e_ref[b]

        @pl.when(e_next != e)
        def _():
            for c in weight_copies(e_next, 1 - s):
                c.start()
        wslot[0] = 1 - s

    @pl.when(b + 1 < n_used)
    def _():
        gather(b + 1, other)

    @pl.when(jnp.logical_and(live, b >= 1))
    def _():
        scatter(b - 1, other)

    @pl.when(live)
    def _():
        x = _load_row_tiles(xbuf.at[slot], MOE_BLOCK).astype(BF16)
        hg = jnp.dot(x, wg_s[...], preferred_element_type=F32)
        hu = jnp.dot(x, wu_s[...], preferred_element_type=F32)
        hb = (hg * _sigmoid(hg) * hu).astype(BF16)
        _store_row_tiles(obuf.at[slot], jnp.dot(hb, wd_s[...], preferred_element_type=F32))

    @pl.when(b == n_used - 1)
    def _():
        scatter(b, slot)
        wait_scatter(other)
        wait_scatter(slot)


def _moe(plan, xs, w_gate, w_up, w_down, layer, n_blocks, tile_rows):
    n_pre = len(plan)
    grid_spec = pltpu.PrefetchScalarGridSpec(
        num_scalar_prefetch=n_pre,
        grid=(n_blocks,),
        in_specs=[pl.BlockSpec(memory_space=pl.ANY)] * 4,
        out_specs=(pl.BlockSpec(memory_space=pl.ANY), pl.BlockSpec(memory_space=pl.ANY)),
        scratch_shapes=[pltpu.VMEM((2, MOE_BLOCK * ROW_TILE, LANES), F32),
                        pltpu.VMEM((2, MOE_BLOCK * ROW_TILE, LANES), F32),
                        pltpu.VMEM((D_MODEL, EXPERT_FF), BF16),
                        pltpu.VMEM((D_MODEL, EXPERT_FF), BF16),
                        pltpu.VMEM((EXPERT_FF, D_MODEL), BF16),
                        pltpu.VMEM((2, D_MODEL, EXPERT_FF), F32),
                        pltpu.VMEM((2, D_MODEL, EXPERT_FF), F32),
                        pltpu.VMEM((2, EXPERT_FF, D_MODEL), F32),
                        pltpu.SMEM((1,), jnp.int32),
                        pltpu.SemaphoreType.DMA((2,)),
                        pltpu.SemaphoreType.DMA((2,)),
                        pltpu.SemaphoreType.DMA((2,))],
    )
    ys, _ = pl.pallas_call(
        functools.partial(_moe_kernel, tile_rows=tile_rows, layer=layer),
        grid_spec=grid_spec,
        out_shape=(jax.ShapeDtypeStruct(xs.shape, F32),
                   jax.ShapeDtypeStruct((2 * MOE_BLOCK * ROW_TILE, LANES), F32)),
        compiler_params=_cparams(("arbitrary",)),
        name="moe",
    )(*plan, xs, w_gate, w_up, w_down)
    return ys


def _ple_kernel(pos_ref, gate_ref, ys_ref, x1_ref, p_ref, gp_ref, wpg_ref, wpe_ref, gfin_ref,
                out_ref, moe_ref, *, final_norm):
    tm = x1_ref.shape[0]
    tile = lambda r: pl.ds(pl.multiple_of(r * ROW_TILE, ROW_TILE), ROW_TILE)

    def combine(t, carry):
        acc = None
        for k in range(TOP_K):
            term = gate_ref[TOP_K * t + k] * ys_ref[tile(pos_ref[TOP_K * t + k]), :]
            acc = term if acc is None else acc + term
        moe_ref[tile(t), :] = acc
        return carry

    lax.fori_loop(0, tm, combine, 0, unroll=8)
    x2 = x1_ref[...] + _load_row_tiles(moe_ref, tm)
    hp = _rms(x2, gp_ref[...]).astype(BF16)
    g_ple = _sigmoid(jnp.dot(hp, wpg_ref[...], preferred_element_type=F32))
    emb = jnp.dot(p_ref[...].astype(BF16), wpe_ref[...], preferred_element_type=F32)
    x3 = x2 + g_ple * emb
    if final_norm:
        x3 = _rms(x3, gfin_ref[...])
    out_ref[...] = x3


def _ple(pos, gate_flat, ys, x1, p, gp, wpg, wpe, gfin, tm, final_norm):
    t = x1.shape[0]
    row = lambda i: (i, 0)
    const = lambda i: (0, 0)
    full = lambda a: pl.BlockSpec(a.shape, const)
    smem = lambda n: pl.BlockSpec((n,), lambda i: (i,), memory_space=pltpu.SMEM)
    return pl.pallas_call(
        functools.partial(_ple_kernel, final_norm=final_norm),
        grid=(t // tm,),
        in_specs=[
            smem(TOP_K * tm),
            smem(TOP_K * tm),
            pl.BlockSpec((TOP_K * tm * ROW_TILE, LANES), row),
            pl.BlockSpec((tm, D_MODEL), row),
            pl.BlockSpec((tm, PLE_DIM), row),
            full(gp), full(wpg), full(wpe), full(gfin),
        ],
        out_specs=pl.BlockSpec((tm, D_MODEL), row),
        out_shape=jax.ShapeDtypeStruct((t, D_MODEL), F32),
        scratch_shapes=[pltpu.VMEM((tm * ROW_TILE, LANES), F32)],
        compiler_params=_cparams(("parallel",)),
        name="ple",
    )(pos, gate_flat, ys, x1, p, gp, wpg, wpe, gfin)


def _rope_tables(seq):
    inv = jnp.power(ROPE_THETA, -jnp.arange(0, HEAD_DIM, 2, dtype=F32) / HEAD_DIM)
    ang = jnp.arange(seq, dtype=F32)[:, None] * inv[None, :]
    cos, sin = jnp.cos(ang), jnp.sin(ang)
    cos_h = jnp.concatenate([cos, cos], axis=-1)
    sin_h = jnp.concatenate([-sin, sin], axis=-1)
    reps = LANES // HEAD_DIM
    return jnp.tile(cos_h, (1, reps)), jnp.tile(sin_h, (1, reps))


def _dispatch_plan(route_counts, group, n_blocks):
    i32 = jnp.int32
    grouped = route_counts.reshape(-1, group, N_EXPERTS)
    counts = jnp.sum(grouped, axis=1)
    base = ((jnp.cumsum(counts, axis=1) - counts)[:, None, :]
            + jnp.cumsum(grouped, axis=1) - grouped).reshape(-1, N_EXPERTS).astype(i32)
    total = jnp.sum(counts, axis=0)
    blocks_of = (total + MOE_BLOCK - 1) // MOE_BLOCK
    block_end = jnp.cumsum(blocks_of)
    bidx = jnp.arange(n_blocks, dtype=i32)
    block_e = jnp.minimum(jnp.sum((block_end[None, :] <= bidx[:, None]).astype(i32), axis=1),
                          N_EXPERTS - 1)
    is_e = (block_e[:, None] == jnp.arange(N_EXPERTS, dtype=i32)[None, :]).astype(i32)
    pick = lambda per_expert: jnp.sum(is_e * per_expert[None, :], axis=1)
    pick_tiles = lambda tab: jnp.sum(is_e[:, None, :] * tab[None, :, :], axis=2)
    block_j = bidx - pick(block_end - blocks_of)
    used = bidx < block_end[-1]
    cum = jnp.cumsum(counts, axis=0) - counts
    off = jnp.cumsum(counts, axis=1) - counts
    lo = block_j * MOE_BLOCK
    hi = jnp.minimum(lo + MOE_BLOCK, pick(total))
    run_start = pick_tiles(cum)
    run_end = run_start + pick_tiles(counts)
    tile_lo = jnp.sum((run_end <= lo[:, None]).astype(i32), axis=1)
    tile_hi = jnp.sum((run_start < hi[:, None]).astype(i32), axis=1) - 1
    valid = jnp.clip(hi - lo, 0, MOE_BLOCK)
    tile_lo = jnp.where(used, tile_lo, 0)
    tile_hi = jnp.where(used, tile_hi, -1)
    valid = jnp.where(used, valid, 0)
    e_ids = jnp.arange(N_EXPERTS, dtype=i32)
    later = jnp.where((e_ids[None, :] > e_ids[:, None]) & (blocks_of[None, :] > 0), e_ids[None, :], N_EXPERTS)
    next_used = jnp.min(later, axis=1)
    next_e = pick(jnp.where(next_used == N_EXPERTS, e_ids, next_used))
    flat = lambda a: a.reshape(-1).astype(i32)
    return base, (flat(block_e), flat(block_j), flat(tile_lo), flat(tile_hi), flat(valid),
                  flat(cum), flat(counts), flat(off), flat(block_end[-1:]), flat(next_e))


def kernel(x, p, norm_mix, w_in, w_dw, b_dw, ln_conv_g, ln_conv_b, w_attn_out, w_conv_out,
           b_conv_out, w_out, norm_ffn, w_route_group, b_route_group, w_route_expert,
           b_route_expert, w_exp_gate, w_exp_up, w_exp_down, norm_ple, w_ple_proj, w_ple_gate,
           norm_final):
    bsz, seq, d = x.shape
    depth = w_in.shape[0]
    t = bsz * seq
    tm = 512
    sort_tm = 1024
    n_assign = t * TOP_K
    assert n_assign > MOE_BLOCK
    n_rows = (-(-n_assign // MOE_BLOCK)) * MOE_BLOCK + N_EXPERTS * MOE_BLOCK
    cos, sin = _rope_tables(seq)
    row2 = lambda a: a.reshape(1, -1)

    xf = x.reshape(t, d)
    for i in range(depth):
        n_qkv = 3 * ATTN_WIDTH
        q, k, v = _inproj(xf, row2(norm_mix[i]), w_in[i, :, :n_qkv].astype(BF16), cos, sin,
                          bsz, seq, tm)
        c, ga, gb = _gateproj(xf, row2(norm_mix[i]), w_in[i, :, n_qkv:].astype(BF16), w_dw[i],
                              row2(b_dw[i]), row2(ln_conv_g[i]), row2(ln_conv_b[i]), seq, tm)
        o = _attention(q, k, v).reshape(t, ATTN_WIDTH)

        w_r = jnp.zeros((d, LANES), F32)
        w_r = w_r.at[:, :N_GROUPS].set(w_route_group[i])
        w_r = w_r.at[:, EXPERT_LANE0:EXPERT_LANE0 + N_EXPERTS].set(w_route_expert[i])
        w_r_hi = w_r.astype(BF16)
        w_r_lo = (w_r - w_r_hi.astype(F32)).astype(BF16)
        b_r = jnp.zeros((1, LANES), F32)
        b_r = b_r.at[0, :N_GROUPS].set(b_route_group[i])
        b_r = b_r.at[0, EXPERT_LANE0:EXPERT_LANE0 + N_EXPERTS].set(b_route_expert[i])

        x1, h2, route, gate, cnt = _merge(
            o, c, ga, gb, xf, w_attn_out[i].astype(BF16), w_conv_out[i].astype(BF16),
            row2(b_conv_out[i]), w_out[i].astype(BF16), row2(norm_ffn[i]), w_r_hi, w_r_lo, b_r, tm)

        counts = cnt[:, 0, :N_EXPERTS].astype(jnp.int32)
        base, plan = _dispatch_plan(counts, sort_tm // tm, n_rows // MOE_BLOCK)
        pos = _positions(route, base, tm).reshape(-1)
        xs = _tile_sort(pos, h2, sort_tm)
        ys = _moe(plan, xs, w_exp_gate, w_exp_up, w_exp_down, i, n_rows // MOE_BLOCK, TOP_K * sort_tm)
        xf = _ple(pos, gate.reshape(-1), ys, x1, p[i].reshape(t, PLE_DIM), row2(norm_ple[i]),
                  w_ple_gate[i].astype(BF16), w_ple_proj[i].astype(BF16), row2(norm_final),
                  sort_tm, final_norm=(i == depth - 1))
    return xf.reshape(bsz, seq, d)
```

```python
import functools

import jax
import jax.numpy as jnp
from jax import lax
from jax.experimental import pallas as pl
from jax.experimental.pallas import tpu as pltpu

D_MODEL = 1024
N_HEADS = 8
HEAD_DIM = 64
ATTN_WIDTH = N_HEADS * HEAD_DIM
CONV_WIDTH = 512
CONV_K = 31
DILATIONS = (16, 4, 1)
BLK = 128
ROPE_THETA = 10000.0
N_GROUPS = 4
EXPERTS_PER_GROUP = 8
N_EXPERTS = N_GROUPS * EXPERTS_PER_GROUP
EXPERT_FF = 512
TOP_K = 2
MOE_BLOCK = 256
PLE_DIM = 256
EPS = 1e-6
NEG_INF = -1e30

LANES = 128
SUBLANES = 8
ROW_TILE = D_MODEL // LANES
VMEM_LIMIT = 56 * 1024 * 1024

F32 = jnp.float32
BF16 = jnp.bfloat16


def _cparams(sem):
    return pltpu.CompilerParams(dimension_semantics=sem, vmem_limit_bytes=VMEM_LIMIT)


def _rms(x, g):
    return x * lax.rsqrt(jnp.mean(x * x, axis=-1, keepdims=True) + EPS) * g


def _sigmoid(x):
    return 0.5 * jnp.tanh(0.5 * x) + 0.5


def _store_row_tiles(ref, x):
    rows = x.shape[0]
    for s in range(ROW_TILE):
        ref[pl.ds(s, rows, stride=ROW_TILE), :] = x[:, s * LANES:(s + 1) * LANES]


def _load_row_tiles(ref, rows):
    return jnp.concatenate([ref[pl.ds(s, rows, stride=ROW_TILE), :] for s in range(ROW_TILE)], axis=1)


QKV_ROWS = 256


def _inproj_kernel(x_ref, g_ref, w_ref, cos_ref, sin_ref, *out_refs):
    n_lay = len(DILATIONS)
    q_refs, k_refs, v_refs = (out_refs[a * n_lay:(a + 1) * n_lay] for a in range(3))
    slab_refs = out_refs[3 * n_lay:]
    tm = x_ref.shape[0]
    n_slab = ATTN_WIDTH // LANES

    def put(ref, d, r, row0, g, val):
        cols = slice(g * LANES, (g + 1) * LANES)
        if d == 1:
            ref[pl.ds(row0, val.shape[0]), cols] = val.astype(BF16)
        else:
            ref[0, r, pl.ds(row0, val.shape[0]), cols] = val.astype(BF16)

    def fill(raw, rows, refs, slabs, is_query, post):
        for g in range(n_slab):
            cols = slice(g * LANES, (g + 1) * LANES)
            t = post(raw[:, cols], rows)
            slabs[g, rows, :] = t
            for d, ref in zip(DILATIONS, refs):
                if d == 1 and not is_query:
                    ref[rows, cols] = t.astype(BF16)

    def relayout(refs, slabs, by4, is_query):
        ref16, ref4, ref1 = refs
        quarter = tm // 4
        class4 = lambda r: pl.multiple_of(r * quarter, quarter)

        def to_by4(r, carry):
            for g in range(n_slab):
                val = slabs[g, pl.ds(r, quarter, stride=4), :]
                by4[g, pl.ds(class4(r), quarter), :] = val
                if not is_query:
                    put(ref4, 4, r, 0, g, val)
            return carry
        lax.fori_loop(0, 4, to_by4, 0, unroll=2)

        def to_16(r, carry):
            for g in range(n_slab):
                put(ref16, 16, r, 0, g, by4[g, pl.ds(class4(r % 4) + r // 4, tm // 16, stride=4), :])
            return carry
        lax.fori_loop(0, 16, to_16, 0, unroll=4)

        if is_query:
            def q4_class(r, carry):
                for bl in range(quarter // BLK):
                    for a in range(4):
                        src = pl.ds(class4(r) + bl * BLK + a, BLK // 4, stride=4)
                        for g in range(n_slab):
                            put(ref4, 4, r, bl * BLK + a * (BLK // 4), g, by4[g, src, :])
                return carry
            lax.fori_loop(0, 4, q4_class, 0, unroll=2)

            def q1_block(bl, carry):
                for a in range(16):
                    src = pl.ds((a % 4) * quarter + (BLK // 4) * bl + a // 4, BLK // 16, stride=4)
                    for g in range(n_slab):
                        put(ref1, 1, 0, pl.multiple_of(bl * BLK, BLK) + a * (BLK // 16), g, by4[g, src, :])
                return carry
            lax.fori_loop(0, tm // BLK, q1_block, 0, unroll=2)

    lane = lax.broadcasted_iota(jnp.int32, (QKV_ROWS, LANES), 1)
    first_half = (lane % HEAD_DIM) < (HEAD_DIM // 2)

    def rope(t, rows):
        partner = jnp.where(first_half,
                            pltpu.roll(t, LANES - HEAD_DIM // 2, 1),
                            pltpu.roll(t, HEAD_DIM // 2, 1))
        return t * cos_ref[rows, :] + partner * sin_ref[rows, :]

    for r0 in range(0, tm, QKV_ROWS):
        rows = slice(r0, r0 + QKV_ROWS)
        h = _rms(x_ref[rows, :], g_ref[...]).astype(BF16)
        proj = lambda a: jnp.dot(h, w_ref[:, a * ATTN_WIDTH:(a + 1) * ATTN_WIDTH],
                                 preferred_element_type=F32)
        fill(proj(0), rows, q_refs, slab_refs[0], True, lambda t, rw: rope(t, rw) * (HEAD_DIM ** -0.5))
        fill(proj(1), rows, k_refs, slab_refs[1], False, rope)
        fill(proj(2), rows, v_refs, slab_refs[2], False, lambda t, rw: t)
    assert DILATIONS == (16, 4, 1) and tm % (4 * BLK) == 0
    relayout(q_refs, slab_refs[0], slab_refs[3], True)
    relayout(k_refs, slab_refs[1], slab_refs[4], False)
    relayout(v_refs, slab_refs[2], slab_refs[5], False)


CONV_HALO = 32
CONV_ROWS = 32


def _gateproj_kernel(x_ref, g_ref, w_ref, cw_ref, cb_ref, lg_ref, lb_ref, c_ref, ga_ref, gb_ref,
                     ext_ref, halo_ref, *, tiles_per_seq):
    tm = x_ref.shape[0]
    h = _rms(x_ref[...], g_ref[...]).astype(BF16)

    def proj(c0, width):
        return jnp.dot(h, w_ref[:, c0:c0 + width], preferred_element_type=F32)

    u = proj(0, CONV_WIDTH) * _sigmoid(proj(CONV_WIDTH, CONV_WIDTH))
    first_of_sequence = pl.program_id(0) % tiles_per_seq == 0
    ext_ref[0, 0:CONV_HALO, :] = jnp.where(first_of_sequence, 0.0, halo_ref[...])
    ext_ref[0, CONV_HALO:CONV_HALO + tm, :] = u
    halo_ref[...] = u[tm - CONV_HALO:, :]
    c0 = 2 * CONV_WIDTH
    ga_ref[...] = proj(c0, D_MODEL).astype(ga_ref.dtype)
    gb_ref[...] = proj(c0 + D_MODEL, D_MODEL).astype(gb_ref.dtype)
    _conv_rows(ext_ref, cw_ref, cb_ref, lg_ref, lb_ref, c_ref, tm)


def _conv_rows(ext_ref, w_ref, b_ref, g_ref, beta_ref, out_ref, ts):
    first = CONV_HALO - (CONV_K - 1)
    n_shift = ext_ref.shape[0]
    rows = CONV_HALO + ts
    ext0 = ext_ref[0]
    for s in range(1, n_shift):
        ext_ref[s] = pltpu.roll(ext0, rows - s, 0)
    bias = b_ref[...]
    gamma = g_ref[...]
    beta = beta_ref[...]

    for r0 in range(0, ts, CONV_ROWS):
        acc = jnp.zeros((CONV_ROWS, CONV_WIDTH), F32) + bias
        for j in range(CONV_K):
            lo = first + j + r0
            s, lo = lo % n_shift, lo - lo % n_shift
            tap = jnp.concatenate([w_ref[j]] * (CONV_ROWS // SUBLANES), axis=0)
            acc = acc + tap * ext_ref[s, lo:lo + CONV_ROWS, :]
        mu = jnp.mean(acc, axis=-1, keepdims=True)
        cen = acc - mu
        var = jnp.mean(cen * cen, axis=-1, keepdims=True)
        y = cen * lax.rsqrt(var + EPS) * gamma + beta
        out_ref[r0:r0 + CONV_ROWS, :] = (y * _sigmoid(y)).astype(out_ref.dtype)


def _gateproj(x, gain, w_rest, w_dw, b_dw, ln_g, ln_b, seq, tm):
    t = x.shape[0]
    row = lambda i: (i, 0)
    const = lambda i: (0, 0)
    taps = jnp.broadcast_to(w_dw[:, None, :], (CONV_K, SUBLANES, CONV_WIDTH))
    return pl.pallas_call(
        functools.partial(_gateproj_kernel, tiles_per_seq=seq // tm),
        grid=(t // tm,),
        in_specs=[pl.BlockSpec((tm, D_MODEL), row), pl.BlockSpec((1, D_MODEL), const),
                  pl.BlockSpec(w_rest.shape, const),
                  pl.BlockSpec(taps.shape, lambda i: (0, 0, 0)),
                  pl.BlockSpec((1, CONV_WIDTH), const), pl.BlockSpec((1, CONV_WIDTH), const),
                  pl.BlockSpec((1, CONV_WIDTH), const)],
        out_specs=(pl.BlockSpec((tm, CONV_WIDTH), row), pl.BlockSpec((tm, D_MODEL), row),
                   pl.BlockSpec((tm, D_MODEL), row)),
        out_shape=(jax.ShapeDtypeStruct((t, CONV_WIDTH), BF16),
                   jax.ShapeDtypeStruct((t, D_MODEL), BF16),
                   jax.ShapeDtypeStruct((t, D_MODEL), BF16)),
        scratch_shapes=[pltpu.VMEM((SUBLANES, CONV_HALO + tm, CONV_WIDTH), F32),
                        pltpu.VMEM((CONV_HALO, CONV_WIDTH), F32)],
        compiler_params=_cparams(("arbitrary",)),
        name="gateproj",
    )(x, gain, w_rest, taps, b_dw, ln_g, ln_b)


def _inproj(x, gain, w_in, cos, sin, bsz, seq, tm):
    t = x.shape[0]
    n_pos = seq // tm
    row = lambda i: (i, 0)
    const = lambda i: (0, 0)
    qkv_shapes, qkv_specs = [], []
    for d in DILATIONS:
        if d == 1:
            qkv_shapes.append(jax.ShapeDtypeStruct((t, ATTN_WIDTH), BF16))
            qkv_specs.append(pl.BlockSpec((tm, ATTN_WIDTH), row))
        else:
            qkv_shapes.append(jax.ShapeDtypeStruct((bsz, d, seq // d, ATTN_WIDTH), BF16))
            qkv_specs.append(pl.BlockSpec((1, d, tm // d, ATTN_WIDTH),
                                          lambda i: (i // n_pos, 0, i % n_pos, 0)))
    out_shape = tuple(qkv_shapes * 3)
    out_specs = tuple(qkv_specs * 3)
    outs = pl.pallas_call(
        _inproj_kernel,
        grid=(t // tm,),
        in_specs=[
            pl.BlockSpec((tm, D_MODEL), row),
            pl.BlockSpec((1, D_MODEL), const),
            pl.BlockSpec(w_in.shape, const),
            pl.BlockSpec((tm, LANES), lambda i: (i % n_pos, 0)),
            pl.BlockSpec((tm, LANES), lambda i: (i % n_pos, 0)),
        ],
        out_specs=out_specs,
        out_shape=out_shape,
        scratch_shapes=[pltpu.VMEM((ATTN_WIDTH // LANES, tm, LANES), F32) for _ in range(6)],
        compiler_params=_cparams(("parallel",)),
        name="inproj",
    )(x, gain, w_in, cos, sin)
    n_lay = len(DILATIONS)
    as_rows = lambda a: a.reshape(bsz, seq, ATTN_WIDTH)
    return tuple([as_rows(a) for a in outs[j * n_lay:(j + 1) * n_lay]] for j in range(3))


STATE_DILATION = max(DILATIONS)


def _query_order(idx, groups):
    run = BLK // groups
    return groups * (idx % run) + idx // run


def _attn_kernel(*refs, seq):
    n_lay = len(DILATIONS)
    srcs = refs[:3 * n_lay]
    o_ref, buf, acc_s, m_s, l_s, sem = refs[3 * n_lay:]
    b = pl.program_id(0)
    n_b = pl.num_programs(0)
    n_blocks = seq // BLK
    pair_w = 2 * HEAD_DIM
    n_pair = ATTN_WIDTH // pair_w

    def load_sweep(lay, bi, slot):
        return [pltpu.make_async_copy(srcs[a * n_lay + lay].at[bi], buf.at[slot, a, pl.ds(BLK, seq)],
                                      sem.at[slot]) for a in range(3)]

    @pl.when(b == 0)
    def _():
        buf[:, :, 0:BLK, :] = jnp.zeros((2, 3, BLK, ATTN_WIDTH), BF16)
        for c in load_sweep(0, 0, 0):
            c.start()

    row = lax.broadcasted_iota(jnp.int32, (BLK, BLK), 0)
    col = lax.broadcasted_iota(jnp.int32, (BLK, BLK), 1)
    lane = lax.broadcasted_iota(jnp.int32, (BLK, pair_w), 1)
    low_head = lane < HEAD_DIM
    unpermute = (row == _query_order(col, STATE_DILATION // DILATIONS[-1])).astype(BF16)

    for lay, d in enumerate(DILATIONS):
        slot = (b * n_lay + lay) % 2
        for c in load_sweep(lay, b, slot):
            c.wait()
        if lay + 1 < n_lay:
            for c in load_sweep(lay + 1, b, 1 - slot):
                c.start()
        else:
            @pl.when(b + 1 < n_b)
            def _():
                for c in load_sweep(0, b + 1, 1 - slot):
                    c.start()

        nb = seq // (d * BLK)
        has_prev = nb > 1
        first, final = lay == 0, lay == n_lay - 1
        kw = 2 * BLK if has_prev else BLK

        groups = STATE_DILATION // d
        run = BLK // groups
        q_pos = _query_order(row, groups)

        def block(j, carry, d=d, nb=nb, has_prev=has_prev, first=first, final=final, kw=kw,
                  slot=slot, groups=groups, run=run, q_pos=q_pos):
            r, n = j // nb, j % nb
            q_row0 = pl.multiple_of(BLK + j * BLK, BLK)
            k_row0 = pl.multiple_of(j * BLK, BLK) if has_prev else q_row0
            runs = [pl.ds(pl.multiple_of((d * a + r) * BLK + run * n, run), run)
                    for a in range(groups)]
            load = lambda ref, g: jnp.concatenate([ref[g, s, :] for s in runs], axis=0)

            def store(ref, g, val):
                for a, s in enumerate(runs):
                    ref[g, s, :] = val[a * run:(a + 1) * run]

            cur_ok = col <= q_pos
            if has_prev:
                prev_ok = jnp.logical_and(col >= q_pos, n > 0)
                allowed = jnp.concatenate([prev_ok, cur_ok], axis=1)
            else:
                allowed = cur_ok
            allowed2 = jnp.concatenate([allowed, allowed], axis=0)
            lanes = [slice(g * pair_w, (g + 1) * pair_w) for g in range(n_pair)]
            scores = []
            for ls in lanes:
                q2 = buf[slot, 0, pl.ds(q_row0, BLK), ls]
                q_both = jnp.concatenate([jnp.where(low_head, q2, 0),
                                          jnp.where(low_head, 0, q2)], axis=0)
                k2 = buf[slot, 1, pl.ds(k_row0, kw), ls]
                scores.append(lax.dot_general(q_both, k2, (((1,), (1,)), ((), ())),
                                              preferred_element_type=F32))
            probs, maxes, sums = [], [], []
            for s in scores:
                s = jnp.where(allowed2, s, NEG_INF)
                m_h = jnp.max(s, axis=-1, keepdims=True)
                p = jnp.exp(s - m_h)
                sums.append(jnp.sum(p, axis=-1, keepdims=True))
                maxes.append(m_h)
                probs.append(p.astype(BF16))
            pv = [jnp.dot(p, buf[slot, 2, pl.ds(k_row0, kw), ls], preferred_element_type=F32)
                  for p, ls in zip(probs, lanes)]
            for g, ls in enumerate(lanes):
                acc = jnp.where(low_head, pv[g][:BLK], pv[g][BLK:])
                m_new = jnp.where(low_head, maxes[g][:BLK], maxes[g][BLK:])
                l_new = jnp.where(low_head, sums[g][:BLK], sums[g][BLK:])
                if not first:
                    m_old = load(m_s, g)
                    m_tot = jnp.maximum(m_old, m_new)
                    a_old = jnp.exp(m_old - m_tot)
                    a_new = jnp.exp(m_new - m_tot)
                    acc = load(acc_s, g) * a_old + acc * a_new
                    l_new = load(l_s, g) * a_old + l_new * a_new
                    m_new = m_tot
                if final:
                    o_blk = jnp.dot(unpermute, (acc / l_new).astype(BF16), preferred_element_type=F32)
                    o_ref[0, pl.ds(pl.multiple_of(j * BLK, BLK), BLK), ls] = o_blk.astype(o_ref.dtype)
                else:
                    store(acc_s, g, acc)
                    store(m_s, g, m_new)
                    store(l_s, g, l_new)
            return carry

        lax.fori_loop(0, n_blocks, block, 0, unroll=4)


def _attention(q, k, v):
    bsz, seq, _ = q[0].shape
    assert all(seq % (d * BLK) == 0 for d in DILATIONS) and DILATIONS[-1] == 1
    n_pair = ATTN_WIDTH // (2 * HEAD_DIM)
    return pl.pallas_call(
        functools.partial(_attn_kernel, seq=seq),
        grid=(bsz,),
        in_specs=[pl.BlockSpec(memory_space=pl.ANY)] * (3 * len(DILATIONS)),
        out_specs=pl.BlockSpec((1, seq, ATTN_WIDTH), lambda b: (b, 0, 0)),
        out_shape=jax.ShapeDtypeStruct((bsz, seq, ATTN_WIDTH), BF16),
        scratch_shapes=[pltpu.VMEM((2, 3, BLK + seq, ATTN_WIDTH), BF16),
                        pltpu.VMEM((n_pair, seq, 2 * HEAD_DIM), F32),
                        pltpu.VMEM((n_pair, seq, 2 * HEAD_DIM), F32),
                        pltpu.VMEM((n_pair, seq, 2 * HEAD_DIM), F32),
                        pltpu.SemaphoreType.DMA((2,))],
        compiler_params=_cparams(("arbitrary",)),
        name="attention",
    )(*q, *k, *v)


ROUTE_COLS = 4
EXPERT_LANE0 = N_GROUPS
MERGE_ROWS = 512


def _merge_kernel(o_ref, c_ref, ga_ref, gb_ref, x_ref, wa_ref, wc_ref, bc_ref, wo_ref,
                  gf_ref, wrh_ref, wrl_ref, br_ref,
                  x1_ref, h2_ref, route_ref, gate_ref, cnt_ref, tri_ref):
    tm = x_ref.shape[0]

    @pl.when(pl.program_id(0) == 0)
    def _():
        r = lax.broadcasted_iota(jnp.int32, (tm, tm), 0)
        c = lax.broadcasted_iota(jnp.int32, (tm, tm), 1)
        tri_ref[...] = (c < r).astype(BF16)

    logit_rows = []
    for r0 in range(0, tm, MERGE_ROWS):
        rows = slice(r0, r0 + MERGE_ROWS)
        y_a = jnp.dot(o_ref[rows, :], wa_ref[...], preferred_element_type=F32)
        y_b = jnp.dot(c_ref[rows, :], wc_ref[...], preferred_element_type=F32) + bc_ref[...]
        merged = (_sigmoid(ga_ref[rows, :].astype(F32)) * y_a
                  + _sigmoid(gb_ref[rows, :].astype(F32)) * y_b)
        x1 = x_ref[rows, :] + jnp.dot(merged.astype(BF16), wo_ref[...], preferred_element_type=F32)
        x1_ref[rows, :] = x1
        h2 = _rms(x1, gf_ref[...])
        _store_row_tiles(h2_ref.at[pl.ds(r0 * ROW_TILE, MERGE_ROWS * ROW_TILE), :], h2)
        h_hi = h2.astype(BF16)
        h_lo = (h2 - h_hi.astype(F32)).astype(BF16)
        logit_rows.append(jnp.dot(h_hi, wrh_ref[...], preferred_element_type=F32)
                          + jnp.dot(h_hi, wrl_ref[...], preferred_element_type=F32)
                          + jnp.dot(h_lo, wrh_ref[...], preferred_element_type=F32))
    logits = jnp.concatenate(logit_rows, axis=0) + br_ref[...]

    lane = lax.broadcasted_iota(jnp.int32, (tm, LANES), 1).astype(F32)
    no_lane = float(LANES)
    is_group = lane < N_GROUPS
    g_max = jnp.max(jnp.where(is_group, logits, -jnp.inf), axis=-1, keepdims=True)
    g_sel = jnp.min(jnp.where(jnp.logical_and(is_group, logits == g_max), lane, no_lane),
                    axis=-1, keepdims=True)
    p_group = 1.0 / jnp.sum(jnp.where(is_group, jnp.exp(logits - g_max), 0.0),
                            axis=-1, keepdims=True)
    lo = EXPERT_LANE0 + EXPERTS_PER_GROUP * g_sel
    in_sel = jnp.logical_and(lane >= lo, lane < lo + EXPERTS_PER_GROUP)
    cand = jnp.where(in_sel, logits, -jnp.inf)
    v1 = jnp.max(cand, axis=-1, keepdims=True)
    i1 = jnp.min(jnp.where(jnp.logical_and(in_sel, cand == v1), lane, no_lane), axis=-1, keepdims=True)
    rest = jnp.logical_and(in_sel, lane != i1)
    cand2 = jnp.where(rest, logits, -jnp.inf)
    v2 = jnp.max(cand2, axis=-1, keepdims=True)
    i2 = jnp.min(jnp.where(jnp.logical_and(rest, cand2 == v2), lane, no_lane), axis=-1, keepdims=True)
    e2 = jnp.exp(v2 - v1)
    gate0 = p_group / (1.0 + e2)
    gate1 = p_group * e2 / (1.0 + e2)

    oh0 = (lane == i1)
    oh1 = (lane == i2)
    before0 = jnp.dot(tri_ref[...], oh0.astype(BF16), preferred_element_type=F32)
    before1 = jnp.dot(tri_ref[...], oh1.astype(BF16), preferred_element_type=F32)
    tot0 = jnp.sum(oh0.astype(F32), axis=0, keepdims=True)
    tot1 = jnp.sum(oh1.astype(F32), axis=0, keepdims=True)
    cnt = tot0 + tot1
    rank0 = jnp.sum(jnp.where(oh0, before0, 0.0), axis=-1, keepdims=True)
    rank1 = jnp.sum(jnp.where(oh1, tot0 + before1, 0.0), axis=-1, keepdims=True)

    rc = lax.broadcasted_iota(jnp.int32, (tm, ROUTE_COLS), 1)
    route = jnp.where(rc == 0, i1 - EXPERT_LANE0,
                      jnp.where(rc == 1, i2 - EXPERT_LANE0, jnp.where(rc == 2, rank0, rank1)))
    route_ref[...] = route.astype(jnp.int32)
    gc = lax.broadcasted_iota(jnp.int32, (tm, TOP_K), 1)
    gate_ref[...] = jnp.where(gc == 0, gate0, gate1)
    cnt_ref[0] = jnp.broadcast_to(pltpu.roll(cnt, LANES - EXPERT_LANE0, 1), (8, LANES))


def _merge(o, c, ga, gb, x, wa, wc, bc, wo, gf, wr_hi, wr_lo, br, tm):
    t = x.shape[0]
    nt = t // tm
    row = lambda i: (i, 0)
    const = lambda i: (0, 0)
    full = lambda a: pl.BlockSpec(a.shape, const)
    return pl.pallas_call(
        _merge_kernel,
        grid=(nt,),
        in_specs=[
            pl.BlockSpec((tm, ATTN_WIDTH), row),
            pl.BlockSpec((tm, CONV_WIDTH), row),
            pl.BlockSpec((tm, D_MODEL), row),
            pl.BlockSpec((tm, D_MODEL), row),
            pl.BlockSpec((tm, D_MODEL), row),
            full(wa), full(wc), full(bc), full(wo), full(gf), full(wr_hi), full(wr_lo), full(br),
        ],
        out_specs=(
            pl.BlockSpec((tm, D_MODEL), row),
            pl.BlockSpec((tm * ROW_TILE, LANES), row),
            pl.BlockSpec((tm, ROUTE_COLS), row),
            pl.BlockSpec((tm, TOP_K), row),
            pl.BlockSpec((1, 8, LANES), lambda i: (i, 0, 0)),
        ),
        out_shape=(
            jax.ShapeDtypeStruct((t, D_MODEL), F32),
            jax.ShapeDtypeStruct((t * ROW_TILE, LANES), F32),
            jax.ShapeDtypeStruct((t, ROUTE_COLS), jnp.int32),
            jax.ShapeDtypeStruct((t, TOP_K), F32),
            jax.ShapeDtypeStruct((nt, 8, LANES), F32),
        ),
        scratch_shapes=[pltpu.VMEM((tm, tm), BF16)],
        compiler_params=_cparams(("arbitrary",)),
        name="merge",
    )(o, c, ga, gb, x, wa, wc, bc, wo, gf, wr_hi, wr_lo, br)


def _pos_kernel(route_ref, base_ref, pos_ref):
    tm = route_ref.shape[0]
    route = route_ref[...].astype(F32)
    lane = lax.broadcasted_iota(jnp.int32, (tm, LANES), 1).astype(F32)
    cols = []
    for k in range(TOP_K):
        start = jnp.sum(jnp.where(lane == route[:, k:k + 1], base_ref[0, 0:1, :], 0.0),
                        axis=-1, keepdims=True)
        cols.append(start + route[:, TOP_K + k:TOP_K + k + 1])
    sel = lax.broadcasted_iota(jnp.int32, (tm, TOP_K), 1)
    pos_ref[...] = jnp.where(sel == 0, cols[0], cols[1]).astype(jnp.int32)


def _positions(route, base, tm):
    nt = base.shape[0]
    base_rows = jnp.zeros((nt, SUBLANES, LANES), F32).at[:, :, :N_EXPERTS].set(
        base.astype(F32)[:, None, :])
    return pl.pallas_call(
        _pos_kernel,
        grid=(nt,),
        in_specs=[pl.BlockSpec((tm, ROUTE_COLS), lambda i: (i, 0)),
                  pl.BlockSpec((1, SUBLANES, LANES), lambda i: (i, 0, 0))],
        out_specs=pl.BlockSpec((tm, TOP_K), lambda i: (i, 0)),
        out_shape=jax.ShapeDtypeStruct((route.shape[0], TOP_K), jnp.int32),
        compiler_params=_cparams(("parallel",)),
        name="positions",
    )(route, base_rows)


def _sort_kernel(pos_ref, h_ref, xs_ref):
    tm = h_ref.shape[0] // ROW_TILE

    def place(t, carry):
        row = h_ref[pl.ds(pl.multiple_of(t * ROW_TILE, ROW_TILE), ROW_TILE), :]
        for k in range(TOP_K):
            pos = pos_ref[TOP_K * t + k]
            xs_ref[pl.ds(pl.multiple_of(pos * ROW_TILE, ROW_TILE), ROW_TILE), :] = row
        return carry

    lax.fori_loop(0, tm, place, 0, unroll=8)


def _tile_sort(pos_flat, h2, tm):
    nt = h2.shape[0] // (tm * ROW_TILE)
    return pl.pallas_call(
        _sort_kernel,
        grid=(nt,),
        in_specs=[pl.BlockSpec((TOP_K * tm,), lambda i: (i,), memory_space=pltpu.SMEM),
                  pl.BlockSpec((tm * ROW_TILE, LANES), lambda i: (i, 0))],
        out_specs=pl.BlockSpec((TOP_K * tm * ROW_TILE, LANES), lambda i: (i, 0)),
        out_shape=jax.ShapeDtypeStruct((TOP_K * h2.shape[0], LANES), F32),
        compiler_params=_cparams(("parallel",)),
        name="tile_sort",
    )(pos_flat, h2)


RARE_RUN = 64
BLOCKS_PER_STEP = 2


def _moe_kernel(be_ref, bj_ref, ilo_ref, ihi_ref, valid_ref, cum_ref, cnt_ref, off_ref, nused_ref,
                next_e_ref, xs_hbm, wg_hbm, wu_hbm, wd_hbm, ys_hbm, trash_hbm,
                xbuf, obuf, wg_s, wu_s, wd_s, wg_f, wu_f, wd_f, wslot, sem_in, sem_out, sem_w,
                *, tile_rows, layer):
    step = pl.program_id(0)
    rows8 = lambda r, n: pl.ds(pl.multiple_of(r * ROW_TILE, ROW_TILE), n * ROW_TILE)

    def for_runs(bb, fn):
        e = be_ref[bb]
        lo_blk = bj_ref[bb] * MOE_BLOCK

        def body(i, carry):
            c = cum_ref[i * N_EXPERTS + e]
            lo = jnp.maximum(c, lo_blk)
            hi = jnp.minimum(c + cnt_ref[i * N_EXPERTS + e], lo_blk + MOE_BLOCK)
            fn(i * tile_rows + off_ref[i * N_EXPERTS + e] + (lo - c), lo - lo_blk,
               jnp.maximum(hi - lo, 0))
            return carry

        lax.fori_loop(ilo_ref[bb], ihi_ref[bb] + 1, body, 0)

    def for_chunks(length, fn):
        def piece(size):
            @pl.when((length & size) != 0)
            def _():
                fn(length & ~(2 * size - 1), size)

        sizes = [1 << bit for bit in reversed(range(MOE_BLOCK.bit_length()))]

        @pl.when(length >= RARE_RUN)
        def _():
            for size in sizes:
                if size >= RARE_RUN:
                    piece(size)

        for size in sizes:
            if size < RARE_RUN:
                piece(size)

    def gather(bb, s):
        def run(src, dst, length):
            for_chunks(length, lambda o, n: pltpu.make_async_copy(
                xs_hbm.at[rows8(src + o, n)], xbuf.at[s, rows8(dst + o, n)], sem_in.at[s]).start())
        for_runs(bb, run)
        v = valid_ref[bb]
        for_chunks(MOE_BLOCK - v, lambda o, n: pltpu.make_async_copy(
            xs_hbm.at[rows8(o, n)], xbuf.at[s, rows8(v + o, n)], sem_in.at[s]).start())

    def scatter(bb, s):
        def run(src, dst, length):
            for_chunks(length, lambda o, n: pltpu.make_async_copy(
                obuf.at[s, rows8(dst + o, n)], ys_hbm.at[rows8(src + o, n)], sem_out.at[s]).start())
        for_runs(bb, run)
        v = valid_ref[bb]
        for_chunks(MOE_BLOCK - v, lambda o, n: pltpu.make_async_copy(
            obuf.at[s, rows8(v + o, n)], trash_hbm.at[rows8(s * MOE_BLOCK + v + o, n)],
            sem_out.at[s]).start())

    def wait_gather(s):
        pltpu.make_async_copy(xs_hbm.at[rows8(0, MOE_BLOCK)], xbuf.at[s], sem_in.at[s]).wait()

    def wait_scatter(s):
        pltpu.make_async_copy(obuf.at[s], ys_hbm.at[rows8(0, MOE_BLOCK)], sem_out.at[s]).wait()

    def weight_copies(expert, s):
        return [pltpu.make_async_copy(src.at[layer, expert], dst.at[s], sem_w.at[s])
                for src, dst in ((wg_hbm, wg_f), (wu_hbm, wu_f), (wd_hbm, wd_f))]

    n_used = nused_ref[0]

    def one_block(b, slot):
        other = 1 - slot
        live = b < n_used
        e = be_ref[b]
        e_prev = be_ref[jnp.maximum(b - 1, 0)]

        if slot == 0:
            @pl.when(b == 0)
            def _():
                obuf[...] = jnp.zeros_like(obuf)
                for s in range(2):
                    init = pltpu.make_async_copy(
                        obuf.at[s], trash_hbm.at[rows8(s * MOE_BLOCK, MOE_BLOCK)], sem_out.at[s])
                    init.start()
                    init.wait()
                gather(0, 0)
                wslot[0] = 0
                for c in weight_copies(e, 0):
                    c.start()

        @pl.when(live)
        def _():
            wait_gather(slot)

        @pl.when(jnp.logical_and(live, b >= 2))
        def _():
            wait_scatter(slot)

        @pl.when(jnp.logical_and(live, jnp.logical_or(b == 0, e != e_prev)))
        def _():
            s = wslot[0]
            for c in weight_copies(e, s):
                c.wait()
            wg_s[...] = wg_f[s].astype(BF16)
            wu_s[...] = wu_f[s].astype(BF16)
            wd_s[...] = wd_f[s].astype(BF16)
            e_next = next_e_ref[b]

            @pl.when(e_next != e)
            def _():
                for c in weight_copies(e_next, 1 - s):
                    c.start()
            wslot[0] = 1 - s

        @pl.when(b + 1 < n_used)
        def _():
            gather(b + 1, other)

        @pl.when(jnp.logical_and(live, b >= 1))
        def _():
            scatter(b - 1, other)

        @pl.when(live)
        def _():
            x = _load_row_tiles(xbuf.at[slot], MOE_BLOCK).astype(BF16)
            hg = jnp.dot(x, wg_s[...], preferred_element_type=F32)
            hu = jnp.dot(x, wu_s[...], preferred_element_type=F32)
            hb = (hg * _sigmoid(hg) * hu).astype(BF16)
            _store_row_tiles(obuf.at[slot], jnp.dot(hb, wd_s[...], preferred_element_type=F32))

        @pl.when(b == n_used - 1)
        def _():
            scatter(b, slot)
            wait_scatter(other)
            wait_scatter(slot)

    for sub in range(BLOCKS_PER_STEP):
        one_block(step * BLOCKS_PER_STEP + sub, sub)


def _moe(plan, xs, w_gate, w_up, w_down, layer, n_blocks, tile_rows):
    n_pre = len(plan)
    assert n_blocks % BLOCKS_PER_STEP == 0
    grid_spec = pltpu.PrefetchScalarGridSpec(
        num_scalar_prefetch=n_pre,
        grid=(n_blocks // BLOCKS_PER_STEP,),
        in_specs=[pl.BlockSpec(memory_space=pl.ANY)] * 4,
        out_specs=(pl.BlockSpec(memory_space=pl.ANY), pl.BlockSpec(memory_space=pl.ANY)),
        scratch_shapes=[pltpu.VMEM((2, MOE_BLOCK * ROW_TILE, LANES), F32),
                        pltpu.VMEM((2, MOE_BLOCK * ROW_TILE, LANES), F32),
                        pltpu.VMEM((D_MODEL, EXPERT_FF), BF16),
                        pltpu.VMEM((D_MODEL, EXPERT_FF), BF16),
                        pltpu.VMEM((EXPERT_FF, D_MODEL), BF16),
                        pltpu.VMEM((2, D_MODEL, EXPERT_FF), F32),
                        pltpu.VMEM((2, D_MODEL, EXPERT_FF), F32),
                        pltpu.VMEM((2, EXPERT_FF, D_MODEL), F32),
                        pltpu.SMEM((1,), jnp.int32),
                        pltpu.SemaphoreType.DMA((2,)),
                        pltpu.SemaphoreType.DMA((2,)),
                        pltpu.SemaphoreType.DMA((2,))],
    )
    ys, _ = pl.pallas_call(
        functools.partial(_moe_kernel, tile_rows=tile_rows, layer=layer),
        grid_spec=grid_spec,
        out_shape=(jax.ShapeDtypeStruct(xs.shape, F32),
                   jax.ShapeDtypeStruct((2 * MOE_BLOCK * ROW_TILE, LANES), F32)),
        compiler_params=_cparams(("arbitrary",)),
        name="moe",
    )(*plan, xs, w_gate, w_up, w_down)
    return ys


def _ple_kernel(pos_ref, gate_ref, ys_ref, x1_ref, p_ref, gp_ref, wpg_ref, wpe_ref, gfin_ref,
                out_ref, moe_ref, *, final_norm):
    tm = x1_ref.shape[0]
    tile = lambda r: pl.ds(pl.multiple_of(r * ROW_TILE, ROW_TILE), ROW_TILE)

    def combine(t, carry):
        acc = None
        for k in range(TOP_K):
            term = gate_ref[TOP_K * t + k] * ys_ref[tile(pos_ref[TOP_K * t + k]), :]
            acc = term if acc is None else acc + term
        moe_ref[tile(t), :] = acc
        return carry

    lax.fori_loop(0, tm, combine, 0, unroll=8)
    x2 = x1_ref[...] + _load_row_tiles(moe_ref, tm)
    hp = _rms(x2, gp_ref[...]).astype(BF16)
    g_ple = _sigmoid(jnp.dot(hp, wpg_ref[...], preferred_element_type=F32))
    emb = jnp.dot(p_ref[...].astype(BF16), wpe_ref[...], preferred_element_type=F32)
    x3 = x2 + g_ple * emb
    if final_norm:
        x3 = _rms(x3, gfin_ref[...])
    out_ref[...] = x3


def _ple(pos, gate_flat, ys, x1, p, gp, wpg, wpe, gfin, tm, final_norm):
    t = x1.shape[0]
    row = lambda i: (i, 0)
    const = lambda i: (0, 0)
    full = lambda a: pl.BlockSpec(a.shape, const)
    smem = lambda n: pl.BlockSpec((n,), lambda i: (i,), memory_space=pltpu.SMEM)
    return pl.pallas_call(
        functools.partial(_ple_kernel, final_norm=final_norm),
        grid=(t // tm,),
        in_specs=[
            smem(TOP_K * tm),
            smem(TOP_K * tm),
            pl.BlockSpec((TOP_K * tm * ROW_TILE, LANES), row),
            pl.BlockSpec((tm, D_MODEL), row),
            pl.BlockSpec((tm, PLE_DIM), row),
            full(gp), full(wpg), full(wpe), full(gfin),
        ],
        out_specs=pl.BlockSpec((tm, D_MODEL), row),
        out_shape=jax.ShapeDtypeStruct((t, D_MODEL), F32),
        scratch_shapes=[pltpu.VMEM((tm * ROW_TILE, LANES), F32)],
        compiler_params=_cparams(("parallel",)),
        name="ple",
    )(pos, gate_flat, ys, x1, p, gp, wpg, wpe, gfin)


def _rope_tables(seq):
    inv = jnp.power(ROPE_THETA, -jnp.arange(0, HEAD_DIM, 2, dtype=F32) / HEAD_DIM)
    ang = jnp.arange(seq, dtype=F32)[:, None] * inv[None, :]
    cos, sin = jnp.cos(ang), jnp.sin(ang)
    cos_h = jnp.concatenate([cos, cos], axis=-1)
    sin_h = jnp.concatenate([-sin, sin], axis=-1)
    reps = LANES // HEAD_DIM
    return jnp.tile(cos_h, (1, reps)), jnp.tile(sin_h, (1, reps))


def _dispatch_plan(route_counts, group, n_blocks):
    i32 = jnp.int32
    grouped = route_counts.reshape(-1, group, N_EXPERTS)
    counts = jnp.sum(grouped, axis=1)
    base = ((jnp.cumsum(counts, axis=1) - counts)[:, None, :]
            + jnp.cumsum(grouped, axis=1) - grouped).reshape(-1, N_EXPERTS).astype(i32)
    total = jnp.sum(counts, axis=0)
    blocks_of = (total + MOE_BLOCK - 1) // MOE_BLOCK
    block_end = jnp.cumsum(blocks_of)
    bidx = jnp.arange(n_blocks, dtype=i32)
    block_e = jnp.minimum(jnp.sum((block_end[None, :] <= bidx[:, None]).astype(i32), axis=1),
                          N_EXPERTS - 1)
    is_e = (block_e[:, None] == jnp.arange(N_EXPERTS, dtype=i32)[None, :]).astype(i32)
    pick = lambda per_expert: jnp.sum(is_e * per_expert[None, :], axis=1)
    pick_tiles = lambda tab: jnp.sum(is_e[:, None, :] * tab[None, :, :], axis=2)
    block_j = bidx - pick(block_end - blocks_of)
    used = bidx < block_end[-1]
    cum = jnp.cumsum(counts, axis=0) - counts
    off = jnp.cumsum(counts, axis=1) - counts
    lo = block_j * MOE_BLOCK
    hi = jnp.minimum(lo + MOE_BLOCK, pick(total))
    run_start = pick_tiles(cum)
    run_end = run_start + pick_tiles(counts)
    tile_lo = jnp.sum((run_end <= lo[:, None]).astype(i32), axis=1)
    tile_hi = jnp.sum((run_start < hi[:, None]).astype(i32), axis=1) - 1
    valid = jnp.clip(hi - lo, 0, MOE_BLOCK)
    tile_lo = jnp.where(used, tile_lo, 0)
    tile_hi = jnp.where(used, tile_hi, -1)
    valid = jnp.where(used, valid, 0)
    e_ids = jnp.arange(N_EXPERTS, dtype=i32)
    later = jnp.where((e_ids[None, :] > e_ids[:, None]) & (blocks_of[None, :] > 0), e_ids[None, :], N_EXPERTS)
    next_used = jnp.min(later, axis=1)
    next_e = pick(jnp.where(next_used == N_EXPERTS, e_ids, next_used))
    flat = lambda a: a.reshape(-1).astype(i32)
    return base, (flat(block_e), flat(block_j), flat(tile_lo), flat(tile_hi), flat(valid),
                  flat(cum), flat(counts), flat(off), flat(block_end[-1:]), flat(next_e))


def kernel(x, p, norm_mix, w_in, w_dw, b_dw, ln_conv_g, ln_conv_b, w_attn_out, w_conv_out,
           b_conv_out, w_out, norm_ffn, w_route_group, b_route_group, w_route_expert,
           b_route_expert, w_exp_gate, w_exp_up, w_exp_down, norm_ple, w_ple_proj, w_ple_gate,
           norm_final):
    bsz, seq, d = x.shape
    depth = w_in.shape[0]
    t = bsz * seq
    tm = 512
    sort_tm = 1024
    n_assign = t * TOP_K
    assert n_assign > MOE_BLOCK
    n_rows = (-(-n_assign // MOE_BLOCK)) * MOE_BLOCK + N_EXPERTS * MOE_BLOCK
    cos, sin = _rope_tables(seq)
    row2 = lambda a: a.reshape(1, -1)

    xf = x.reshape(t, d)
    for i in range(depth):
        n_qkv = 3 * ATTN_WIDTH
        q, k, v = _inproj(xf, row2(norm_mix[i]), w_in[i, :, :n_qkv].astype(BF16), cos, sin,
                          bsz, seq, tm)
        c, ga, gb = _gateproj(xf, row2(norm_mix[i]), w_in[i, :, n_qkv:].astype(BF16), w_dw[i],
                              row2(b_dw[i]), row2(ln_conv_g[i]), row2(ln_conv_b[i]), seq, tm)
        o = _attention(q, k, v).reshape(t, ATTN_WIDTH)

        w_r = jnp.zeros((d, LANES), F32)
        w_r = w_r.at[:, :N_GROUPS].set(w_route_group[i])
        w_r = w_r.at[:, EXPERT_LANE0:EXPERT_LANE0 + N_EXPERTS].set(w_route_expert[i])
        w_r_hi = w_r.astype(BF16)
        w_r_lo = (w_r - w_r_hi.astype(F32)).astype(BF16)
        b_r = jnp.zeros((1, LANES), F32)
        b_r = b_r.at[0, :N_GROUPS].set(b_route_group[i])
        b_r = b_r.at[0, EXPERT_LANE0:EXPERT_LANE0 + N_EXPERTS].set(b_route_expert[i])

        x1, h2, route, gate, cnt = _merge(
            o, c, ga, gb, xf, w_attn_out[i].astype(BF16), w_conv_out[i].astype(BF16),
            row2(b_conv_out[i]), w_out[i].astype(BF16), row2(norm_ffn[i]), w_r_hi, w_r_lo, b_r, tm)

        counts = cnt[:, 0, :N_EXPERTS].astype(jnp.int32)
        base, plan = _dispatch_plan(counts, sort_tm // tm, n_rows // MOE_BLOCK)
        pos = _positions(route, base, tm).reshape(-1)
        xs = _tile_sort(pos, h2, sort_tm)
        ys = _moe(plan, xs, w_exp_gate, w_exp_up, w_exp_down, i, n_rows // MOE_BLOCK, TOP_K * sort_tm)
        xf = _ple(pos, gate.reshape(-1), ys, x1, p[i].reshape(t, PLE_DIM), row2(norm_ple[i]),
                  w_ple_gate[i].astype(BF16), w_ple_proj[i].astype(BF16), row2(norm_final),
                  sort_tm, final_norm=(i == depth - 1))
    return xf.reshape(bsz, seq, d)
```

```python
import functools

import jax
import jax.numpy as jnp
from jax import lax
from jax.experimental import pallas as pl
from jax.experimental.pallas import tpu as pltpu

D_MODEL = 1024
N_HEADS = 8
HEAD_DIM = 64
ATTN_WIDTH = N_HEADS * HEAD_DIM
CONV_WIDTH = 512
CONV_K = 31
DILATIONS = (16, 4, 1)
BLK = 128
ROPE_THETA = 10000.0
N_GROUPS = 4
EXPERTS_PER_GROUP = 8
N_EXPERTS = N_GROUPS * EXPERTS_PER_GROUP
EXPERT_FF = 512
TOP_K = 2
MOE_BLOCK = 256
PLE_DIM = 256
EPS = 1e-6
NEG_INF = -1e30

LANES = 128
SUBLANES = 8
ROW_TILE = D_MODEL // LANES
VMEM_LIMIT = 56 * 1024 * 1024

F32 = jnp.float32
BF16 = jnp.bfloat16


def _cparams(sem):
    return pltpu.CompilerParams(dimension_semantics=sem, vmem_limit_bytes=VMEM_LIMIT)


def _rms(x, g):
    return x * lax.rsqrt(jnp.mean(x * x, axis=-1, keepdims=True) + EPS) * g


def _sigmoid(x):
    return 0.5 * jnp.tanh(0.5 * x) + 0.5


def _store_row_tiles(ref, x):
    rows = x.shape[0]
    for s in range(ROW_TILE):
        ref[pl.ds(s, rows, stride=ROW_TILE), :] = x[:, s * LANES:(s + 1) * LANES]


def _load_row_tiles(ref, rows):
    return jnp.concatenate([ref[pl.ds(s, rows, stride=ROW_TILE), :] for s in range(ROW_TILE)], axis=1)


QKV_ROWS = 256


def _inproj_kernel(x_ref, g_ref, w_ref, cos_ref, sin_ref, *out_refs):
    n_lay = len(DILATIONS)
    q_refs, k_refs, v_refs = (out_refs[a * n_lay:(a + 1) * n_lay] for a in range(3))
    slab_refs = out_refs[3 * n_lay:]
    tm = x_ref.shape[0]
    n_slab = ATTN_WIDTH // LANES

    def put(ref, d, r, row0, g, val):
        cols = slice(g * LANES, (g + 1) * LANES)
        if d == 1:
            ref[pl.ds(row0, val.shape[0]), cols] = val.astype(BF16)
        else:
            ref[0, r, pl.ds(row0, val.shape[0]), cols] = val.astype(BF16)

    def fill(raw, rows, refs, slabs, is_query, post):
        for g in range(n_slab):
            cols = slice(g * LANES, (g + 1) * LANES)
            t = post(raw[:, cols], rows)
            slabs[g, rows, :] = t
            for d, ref in zip(DILATIONS, refs):
                if d == 1 and not is_query:
                    ref[rows, cols] = t.astype(BF16)

    def relayout(refs, slabs, by4, is_query):
        ref16, ref4, ref1 = refs
        quarter = tm // 4
        class4 = lambda r: pl.multiple_of(r * quarter, quarter)

        def to_by4(r, carry):
            for g in range(n_slab):
                val = slabs[g, pl.ds(r, quarter, stride=4), :]
                by4[g, pl.ds(class4(r), quarter), :] = val
                if not is_query:
                    put(ref4, 4, r, 0, g, val)
            return carry
        lax.fori_loop(0, 4, to_by4, 0, unroll=2)

        def to_16(r, carry):
            for g in range(n_slab):
                put(ref16, 16, r, 0, g, by4[g, pl.ds(class4(r % 4) + r // 4, tm // 16, stride=4), :])
            return carry
        lax.fori_loop(0, 16, to_16, 0, unroll=4)

        if is_query:
            def q4_class(r, carry):
                for bl in range(quarter // BLK):
                    for a in range(4):
                        src = pl.ds(class4(r) + bl * BLK + a, BLK // 4, stride=4)
                        for g in range(n_slab):
                            put(ref4, 4, r, bl * BLK + a * (BLK // 4), g, by4[g, src, :])
                return carry
            lax.fori_loop(0, 4, q4_class, 0, unroll=2)

            def q1_block(bl, carry):
                for a in range(16):
                    src = pl.ds((a % 4) * quarter + (BLK // 4) * bl + a // 4, BLK // 16, stride=4)
                    for g in range(n_slab):
                        put(ref1, 1, 0, pl.multiple_of(bl * BLK, BLK) + a * (BLK // 16), g, by4[g, src, :])
                return carry
            lax.fori_loop(0, tm // BLK, q1_block, 0, unroll=2)

    lane = lax.broadcasted_iota(jnp.int32, (QKV_ROWS, LANES), 1)
    first_half = (lane % HEAD_DIM) < (HEAD_DIM // 2)

    def rope(t, rows):
        partner = jnp.where(first_half,
                            pltpu.roll(t, LANES - HEAD_DIM // 2, 1),
                            pltpu.roll(t, HEAD_DIM // 2, 1))
        return t * cos_ref[rows, :] + partner * sin_ref[rows, :]

    for r0 in range(0, tm, QKV_ROWS):
        rows = slice(r0, r0 + QKV_ROWS)
        h = _rms(x_ref[rows, :], g_ref[...]).astype(BF16)
        proj = lambda a: jnp.dot(h, w_ref[:, a * ATTN_WIDTH:(a + 1) * ATTN_WIDTH],
                                 preferred_element_type=F32)
        fill(proj(0), rows, q_refs, slab_refs[0], True, lambda t, rw: rope(t, rw) * (HEAD_DIM ** -0.5))
        fill(proj(1), rows, k_refs, slab_refs[1], False, rope)
        fill(proj(2), rows, v_refs, slab_refs[2], False, lambda t, rw: t)
    assert DILATIONS == (16, 4, 1) and tm % (4 * BLK) == 0
    relayout(q_refs, slab_refs[0], slab_refs[3], True)
    relayout(k_refs, slab_refs[1], slab_refs[4], False)
    relayout(v_refs, slab_refs[2], slab_refs[5], False)


CONV_HALO = 32
CONV_ROWS = 32


def _gateproj_kernel(x_ref, g_ref, w_ref, cw_ref, cb_ref, lg_ref, lb_ref, c_ref, ga_ref, gb_ref,
                     ext_ref, halo_ref, *, tiles_per_seq):
    tm = x_ref.shape[0]
    h = _rms(x_ref[...], g_ref[...]).astype(BF16)

    def proj(c0, width):
        return jnp.dot(h, w_ref[:, c0:c0 + width], preferred_element_type=F32)

    u = proj(0, CONV_WIDTH) * _sigmoid(proj(CONV_WIDTH, CONV_WIDTH))
    first_of_sequence = pl.program_id(0) % tiles_per_seq == 0
    ext_ref[0, 0:CONV_HALO, :] = jnp.where(first_of_sequence, 0.0, halo_ref[...])
    ext_ref[0, CONV_HALO:CONV_HALO + tm, :] = u
    halo_ref[...] = u[tm - CONV_HALO:, :]
    c0 = 2 * CONV_WIDTH
    ga_ref[...] = proj(c0, D_MODEL).astype(ga_ref.dtype)
    gb_ref[...] = proj(c0 + D_MODEL, D_MODEL).astype(gb_ref.dtype)
    _conv_rows(ext_ref, cw_ref, cb_ref, lg_ref, lb_ref, c_ref, tm)


def _conv_rows(ext_ref, w_ref, b_ref, g_ref, beta_ref, out_ref, ts):
    first = CONV_HALO - (CONV_K - 1)
    n_shift = ext_ref.shape[0]
    rows = CONV_HALO + ts
    ext0 = ext_ref[0]
    for s in range(1, n_shift):
        ext_ref[s] = pltpu.roll(ext0, rows - s, 0)
    bias = b_ref[...]
    gamma = g_ref[...]
    beta = beta_ref[...]

    for r0 in range(0, ts, CONV_ROWS):
        acc = jnp.zeros((CONV_ROWS, CONV_WIDTH), F32) + bias
        for j in range(CONV_K):
            lo = first + j + r0
            s, lo = lo % n_shift, lo - lo % n_shift
            tap = jnp.concatenate([w_ref[j]] * (CONV_ROWS // SUBLANES), axis=0)
            acc = acc + tap * ext_ref[s, lo:lo + CONV_ROWS, :]
        mu = jnp.mean(acc, axis=-1, keepdims=True)
        cen = acc - mu
        var = jnp.mean(cen * cen, axis=-1, keepdims=True)
        y = cen * lax.rsqrt(var + EPS) * gamma + beta
        out_ref[r0:r0 + CONV_ROWS, :] = (y * _sigmoid(y)).astype(out_ref.dtype)


def _gateproj(x, gain, w_rest, w_dw, b_dw, ln_g, ln_b, seq, tm):
    t = x.shape[0]
    row = lambda i: (i, 0)
    const = lambda i: (0, 0)
    taps = jnp.broadcast_to(w_dw[:, None, :], (CONV_K, SUBLANES, CONV_WIDTH))
    return pl.pallas_call(
        functools.partial(_gateproj_kernel, tiles_per_seq=seq // tm),
        grid=(t // tm,),
        in_specs=[pl.BlockSpec((tm, D_MODEL), row), pl.BlockSpec((1, D_MODEL), const),
                  pl.BlockSpec(w_rest.shape, const),
                  pl.BlockSpec(taps.shape, lambda i: (0, 0, 0)),
                  pl.BlockSpec((1, CONV_WIDTH), const), pl.BlockSpec((1, CONV_WIDTH), const),
                  pl.BlockSpec((1, CONV_WIDTH), const)],
        out_specs=(pl.BlockSpec((tm, CONV_WIDTH), row), pl.BlockSpec((tm, D_MODEL), row),
                   pl.BlockSpec((tm, D_MODEL), row)),
        out_shape=(jax.ShapeDtypeStruct((t, CONV_WIDTH), BF16),
                   jax.ShapeDtypeStruct((t, D_MODEL), BF16),
                   jax.ShapeDtypeStruct((t, D_MODEL), BF16)),
        scratch_shapes=[pltpu.VMEM((SUBLANES, CONV_HALO + tm, CONV_WIDTH), F32),
                        pltpu.VMEM((CONV_HALO, CONV_WIDTH), F32)],
        compiler_params=_cparams(("arbitrary",)),
        name="gateproj",
    )(x, gain, w_rest, taps, b_dw, ln_g, ln_b)


def _inproj(x, gain, w_in, cos, sin, bsz, seq, tm):
    t = x.shape[0]
    n_pos = seq // tm
    row = lambda i: (i, 0)
    const = lambda i: (0, 0)
    qkv_shapes, qkv_specs = [], []
    for d in DILATIONS:
        if d == 1:
            qkv_shapes.append(jax.ShapeDtypeStruct((t, ATTN_WIDTH), BF16))
            qkv_specs.append(pl.BlockSpec((tm, ATTN_WIDTH), row))
        else:
            qkv_shapes.append(jax.ShapeDtypeStruct((bsz, d, seq // d, ATTN_WIDTH), BF16))
            qkv_specs.append(pl.BlockSpec((1, d, tm // d, ATTN_WIDTH),
                                          lambda i: (i // n_pos, 0, i % n_pos, 0)))
    out_shape = tuple(qkv_shapes * 3)
    out_specs = tuple(qkv_specs * 3)
    outs = pl.pallas_call(
        _inproj_kernel,
        grid=(t // tm,),
        in_specs=[
            pl.BlockSpec((tm, D_MODEL), row),
            pl.BlockSpec((1, D_MODEL), const),
            pl.BlockSpec(w_in.shape, const),
            pl.BlockSpec((tm, LANES), lambda i: (i % n_pos, 0)),
            pl.BlockSpec((tm, LANES), lambda i: (i % n_pos, 0)),
        ],
        out_specs=out_specs,
        out_shape=out_shape,
        scratch_shapes=[pltpu.VMEM((ATTN_WIDTH // LANES, tm, LANES), F32) for _ in range(6)],
        compiler_params=_cparams(("parallel",)),
        name="inproj",
    )(x, gain, w_in, cos, sin)
    n_lay = len(DILATIONS)
    as_rows = lambda a: a.reshape(bsz, seq, ATTN_WIDTH)
    return tuple([as_rows(a) for a in outs[j * n_lay:(j + 1) * n_lay]] for j in range(3))


STATE_DILATION = max(DILATIONS)


def _query_order(idx, groups):
    run = BLK // groups
    return groups * (idx % run) + idx // run


def _attn_kernel(*refs, seq):
    n_lay = len(DILATIONS)
    srcs = refs[:3 * n_lay]
    o_ref, buf, acc_s, m_s, l_s, sem = refs[3 * n_lay:]
    b = pl.program_id(0)
    n_b = pl.num_programs(0)
    n_blocks = seq // BLK
    pair_w = 2 * HEAD_DIM
    n_pair = ATTN_WIDTH // pair_w

    def load_sweep(lay, bi, slot):
        return [pltpu.make_async_copy(srcs[a * n_lay + lay].at[bi], buf.at[slot, a, pl.ds(BLK, seq)],
                                      sem.at[slot]) for a in range(3)]

    @pl.when(b == 0)
    def _():
        buf[:, :, 0:BLK, :] = jnp.zeros((2, 3, BLK, ATTN_WIDTH), BF16)
        for c in load_sweep(0, 0, 0):
            c.start()

    row = lax.broadcasted_iota(jnp.int32, (BLK, BLK), 0)
    col = lax.broadcasted_iota(jnp.int32, (BLK, BLK), 1)
    lane = lax.broadcasted_iota(jnp.int32, (BLK, pair_w), 1)
    low_head = lane < HEAD_DIM
    unpermute = (row == _query_order(col, STATE_DILATION // DILATIONS[-1])).astype(BF16)

    for lay, d in enumerate(DILATIONS):
        slot = (b * n_lay + lay) % 2
        for c in load_sweep(lay, b, slot):
            c.wait()
        if lay + 1 < n_lay:
            for c in load_sweep(lay + 1, b, 1 - slot):
                c.start()
        else:
            @pl.when(b + 1 < n_b)
            def _():
                for c in load_sweep(0, b + 1, 1 - slot):
                    c.start()

        nb = seq // (d * BLK)
        has_prev = nb > 1
        first, final = lay == 0, lay == n_lay - 1
        kw = 2 * BLK if has_prev else BLK

        groups = STATE_DILATION // d
        run = BLK // groups
        q_pos = _query_order(row, groups)

        def block(j, carry, d=d, nb=nb, has_prev=has_prev, first=first, final=final, kw=kw,
                  slot=slot, groups=groups, run=run, q_pos=q_pos):
            r, n = j // nb, j % nb
            q_row0 = pl.multiple_of(BLK + j * BLK, BLK)
            k_row0 = pl.multiple_of(j * BLK, BLK) if has_prev else q_row0
            runs = [pl.ds(pl.multiple_of((d * a + r) * BLK + run * n, run), run)
                    for a in range(groups)]
            load = lambda ref, g: jnp.concatenate([ref[g, s, :] for s in runs], axis=0)

            def store(ref, g, val):
                for a, s in enumerate(runs):
                    ref[g, s, :] = val[a * run:(a + 1) * run]

            cur_ok = col <= q_pos
            if has_prev:
                prev_ok = jnp.logical_and(col >= q_pos, n > 0)
                allowed = jnp.concatenate([prev_ok, cur_ok], axis=1)
            else:
                allowed = cur_ok
            allowed2 = jnp.concatenate([allowed, allowed], axis=0)
            lanes = [slice(g * pair_w, (g + 1) * pair_w) for g in range(n_pair)]
            scores = []
            for ls in lanes:
                q2 = buf[slot, 0, pl.ds(q_row0, BLK), ls]
                q_both = jnp.concatenate([jnp.where(low_head, q2, 0),
                                          jnp.where(low_head, 0, q2)], axis=0)
                k2 = buf[slot, 1, pl.ds(k_row0, kw), ls]
                scores.append(lax.dot_general(q_both, k2, (((1,), (1,)), ((), ())),
                                              preferred_element_type=F32))
            probs, maxes, sums = [], [], []
            for s in scores:
                s = jnp.where(allowed2, s, NEG_INF)
                m_h = jnp.max(s, axis=-1, keepdims=True)
                p = jnp.exp(s - m_h)
                sums.append(jnp.sum(p, axis=-1, keepdims=True))
                maxes.append(m_h)
                probs.append(p.astype(BF16))
            pv = [jnp.dot(p, buf[slot, 2, pl.ds(k_row0, kw), ls], preferred_element_type=F32)
                  for p, ls in zip(probs, lanes)]
            for g, ls in enumerate(lanes):
                acc = jnp.where(low_head, pv[g][:BLK], pv[g][BLK:])
                m_new = jnp.where(low_head, maxes[g][:BLK], maxes[g][BLK:])
                l_new = jnp.where(low_head, sums[g][:BLK], sums[g][BLK:])
                if not first:
                    m_old = load(m_s, g)
                    m_tot = jnp.maximum(m_old, m_new)
                    a_old = jnp.exp(m_old - m_tot)
                    a_new = jnp.exp(m_new - m_tot)
                    acc = load(acc_s, g) * a_old + acc * a_new
                    l_new = load(l_s, g) * a_old + l_new * a_new
                    m_new = m_tot
                if final:
                    o_blk = jnp.dot(unpermute, (acc / l_new).astype(BF16), preferred_element_type=F32)
                    o_ref[0, pl.ds(pl.multiple_of(j * BLK, BLK), BLK), ls] = o_blk.astype(o_ref.dtype)
                else:
                    store(acc_s, g, acc)
                    store(m_s, g, m_new)
                    store(l_s, g, l_new)
            return carry

        lax.fori_loop(0, n_blocks, block, 0, unroll=4)


def _attention(q, k, v):
    bsz, seq, _ = q[0].shape
    assert all(seq % (d * BLK) == 0 for d in DILATIONS) and DILATIONS[-1] == 1
    n_pair = ATTN_WIDTH // (2 * HEAD_DIM)
    return pl.pallas_call(
        functools.partial(_attn_kernel, seq=seq),
        grid=(bsz,),
        in_specs=[pl.BlockSpec(memory_space=pl.ANY)] * (3 * len(DILATIONS)),
        out_specs=pl.BlockSpec((1, seq, ATTN_WIDTH), lambda b: (b, 0, 0)),
        out_shape=jax.ShapeDtypeStruct((bsz, seq, ATTN_WIDTH), BF16),
        scratch_shapes=[pltpu.VMEM((2, 3, BLK + seq, ATTN_WIDTH), BF16),
                        pltpu.VMEM((n_pair, seq, 2 * HEAD_DIM), F32),
                        pltpu.VMEM((n_pair, seq, 2 * HEAD_DIM), F32),
                        pltpu.VMEM((n_pair, seq, 2 * HEAD_DIM), F32),
                        pltpu.SemaphoreType.DMA((2,))],
        compiler_params=_cparams(("arbitrary",)),
        name="attention",
    )(*q, *k, *v)


ROUTE_COLS = 4
EXPERT_LANE0 = N_GROUPS
MERGE_ROWS = 512


def _merge_kernel(o_ref, c_ref, ga_ref, gb_ref, x_ref, wa_ref, wc_ref, bc_ref, wo_ref,
                  gf_ref, wrh_ref, wrl_ref, br_ref,
                  x1_ref, h2_ref, route_ref, gate_ref, cnt_ref, tri_ref):
    tm = x_ref.shape[0]

    @pl.when(pl.program_id(0) == 0)
    def _():
        r = lax.broadcasted_iota(jnp.int32, (tm, tm), 0)
        c = lax.broadcasted_iota(jnp.int32, (tm, tm), 1)
        tri_ref[...] = (c < r).astype(BF16)

    logit_rows = []
    for r0 in range(0, tm, MERGE_ROWS):
        rows = slice(r0, r0 + MERGE_ROWS)
        y_a = jnp.dot(o_ref[rows, :], wa_ref[...], preferred_element_type=F32)
        y_b = jnp.dot(c_ref[rows, :], wc_ref[...], preferred_element_type=F32) + bc_ref[...]
        merged = (_sigmoid(ga_ref[rows, :].astype(F32)) * y_a
                  + _sigmoid(gb_ref[rows, :].astype(F32)) * y_b)
        x1 = x_ref[rows, :] + jnp.dot(merged.astype(BF16), wo_ref[...], preferred_element_type=F32)
        x1_ref[rows, :] = x1
        h2 = _rms(x1, gf_ref[...])
        _store_row_tiles(h2_ref.at[pl.ds(r0 * ROW_TILE, MERGE_ROWS * ROW_TILE), :], h2)
        h_hi = h2.astype(BF16)
        h_lo = (h2 - h_hi.astype(F32)).astype(BF16)
        logit_rows.append(jnp.dot(h_hi, wrh_ref[...], preferred_element_type=F32)
                          + jnp.dot(h_hi, wrl_ref[...], preferred_element_type=F32)
                          + jnp.dot(h_lo, wrh_ref[...], preferred_element_type=F32))
    logits = jnp.concatenate(logit_rows, axis=0) + br_ref[...]

    lane = lax.broadcasted_iota(jnp.int32, (tm, LANES), 1).astype(F32)
    no_lane = float(LANES)
    is_group = lane < N_GROUPS
    g_max = jnp.max(jnp.where(is_group, logits, -jnp.inf), axis=-1, keepdims=True)
    g_sel = jnp.min(jnp.where(jnp.logical_and(is_group, logits == g_max), lane, no_lane),
                    axis=-1, keepdims=True)
    p_group = 1.0 / jnp.sum(jnp.where(is_group, jnp.exp(logits - g_max), 0.0),
                            axis=-1, keepdims=True)
    lo = EXPERT_LANE0 + EXPERTS_PER_GROUP * g_sel
    in_sel = jnp.logical_and(lane >= lo, lane < lo + EXPERTS_PER_GROUP)
    cand = jnp.where(in_sel, logits, -jnp.inf)
    v1 = jnp.max(cand, axis=-1, keepdims=True)
    i1 = jnp.min(jnp.where(jnp.logical_and(in_sel, cand == v1), lane, no_lane), axis=-1, keepdims=True)
    rest = jnp.logical_and(in_sel, lane != i1)
    cand2 = jnp.where(rest, logits, -jnp.inf)
    v2 = jnp.max(cand2, axis=-1, keepdims=True)
    i2 = jnp.min(jnp.where(jnp.logical_and(rest, cand2 == v2), lane, no_lane), axis=-1, keepdims=True)
    e2 = jnp.exp(v2 - v1)
    gate0 = p_group / (1.0 + e2)
    gate1 = p_group * e2 / (1.0 + e2)

    oh0 = (lane == i1)
    oh1 = (lane == i2)
    before0 = jnp.dot(tri_ref[...], oh0.astype(BF16), preferred_element_type=F32)
    before1 = jnp.dot(tri_ref[...], oh1.astype(BF16), preferred_element_type=F32)
    tot0 = jnp.sum(oh0.astype(F32), axis=0, keepdims=True)
    tot1 = jnp.sum(oh1.astype(F32), axis=0, keepdims=True)
    cnt = tot0 + tot1
    rank0 = jnp.sum(jnp.where(oh0, before0, 0.0), axis=-1, keepdims=True)
    rank1 = jnp.sum(jnp.where(oh1, tot0 + before1, 0.0), axis=-1, keepdims=True)

    rc = lax.broadcasted_iota(jnp.int32, (tm, ROUTE_COLS), 1)
    route = jnp.where(rc == 0, i1 - EXPERT_LANE0,
                      jnp.where(rc == 1, i2 - EXPERT_LANE0, jnp.where(rc == 2, rank0, rank1)))
    route_ref[...] = route.astype(jnp.int32)
    gc = lax.broadcasted_iota(jnp.int32, (tm, TOP_K), 1)
    gate_ref[...] = jnp.where(gc == 0, gate0, gate1)
    cnt_ref[0] = jnp.broadcast_to(pltpu.roll(cnt, LANES - EXPERT_LANE0, 1), (8, LANES))


def _merge(o, c, ga, gb, x, wa, wc, bc, wo, gf, wr_hi, wr_lo, br, tm):
    t = x.shape[0]
    nt = t // tm
    row = lambda i: (i, 0)
    const = lambda i: (0, 0)
    full = lambda a: pl.BlockSpec(a.shape, const)
    return pl.pallas_call(
        _merge_kernel,
        grid=(nt,),
        in_specs=[
            pl.BlockSpec((tm, ATTN_WIDTH), row),
            pl.BlockSpec((tm, CONV_WIDTH), row),
            pl.BlockSpec((tm, D_MODEL), row),
            pl.BlockSpec((tm, D_MODEL), row),
            pl.BlockSpec((tm, D_MODEL), row),
            full(wa), full(wc), full(bc), full(wo), full(gf), full(wr_hi), full(wr_lo), full(br),
        ],
        out_specs=(
            pl.BlockSpec((tm, D_MODEL), row),
            pl.BlockSpec((tm * ROW_TILE, LANES), row),
            pl.BlockSpec((tm, ROUTE_COLS), row),
            pl.BlockSpec((tm, TOP_K), row),
            pl.BlockSpec((1, 8, LANES), lambda i: (i, 0, 0)),
        ),
        out_shape=(
            jax.ShapeDtypeStruct((t, D_MODEL), F32),
            jax.ShapeDtypeStruct((t * ROW_TILE, LANES), F32),
            jax.ShapeDtypeStruct((t, ROUTE_COLS), jnp.int32),
            jax.ShapeDtypeStruct((t, TOP_K), F32),
            jax.ShapeDtypeStruct((nt, 8, LANES), F32),
        ),
        scratch_shapes=[pltpu.VMEM((tm, tm), BF16)],
        compiler_params=_cparams(("arbitrary",)),
        name="merge",
    )(o, c, ga, gb, x, wa, wc, bc, wo, gf, wr_hi, wr_lo, br)


def _pos_kernel(route_ref, base_ref, pos_ref):
    tm = route_ref.shape[0]
    route = route_ref[...].astype(F32)
    lane = lax.broadcasted_iota(jnp.int32, (tm, LANES), 1).astype(F32)
    cols = []
    for k in range(TOP_K):
        start = jnp.sum(jnp.where(lane == route[:, k:k + 1], base_ref[0, 0:1, :], 0.0),
                        axis=-1, keepdims=True)
        cols.append(start + route[:, TOP_K + k:TOP_K + k + 1])
    sel = lax.broadcasted_iota(jnp.int32, (tm, TOP_K), 1)
    pos_ref[...] = (jnp.where(sel == 0, cols[0], cols[1]) * ROW_TILE).astype(jnp.int32)


def _positions(route, base, tm):
    nt = base.shape[0]
    base_rows = jnp.zeros((nt, SUBLANES, LANES), F32).at[:, :, :N_EXPERTS].set(
        base.astype(F32)[:, None, :])
    return pl.pallas_call(
        _pos_kernel,
        grid=(nt,),
        in_specs=[pl.BlockSpec((tm, ROUTE_COLS), lambda i: (i, 0)),
                  pl.BlockSpec((1, SUBLANES, LANES), lambda i: (i, 0, 0))],
        out_specs=pl.BlockSpec((tm, TOP_K), lambda i: (i, 0)),
        out_shape=jax.ShapeDtypeStruct((route.shape[0], TOP_K), jnp.int32),
        compiler_params=_cparams(("parallel",)),
        name="positions",
    )(route, base_rows)


def _sort_kernel(pos_ref, h_ref, xs_ref):
    tm = h_ref.shape[0] // ROW_TILE

    def place(t, carry):
        row = h_ref[pl.ds(pl.multiple_of(t * ROW_TILE, ROW_TILE), ROW_TILE), :]
        for k in range(TOP_K):
            xs_ref[pl.ds(pl.multiple_of(pos_ref[TOP_K * t + k], ROW_TILE), ROW_TILE), :] = row
        return carry

    lax.fori_loop(0, tm, place, 0, unroll=8)


def _tile_sort(pos_flat, h2, tm):
    nt = h2.shape[0] // (tm * ROW_TILE)
    return pl.pallas_call(
        _sort_kernel,
        grid=(nt,),
        in_specs=[pl.BlockSpec((TOP_K * tm,), lambda i: (i,), memory_space=pltpu.SMEM),
                  pl.BlockSpec((tm * ROW_TILE, LANES), lambda i: (i, 0))],
        out_specs=pl.BlockSpec((TOP_K * tm * ROW_TILE, LANES), lambda i: (i, 0)),
        out_shape=jax.ShapeDtypeStruct((TOP_K * h2.shape[0], LANES), F32),
        compiler_params=_cparams(("parallel",)),
        name="tile_sort",
    )(pos_flat, h2)


RARE_RUN = 64
BLOCKS_PER_STEP = 2


def _moe_kernel(be_ref, bj_ref, ilo_ref, ihi_ref, valid_ref, cum_ref, cnt_ref, off_ref, nused_ref,
                next_e_ref, xs_hbm, wg_hbm, wu_hbm, wd_hbm, ys_hbm, trash_hbm,
                xbuf, obuf, wg_s, wu_s, wd_s, wg_f, wu_f, wd_f, wslot, sem_in, sem_out, sem_w,
                *, tile_rows, layer):
    step = pl.program_id(0)
    rows8 = lambda r, n: pl.ds(pl.multiple_of(r * ROW_TILE, ROW_TILE), n * ROW_TILE)

    def for_runs(bb, fn):
        e = be_ref[bb]
        lo_blk = bj_ref[bb] * MOE_BLOCK

        def body(i, carry):
            c = cum_ref[i * N_EXPERTS + e]
            lo = jnp.maximum(c, lo_blk)
            hi = jnp.minimum(c + cnt_ref[i * N_EXPERTS + e], lo_blk + MOE_BLOCK)
            fn(i * tile_rows + off_ref[i * N_EXPERTS + e] + (lo - c), lo - lo_blk,
               jnp.maximum(hi - lo, 0))
            return carry

        lax.fori_loop(ilo_ref[bb], ihi_ref[bb] + 1, body, 0)

    def for_chunks(length, fn):
        def piece(size):
            @pl.when((length & size) != 0)
            def _():
                fn(length & ~(2 * size - 1), size)

        sizes = [1 << bit for bit in reversed(range(MOE_BLOCK.bit_length()))]

        @pl.when(length >= RARE_RUN)
        def _():
            for size in sizes:
                if size >= RARE_RUN:
                    piece(size)

        for size in sizes:
            if size < RARE_RUN:
                piece(size)

    def gather(bb, s):
        def run(src, dst, length):
            for_chunks(length, lambda o, n: pltpu.make_async_copy(
                xs_hbm.at[rows8(src + o, n)], xbuf.at[s, rows8(dst + o, n)], sem_in.at[s]).start())
        for_runs(bb, run)
        v = valid_ref[bb]
        for_chunks(MOE_BLOCK - v, lambda o, n: pltpu.make_async_copy(
            xs_hbm.at[rows8(o, n)], xbuf.at[s, rows8(v + o, n)], sem_in.at[s]).start())

    def scatter(bb, s):
        def run(src, dst, length):
            for_chunks(length, lambda o, n: pltpu.make_async_copy(
                obuf.at[s, rows8(dst + o, n)], ys_hbm.at[rows8(src + o, n)], sem_out.at[s]).start())
        for_runs(bb, run)
        v = valid_ref[bb]
        for_chunks(MOE_BLOCK - v, lambda o, n: pltpu.make_async_copy(
            obuf.at[s, rows8(v + o, n)], trash_hbm.at[rows8(s * MOE_BLOCK + v + o, n)],
            sem_out.at[s]).start())

    def wait_gather(s):
        pltpu.make_async_copy(xs_hbm.at[rows8(0, MOE_BLOCK)], xbuf.at[s], sem_in.at[s]).wait()

    def wait_scatter(s):
        pltpu.make_async_copy(obuf.at[s], ys_hbm.at[rows8(0, MOE_BLOCK)], sem_out.at[s]).wait()

    def weight_copies(expert, s):
        return [pltpu.make_async_copy(src.at[layer, expert], dst.at[s], sem_w.at[s])
                for src, dst in ((wg_hbm, wg_f), (wu_hbm, wu_f), (wd_hbm, wd_f))]

    n_used = nused_ref[0]

    def one_block(b, slot):
        other = 1 - slot
        live = b < n_used
        e = be_ref[b]
        e_prev = be_ref[jnp.maximum(b - 1, 0)]

        if slot == 0:
            @pl.when(b == 0)
            def _():
                obuf[...] = jnp.zeros_like(obuf)
                for s in range(2):
                    init = pltpu.make_async_copy(
                        obuf.at[s], trash_hbm.at[rows8(s * MOE_BLOCK, MOE_BLOCK)], sem_out.at[s])
                    init.start()
                    init.wait()
                gather(0, 0)
                wslot[0] = 0
                for c in weight_copies(e, 0):
                    c.start()

        @pl.when(live)
        def _():
            wait_gather(slot)

        @pl.when(jnp.logical_and(live, b >= 2))
        def _():
            wait_scatter(slot)

        @pl.when(jnp.logical_and(live, jnp.logical_or(b == 0, e != e_prev)))
        def _():
            s = wslot[0]
            for c in weight_copies(e, s):
                c.wait()
            wg_s[...] = wg_f[s].astype(BF16)
            wu_s[...] = wu_f[s].astype(BF16)
            wd_s[...] = wd_f[s].astype(BF16)
            e_next = next_e_ref[b]

            @pl.when(e_next != e)
            def _():
                for c in weight_copies(e_next, 1 - s):
                    c.start()
            wslot[0] = 1 - s

        @pl.when(b + 1 < n_used)
        def _():
            gather(b + 1, other)

        @pl.when(jnp.logical_and(live, b >= 1))
        def _():
            scatter(b - 1, other)

        @pl.when(live)
        def _():
            x = _load_row_tiles(xbuf.at[slot], MOE_BLOCK).astype(BF16)
            hg = jnp.dot(x, wg_s[...], preferred_element_type=F32)
            hu = jnp.dot(x, wu_s[...], preferred_element_type=F32)
            hb = (hg * _sigmoid(hg) * hu).astype(BF16)
            _store_row_tiles(obuf.at[slot], jnp.dot(hb, wd_s[...], preferred_element_type=F32))

        @pl.when(b == n_used - 1)
        def _():
            scatter(b, slot)
            wait_scatter(other)
            wait_scatter(slot)

    for sub in range(BLOCKS_PER_STEP):
        one_block(step * BLOCKS_PER_STEP + sub, sub)


def _moe(plan, xs, w_gate, w_up, w_down, layer, n_blocks, tile_rows):
    n_pre = len(plan)
    assert n_blocks % BLOCKS_PER_STEP == 0
    grid_spec = pltpu.PrefetchScalarGridSpec(
        num_scalar_prefetch=n_pre,
        grid=(n_blocks // BLOCKS_PER_STEP,),
        in_specs=[pl.BlockSpec(memory_space=pl.ANY)] * 4,
        out_specs=(pl.BlockSpec(memory_space=pl.ANY), pl.BlockSpec(memory_space=pl.ANY)),
        scratch_shapes=[pltpu.VMEM((2, MOE_BLOCK * ROW_TILE, LANES), F32),
                        pltpu.VMEM((2, MOE_BLOCK * ROW_TILE, LANES), F32),
                        pltpu.VMEM((D_MODEL, EXPERT_FF), BF16),
                        pltpu.VMEM((D_MODEL, EXPERT_FF), BF16),
                        pltpu.VMEM((EXPERT_FF, D_MODEL), BF16),
                        pltpu.VMEM((2, D_MODEL, EXPERT_FF), F32),
                        pltpu.VMEM((2, D_MODEL, EXPERT_FF), F32),
                        pltpu.VMEM((2, EXPERT_FF, D_MODEL), F32),
                        pltpu.SMEM((1,), jnp.int32),
                        pltpu.SemaphoreType.DMA((2,)),
                        pltpu.SemaphoreType.DMA((2,)),
                        pltpu.SemaphoreType.DMA((2,))],
    )
    ys, _ = pl.pallas_call(
        functools.partial(_moe_kernel, tile_rows=tile_rows, layer=layer),
        grid_spec=grid_spec,
        out_shape=(jax.ShapeDtypeStruct(xs.shape, F32),
                   jax.ShapeDtypeStruct((2 * MOE_BLOCK * ROW_TILE, LANES), F32)),
        compiler_params=_cparams(("arbitrary",)),
        name="moe",
    )(*plan, xs, w_gate, w_up, w_down)
    return ys


def _ple_kernel(pos_ref, gate_ref, ys_ref, x1_ref, p_ref, gp_ref, wpg_ref, wpe_ref, gfin_ref,
                out_ref, moe_ref, *, final_norm):
    tm = x1_ref.shape[0]
    tile = lambda r: pl.ds(pl.multiple_of(r * ROW_TILE, ROW_TILE), ROW_TILE)

    def combine(t, carry):
        acc = None
        for k in range(TOP_K):
            row0 = pl.multiple_of(pos_ref[TOP_K * t + k], ROW_TILE)
            term = gate_ref[TOP_K * t + k] * ys_ref[pl.ds(row0, ROW_TILE), :]
            acc = term if acc is None else acc + term
        moe_ref[tile(t), :] = acc
        return carry

    lax.fori_loop(0, tm, combine, 0, unroll=16)
    x2 = x1_ref[...] + _load_row_tiles(moe_ref, tm)
    hp = _rms(x2, gp_ref[...]).astype(BF16)
    g_ple = _sigmoid(jnp.dot(hp, wpg_ref[...], preferred_element_type=F32))
    emb = jnp.dot(p_ref[...].astype(BF16), wpe_ref[...], preferred_element_type=F32)
    x3 = x2 + g_ple * emb
    if final_norm:
        x3 = _rms(x3, gfin_ref[...])
    out_ref[...] = x3


def _ple(pos, gate_flat, ys, x1, p, gp, wpg, wpe, gfin, tm, final_norm):
    t = x1.shape[0]
    row = lambda i: (i, 0)
    const = lambda i: (0, 0)
    full = lambda a: pl.BlockSpec(a.shape, const)
    smem = lambda n: pl.BlockSpec((n,), lambda i: (i,), memory_space=pltpu.SMEM)
    return pl.pallas_call(
        functools.partial(_ple_kernel, final_norm=final_norm),
        grid=(t // tm,),
        in_specs=[
            smem(TOP_K * tm),
            smem(TOP_K * tm),
            pl.BlockSpec((TOP_K * tm * ROW_TILE, LANES), row),
            pl.BlockSpec((tm, D_MODEL), row),
            pl.BlockSpec((tm, PLE_DIM), row),
            full(gp), full(wpg), full(wpe), full(gfin),
        ],
        out_specs=pl.BlockSpec((tm, D_MODEL), row),
        out_shape=jax.ShapeDtypeStruct((t, D_MODEL), F32),
        scratch_shapes=[pltpu.VMEM((tm * ROW_TILE, LANES), F32)],
        compiler_params=_cparams(("parallel",)),
        name="ple",
    )(pos, gate_flat, ys, x1, p, gp, wpg, wpe, gfin)


def _rope_tables(seq):
    inv = jnp.power(ROPE_THETA, -jnp.arange(0, HEAD_DIM, 2, dtype=F32) / HEAD_DIM)
    ang = jnp.arange(seq, dtype=F32)[:, None] * inv[None, :]
    cos, sin = jnp.cos(ang), jnp.sin(ang)
    cos_h = jnp.concatenate([cos, cos], axis=-1)
    sin_h = jnp.concatenate([-sin, sin], axis=-1)
    reps = LANES // HEAD_DIM
    return jnp.tile(cos_h, (1, reps)), jnp.tile(sin_h, (1, reps))


def _dispatch_plan(route_counts, group, n_blocks):
    i32 = jnp.int32
    grouped = route_counts.reshape(-1, group, N_EXPERTS)
    counts = jnp.sum(grouped, axis=1)
    base = ((jnp.cumsum(counts, axis=1) - counts)[:, None, :]
            + jnp.cumsum(grouped, axis=1) - grouped).reshape(-1, N_EXPERTS).astype(i32)
    total = jnp.sum(counts, axis=0)
    blocks_of = (total + MOE_BLOCK - 1) // MOE_BLOCK
    block_end = jnp.cumsum(blocks_of)
    bidx = jnp.arange(n_blocks, dtype=i32)
    block_e = jnp.minimum(jnp.sum((block_end[None, :] <= bidx[:, None]).astype(i32), axis=1),
                          N_EXPERTS - 1)
    is_e = (block_e[:, None] == jnp.arange(N_EXPERTS, dtype=i32)[None, :]).astype(i32)
    pick = lambda per_expert: jnp.sum(is_e * per_expert[None, :], axis=1)
    pick_tiles = lambda tab: jnp.sum(is_e[:, None, :] * tab[None, :, :], axis=2)
    block_j = bidx - pick(block_end - blocks_of)
    used = bidx < block_end[-1]
    cum = jnp.cumsum(counts, axis=0) - counts
    off = jnp.cumsum(counts, axis=1) - counts
    lo = block_j * MOE_BLOCK
    hi = jnp.minimum(lo + MOE_BLOCK, pick(total))
    run_start = pick_tiles(cum)
    run_end = run_start + pick_tiles(counts)
    tile_lo = jnp.sum((run_end <= lo[:, None]).astype(i32), axis=1)
    tile_hi = jnp.sum((run_start < hi[:, None]).astype(i32), axis=1) - 1
    valid = jnp.clip(hi - lo, 0, MOE_BLOCK)
    tile_lo = jnp.where(used, tile_lo, 0)
    tile_hi = jnp.where(used, tile_hi, -1)
    valid = jnp.where(used, valid, 0)
    e_ids = jnp.arange(N_EXPERTS, dtype=i32)
    later = jnp.where((e_ids[None, :] > e_ids[:, None]) & (blocks_of[None, :] > 0), e_ids[None, :], N_EXPERTS)
    next_used = jnp.min(later, axis=1)
    next_e = pick(jnp.where(next_used == N_EXPERTS, e_ids, next_used))
    flat = lambda a: a.reshape(-1).astype(i32)
    return base, (flat(block_e), flat(block_j), flat(tile_lo), flat(tile_hi), flat(valid),
                  flat(cum), flat(counts), flat(off), flat(block_end[-1:]), flat(next_e))


def kernel(x, p, norm_mix, w_in, w_dw, b_dw, ln_conv_g, ln_conv_b, w_attn_out, w_conv_out,
           b_conv_out, w_out, norm_ffn, w_route_group, b_route_group, w_route_expert,
           b_route_expert, w_exp_gate, w_exp_up, w_exp_down, norm_ple, w_ple_proj, w_ple_gate,
           norm_final):
    bsz, seq, d = x.shape
    depth = w_in.shape[0]
    t = bsz * seq
    tm = 512
    sort_tm = 1024
    n_assign = t * TOP_K
    assert n_assign > MOE_BLOCK
    n_rows = (-(-n_assign // MOE_BLOCK)) * MOE_BLOCK + N_EXPERTS * MOE_BLOCK
    cos, sin = _rope_tables(seq)
    row2 = lambda a: a.reshape(1, -1)

    xf = x.reshape(t, d)
    for i in range(depth):
        n_qkv = 3 * ATTN_WIDTH
        q, k, v = _inproj(xf, row2(norm_mix[i]), w_in[i, :, :n_qkv].astype(BF16), cos, sin,
                          bsz, seq, tm)
        c, ga, gb = _gateproj(xf, row2(norm_mix[i]), w_in[i, :, n_qkv:].astype(BF16), w_dw[i],
                              row2(b_dw[i]), row2(ln_conv_g[i]), row2(ln_conv_b[i]), seq, tm)
        o = _attention(q, k, v).reshape(t, ATTN_WIDTH)

        w_r = jnp.zeros((d, LANES), F32)
        w_r = w_r.at[:, :N_GROUPS].set(w_route_group[i])
        w_r = w_r.at[:, EXPERT_LANE0:EXPERT_LANE0 + N_EXPERTS].set(w_route_expert[i])
        w_r_hi = w_r.astype(BF16)
        w_r_lo = (w_r - w_r_hi.astype(F32)).astype(BF16)
        b_r = jnp.zeros((1, LANES), F32)
        b_r = b_r.at[0, :N_GROUPS].set(b_route_group[i])
        b_r = b_r.at[0, EXPERT_LANE0:EXPERT_LANE0 + N_EXPERTS].set(b_route_expert[i])

        x1, h2, route, gate, cnt = _merge(
            o, c, ga, gb, xf, w_attn_out[i].astype(BF16), w_conv_out[i].astype(BF16),
            row2(b_conv_out[i]), w_out[i].astype(BF16), row2(norm_ffn[i]), w_r_hi, w_r_lo, b_r, tm)

        counts = cnt[:, 0, :N_EXPERTS].astype(jnp.int32)
        base, plan = _dispatch_plan(counts, sort_tm // tm, n_rows // MOE_BLOCK)
        pos = _positions(route, base, tm).reshape(-1)
        xs = _tile_sort(pos, h2, sort_tm)
        ys = _moe(plan, xs, w_exp_gate, w_exp_up, w_exp_down, i, n_rows // MOE_BLOCK, TOP_K * sort_tm)
        xf = _ple(pos, gate.reshape(-1), ys, x1, p[i].reshape(t, PLE_DIM), row2(norm_ple[i]),
                  w_ple_gate[i].astype(BF16), w_ple_proj[i].astype(BF16), row2(norm_final),
                  sort_tm, final_norm=(i == depth - 1))
    return xf.reshape(bsz, seq, d)
```

```python
import functools

import jax
import jax.numpy as jnp
from jax import lax
from jax.experimental import pallas as pl
from jax.experimental.pallas import tpu as pltpu

D_MODEL = 1024
N_HEADS = 8
HEAD_DIM = 64
ATTN_WIDTH = N_HEADS * HEAD_DIM
CONV_WIDTH = 512
CONV_K = 31
DILATIONS = (16, 4, 1)
BLK = 128
ROPE_THETA = 10000.0
N_GROUPS = 4
EXPERTS_PER_GROUP = 8
N_EXPERTS = N_GROUPS * EXPERTS_PER_GROUP
EXPERT_FF = 512
TOP_K = 2
MOE_BLOCK = 256
PLE_DIM = 256
EPS = 1e-6
NEG_INF = -1e30

LANES = 128
SUBLANES = 8
ROW_TILE = D_MODEL // LANES
VMEM_LIMIT = 56 * 1024 * 1024

F32 = jnp.float32
BF16 = jnp.bfloat16


def _cparams(sem):
    return pltpu.CompilerParams(dimension_semantics=sem, vmem_limit_bytes=VMEM_LIMIT)


def _rms(x, g):
    return x * lax.rsqrt(jnp.mean(x * x, axis=-1, keepdims=True) + EPS) * g


def _sigmoid(x):
    return 0.5 * jnp.tanh(0.5 * x) + 0.5


def _store_row_tiles(ref, x):
    rows = x.shape[0]
    for s in range(ROW_TILE):
        ref[pl.ds(s, rows, stride=ROW_TILE), :] = x[:, s * LANES:(s + 1) * LANES]


def _load_row_tiles(ref, rows):
    return jnp.concatenate([ref[pl.ds(s, rows, stride=ROW_TILE), :] for s in range(ROW_TILE)], axis=1)


QKV_ROWS = 256


def _inproj_kernel(x_ref, g_ref, w_ref, cos_ref, sin_ref, *out_refs):
    n_lay = len(DILATIONS)
    q_refs, k_refs, v_refs = (out_refs[a * n_lay:(a + 1) * n_lay] for a in range(3))
    slab_refs = out_refs[3 * n_lay:]
    tm = x_ref.shape[0]
    n_slab = ATTN_WIDTH // LANES

    def put(ref, d, r, row0, g, val):
        cols = slice(g * LANES, (g + 1) * LANES)
        if d == 1:
            ref[pl.ds(row0, val.shape[0]), cols] = val.astype(BF16)
        else:
            ref[0, r, pl.ds(row0, val.shape[0]), cols] = val.astype(BF16)

    def fill(raw, rows, refs, slabs, is_query, post):
        for g in range(n_slab):
            cols = slice(g * LANES, (g + 1) * LANES)
            t = post(raw[:, cols], rows)
            slabs[g, rows, :] = t
            for d, ref in zip(DILATIONS, refs):
                if d == 1 and not is_query:
                    ref[rows, cols] = t.astype(BF16)

    def relayout(refs, slabs, by4, is_query):
        ref16, ref4, ref1 = refs
        quarter = tm // 4
        class4 = lambda r: pl.multiple_of(r * quarter, quarter)

        def to_by4(r, carry):
            for g in range(n_slab):
                val = slabs[g, pl.ds(r, quarter, stride=4), :]
                by4[g, pl.ds(class4(r), quarter), :] = val
                if not is_query:
                    put(ref4, 4, r, 0, g, val)
            return carry
        lax.fori_loop(0, 4, to_by4, 0, unroll=2)

        def to_16(r, carry):
            for g in range(n_slab):
                put(ref16, 16, r, 0, g, by4[g, pl.ds(class4(r % 4) + r // 4, tm // 16, stride=4), :])
            return carry
        lax.fori_loop(0, 16, to_16, 0, unroll=8)

        if is_query:
            def q4_class(r, carry):
                for bl in range(quarter // BLK):
                    for a in range(4):
                        src = pl.ds(class4(r) + bl * BLK + a, BLK // 4, stride=4)
                        for g in range(n_slab):
                            put(ref4, 4, r, bl * BLK + a * (BLK // 4), g, by4[g, src, :])
                return carry
            lax.fori_loop(0, 4, q4_class, 0, unroll=2)

            def q1_block(bl, carry):
                for a in range(16):
                    src = pl.ds((a % 4) * quarter + (BLK // 4) * bl + a // 4, BLK // 16, stride=4)
                    for g in range(n_slab):
                        put(ref1, 1, 0, pl.multiple_of(bl * BLK, BLK) + a * (BLK // 16), g, by4[g, src, :])
                return carry
            lax.fori_loop(0, tm // BLK, q1_block, 0, unroll=2)

    lane = lax.broadcasted_iota(jnp.int32, (QKV_ROWS, LANES), 1)
    first_half = (lane % HEAD_DIM) < (HEAD_DIM // 2)

    def rope(t, rows):
        partner = jnp.where(first_half,
                            pltpu.roll(t, LANES - HEAD_DIM // 2, 1),
                            pltpu.roll(t, HEAD_DIM // 2, 1))
        return t * cos_ref[rows, :] + partner * sin_ref[rows, :]

    for r0 in range(0, tm, QKV_ROWS):
        rows = slice(r0, r0 + QKV_ROWS)
        h = _rms(x_ref[rows, :], g_ref[...]).astype(BF16)
        proj = lambda a: jnp.dot(h, w_ref[:, a * ATTN_WIDTH:(a + 1) * ATTN_WIDTH],
                                 preferred_element_type=F32)
        fill(proj(0), rows, q_refs, slab_refs[0], True, lambda t, rw: rope(t, rw) * (HEAD_DIM ** -0.5))
        fill(proj(1), rows, k_refs, slab_refs[1], False, rope)
        fill(proj(2), rows, v_refs, slab_refs[2], False, lambda t, rw: t)
    assert DILATIONS == (16, 4, 1) and tm % (4 * BLK) == 0
    relayout(q_refs, slab_refs[0], slab_refs[3], True)
    relayout(k_refs, slab_refs[1], slab_refs[4], False)
    relayout(v_refs, slab_refs[2], slab_refs[5], False)


CONV_HALO = 32
CONV_ROWS = 32


def _gateproj_kernel(x_ref, g_ref, w_ref, cw_ref, cb_ref, lg_ref, lb_ref, c_ref, ga_ref, gb_ref,
                     ext_ref, halo_ref, *, tiles_per_seq):
    tm = x_ref.shape[0]
    h = _rms(x_ref[...], g_ref[...]).astype(BF16)

    def proj(c0, width):
        return jnp.dot(h, w_ref[:, c0:c0 + width], preferred_element_type=F32)

    u = proj(0, CONV_WIDTH) * _sigmoid(proj(CONV_WIDTH, CONV_WIDTH))
    first_of_sequence = pl.program_id(0) % tiles_per_seq == 0
    ext_ref[0, 0:CONV_HALO, :] = jnp.where(first_of_sequence, 0.0, halo_ref[...])
    ext_ref[0, CONV_HALO:CONV_HALO + tm, :] = u
    halo_ref[...] = u[tm - CONV_HALO:, :]
    c0 = 2 * CONV_WIDTH
    ga_ref[...] = proj(c0, D_MODEL).astype(ga_ref.dtype)
    gb_ref[...] = proj(c0 + D_MODEL, D_MODEL).astype(gb_ref.dtype)
    _conv_rows(ext_ref, cw_ref, cb_ref, lg_ref, lb_ref, c_ref, tm)


def _conv_rows(ext_ref, w_ref, b_ref, g_ref, beta_ref, out_ref, ts):
    first = CONV_HALO - (CONV_K - 1)
    n_shift = ext_ref.shape[0]
    rows = CONV_HALO + ts
    ext0 = ext_ref[0]
    for s in range(1, n_shift):
        ext_ref[s] = pltpu.roll(ext0, rows - s, 0)
    bias = b_ref[...]
    gamma = g_ref[...]
    beta = beta_ref[...]

    for r0 in range(0, ts, CONV_ROWS):
        acc = jnp.zeros((CONV_ROWS, CONV_WIDTH), F32) + bias
        for j in range(CONV_K):
            lo = first + j + r0
            s, lo = lo % n_shift, lo - lo % n_shift
            tap = jnp.concatenate([w_ref[j]] * (CONV_ROWS // SUBLANES), axis=0)
            acc = acc + tap * ext_ref[s, lo:lo + CONV_ROWS, :]
        mu = jnp.mean(acc, axis=-1, keepdims=True)
        cen = acc - mu
        var = jnp.mean(cen * cen, axis=-1, keepdims=True)
        y = cen * lax.rsqrt(var + EPS) * gamma + beta
        out_ref[r0:r0 + CONV_ROWS, :] = (y * _sigmoid(y)).astype(out_ref.dtype)


def _gateproj(x, gain, w_rest, w_dw, b_dw, ln_g, ln_b, seq, tm):
    t = x.shape[0]
    row = lambda i: (i, 0)
    const = lambda i: (0, 0)
    taps = jnp.broadcast_to(w_dw[:, None, :], (CONV_K, SUBLANES, CONV_WIDTH))
    return pl.pallas_call(
        functools.partial(_gateproj_kernel, tiles_per_seq=seq // tm),
        grid=(t // tm,),
        in_specs=[pl.BlockSpec((tm, D_MODEL), row), pl.BlockSpec((1, D_MODEL), const),
                  pl.BlockSpec(w_rest.shape, const),
                  pl.BlockSpec(taps.shape, lambda i: (0, 0, 0)),
                  pl.BlockSpec((1, CONV_WIDTH), const), pl.BlockSpec((1, CONV_WIDTH), const),
                  pl.BlockSpec((1, CONV_WIDTH), const)],
        out_specs=(pl.BlockSpec((tm, CONV_WIDTH), row), pl.BlockSpec((tm, D_MODEL), row),
                   pl.BlockSpec((tm, D_MODEL), row)),
        out_shape=(jax.ShapeDtypeStruct((t, CONV_WIDTH), BF16),
                   jax.ShapeDtypeStruct((t, D_MODEL), BF16),
                   jax.ShapeDtypeStruct((t, D_MODEL), BF16)),
        scratch_shapes=[pltpu.VMEM((SUBLANES, CONV_HALO + tm, CONV_WIDTH), F32),
                        pltpu.VMEM((CONV_HALO, CONV_WIDTH), F32)],
        compiler_params=_cparams(("arbitrary",)),
        name="gateproj",
    )(x, gain, w_rest, taps, b_dw, ln_g, ln_b)


def _inproj(x, gain, w_in, cos, sin, bsz, seq, tm):
    t = x.shape[0]
    n_pos = seq // tm
    row = lambda i: (i, 0)
    const = lambda i: (0, 0)
    qkv_shapes, qkv_specs = [], []
    for d in DILATIONS:
        if d == 1:
            qkv_shapes.append(jax.ShapeDtypeStruct((t, ATTN_WIDTH), BF16))
            qkv_specs.append(pl.BlockSpec((tm, ATTN_WIDTH), row))
        else:
            qkv_shapes.append(jax.ShapeDtypeStruct((bsz, d, seq // d, ATTN_WIDTH), BF16))
            qkv_specs.append(pl.BlockSpec((1, d, tm // d, ATTN_WIDTH),
                                          lambda i: (i // n_pos, 0, i % n_pos, 0)))
    out_shape = tuple(qkv_shapes * 3)
    out_specs = tuple(qkv_specs * 3)
    outs = pl.pallas_call(
        _inproj_kernel,
        grid=(t // tm,),
        in_specs=[
            pl.BlockSpec((tm, D_MODEL), row),
            pl.BlockSpec((1, D_MODEL), const),
            pl.BlockSpec(w_in.shape, const),
            pl.BlockSpec((tm, LANES), lambda i: (i % n_pos, 0)),
            pl.BlockSpec((tm, LANES), lambda i: (i % n_pos, 0)),
        ],
        out_specs=out_specs,
        out_shape=out_shape,
        scratch_shapes=[pltpu.VMEM((ATTN_WIDTH // LANES, tm, LANES), F32) for _ in range(6)],
        compiler_params=_cparams(("parallel",)),
        name="inproj",
    )(x, gain, w_in, cos, sin)
    n_lay = len(DILATIONS)
    as_rows = lambda a: a.reshape(bsz, seq, ATTN_WIDTH)
    return tuple([as_rows(a) for a in outs[j * n_lay:(j + 1) * n_lay]] for j in range(3))


STATE_DILATION = max(DILATIONS)


def _query_order(idx, groups):
    run = BLK // groups
    return groups * (idx % run) + idx // run


def _attn_kernel(*refs, seq):
    n_lay = len(DILATIONS)
    srcs = refs[:3 * n_lay]
    o_ref, buf, acc_s, m_s, l_s, sem = refs[3 * n_lay:]
    b = pl.program_id(0)
    n_b = pl.num_programs(0)
    n_blocks = seq // BLK
    pair_w = 2 * HEAD_DIM
    n_pair = ATTN_WIDTH // pair_w

    def load_sweep(lay, bi, slot):
        return [pltpu.make_async_copy(srcs[a * n_lay + lay].at[bi], buf.at[slot, a, pl.ds(BLK, seq)],
                                      sem.at[slot]) for a in range(3)]

    @pl.when(b == 0)
    def _():
        buf[:, :, 0:BLK, :] = jnp.zeros((2, 3, BLK, ATTN_WIDTH), BF16)
        for c in load_sweep(0, 0, 0):
            c.start()

    row = lax.broadcasted_iota(jnp.int32, (BLK, BLK), 0)
    col = lax.broadcasted_iota(jnp.int32, (BLK, BLK), 1)
    lane = lax.broadcasted_iota(jnp.int32, (BLK, pair_w), 1)
    low_head = lane < HEAD_DIM
    unpermute = (row == _query_order(col, STATE_DILATION // DILATIONS[-1])).astype(BF16)

    for lay, d in enumerate(DILATIONS):
        slot = (b * n_lay + lay) % 2
        for c in load_sweep(lay, b, slot):
            c.wait()
        if lay + 1 < n_lay:
            for c in load_sweep(lay + 1, b, 1 - slot):
                c.start()
        else:
            @pl.when(b + 1 < n_b)
            def _():
                for c in load_sweep(0, b + 1, 1 - slot):
                    c.start()

        nb = seq // (d * BLK)
        has_prev = nb > 1
        first, final = lay == 0, lay == n_lay - 1
        kw = 2 * BLK if has_prev else BLK

        groups = STATE_DILATION // d
        run = BLK // groups
        q_pos = _query_order(row, groups)

        def block(j, carry, d=d, nb=nb, has_prev=has_prev, first=first, final=final, kw=kw,
                  slot=slot, groups=groups, run=run, q_pos=q_pos):
            r, n = j // nb, j % nb
            q_row0 = pl.multiple_of(BLK + j * BLK, BLK)
            k_row0 = pl.multiple_of(j * BLK, BLK) if has_prev else q_row0
            runs = [pl.ds(pl.multiple_of((d * a + r) * BLK + run * n, run), run)
                    for a in range(groups)]
            load = lambda ref, g: jnp.concatenate([ref[g, s, :] for s in runs], axis=0)

            def store(ref, g, val):
                for a, s in enumerate(runs):
                    ref[g, s, :] = val[a * run:(a + 1) * run]

            cur_ok = col <= q_pos
            if has_prev:
                prev_ok = jnp.logical_and(col >= q_pos, n > 0)
                allowed = jnp.concatenate([prev_ok, cur_ok], axis=1)
            else:
                allowed = cur_ok
            allowed2 = jnp.concatenate([allowed, allowed], axis=0)
            lanes = [slice(g * pair_w, (g + 1) * pair_w) for g in range(n_pair)]
            scores = []
            for ls in lanes:
                q2 = buf[slot, 0, pl.ds(q_row0, BLK), ls]
                q_both = jnp.concatenate([jnp.where(low_head, q2, 0),
                                          jnp.where(low_head, 0, q2)], axis=0)
                k2 = buf[slot, 1, pl.ds(k_row0, kw), ls]
                scores.append(lax.dot_general(q_both, k2, (((1,), (1,)), ((), ())),
                                              preferred_element_type=F32))
            probs, maxes, sums = [], [], []
            for s in scores:
                s = jnp.where(allowed2, s, NEG_INF)
                m_h = jnp.max(s, axis=-1, keepdims=True)
                p = jnp.exp(s - m_h)
                sums.append(jnp.sum(p, axis=-1, keepdims=True))
                maxes.append(m_h)
                probs.append(p.astype(BF16))
            pv = [jnp.dot(p, buf[slot, 2, pl.ds(k_row0, kw), ls], preferred_element_type=F32)
                  for p, ls in zip(probs, lanes)]
            for g, ls in enumerate(lanes):
                acc = jnp.where(low_head, pv[g][:BLK], pv[g][BLK:])
                m_new = jnp.where(low_head, maxes[g][:BLK], maxes[g][BLK:])
                l_new = jnp.where(low_head, sums[g][:BLK], sums[g][BLK:])
                if not first:
                    m_old = load(m_s, g)
                    m_tot = jnp.maximum(m_old, m_new)
                    a_old = jnp.exp(m_old - m_tot)
                    a_new = jnp.exp(m_new - m_tot)
                    acc = load(acc_s, g) * a_old + acc * a_new
                    l_new = load(l_s, g) * a_old + l_new * a_new
                    m_new = m_tot
                if final:
                    o_blk = jnp.dot(unpermute, (acc / l_new).astype(BF16), preferred_element_type=F32)
                    o_ref[0, pl.ds(pl.multiple_of(j * BLK, BLK), BLK), ls] = o_blk.astype(o_ref.dtype)
                else:
                    store(acc_s, g, acc)
                    store(m_s, g, m_new)
                    store(l_s, g, l_new)
            return carry

        lax.fori_loop(0, n_blocks, block, 0, unroll=4)


def _attention(q, k, v):
    bsz, seq, _ = q[0].shape
    assert all(seq % (d * BLK) == 0 for d in DILATIONS) and DILATIONS[-1] == 1
    n_pair = ATTN_WIDTH // (2 * HEAD_DIM)
    return pl.pallas_call(
        functools.partial(_attn_kernel, seq=seq),
        grid=(bsz,),
        in_specs=[pl.BlockSpec(memory_space=pl.ANY)] * (3 * len(DILATIONS)),
        out_specs=pl.BlockSpec((1, seq, ATTN_WIDTH), lambda b: (b, 0, 0)),
        out_shape=jax.ShapeDtypeStruct((bsz, seq, ATTN_WIDTH), BF16),
        scratch_shapes=[pltpu.VMEM((2, 3, BLK + seq, ATTN_WIDTH), BF16),
                        pltpu.VMEM((n_pair, seq, 2 * HEAD_DIM), F32),
                        pltpu.VMEM((n_pair, seq, 2 * HEAD_DIM), F32),
                        pltpu.VMEM((n_pair, seq, 2 * HEAD_DIM), F32),
                        pltpu.SemaphoreType.DMA((2,))],
        compiler_params=_cparams(("arbitrary",)),
        name="attention",
    )(*q, *k, *v)


ROUTE_COLS = 4
EXPERT_LANE0 = N_GROUPS
MERGE_ROWS = 512


def _merge_kernel(o_ref, c_ref, ga_ref, gb_ref, x_ref, wa_ref, wc_ref, bc_ref, wo_ref,
                  gf_ref, wrh_ref, wrl_ref, br_ref,
                  x1_ref, h2_ref, route_ref, gate_ref, cnt_ref, tri_ref):
    tm = x_ref.shape[0]

    @pl.when(pl.program_id(0) == 0)
    def _():
        r = lax.broadcasted_iota(jnp.int32, (tm, tm), 0)
        c = lax.broadcasted_iota(jnp.int32, (tm, tm), 1)
        tri_ref[...] = (c < r).astype(BF16)

    logit_rows = []
    for r0 in range(0, tm, MERGE_ROWS):
        rows = slice(r0, r0 + MERGE_ROWS)
        y_a = jnp.dot(o_ref[rows, :], wa_ref[...], preferred_element_type=F32)
        y_b = jnp.dot(c_ref[rows, :], wc_ref[...], preferred_element_type=F32) + bc_ref[...]
        merged = (_sigmoid(ga_ref[rows, :].astype(F32)) * y_a
                  + _sigmoid(gb_ref[rows, :].astype(F32)) * y_b)
        x1 = x_ref[rows, :] + jnp.dot(merged.astype(BF16), wo_ref[...], preferred_element_type=F32)
        x1_ref[rows, :] = x1
        h2 = _rms(x1, gf_ref[...])
        _store_row_tiles(h2_ref.at[pl.ds(r0 * ROW_TILE, MERGE_ROWS * ROW_TILE), :], h2)
        h_hi = h2.astype(BF16)
        h_lo = (h2 - h_hi.astype(F32)).astype(BF16)
        logit_rows.append(jnp.dot(h_hi, wrh_ref[...], preferred_element_type=F32)
                          + jnp.dot(h_hi, wrl_ref[...], preferred_element_type=F32)
                          + jnp.dot(h_lo, wrh_ref[...], preferred_element_type=F32))
    logits = jnp.concatenate(logit_rows, axis=0) + br_ref[...]

    lane = lax.broadcasted_iota(jnp.int32, (tm, LANES), 1).astype(F32)
    no_lane = float(LANES)
    is_group = lane < N_GROUPS
    g_max = jnp.max(jnp.where(is_group, logits, -jnp.inf), axis=-1, keepdims=True)
    g_sel = jnp.min(jnp.where(jnp.logical_and(is_group, logits == g_max), lane, no_lane),
                    axis=-1, keepdims=True)
    p_group = 1.0 / jnp.sum(jnp.where(is_group, jnp.exp(logits - g_max), 0.0),
                            axis=-1, keepdims=True)
    lo = EXPERT_LANE0 + EXPERTS_PER_GROUP * g_sel
    in_sel = jnp.logical_and(lane >= lo, lane < lo + EXPERTS_PER_GROUP)
    cand = jnp.where(in_sel, logits, -jnp.inf)
    v1 = jnp.max(cand, axis=-1, keepdims=True)
    i1 = jnp.min(jnp.where(jnp.logical_and(in_sel, cand == v1), lane, no_lane), axis=-1, keepdims=True)
    rest = jnp.logical_and(in_sel, lane != i1)
    cand2 = jnp.where(rest, logits, -jnp.inf)
    v2 = jnp.max(cand2, axis=-1, keepdims=True)
    i2 = jnp.min(jnp.where(jnp.logical_and(rest, cand2 == v2), lane, no_lane), axis=-1, keepdims=True)
    e2 = jnp.exp(v2 - v1)
    gate0 = p_group / (1.0 + e2)
    gate1 = p_group * e2 / (1.0 + e2)

    oh0 = (lane == i1)
    oh1 = (lane == i2)
    before0 = jnp.dot(tri_ref[...], oh0.astype(BF16), preferred_element_type=F32)
    before1 = jnp.dot(tri_ref[...], oh1.astype(BF16), preferred_element_type=F32)
    tot0 = jnp.sum(oh0.astype(F32), axis=0, keepdims=True)
    tot1 = jnp.sum(oh1.astype(F32), axis=0, keepdims=True)
    cnt = tot0 + tot1
    rank0 = jnp.sum(jnp.where(oh0, before0, 0.0), axis=-1, keepdims=True)
    rank1 = jnp.sum(jnp.where(oh1, tot0 + before1, 0.0), axis=-1, keepdims=True)

    rc = lax.broadcasted_iota(jnp.int32, (tm, ROUTE_COLS), 1)
    route = jnp.where(rc == 0, i1 - EXPERT_LANE0,
                      jnp.where(rc == 1, i2 - EXPERT_LANE0, jnp.where(rc == 2, rank0, rank1)))
    route_ref[...] = route.astype(jnp.int32)
    gc = lax.broadcasted_iota(jnp.int32, (tm, TOP_K), 1)
    gate_ref[...] = jnp.where(gc == 0, gate0, gate1)
    cnt_ref[0] = jnp.broadcast_to(pltpu.roll(cnt, LANES - EXPERT_LANE0, 1), (8, LANES))


def _merge(o, c, ga, gb, x, wa, wc, bc, wo, gf, wr_hi, wr_lo, br, tm):
    t = x.shape[0]
    nt = t // tm
    row = lambda i: (i, 0)
    const = lambda i: (0, 0)
    full = lambda a: pl.BlockSpec(a.shape, const)
    return pl.pallas_call(
        _merge_kernel,
        grid=(nt,),
        in_specs=[
            pl.BlockSpec((tm, ATTN_WIDTH), row),
            pl.BlockSpec((tm, CONV_WIDTH), row),
            pl.BlockSpec((tm, D_MODEL), row),
            pl.BlockSpec((tm, D_MODEL), row),
            pl.BlockSpec((tm, D_MODEL), row),
            full(wa), full(wc), full(bc), full(wo), full(gf), full(wr_hi), full(wr_lo), full(br),
        ],
        out_specs=(
            pl.BlockSpec((tm, D_MODEL), row),
            pl.BlockSpec((tm * ROW_TILE, LANES), row),
            pl.BlockSpec((tm, ROUTE_COLS), row),
            pl.BlockSpec((tm, TOP_K), row),
            pl.BlockSpec((1, 8, LANES), lambda i: (i, 0, 0)),
        ),
        out_shape=(
            jax.ShapeDtypeStruct((t, D_MODEL), F32),
            jax.ShapeDtypeStruct((t * ROW_TILE, LANES), F32),
            jax.ShapeDtypeStruct((t, ROUTE_COLS), jnp.int32),
            jax.ShapeDtypeStruct((t, TOP_K), F32),
            jax.ShapeDtypeStruct((nt, 8, LANES), F32),
        ),
        scratch_shapes=[pltpu.VMEM((tm, tm), BF16)],
        compiler_params=_cparams(("arbitrary",)),
        name="merge",
    )(o, c, ga, gb, x, wa, wc, bc, wo, gf, wr_hi, wr_lo, br)


def _pos_kernel(route_ref, base_ref, pos_ref):
    tm = route_ref.shape[0]
    route = route_ref[...].astype(F32)
    lane = lax.broadcasted_iota(jnp.int32, (tm, LANES), 1).astype(F32)
    cols = []
    for k in range(TOP_K):
        start = jnp.sum(jnp.where(lane == route[:, k:k + 1], base_ref[0, 0:1, :], 0.0),
                        axis=-1, keepdims=True)
        cols.append(start + route[:, TOP_K + k:TOP_K + k + 1])
    sel = lax.broadcasted_iota(jnp.int32, (tm, TOP_K), 1)
    pos_ref[...] = (jnp.where(sel == 0, cols[0], cols[1]) * ROW_TILE).astype(jnp.int32)


def _positions(route, base, tm):
    nt = base.shape[0]
    base_rows = jnp.zeros((nt, SUBLANES, LANES), F32).at[:, :, :N_EXPERTS].set(
        base.astype(F32)[:, None, :])
    return pl.pallas_call(
        _pos_kernel,
        grid=(nt,),
        in_specs=[pl.BlockSpec((tm, ROUTE_COLS), lambda i: (i, 0)),
                  pl.BlockSpec((1, SUBLANES, LANES), lambda i: (i, 0, 0))],
        out_specs=pl.BlockSpec((tm, TOP_K), lambda i: (i, 0)),
        out_shape=jax.ShapeDtypeStruct((route.shape[0], TOP_K), jnp.int32),
        compiler_params=_cparams(("parallel",)),
        name="positions",
    )(route, base_rows)


def _sort_kernel(pos_ref, h_ref, xs_ref):
    tm = h_ref.shape[0] // ROW_TILE

    def place(t, carry):
        row = h_ref[pl.ds(pl.multiple_of(t * ROW_TILE, ROW_TILE), ROW_TILE), :]
        for k in range(TOP_K):
            xs_ref[pl.ds(pl.multiple_of(pos_ref[TOP_K * t + k], ROW_TILE), ROW_TILE), :] = row
        return carry

    lax.fori_loop(0, tm, place, 0, unroll=8)


def _tile_sort(pos_flat, h2, tm):
    nt = h2.shape[0] // (tm * ROW_TILE)
    return pl.pallas_call(
        _sort_kernel,
        grid=(nt,),
        in_specs=[pl.BlockSpec((TOP_K * tm,), lambda i: (i,), memory_space=pltpu.SMEM),
                  pl.BlockSpec((tm * ROW_TILE, LANES), lambda i: (i, 0))],
        out_specs=pl.BlockSpec((TOP_K * tm * ROW_TILE, LANES), lambda i: (i, 0)),
        out_shape=jax.ShapeDtypeStruct((TOP_K * h2.shape[0], LANES), F32),
        compiler_params=_cparams(("parallel",)),
        name="tile_sort",
    )(pos_flat, h2)


RARE_RUN = 64
BLOCKS_PER_STEP = 2


def _moe_kernel(be_ref, bj_ref, ilo_ref, ihi_ref, valid_ref, cum_ref, cnt_ref, off_ref, nused_ref,
                next_e_ref, xs_hbm, wg_hbm, wu_hbm, wd_hbm, ys_hbm, trash_hbm,
                xbuf, obuf, wg_s, wu_s, wd_s, wg_f, wu_f, wd_f, wslot, sem_in, sem_out, sem_w,
                *, tile_rows, layer):
    step = pl.program_id(0)
    rows8 = lambda r, n: pl.ds(pl.multiple_of(r * ROW_TILE, ROW_TILE), n * ROW_TILE)

    def for_runs(bb, fn):
        e = be_ref[bb]
        lo_blk = bj_ref[bb] * MOE_BLOCK

        def body(i, carry):
            c = cum_ref[i * N_EXPERTS + e]
            lo = jnp.maximum(c, lo_blk)
            hi = jnp.minimum(c + cnt_ref[i * N_EXPERTS + e], lo_blk + MOE_BLOCK)
            fn(i * tile_rows + off_ref[i * N_EXPERTS + e] + (lo - c), lo - lo_blk,
               jnp.maximum(hi - lo, 0))
            return carry

        lax.fori_loop(ilo_ref[bb], ihi_ref[bb] + 1, body, 0)

    def for_chunks(length, fn):
        def piece(size):
            @pl.when((length & size) != 0)
            def _():
                fn(length & ~(2 * size - 1), size)

        sizes = [1 << bit for bit in reversed(range(MOE_BLOCK.bit_length()))]

        @pl.when(length >= RARE_RUN)
        def _():
            for size in sizes:
                if size >= RARE_RUN:
                    piece(size)

        for size in sizes:
            if size < RARE_RUN:
                piece(size)

    def gather(bb, s):
        def run(src, dst, length):
            for_chunks(length, lambda o, n: pltpu.make_async_copy(
                xs_hbm.at[rows8(src + o, n)], xbuf.at[s, rows8(dst + o, n)], sem_in.at[s]).start())
        for_runs(bb, run)
        v = valid_ref[bb]
        for_chunks(MOE_BLOCK - v, lambda o, n: pltpu.make_async_copy(
            xs_hbm.at[rows8(o, n)], xbuf.at[s, rows8(v + o, n)], sem_in.at[s]).start())

    def scatter(bb, s):
        def run(src, dst, length):
            for_chunks(length, lambda o, n: pltpu.make_async_copy(
                obuf.at[s, rows8(dst + o, n)], ys_hbm.at[rows8(src + o, n)], sem_out.at[s]).start())
        for_runs(bb, run)
        v = valid_ref[bb]
        for_chunks(MOE_BLOCK - v, lambda o, n: pltpu.make_async_copy(
            obuf.at[s, rows8(v + o, n)], trash_hbm.at[rows8(s * MOE_BLOCK + v + o, n)],
            sem_out.at[s]).start())

    def wait_gather(s):
        pltpu.make_async_copy(xs_hbm.at[rows8(0, MOE_BLOCK)], xbuf.at[s], sem_in.at[s]).wait()

    def wait_scatter(s):
        pltpu.make_async_copy(obuf.at[s], ys_hbm.at[rows8(0, MOE_BLOCK)], sem_out.at[s]).wait()

    def weight_copies(expert, s):
        return [pltpu.make_async_copy(src.at[layer, expert], dst.at[s], sem_w.at[s])
                for src, dst in ((wg_hbm, wg_f), (wu_hbm, wu_f), (wd_hbm, wd_f))]

    n_used = nused_ref[0]

    def one_block(b, slot):
        other = 1 - slot
        live = b < n_used
        e = be_ref[b]
        e_prev = be_ref[jnp.maximum(b - 1, 0)]

        if slot == 0:
            @pl.when(b == 0)
            def _():
                obuf[...] = jnp.zeros_like(obuf)
                for s in range(2):
                    init = pltpu.make_async_copy(
                        obuf.at[s], trash_hbm.at[rows8(s * MOE_BLOCK, MOE_BLOCK)], sem_out.at[s])
                    init.start()
                    init.wait()
                gather(0, 0)
                wslot[0] = 0
                for c in weight_copies(e, 0):
                    c.start()

        @pl.when(live)
        def _():
            wait_gather(slot)

        @pl.when(jnp.logical_and(live, b >= 2))
        def _():
            wait_scatter(slot)

        @pl.when(jnp.logical_and(live, jnp.logical_or(b == 0, e != e_prev)))
        def _():
            s = wslot[0]
            for c in weight_copies(e, s):
                c.wait()
            wg_s[...] = wg_f[s].astype(BF16)
            wu_s[...] = wu_f[s].astype(BF16)
            wd_s[...] = wd_f[s].astype(BF16)
            e_next = next_e_ref[b]

            @pl.when(e_next != e)
            def _():
                for c in weight_copies(e_next, 1 - s):
                    c.start()
            wslot[0] = 1 - s

        @pl.when(b + 1 < n_used)
        def _():
            gather(b + 1, other)

        @pl.when(jnp.logical_and(live, b >= 1))
        def _():
            scatter(b - 1, other)

        @pl.when(live)
        def _():
            x = _load_row_tiles(xbuf.at[slot], MOE_BLOCK).astype(BF16)
            hg = jnp.dot(x, wg_s[...], preferred_element_type=F32)
            hu = jnp.dot(x, wu_s[...], preferred_element_type=F32)
            hb = (hg * _sigmoid(hg) * hu).astype(BF16)
            _store_row_tiles(obuf.at[slot], jnp.dot(hb, wd_s[...], preferred_element_type=F32))

        @pl.when(b == n_used - 1)
        def _():
            scatter(b, slot)
            wait_scatter(other)
            wait_scatter(slot)

    for sub in range(BLOCKS_PER_STEP):
        one_block(step * BLOCKS_PER_STEP + sub, sub)


def _moe(plan, xs, w_gate, w_up, w_down, layer, n_blocks, tile_rows):
    n_pre = len(plan)
    assert n_blocks % BLOCKS_PER_STEP == 0
    grid_spec = pltpu.PrefetchScalarGridSpec(
        num_scalar_prefetch=n_pre,
        grid=(n_blocks // BLOCKS_PER_STEP,),
        in_specs=[pl.BlockSpec(memory_space=pl.ANY)] * 4,
        out_specs=(pl.BlockSpec(memory_space=pl.ANY), pl.BlockSpec(memory_space=pl.ANY)),
        scratch_shapes=[pltpu.VMEM((2, MOE_BLOCK * ROW_TILE, LANES), F32),
                        pltpu.VMEM((2, MOE_BLOCK * ROW_TILE, LANES), F32),
                        pltpu.VMEM((D_MODEL, EXPERT_FF), BF16),
                        pltpu.VMEM((D_MODEL, EXPERT_FF), BF16),
                        pltpu.VMEM((EXPERT_FF, D_MODEL), BF16),
                        pltpu.VMEM((2, D_MODEL, EXPERT_FF), F32),
                        pltpu.VMEM((2, D_MODEL, EXPERT_FF), F32),
                        pltpu.VMEM((2, EXPERT_FF, D_MODEL), F32),
                        pltpu.SMEM((1,), jnp.int32),
                        pltpu.SemaphoreType.DMA((2,)),
                        pltpu.SemaphoreType.DMA((2,)),
                        pltpu.SemaphoreType.DMA((2,))],
    )
    ys, _ = pl.pallas_call(
        functools.partial(_moe_kernel, tile_rows=tile_rows, layer=layer),
        grid_spec=grid_spec,
        out_shape=(jax.ShapeDtypeStruct(xs.shape, F32),
                   jax.ShapeDtypeStruct((2 * MOE_BLOCK * ROW_TILE, LANES), F32)),
        compiler_params=_cparams(("arbitrary",)),
        name="moe",
    )(*plan, xs, w_gate, w_up, w_down)
    return ys


def _ple_kernel(pos_ref, gate_ref, ys_ref, x1_ref, p_ref, gp_ref, wpg_ref, wpe_ref, gfin_ref,
                out_ref, moe_ref, *, final_norm):
    tm = x1_ref.shape[0]
    tile = lambda r: pl.ds(pl.multiple_of(r * ROW_TILE, ROW_TILE), ROW_TILE)

    def combine(t, carry):
        acc = None
        for k in range(TOP_K):
            row0 = pl.multiple_of(pos_ref[TOP_K * t + k], ROW_TILE)
            term = gate_ref[TOP_K * t + k] * ys_ref[pl.ds(row0, ROW_TILE), :]
            acc = term if acc is None else acc + term
        moe_ref[tile(t), :] = acc
        return carry

    lax.fori_loop(0, tm, combine, 0, unroll=32)
    x2 = x1_ref[...] + _load_row_tiles(moe_ref, tm)
    hp = _rms(x2, gp_ref[...]).astype(BF16)
    g_ple = _sigmoid(jnp.dot(hp, wpg_ref[...], preferred_element_type=F32))
    emb = jnp.dot(p_ref[...].astype(BF16), wpe_ref[...], preferred_element_type=F32)
    x3 = x2 + g_ple * emb
    if final_norm:
        x3 = _rms(x3, gfin_ref[...])
    out_ref[...] = x3


def _ple(pos, gate_flat, ys, x1, p, gp, wpg, wpe, gfin, tm, final_norm):
    t = x1.shape[0]
    row = lambda i: (i, 0)
    const = lambda i: (0, 0)
    full = lambda a: pl.BlockSpec(a.shape, const)
    smem = lambda n: pl.BlockSpec((n,), lambda i: (i,), memory_space=pltpu.SMEM)
    return pl.pallas_call(
        functools.partial(_ple_kernel, final_norm=final_norm),
        grid=(t // tm,),
        in_specs=[
            smem(TOP_K * tm),
            smem(TOP_K * tm),
            pl.BlockSpec((TOP_K * tm * ROW_TILE, LANES), row),
            pl.BlockSpec((tm, D_MODEL), row),
            pl.BlockSpec((tm, PLE_DIM), row),
            full(gp), full(wpg), full(wpe), full(gfin),
        ],
        out_specs=pl.BlockSpec((tm, D_MODEL), row),
        out_shape=jax.ShapeDtypeStruct((t, D_MODEL), F32),
        scratch_shapes=[pltpu.VMEM((tm * ROW_TILE, LANES), F32)],
        compiler_params=_cparams(("parallel",)),
        name="ple",
    )(pos, gate_flat, ys, x1, p, gp, wpg, wpe, gfin)


def _rope_tables(seq):
    inv = jnp.power(ROPE_THETA, -jnp.arange(0, HEAD_DIM, 2, dtype=F32) / HEAD_DIM)
    ang = jnp.arange(seq, dtype=F32)[:, None] * inv[None, :]
    cos, sin = jnp.cos(ang), jnp.sin(ang)
    cos_h = jnp.concatenate([cos, cos], axis=-1)
    sin_h = jnp.concatenate([-sin, sin], axis=-1)
    reps = LANES // HEAD_DIM
    return jnp.tile(cos_h, (1, reps)), jnp.tile(sin_h, (1, reps))


def _dispatch_plan(route_counts, group, n_blocks):
    i32 = jnp.int32
    grouped = route_counts.reshape(-1, group, N_EXPERTS)
    counts = jnp.sum(grouped, axis=1)
    base = ((jnp.cumsum(counts, axis=1) - counts)[:, None, :]
            + jnp.cumsum(grouped, axis=1) - grouped).reshape(-1, N_EXPERTS).astype(i32)
    total = jnp.sum(counts, axis=0)
    blocks_of = (total + MOE_BLOCK - 1) // MOE_BLOCK
    block_end = jnp.cumsum(blocks_of)
    bidx = jnp.arange(n_blocks, dtype=i32)
    block_e = jnp.minimum(jnp.sum((block_end[None, :] <= bidx[:, None]).astype(i32), axis=1),
                          N_EXPERTS - 1)
    is_e = (block_e[:, None] == jnp.arange(N_EXPERTS, dtype=i32)[None, :]).astype(i32)
    pick = lambda per_expert: jnp.sum(is_e * per_expert[None, :], axis=1)
    pick_tiles = lambda tab: jnp.sum(is_e[:, None, :] * tab[None, :, :], axis=2)
    block_j = bidx - pick(block_end - blocks_of)
    used = bidx < block_end[-1]
    cum = jnp.cumsum(counts, axis=0) - counts
    off = jnp.cumsum(counts, axis=1) - counts
    lo = block_j * MOE_BLOCK
    hi = jnp.minimum(lo + MOE_BLOCK, pick(total))
    run_start = pick_tiles(cum)
    run_end = run_start + pick_tiles(counts)
    tile_lo = jnp.sum((run_end <= lo[:, None]).astype(i32), axis=1)
    tile_hi = jnp.sum((run_start < hi[:, None]).astype(i32), axis=1) - 1
    valid = jnp.clip(hi - lo, 0, MOE_BLOCK)
    tile_lo = jnp.where(used, tile_lo, 0)
    tile_hi = jnp.where(used, tile_hi, -1)
    valid = jnp.where(used, valid, 0)
    e_ids = jnp.arange(N_EXPERTS, dtype=i32)
    later = jnp.where((e_ids[None, :] > e_ids[:, None]) & (blocks_of[None, :] > 0), e_ids[None, :], N_EXPERTS)
    next_used = jnp.min(later, axis=1)
    next_e = pick(jnp.where(next_used == N_EXPERTS, e_ids, next_used))
    flat = lambda a: a.reshape(-1).astype(i32)
    return base, (flat(block_e), flat(block_j), flat(tile_lo), flat(tile_hi), flat(valid),
                  flat(cum), flat(counts), flat(off), flat(block_end[-1:]), flat(next_e))


def kernel(x, p, norm_mix, w_in, w_dw, b_dw, ln_conv_g, ln_conv_b, w_attn_out, w_conv_out,
           b_conv_out, w_out, norm_ffn, w_route_group, b_route_group, w_route_expert,
           b_route_expert, w_exp_gate, w_exp_up, w_exp_down, norm_ple, w_ple_proj, w_ple_gate,
           norm_final):
    bsz, seq, d = x.shape
    depth = w_in.shape[0]
    t = bsz * seq
    tm = 512
    sort_tm = 1024
    n_assign = t * TOP_K
    assert n_assign > MOE_BLOCK
    n_rows = (-(-n_assign // MOE_BLOCK)) * MOE_BLOCK + N_EXPERTS * MOE_BLOCK
    cos, sin = _rope_tables(seq)
    row2 = lambda a: a.reshape(1, -1)

    xf = x.reshape(t, d)
    for i in range(depth):
        n_qkv = 3 * ATTN_WIDTH
        q, k, v = _inproj(xf, row2(norm_mix[i]), w_in[i, :, :n_qkv].astype(BF16), cos, sin,
                          bsz, seq, tm)
        c, ga, gb = _gateproj(xf, row2(norm_mix[i]), w_in[i, :, n_qkv:].astype(BF16), w_dw[i],
                              row2(b_dw[i]), row2(ln_conv_g[i]), row2(ln_conv_b[i]), seq, tm)
        o = _attention(q, k, v).reshape(t, ATTN_WIDTH)

        w_r = jnp.zeros((d, LANES), F32)
        w_r = w_r.at[:, :N_GROUPS].set(w_route_group[i])
        w_r = w_r.at[:, EXPERT_LANE0:EXPERT_LANE0 + N_EXPERTS].set(w_route_expert[i])
        w_r_hi = w_r.astype(BF16)
        w_r_lo = (w_r - w_r_hi.astype(F32)).astype(BF16)
        b_r = jnp.zeros((1, LANES), F32)
        b_r = b_r.at[0, :N_GROUPS].set(b_route_group[i])
        b_r = b_r.at[0, EXPERT_LANE0:EXPERT_LANE0 + N_EXPERTS].set(b_route_expert[i])

        x1, h2, route, gate, cnt = _merge(
            o, c, ga, gb, xf, w_attn_out[i].astype(BF16), w_conv_out[i].astype(BF16),
            row2(b_conv_out[i]), w_out[i].astype(BF16), row2(norm_ffn[i]), w_r_hi, w_r_lo, b_r, tm)

        counts = cnt[:, 0, :N_EXPERTS].astype(jnp.int32)
        base, plan = _dispatch_plan(counts, sort_tm // tm, n_rows // MOE_BLOCK)
        pos = _positions(route, base, tm).reshape(-1)
        xs = _tile_sort(pos, h2, sort_tm)
        ys = _moe(plan, xs, w_exp_gate, w_exp_up, w_exp_down, i, n_rows // MOE_BLOCK, TOP_K * sort_tm)
        xf = _ple(pos, gate.reshape(-1), ys, x1, p[i].reshape(t, PLE_DIM), row2(norm_ple[i]),
                  w_ple_gate[i].astype(BF16), w_ple_proj[i].astype(BF16), row2(norm_final),
                  sort_tm, final_norm=(i == depth - 1))
    return xf.reshape(bsz, seq, d)
```
